```python
import numpy as np
import jax
import jax.numpy as jnp
from jax import lax


D_MODEL = 1024
BATCH = 8
SEQ = 2048
DEPTH = 2

N_MIXERS = 4
GROUP_WIDTH = D_MODEL // N_MIXERS
D_MIX = N_MIXERS * GROUP_WIDTH
CONV_WIDTH = 31
POOL_WINDOWS = (2, 4, 8, 16)
POOL_GROUP = GROUP_WIDTH // len(POOL_WINDOWS)
SGU_HEADS = 4
SGU_HEAD_DIM = GROUP_WIDTH // SGU_HEADS
SGU_CHUNK = 128
ATT_HEADS = 4
HEAD_DIM = GROUP_WIDTH // ATT_HEADS
KV_DIM = HEAD_DIM
ROPE_DIM = HEAD_DIM // 4
ROPE_THETA = 500000.0
CMP_BLOCK = 32
CMP_STRIDE = 16
SLC_BLOCK = 64
N_SELECT = 8
N_LOCAL = 2
WINDOW = 512
Q_BLOCK = 128
N_BRANCH = 3
D_FF = 4 * D_MODEL
NORM_EPS = 1e-6
NEG_INF = -1e30
FORCE_SCORE = 1e9
IN_WIDTHS = (GROUP_WIDTH, GROUP_WIDTH, GROUP_WIDTH, GROUP_WIDTH, GROUP_WIDTH, ATT_HEADS * HEAD_DIM, KV_DIM, KV_DIM, KV_DIM, KV_DIM, KV_DIM, KV_DIM, ATT_HEADS * N_BRANCH)
D_IN = 5 * GROUP_WIDTH + ATT_HEADS * HEAD_DIM + 6 * KV_DIM + ATT_HEADS * N_BRANCH

kernel_name = 'hybrid_parallel_group_block'


def rms_norm(x, g):
    xf = x.astype(jnp.float32)
    y = xf * lax.rsqrt(jnp.mean(xf * xf, axis=-1, keepdims=True) + NORM_EPS)
    return (y * g.astype(jnp.float32)).astype(x.dtype)


def layer_norm(x, g, b):
    xf = x.astype(jnp.float32)
    mu = jnp.mean(xf, axis=-1, keepdims=True)
    var = jnp.mean(jnp.square(xf - mu), axis=-1, keepdims=True)
    y = (xf - mu) * lax.rsqrt(var + NORM_EPS)
    return (y * g.astype(jnp.float32) + b.astype(jnp.float32)).astype(x.dtype)


def rope(x, pos):
    half = ROPE_DIM // 2
    inv = ROPE_THETA ** (-jnp.arange(half, dtype=jnp.float32) * 2.0 / ROPE_DIM)
    ang = pos.astype(jnp.float32)[..., None] * inv
    cos = jnp.cos(ang)[:, :, None, :]
    sin = jnp.sin(ang)[:, :, None, :]
    xf = x.astype(jnp.float32)
    x1 = xf[..., :half]
    x2 = xf[..., half:ROPE_DIM]
    out = jnp.concatenate([x1 * cos - x2 * sin, x2 * cos + x1 * sin, xf[..., ROPE_DIM:]], axis=-1)
    return out.astype(x.dtype)


def conv_group(a, gate, w, b, ln_g, ln_b):
    h = a * jax.nn.sigmoid(gate)
    c = h.shape[-1]
    h = lax.conv_general_dilated(h, w[:, None, :], window_strides=(1,), padding=[(CONV_WIDTH - 1, 0)], dimension_numbers=('NWC', 'WIO', 'NWC'), feature_group_count=c) + b
    return jax.nn.silu(layer_norm(h, ln_g, ln_b))


def pool_group(p, w, scale):
    bsz, s, c = p.shape
    cs = jnp.pad(jnp.cumsum(p.astype(jnp.float32), axis=1), ((0, 0), (1, 0), (0, 0)))
    t = jnp.arange(s)
    outs = []
    for gi, win in enumerate(POOL_WINDOWS):
        sl = slice(gi * POOL_GROUP, (gi + 1) * POOL_GROUP)
        lo = jnp.maximum(t + 1 - win, 0)
        total = cs[:, 1:, sl] - cs[:, lo, sl]
        count = (t + 1 - lo).astype(jnp.float32)[None, :, None]
        mixed = (total / count - p[..., sl].astype(jnp.float32)).astype(p.dtype)
        outs.append(jnp.einsum('bsc,cd->bsd', mixed, w[gi]))
    return jnp.concatenate(outs, axis=-1) * scale


def sgu_group(u, v, ln_g, ln_b, w_s, b_s):
    bsz, s, c = u.shape
    u = jax.nn.gelu(u)
    v = layer_norm(jax.nn.gelu(v), ln_g, ln_b)
    nc = s // SGU_CHUNK
    v = v.reshape(bsz, nc, SGU_CHUNK, SGU_HEADS, SGU_HEAD_DIM)
    mask = jnp.tril(jnp.ones((SGU_CHUNK, SGU_CHUNK), dtype=bool))
    w = jnp.where(mask[None], w_s, 0)
    mixed = jnp.einsum('hts,bcshd->bcthd', w, v) + jnp.transpose(b_s)[None, None, :, :, None]
    return u * mixed.reshape(bsz, s, c)


def nsa_group(q, kc, vc, ks, vs, kw, vw, gates, positions, pe_k, w1_k, w2_k, pe_v, w1_v, w2_v):
    bsz, s, _ = q.shape
    scale = HEAD_DIM ** -0.5
    t = jnp.arange(s)
    q = rope(q.reshape(bsz, s, ATT_HEADS, HEAD_DIM), positions)

    n_cmp = (s - CMP_BLOCK) // CMP_STRIDE + 1
    blk_start = jnp.arange(n_cmp) * CMP_STRIDE
    blk_end = blk_start + CMP_BLOCK - 1
    blk_idx = blk_start[:, None] + jnp.arange(CMP_BLOCK)[None, :]

    def compress(xx, pe, w1, w2):
        xb = xx[:, blk_idx] + pe
        hid = jax.nn.gelu(jnp.einsum('bnf,fd->bnd', xb.reshape(bsz, n_cmp, CMP_BLOCK * KV_DIM), w1))
        return jnp.einsum('bnd,de->bne', hid, w2)

    k_cmp = compress(kc, pe_k, w1_k, w2_k)
    v_cmp = compress(vc, pe_v, w1_v, w2_v)
    k_cmp = rope(k_cmp[:, :, None], positions[:, blk_end])[:, :, 0]
    s_cmp = jnp.einsum('bthd,bnd->bhtn', q, k_cmp).astype(jnp.float32) * scale
    cmp_mask = blk_end[None, :] <= t[:, None]
    p_cmp = jax.nn.softmax(jnp.where(cmp_mask, s_cmp, NEG_INF), axis=-1) * cmp_mask
    o_cmp = jnp.einsum('bhtn,bnd->bthd', p_cmp.astype(v_cmp.dtype), v_cmp)

    n_slc = s // SLC_BLOCK
    k_sel = min(N_SELECT, n_slc)
    slc_start = jnp.arange(n_slc) * SLC_BLOCK
    overlap = jnp.clip(jnp.minimum(blk_start[:, None] + CMP_BLOCK, slc_start[None, :] + SLC_BLOCK) - jnp.maximum(blk_start[:, None], slc_start[None, :]), 0).astype(jnp.float32) / CMP_STRIDE
    imp = jnp.einsum('bhtn,nj->btj', p_cmp, overlap)
    q_blk = t // SLC_BLOCK
    j = jnp.arange(n_slc)
    back = q_blk[:, None] - j[None, :]
    forced = (j[None, :] == 0) | ((back >= 0) & (back < N_LOCAL))
    imp = jnp.where(forced, FORCE_SCORE, jnp.where(back < 0, -1.0, imp))
    _, sel = lax.top_k(imp, k_sel)

    k_blocks = rope(ks[:, :, None], positions)[:, :, 0].reshape(bsz, n_slc, SLC_BLOCK, KV_DIM)
    v_blocks = vs.reshape(bsz, n_slc, SLC_BLOCK, KV_DIM)
    nq = s // Q_BLOCK

    def slc_chunk(args):
        qc, sc, tc = args
        kg = jax.vmap(lambda kb, ix: kb[ix])(k_blocks, sc)
        vg = jax.vmap(lambda vb, ix: vb[ix])(v_blocks, sc)
        sco = jnp.einsum('bqhd,bqkld->bhqkl', qc, kg).astype(jnp.float32) * scale
        kpos = sc[..., None] * SLC_BLOCK + jnp.arange(SLC_BLOCK)
        mask = (kpos <= tc[None, :, None, None])[:, None]
        sco = jnp.where(mask, sco, NEG_INF).reshape(bsz, ATT_HEADS, Q_BLOCK, k_sel * SLC_BLOCK)
        p = jax.nn.softmax(sco, axis=-1).reshape(bsz, ATT_HEADS, Q_BLOCK, k_sel, SLC_BLOCK)
        return jnp.einsum('bhqkl,bqkld->bqhd', p.astype(vg.dtype), vg)

    q_chunks = jnp.transpose(q.reshape(bsz, nq, Q_BLOCK, ATT_HEADS, HEAD_DIM), (1, 0, 2, 3, 4))
    sel_chunks = jnp.transpose(sel.reshape(bsz, nq, Q_BLOCK, k_sel), (1, 0, 2, 3))
    o_slc = lax.map(slc_chunk, (q_chunks, sel_chunks, t.reshape(nq, Q_BLOCK)))
    o_slc = jnp.transpose(o_slc, (1, 0, 2, 3, 4)).reshape(bsz, s, ATT_HEADS, HEAD_DIM)

    kw_r = rope(kw[:, :, None], positions)[:, :, 0]
    pad = ((0, 0), (WINDOW, 0), (0, 0))
    kw_p = jnp.pad(kw_r, pad)
    vw_p = jnp.pad(vw, pad)
    span = WINDOW + Q_BLOCK
    band = jnp.arange(nq)[:, None] * Q_BLOCK + jnp.arange(span)[None, :]
    kb = kw_p[:, band]
    vb = vw_p[:, band]
    qb = q.reshape(bsz, nq, Q_BLOCK, ATT_HEADS, HEAD_DIM)
    sco = jnp.einsum('bcqhd,bckd->bhcqk', qb, kb).astype(jnp.float32) * scale
    kpos = band - WINDOW
    diff = t.reshape(nq, Q_BLOCK)[:, :, None] - kpos[:, None, :]
    wmask = (kpos[:, None, :] >= 0) & (diff >= 0) & (diff < WINDOW)
    p = jax.nn.softmax(jnp.where(wmask, sco, NEG_INF), axis=-1)
    o_win = jnp.einsum('bhcqk,bckd->bcqhd', p.astype(vb.dtype), vb).reshape(bsz, s, ATT_HEADS, HEAD_DIM)

    g = jax.nn.sigmoid(gates.reshape(bsz, s, ATT_HEADS, N_BRANCH))
    o = g[..., 0:1] * o_cmp + g[..., 1:2] * o_slc + g[..., 2:3] * o_win
    return o.reshape(bsz, s, ATT_HEADS * HEAD_DIM)


def setup_inputs(seed: int = 0) -> dict:
    key = jax.random.key(seed)
    keys = iter(jax.random.split(key, 32))
    f32 = jnp.float32
    L = DEPTH

    def nrm(shape, scale):
        return jax.random.normal(next(keys), shape, f32) * scale

    def gain(shape):
        return 1.0 + nrm(shape, 0.05)

    x = jax.random.normal(next(keys), (BATCH, SEQ, D_MODEL), f32)
    positions = jnp.arange(SEQ, dtype=jnp.int32)[None, :] + jax.random.randint(next(keys), (BATCH, 1), 0, 4096, dtype=jnp.int32)
    return {
        'x': x,
        'positions': positions,
        'pre_mix_norm': gain((L, D_MODEL)),
        'post_mix_norm': gain((L, D_MODEL)),
        'pre_ffn_norm': gain((L, D_MODEL)),
        'post_ffn_norm': gain((L, D_MODEL)),
        'w_in': nrm((L, D_MODEL, D_IN), D_MODEL ** -0.5),
        'conv_w': nrm((L, CONV_WIDTH, GROUP_WIDTH), CONV_WIDTH ** -0.5),
        'conv_b': nrm((L, GROUP_WIDTH), 0.02),
        'conv_ln_g': gain((L, GROUP_WIDTH)),
        'conv_ln_b': nrm((L, GROUP_WIDTH), 0.02),
        'pool_w': nrm((L, len(POOL_WINDOWS), POOL_GROUP, POOL_GROUP), POOL_GROUP ** -0.5),
        'pool_scale': 1.0 + nrm((L, GROUP_WIDTH), 0.1),
        'sgu_ln_g': gain((L, GROUP_WIDTH)),
        'sgu_ln_b': nrm((L, GROUP_WIDTH), 0.02),
        'sgu_w': nrm((L, SGU_HEADS, SGU_CHUNK, SGU_CHUNK), SGU_CHUNK ** -0.5),
        'sgu_b': 1.0 + nrm((L, SGU_HEADS, SGU_CHUNK), 0.05),
        'cmp_k_pe': nrm((L, CMP_BLOCK, KV_DIM), 0.1),
        'cmp_k_w1': nrm((L, CMP_BLOCK * KV_DIM, KV_DIM), (CMP_BLOCK * KV_DIM) ** -0.5),
        'cmp_k_w2': nrm((L, KV_DIM, KV_DIM), KV_DIM ** -0.5),
        'cmp_v_pe': nrm((L, CMP_BLOCK, KV_DIM), 0.1),
        'cmp_v_w1': nrm((L, CMP_BLOCK * KV_DIM, KV_DIM), (CMP_BLOCK * KV_DIM) ** -0.5),
        'cmp_v_w2': nrm((L, KV_DIM, KV_DIM), KV_DIM ** -0.5),
        'w_out': nrm((L, D_MIX, D_MODEL), D_MIX ** -0.5),
        'ffn_w1': nrm((L, D_MODEL, D_FF), D_MODEL ** -0.5),
        'ffn_w2': nrm((L, D_FF, D_MODEL), D_FF ** -0.5),
    }


def reference(x, positions, pre_mix_norm, post_mix_norm, pre_ffn_norm, post_ffn_norm, w_in, conv_w, conv_b, conv_ln_g, conv_ln_b, pool_w, pool_scale, sgu_ln_g, sgu_ln_b, sgu_w, sgu_b, cmp_k_pe, cmp_k_w1, cmp_k_w2, cmp_v_pe, cmp_v_w1, cmp_v_w2, w_out, ffn_w1, ffn_w2):
    splits = np.cumsum(IN_WIDTHS)[:-1].tolist()
    for l in range(DEPTH):
        h = rms_norm(x, pre_mix_norm[l])
        z = jnp.einsum('bsd,de->bse', h, w_in[l])
        (c_val, c_gate, p_in, s_u, s_v, q, kc, vc, ks, vs, kw, vw, g) = jnp.split(z, splits, axis=-1)
        y_conv = conv_group(c_val, c_gate, conv_w[l], conv_b[l], conv_ln_g[l], conv_ln_b[l])
        y_pool = pool_group(p_in, pool_w[l], pool_scale[l])
        y_sgu = sgu_group(s_u, s_v, sgu_ln_g[l], sgu_ln_b[l], sgu_w[l], sgu_b[l])
        y_nsa = nsa_group(q, kc, vc, ks, vs, kw, vw, g, positions, cmp_k_pe[l], cmp_k_w1[l], cmp_k_w2[l], cmp_v_pe[l], cmp_v_w1[l], cmp_v_w2[l])
        mix = jnp.concatenate([y_conv, y_pool, y_sgu, y_nsa], axis=-1)
        x = x + rms_norm(jnp.einsum('bse,ed->bsd', mix, w_out[l]), post_mix_norm[l])
        h = rms_norm(x, pre_ffn_norm[l])
        f = jnp.square(jax.nn.relu(jnp.einsum('bsd,df->bsf', h, ffn_w1[l])))
        x = x + rms_norm(jnp.einsum('bsf,fd->bsd', f, ffn_w2[l]), post_ffn_norm[l])
    return x
```

```python
import functools

import jax
import jax.numpy as jnp
from jax import lax
from jax.experimental import pallas as pl
from jax.experimental.pallas import tpu as pltpu

F32 = jnp.float32
BF16 = jnp.bfloat16

D_MODEL = 1024
GROUP_WIDTH = 256
CONV_WIDTH = 31
POOL_WINDOWS = (2, 4, 8, 16)
POOL_GROUP = 64
SGU_HEADS = 4
SGU_CHUNK = 128
ATT_HEADS = 4
HEAD_DIM = 64
ROPE_DIM = 16
ROPE_HALF = 8
ROPE_THETA = 500000.0
CMP_BLOCK = 32
CMP_STRIDE = 16
SLC_BLOCK = 64
N_SELECT = 8
N_LOCAL = 2
WINDOW = 512
N_BRANCH = 3
D_FF = 4096
NORM_EPS = 1e-6
NEG_INF = -1e30
FORCE_SCORE = 1e9

LANES = 128
SUBLANES = 8
VMEM_LIMIT_BYTES = 56 * 1024 * 1024

PROJ_ROWS = 512
OUT_ROWS = 512
FFN_ROWS = 1024
FFN_COLS = 1024
CONV_ROWS = 64
POOL_ROWS = 64
ATT_Q = 128
ATT_K = 128

COL_CVAL, COL_CGATE, COL_POOL, COL_SGU_U, COL_SGU_V = 0, 1, 2, 3, 4
COL_KCVC, COL_KS, COL_KW = 10, 11, 12
ZA_WIDTH = 1664
ROW_Q, ROW_VS, ROW_VW, ROW_G = 0, 256, 320, 384
ZB_ROWS = 400


def _cparams(*sem):
    return pltpu.CompilerParams(dimension_semantics=sem, vmem_limit_bytes=VMEM_LIMIT_BYTES)


def _gelu(x):
    return 0.5 * x * (1.0 + jnp.tanh(0.7978845608028654 * (x + 0.044715 * (x * x * x))))


def _sigmoid(x):
    return 1.0 / (1.0 + jnp.exp(-x))


def _layer_norm(x, g, b):
    mu = jnp.mean(x, axis=-1, keepdims=True)
    d = x - mu
    var = jnp.mean(d * d, axis=-1, keepdims=True)
    return d * lax.rsqrt(var + NORM_EPS) * g + b


def _rms_norm(x, g):
    return x * lax.rsqrt(jnp.mean(x * x, axis=-1, keepdims=True) + NORM_EPS) * g


def _rope_tok_kernel(pos_ref, inv_ref, sgn_ref, cs_ref):
    ang = pos_ref[0] * inv_ref[...]
    lane = lax.broadcasted_iota(jnp.int32, ang.shape, 1)
    cs_ref[0] = jnp.where(lane < HEAD_DIM, jnp.cos(ang), jnp.sin(ang) * sgn_ref[...])


def _rope_tok_table(pos_col, inv128, sgn128):
    b, n, _ = pos_col.shape
    return pl.pallas_call(
        _rope_tok_kernel,
        grid=(b,),
        in_specs=[pl.BlockSpec((1, n, 1), lambda i: (i, 0, 0)),
                  pl.BlockSpec((1, LANES), lambda i: (0, 0)),
                  pl.BlockSpec((1, LANES), lambda i: (0, 0))],
        out_specs=pl.BlockSpec((1, n, LANES), lambda i: (i, 0, 0)),
        out_shape=jax.ShapeDtypeStruct((b, n, LANES), F32),
        compiler_params=_cparams("parallel"),
        name="rope_tok_table",
    )(pos_col, inv128, sgn128)


def _rope_chan_kernel(pos_ref, inv_ref, cos_ref, sin_ref):
    ang = inv_ref[...] * pos_ref[0]
    cos_ref[0] = jnp.cos(ang)
    sin_ref[0] = jnp.sin(ang)


def _rope_chan_table(pos_row, inv_col):
    b, _, s = pos_row.shape
    spec = pl.BlockSpec((1, ROPE_HALF, s), lambda i: (i, 0, 0))
    return pl.pallas_call(
        _rope_chan_kernel,
        grid=(b,),
        in_specs=[pl.BlockSpec((1, 1, s), lambda i: (i, 0, 0)),
                  pl.BlockSpec((ROPE_HALF, 1), lambda i: (0, 0))],
        out_specs=[spec, spec],
        out_shape=[jax.ShapeDtypeStruct((b, ROPE_HALF, s), F32)] * 2,
        compiler_params=_cparams("parallel"),
        name="rope_chan_table",
    )(pos_row, inv_col)


def _proj_kernel(x_ref, g_ref, wa_ref, wb_ref, za_ref, zb_ref):
    h = _rms_norm(x_ref[0], g_ref[...]).astype(BF16)
    za_ref[0] = jnp.dot(h, wa_ref[...], preferred_element_type=F32)
    zb_ref[0] = lax.dot_general(wb_ref[...], h, (((1,), (1,)), ((), ())),
                                preferred_element_type=F32)


def _proj(x, g, wa, wb):
    b, s, d = x.shape
    tm = min(PROJ_ROWS, s)
    return pl.pallas_call(
        _proj_kernel,
        grid=(b, s // tm),
        in_specs=[pl.BlockSpec((1, tm, d), lambda i, j: (i, j, 0)),
                  pl.BlockSpec((1, d), lambda i, j: (0, 0)),
                  pl.BlockSpec((d, ZA_WIDTH), lambda i, j: (0, 0)),
                  pl.BlockSpec((ZB_ROWS, d), lambda i, j: (0, 0))],
        out_specs=[pl.BlockSpec((1, tm, ZA_WIDTH), lambda i, j: (i, j, 0)),
                   pl.BlockSpec((1, ZB_ROWS, tm), lambda i, j: (i, 0, j))],
        out_shape=[jax.ShapeDtypeStruct((b, s, ZA_WIDTH), F32),
                   jax.ShapeDtypeStruct((b, ZB_ROWS, s), F32)],
        compiler_params=_cparams("parallel", "parallel"),
        name="in_proj",
    )(x, g, wa, wb)


CONV_PAD = 32


def _conv_kernel(a_ref, gate_ref, w_ref, b_ref, lg_ref, lb_ref, o_ref, hp_ref):
    s = a_ref.shape[1]
    hp_ref[0:CONV_PAD, :] = jnp.zeros((CONV_PAD, GROUP_WIDTH), F32)
    hp_ref[CONV_PAD:CONV_PAD + s, :] = a_ref[0] * _sigmoid(gate_ref[0])
    shift = CONV_PAD - (CONV_WIDTH - 1)

    def body(i, carry):
        t0 = pl.multiple_of(i * CONV_ROWS, CONV_ROWS)
        acc = jnp.zeros((CONV_ROWS, GROUP_WIDTH), F32) + b_ref[...]
        win = hp_ref[pl.ds(t0, CONV_ROWS + CONV_PAD), :]
        for r in range(SUBLANES):
            taps = [k for k in range(CONV_WIDTH) if (k + shift) % SUBLANES == r]
            span = max(k + shift - r for k in taps) + CONV_ROWS
            wr = win[r:r + span, :]
            for k in taps:
                off = k + shift - r
                acc = acc + wr[off:off + CONV_ROWS, :] * w_ref[k:k + 1, :]
        y = _layer_norm(acc, lg_ref[...], lb_ref[...])
        o_ref[0, pl.ds(t0, CONV_ROWS), :] = y * _sigmoid(y)
        return carry

    lax.fori_loop(0, s // CONV_ROWS, body, 0)


def _conv_mixer(za, w, b, lg, lb):
    bsz, s, _ = za.shape
    vec = pl.BlockSpec((1, GROUP_WIDTH), lambda i: (0, 0))
    return pl.pallas_call(
        _conv_kernel,
        grid=(bsz,),
        in_specs=[pl.BlockSpec((1, s, GROUP_WIDTH), lambda i: (i, 0, COL_CVAL)),
                  pl.BlockSpec((1, s, GROUP_WIDTH), lambda i: (i, 0, COL_CGATE)),
                  pl.BlockSpec((CONV_WIDTH, GROUP_WIDTH), lambda i: (0, 0)),
                  vec, vec, vec],
        out_specs=pl.BlockSpec((1, s, GROUP_WIDTH), lambda i: (i, 0, 0)),
        out_shape=jax.ShapeDtypeStruct((bsz, s, GROUP_WIDTH), F32),
        scratch_shapes=[pltpu.VMEM((CONV_PAD + s, GROUP_WIDTH), F32)],
        compiler_params=_cparams("parallel"),
        name="conv_mixer",
    )(za, za, w, b, lg, lb)


POOL_PAD = 16


def _pool_kernel(p_ref, w_ref, sc_ref, o_ref, pp_ref):
    s = p_ref.shape[1]
    pp_ref[0:POOL_PAD, :] = jnp.zeros((POOL_PAD, GROUP_WIDTH), F32)
    pp_ref[POOL_PAD:POOL_PAD + s, :] = p_ref[0]
    lane = lax.broadcasted_iota(jnp.int32, (1, GROUP_WIDTH), 1)
    grp = lane // POOL_GROUP
    win = jnp.where(grp == 0, POOL_WINDOWS[0],
                    jnp.where(grp == 1, POOL_WINDOWS[1],
                              jnp.where(grp == 2, POOL_WINDOWS[2], POOL_WINDOWS[3])))

    def body(i, carry):
        t0 = pl.multiple_of(i * POOL_ROWS, POOL_ROWS)
        rows = pp_ref[pl.ds(t0, POOL_ROWS + POOL_PAD), :]
        p = rows[POOL_PAD:POOL_PAD + POOL_ROWS, :]
        acc = p
        sums = []
        back = 1
        for w in POOL_WINDOWS:
            while back < w:
                acc = acc + rows[POOL_PAD - back:POOL_PAD - back + POOL_ROWS, :]
                back += 1
            sums.append(acc)
        total = jnp.where(grp == 0, sums[0],
                          jnp.where(grp == 1, sums[1],
                                    jnp.where(grp == 2, sums[2], sums[3])))
        t = t0 + lax.broadcasted_iota(jnp.int32, (POOL_ROWS, 1), 0)
        count = jnp.minimum(t + 1, win).astype(F32)
        mixed = total / count - p
        y = jnp.dot(mixed.astype(BF16), w_ref[...], preferred_element_type=F32)
        o_ref[0, pl.ds(t0, POOL_ROWS), :] = y * sc_ref[...]
        return carry

    lax.fori_loop(0, s // POOL_ROWS, body, 0)


def _pool_mixer(za, w_bd, scale):
    bsz, s, _ = za.shape
    return pl.pallas_call(
        _pool_kernel,
        grid=(bsz,),
        in_specs=[pl.BlockSpec((1, s, GROUP_WIDTH), lambda i: (i, 0, COL_POOL)),
                  pl.BlockSpec((GROUP_WIDTH, GROUP_WIDTH), lambda i: (0, 0)),
                  pl.BlockSpec((1, GROUP_WIDTH), lambda i: (0, 0))],
        out_specs=pl.BlockSpec((1, s, GROUP_WIDTH), lambda i: (i, 0, 0)),
        out_shape=jax.ShapeDtypeStruct((bsz, s, GROUP_WIDTH), F32),
        scratch_shapes=[pltpu.VMEM((POOL_PAD + s, GROUP_WIDTH), F32)],
        compiler_params=_cparams("parallel"),
        name="pool_mixer",
    )(za, w_bd, scale)


def _sgu_kernel(u_ref, v_ref, lg_ref, lb_ref, w_ref, bias_ref, o_ref):
    s = u_ref.shape[1]
    row = lax.broadcasted_iota(jnp.int32, (SGU_CHUNK, SGU_CHUNK), 0)
    col = lax.broadcasted_iota(jnp.int32, (SGU_CHUNK, SGU_CHUNK), 1)
    head = lax.broadcasted_iota(jnp.int32, (1, GROUP_WIDTH), 1) // (GROUP_WIDTH // SGU_HEADS)

    def body(c, carry):
        t0 = pl.multiple_of(c * SGU_CHUNK, SGU_CHUNK)
        v = _layer_norm(_gelu(v_ref[0, pl.ds(t0, SGU_CHUNK), :]), lg_ref[...], lb_ref[...])
        vb = v.astype(BF16)
        mixed = jnp.zeros((SGU_CHUNK, GROUP_WIDTH), F32)
        for h in range(SGU_HEADS):
            w = jnp.where(row >= col, w_ref[h], 0.0).astype(BF16)
            r = jnp.dot(w, vb, preferred_element_type=F32)
            mixed = jnp.where(head == h, r, mixed)
        u = _gelu(u_ref[0, pl.ds(t0, SGU_CHUNK), :])
        o_ref[0, pl.ds(t0, SGU_CHUNK), :] = u * (mixed + bias_ref[...])
        return carry

    lax.fori_loop(0, s // SGU_CHUNK, body, 0)


def _sgu_mixer(za, lg, lb, w, bias):
    bsz, s, _ = za.shape
    vec = pl.BlockSpec((1, GROUP_WIDTH), lambda i: (0, 0))
    return pl.pallas_call(
        _sgu_kernel,
        grid=(bsz,),
        in_specs=[pl.BlockSpec((1, s, GROUP_WIDTH), lambda i: (i, 0, COL_SGU_U)),
                  pl.BlockSpec((1, s, GROUP_WIDTH), lambda i: (i, 0, COL_SGU_V)),
                  vec, vec,
                  pl.BlockSpec((SGU_HEADS, SGU_CHUNK, SGU_CHUNK), lambda i: (0, 0, 0)),
                  pl.BlockSpec((SGU_CHUNK, GROUP_WIDTH), lambda i: (0, 0))],
        out_specs=pl.BlockSpec((1, s, GROUP_WIDTH), lambda i: (i, 0, 0)),
        out_shape=jax.ShapeDtypeStruct((bsz, s, GROUP_WIDTH), F32),
        compiler_params=_cparams("parallel"),
        name="sgu_mixer",
    )(za, za, lg, lb, w, bias)


def _att_prep_kernel(kcvc_ref, ksx_ref, kwx_ref, zb_ref, cs_ref, cscmp_ref, cos_ref, sin_ref,
                     wa_ref, wb_ref, pe_ref, w1_ref, w2k_ref, w2vt_ref,
                     q_ref, ks_ref, kw_ref, vs_ref, vw_ref, kc_ref, vct_ref, g_ref, sh_ref):
    s = ksx_ref.shape[1]
    n_blk = s // CMP_STRIDE

    def rope_tok(x, cs):
        r = x * cs
        return (r + pltpu.roll(r, HEAD_DIM, axis=1))[:, :HEAD_DIM]

    cs = cs_ref[0]
    ks_ref[0] = rope_tok(ksx_ref[0], cs).astype(BF16)
    kw_ref[0] = rope_tok(kwx_ref[0], cs).astype(BF16)

    cos = cos_ref[0]
    sin = sin_ref[0]
    scale = HEAD_DIM ** -0.5
    parts = []
    for h in range(ATT_HEADS):
        r0 = ROW_Q + h * HEAD_DIM
        x1 = zb_ref[0, r0:r0 + ROPE_HALF, :]
        x2 = zb_ref[0, r0 + ROPE_HALF:r0 + ROPE_DIM, :]
        parts += [x1 * cos - x2 * sin, x2 * cos + x1 * sin, zb_ref[0, r0 + ROPE_DIM:r0 + HEAD_DIM, :]]
    q_ref[0] = (jnp.concatenate(parts, axis=0) * scale).astype(BF16)

    for c in range(s // ATT_K):
        vs_ref[0, c] = zb_ref[0, ROW_VS:ROW_VS + HEAD_DIM, c * ATT_K:(c + 1) * ATT_K].astype(BF16)
        vw_ref[0, c] = zb_ref[0, ROW_VW:ROW_VW + HEAD_DIM, c * ATT_K:(c + 1) * ATT_K].astype(BF16)
    g_ref[0] = _sigmoid(zb_ref[0, ROW_G:ROW_G + 16, :])

    acc_a = jnp.zeros((n_blk, LANES), F32)
    acc_b = jnp.zeros((n_blk, LANES), F32)
    for l in range(CMP_STRIDE):
        xl = kcvc_ref[0, pl.ds(l, n_blk, stride=CMP_STRIDE), :].astype(BF16)
        acc_a = acc_a + jnp.dot(xl, wa_ref[l], preferred_element_type=F32)
        acc_b = acc_b + jnp.dot(xl, wb_ref[l], preferred_element_type=F32)
    sh_ref[0:n_blk, :] = acc_b
    sh_ref[n_blk:n_blk + SUBLANES, :] = jnp.zeros((SUBLANES, LANES), F32)
    pe_term = jnp.sum(pe_ref[...] * w1_ref[...], axis=0, keepdims=True)
    hid = _gelu(acc_a + sh_ref[1:n_blk + 1, :] + pe_term).astype(BF16)
    kk = jnp.dot(hid, w2k_ref[...], preferred_element_type=F32)
    kc_ref[0] = rope_tok(kk, cscmp_ref[0]).astype(BF16)
    vct_ref[0] = lax.dot_general(w2vt_ref[...], hid, (((1,), (1,)), ((), ())),
                                 preferred_element_type=F32).astype(BF16)


def _att_prep(za, zb, cs_tok, cs_cmp, cos_t, sin_t, wa, wb, pe, w1, w2k, w2vt):
    bsz, s, _ = za.shape
    n_blk = s // CMP_STRIDE
    nt = s // ATT_K
    tok = lambda c: pl.BlockSpec((1, s, LANES), lambda i: (i, 0, c))
    full3 = lambda shp: pl.BlockSpec(shp, lambda i: (0, 0, 0))
    full2 = lambda shp: pl.BlockSpec(shp, lambda i: (0, 0))
    per_b3 = lambda shp: pl.BlockSpec((1,) + shp, lambda i: (i, 0, 0))
    per_b4 = lambda shp: pl.BlockSpec((1,) + shp, lambda i: (i, 0, 0, 0))
    return pl.pallas_call(
        _att_prep_kernel,
        grid=(bsz,),
        in_specs=[tok(COL_KCVC), tok(COL_KS), tok(COL_KW),
                  per_b3((ZB_ROWS, s)), per_b3((s, LANES)), per_b3((n_blk, LANES)),
                  per_b3((ROPE_HALF, s)), per_b3((ROPE_HALF, s)),
                  full3((CMP_STRIDE, LANES, LANES)), full3((CMP_STRIDE, LANES, LANES)),
                  full2((CMP_BLOCK * HEAD_DIM, LANES)), full2((CMP_BLOCK * HEAD_DIM, LANES)),
                  full2((LANES, LANES)), full2((HEAD_DIM, LANES))],
        out_specs=[per_b3((ATT_HEADS * HEAD_DIM, s)), per_b3((s, HEAD_DIM)), per_b3((s, HEAD_DIM)),
                   per_b4((nt, HEAD_DIM, ATT_K)), per_b4((nt, HEAD_DIM, ATT_K)),
                   per_b3((n_blk, HEAD_DIM)), per_b3((HEAD_DIM, n_blk)), per_b3((16, s))],
        out_shape=[jax.ShapeDtypeStruct((bsz, ATT_HEADS * HEAD_DIM, s), BF16),
                   jax.ShapeDtypeStruct((bsz, s, HEAD_DIM), BF16),
                   jax.ShapeDtypeStruct((bsz, s, HEAD_DIM), BF16),
                   jax.ShapeDtypeStruct((bsz, nt, HEAD_DIM, ATT_K), BF16),
                   jax.ShapeDtypeStruct((bsz, nt, HEAD_DIM, ATT_K), BF16),
                   jax.ShapeDtypeStruct((bsz, n_blk, HEAD_DIM), BF16),
                   jax.ShapeDtypeStruct((bsz, HEAD_DIM, n_blk), BF16),
                   jax.ShapeDtypeStruct((bsz, 16, s), F32)],
        scratch_shapes=[pltpu.VMEM((n_blk + SUBLANES, LANES), F32)],
        compiler_params=_cparams("parallel"),
        name="att_prep",
    )(za, za, za, zb, cs_tok, cs_cmp, cos_t, sin_t, wa, wb, pe, w1, w2k, w2vt)


def _att_kernel(q_ref, ks_ref, kw_ref, vs_ref, vw_ref, kc_ref, vct_ref, g_ref, ov_ref,
                o_ref, sel_ref):
    n_blk = kc_ref.shape[1]
    n_slc = ov_ref.shape[0]
    tq = ATT_Q
    wide = ATT_HEADS * tq
    i = pl.program_id(1)
    t0 = i * tq

    q = q_ref[0]
    qs = jnp.concatenate([q[h * HEAD_DIM:(h + 1) * HEAD_DIM] for h in range(ATT_HEADS)], axis=1)
    lane_w = lax.broadcasted_iota(jnp.int32, (1, wide), 1)
    t_w = t0 + (lane_w & (tq - 1))
    t_q = t0 + lax.broadcasted_iota(jnp.int32, (1, tq), 1)

    s_c = jnp.dot(kc_ref[0], qs, preferred_element_type=F32)
    blk_end = lax.broadcasted_iota(jnp.int32, (n_blk, 1), 0) * CMP_STRIDE + (CMP_BLOCK - 1)
    cmask = blk_end <= t_w
    s_m = jnp.where(cmask, s_c, NEG_INF)
    e = jnp.exp(s_m - jnp.max(s_m, axis=0, keepdims=True))
    p_c = jnp.where(cmask, e / jnp.sum(e, axis=0, keepdims=True), 0.0)
    o_cmp = jnp.dot(vct_ref[0], p_c.astype(BF16), preferred_element_type=F32)

    p_sum = p_c[:, 0:tq]
    for h in range(1, ATT_HEADS):
        p_sum = p_sum + p_c[:, h * tq:(h + 1) * tq]
    p_hi = p_sum.astype(BF16)
    p_lo = (p_sum - p_hi.astype(F32)).astype(BF16)
    imp = (jnp.dot(ov_ref[...], p_hi, preferred_element_type=F32)
           + jnp.dot(ov_ref[...], p_lo, preferred_element_type=F32))
    j = lax.broadcasted_iota(jnp.int32, (n_slc, 1), 0)
    back = t_q // SLC_BLOCK - j
    forced = (j == 0) | ((back >= 0) & (back < N_LOCAL))
    imp = jnp.where(forced, FORCE_SCORE, jnp.where(back < 0, -1.0, imp))
    rank = jnp.zeros((n_slc, tq), F32)
    for r in range(n_slc):
        row = imp[r:r + 1, :]
        ahead = (row > imp) | ((row == imp) & (j > r))
        rank = rank + jnp.where(ahead, 1.0, 0.0)
    sel = jnp.where(rank < min(N_SELECT, n_slc), 1.0, 0.0)
    per_tile = ATT_K // SLC_BLOCK
    for kt in range(n_slc // per_tile):
        sel_ref[kt, 0:per_tile, :] = sel[kt * per_tile:(kt + 1) * per_tile, :]

    k_iota = lax.broadcasted_iota(jnp.int32, (ATT_K, 1), 0)

    def attend(k_tile, v_tile, mask, carry):
        m, l, acc = carry
        s_t = jnp.dot(k_tile, qs, preferred_element_type=F32)
        s_t = jnp.where(mask, s_t, -jnp.inf)
        m_new = jnp.maximum(m, jnp.max(s_t, axis=0, keepdims=True))
        alpha = jnp.exp(m - m_new)
        p = jnp.exp(s_t - m_new)
        l = alpha * l + jnp.sum(p, axis=0, keepdims=True)
        acc = alpha * acc + jnp.dot(v_tile, p.astype(BF16), preferred_element_type=F32)
        return m_new, l, acc

    init = (jnp.full((1, wide), NEG_INF, F32), jnp.zeros((1, wide), F32),
            jnp.zeros((HEAD_DIM, wide), F32))

    def slc_body(kt, carry):
        k0 = pl.multiple_of(kt * ATT_K, ATT_K)
        st = sel_ref[kt, 0:per_tile, :]
        rows = [jnp.broadcast_to(st[b:b + 1, :], (SLC_BLOCK, tq)) for b in range(per_tile)]
        chosen = jnp.concatenate(rows, axis=0)
        chosen = jnp.concatenate([chosen] * ATT_HEADS, axis=1)
        mask = (chosen > 0.5) & ((k0 + k_iota) <= t_w)
        return attend(ks_ref[0, pl.ds(k0, ATT_K), :], vs_ref[0, kt], mask, carry)

    _, l_s, acc_s = lax.fori_loop(0, i + 1, slc_body, init)

    def win_body(kt, carry):
        k0 = pl.multiple_of(kt * ATT_K, ATT_K)
        diff = t_w - (k0 + k_iota)
        mask = (diff >= 0) & (diff < WINDOW)
        return attend(kw_ref[0, pl.ds(k0, ATT_K), :], vw_ref[0, kt], mask, carry)

    first = jnp.maximum(i - WINDOW // ATT_K, 0)
    _, l_w, acc_w = lax.fori_loop(first, i + 1, win_body, init)

    g = g_ref[0]
    gate = lambda br: jnp.concatenate(
        [g[h * N_BRANCH + br:h * N_BRANCH + br + 1, :] for h in range(ATT_HEADS)], axis=1)
    o = gate(0) * o_cmp + gate(1) * (acc_s / l_s) + gate(2) * (acc_w / l_w)
    o = jnp.concatenate([o[:, h * tq:(h + 1) * tq] for h in range(ATT_HEADS)], axis=0)
    o_ref[0] = o.T


def _att(q_t, ks, kw, vs_t, vw_t, kc, vc_t, g_t, ov_t):
    bsz, _, s = q_t.shape
    n_blk = kc.shape[1]
    nt = s // ATT_K
    n_slc = s // SLC_BLOCK
    per_b3 = lambda shp: pl.BlockSpec((1,) + shp, lambda b, i: (b, 0, 0))
    per_b4 = lambda shp: pl.BlockSpec((1,) + shp, lambda b, i: (b, 0, 0, 0))
    return pl.pallas_call(
        _att_kernel,
        grid=(bsz, s // ATT_Q),
        in_specs=[pl.BlockSpec((1, ATT_HEADS * HEAD_DIM, ATT_Q), lambda b, i: (b, 0, i)),
                  per_b3((s, HEAD_DIM)), per_b3((s, HEAD_DIM)),
                  per_b4((nt, HEAD_DIM, ATT_K)), per_b4((nt, HEAD_DIM, ATT_K)),
                  per_b3((n_blk, HEAD_DIM)), per_b3((HEAD_DIM, n_blk)),
                  pl.BlockSpec((1, 16, ATT_Q), lambda b, i: (b, 0, i)),
                  pl.BlockSpec((n_slc, n_blk), lambda b, i: (0, 0))],
        out_specs=pl.BlockSpec((1, ATT_Q, GROUP_WIDTH), lambda b, i: (b, i, 0)),
        out_shape=jax.ShapeDtypeStruct((bsz, s, GROUP_WIDTH), F32),
        scratch_shapes=[pltpu.VMEM((nt, SUBLANES, ATT_Q), F32)],
        compiler_params=_cparams("parallel", "arbitrary"),
        name="sparse_attention",
    )(q_t, ks, kw, vs_t, vw_t, kc, vc_t, g_t, ov_t)


def _out_kernel(x_ref, y0_ref, y1_ref, y2_ref, y3_ref, w_ref, g_ref, o_ref):
    acc = None
    for n, y_ref in enumerate((y0_ref, y1_ref, y2_ref, y3_ref)):
        part = jnp.dot(y_ref[...].astype(BF16), w_ref[n * GROUP_WIDTH:(n + 1) * GROUP_WIDTH, :],
                       preferred_element_type=F32)
        acc = part if acc is None else acc + part
    o_ref[...] = x_ref[...] + _rms_norm(acc, g_ref[...])


def _out_proj(x2, ys, w, g):
    t, d = x2.shape
    tm = min(OUT_ROWS, t)
    yspec = pl.BlockSpec((tm, GROUP_WIDTH), lambda i: (i, 0))
    return pl.pallas_call(
        _out_kernel,
        grid=(t // tm,),
        in_specs=[pl.BlockSpec((tm, d), lambda i: (i, 0)), yspec, yspec, yspec, yspec,
                  pl.BlockSpec((d, d), lambda i: (0, 0)),
                  pl.BlockSpec((1, d), lambda i: (0, 0))],
        out_specs=pl.BlockSpec((tm, d), lambda i: (i, 0)),
        out_shape=jax.ShapeDtypeStruct((t, d), F32),
        compiler_params=_cparams("parallel"),
        name="out_proj",
    )(x2, *ys, w, g)


def _ffn_kernel(x_ref, g1_ref, w1_ref, w2_ref, g2_ref, o_ref, h_ref, acc_ref):
    k = pl.program_id(1)

    @pl.when(k == 0)
    def _():
        h_ref[...] = _rms_norm(x_ref[...], g1_ref[...]).astype(BF16)
        acc_ref[...] = jnp.zeros_like(acc_ref)

    f = jnp.maximum(jnp.dot(h_ref[...], w1_ref[...], preferred_element_type=F32), 0.0)
    acc_ref[...] += jnp.dot((f * f).astype(BF16), w2_ref[...], preferred_element_type=F32)

    @pl.when(k == pl.num_programs(1) - 1)
    def _():
        o_ref[...] = x_ref[...] + _rms_norm(acc_ref[...], g2_ref[...])


def _ffn(x2, g1, w1, w2, g2):
    t, d = x2.shape
    dff = w1.shape[1]
    tm = min(FFN_ROWS, t)
    tf = FFN_COLS
    return pl.pallas_call(
        _ffn_kernel,
        grid=(t // tm, dff // tf),
        in_specs=[pl.BlockSpec((tm, d), lambda i, k: (i, 0)),
                  pl.BlockSpec((1, d), lambda i, k: (0, 0)),
                  pl.BlockSpec((d, tf), lambda i, k: (0, k)),
                  pl.BlockSpec((tf, d), lambda i, k: (k, 0)),
                  pl.BlockSpec((1, d), lambda i, k: (0, 0))],
        out_specs=pl.BlockSpec((tm, d), lambda i, k: (i, 0)),
        out_shape=jax.ShapeDtypeStruct((t, d), F32),
        scratch_shapes=[pltpu.VMEM((tm, d), BF16), pltpu.VMEM((tm, d), F32)],
        compiler_params=_cparams("parallel", "arbitrary"),
        name="ffn",
    )(x2, g1, w1, w2, g2)


def _rope_perm():
    idx = list(range(HEAD_DIM))
    for c in range(ROPE_HALF):
        idx[c], idx[c + ROPE_HALF] = c + ROPE_HALF, c
    return jnp.array(idx, jnp.int32)


def _split_w_in(w_in):
    gw = GROUP_WIDTH
    edges = [0, gw, 2 * gw, 3 * gw, 4 * gw, 5 * gw, 6 * gw]
    for _ in range(6):
        edges.append(edges[-1] + HEAD_DIM)
    edges.append(edges[-1] + ATT_HEADS * N_BRANCH)
    names = ("cval", "cgate", "pool", "su", "sv", "q", "kc", "vc", "ks", "vs", "kw", "vw", "g")
    return {n: w_in[:, edges[k]:edges[k + 1]] for k, n in enumerate(names)}


def _layer_weights(p):
    perm = _rope_perm()
    w = _split_w_in(p["w_in"])
    wa = jnp.concatenate([w["cval"], w["cgate"], w["pool"], w["su"], w["sv"], w["kc"], w["vc"],
                          w["ks"], w["ks"][:, perm], w["kw"], w["kw"][:, perm]], axis=1).astype(BF16)
    pad = jnp.zeros((D_MODEL, ZB_ROWS - ROW_G - ATT_HEADS * N_BRANCH), F32)
    wb = jnp.concatenate([w["q"], w["vs"], w["vw"], w["g"], pad], axis=1).T.astype(BF16)

    pool_bd = jnp.zeros((GROUP_WIDTH, GROUP_WIDTH), F32)
    for gi in range(len(POOL_WINDOWS)):
        sl = slice(gi * POOL_GROUP, (gi + 1) * POOL_GROUP)
        pool_bd = pool_bd.at[sl, sl].set(p["pool_w"][gi])
    sgu_bias = jnp.repeat(jnp.transpose(p["sgu_b"]), GROUP_WIDTH // SGU_HEADS, axis=1)

    w1k = p["cmp_k_w1"].reshape(CMP_BLOCK, HEAD_DIM, HEAD_DIM)
    w1v = p["cmp_v_w1"].reshape(CMP_BLOCK, HEAD_DIM, HEAD_DIM)
    z = jnp.zeros((CMP_STRIDE, HEAD_DIM, HEAD_DIM), F32)

    def kv_diag(a, b):
        return jnp.concatenate([jnp.concatenate([a, z], axis=2), jnp.concatenate([z, b], axis=2)], axis=1)

    cmp_wa = kv_diag(w1k[:CMP_STRIDE], w1v[:CMP_STRIDE]).astype(BF16)
    cmp_wb = kv_diag(w1k[CMP_STRIDE:], w1v[CMP_STRIDE:]).astype(BF16)
    ones = jnp.ones((1, HEAD_DIM), F32)
    cmp_pe = jnp.concatenate([p["cmp_k_pe"].reshape(-1, 1) * ones, p["cmp_v_pe"].reshape(-1, 1) * ones], axis=1)
    cmp_w1 = jnp.concatenate([p["cmp_k_w1"], p["cmp_v_w1"]], axis=1)
    zk = jnp.zeros((HEAD_DIM, LANES), F32)
    cmp_w2k = jnp.concatenate([jnp.concatenate([p["cmp_k_w2"], p["cmp_k_w2"][:, perm]], axis=1), zk], axis=0).astype(BF16)
    cmp_w2vt = jnp.concatenate([jnp.zeros((HEAD_DIM, HEAD_DIM), F32), p["cmp_v_w2"].T], axis=1).astype(BF16)

    row = lambda v: v.reshape(1, -1)
    return dict(
        wa=wa, wb=wb, pre_mix=row(p["pre_mix_norm"]), post_mix=row(p["post_mix_norm"]),
        pre_ffn=row(p["pre_ffn_norm"]), post_ffn=row(p["post_ffn_norm"]),
        conv_w=p["conv_w"], conv_b=row(p["conv_b"]), conv_lg=row(p["conv_ln_g"]), conv_lb=row(p["conv_ln_b"]),
        pool_bd=pool_bd.astype(BF16), pool_scale=row(p["pool_scale"]),
        sgu_lg=row(p["sgu_ln_g"]), sgu_lb=row(p["sgu_ln_b"]), sgu_w=p["sgu_w"], sgu_bias=sgu_bias,
        cmp_wa=cmp_wa, cmp_wb=cmp_wb, cmp_pe=cmp_pe, cmp_w1=cmp_w1, cmp_w2k=cmp_w2k, cmp_w2vt=cmp_w2vt,
        w_out=p["w_out"].astype(BF16), ffn_w1=p["ffn_w1"].astype(BF16), ffn_w2=p["ffn_w2"].astype(BF16),
    )


def _overlap_t(s):
    n_blk = s // CMP_STRIDE
    n_slc = s // SLC_BLOCK
    bs = jnp.arange(n_blk)[None, :] * CMP_STRIDE
    ss = jnp.arange(n_slc)[:, None] * SLC_BLOCK
    ov = jnp.clip(jnp.minimum(bs + CMP_BLOCK, ss + SLC_BLOCK) - jnp.maximum(bs, ss), 0)
    return (ov.astype(F32) / CMP_STRIDE).astype(BF16)


def _rope_tables(positions):
    bsz, s = positions.shape
    inv = ROPE_THETA ** (-jnp.arange(ROPE_HALF, dtype=F32) * 2.0 / ROPE_DIM)
    inv64 = jnp.concatenate([inv, inv, jnp.zeros((HEAD_DIM - ROPE_DIM,), F32)])
    inv128 = jnp.concatenate([inv64, inv64]).reshape(1, LANES)
    sgn64 = jnp.concatenate([-jnp.ones((ROPE_HALF,), F32), jnp.ones((ROPE_HALF,), F32),
                             jnp.zeros((HEAD_DIM - ROPE_DIM,), F32)])
    sgn128 = jnp.concatenate([sgn64, sgn64]).reshape(1, LANES)
    posf = positions.astype(F32)
    n_blk = s // CMP_STRIDE
    pos_end = posf[:, CMP_BLOCK - 1::CMP_STRIDE]
    pos_end = jnp.pad(pos_end, ((0, 0), (0, n_blk - pos_end.shape[1])))
    cs_tok = _rope_tok_table(posf[:, :, None], inv128, sgn128)
    cs_cmp = _rope_tok_table(pos_end[:, :, None], inv128, sgn128)
    cos_t, sin_t = _rope_chan_table(posf[:, None, :], inv.reshape(ROPE_HALF, 1))
    return cs_tok, cs_cmp, cos_t, sin_t


def _mixers(x, lw, tables, ov_t):
    cs_tok, cs_cmp, cos_t, sin_t = tables
    za, zb = _proj(x, lw["pre_mix"], lw["wa"], lw["wb"])
    y_conv = _conv_mixer(za, lw["conv_w"], lw["conv_b"], lw["conv_lg"], lw["conv_lb"])
    y_pool = _pool_mixer(za, lw["pool_bd"], lw["pool_scale"])
    y_sgu = _sgu_mixer(za, lw["sgu_lg"], lw["sgu_lb"], lw["sgu_w"], lw["sgu_bias"])
    prep = _att_prep(za, zb, cs_tok, cs_cmp, cos_t, sin_t, lw["cmp_wa"], lw["cmp_wb"],
                     lw["cmp_pe"], lw["cmp_w1"], lw["cmp_w2k"], lw["cmp_w2vt"])
    y_att = _att(*prep, ov_t)
    return y_conv, y_pool, y_sgu, y_att


def kernel(x, positions, pre_mix_norm, post_mix_norm, pre_ffn_norm, post_ffn_norm, w_in, conv_w, conv_b, conv_ln_g, conv_ln_b, pool_w, pool_scale, sgu_ln_g, sgu_ln_b, sgu_w, sgu_b, cmp_k_pe, cmp_k_w1, cmp_k_w2, cmp_v_pe, cmp_v_w1, cmp_v_w2, w_out, ffn_w1, ffn_w2):
    params = dict(pre_mix_norm=pre_mix_norm, post_mix_norm=post_mix_norm, pre_ffn_norm=pre_ffn_norm,
                  post_ffn_norm=post_ffn_norm, w_in=w_in, conv_w=conv_w, conv_b=conv_b,
                  conv_ln_g=conv_ln_g, conv_ln_b=conv_ln_b, pool_w=pool_w, pool_scale=pool_scale,
                  sgu_ln_g=sgu_ln_g, sgu_ln_b=sgu_ln_b, sgu_w=sgu_w, sgu_b=sgu_b,
                  cmp_k_pe=cmp_k_pe, cmp_k_w1=cmp_k_w1, cmp_k_w2=cmp_k_w2, cmp_v_pe=cmp_v_pe,
                  cmp_v_w1=cmp_v_w1, cmp_v_w2=cmp_v_w2, w_out=w_out, ffn_w1=ffn_w1, ffn_w2=ffn_w2)
    bsz, s, d = x.shape
    depth = w_in.shape[0]
    tables = _rope_tables(positions)
    ov_t = _overlap_t(s)
    for l in range(depth):
        lw = _layer_weights({k: v[l] for k, v in params.items()})
        ys = _mixers(x, lw, tables, ov_t)
        x2 = _out_proj(x.reshape(bsz * s, d), [y.reshape(bsz * s, GROUP_WIDTH) for y in ys],
                       lw["w_out"], lw["post_mix"])
        x2 = _ffn(x2, lw["pre_ffn"], lw["ffn_w1"], lw["ffn_w2"], lw["post_ffn"])
        x = x2.reshape(bsz, s, d)
    return x
```

```python
import functools

import jax
import jax.numpy as jnp
from jax import lax
from jax.experimental import pallas as pl
from jax.experimental.pallas import tpu as pltpu

F32 = jnp.float32
BF16 = jnp.bfloat16

D_MODEL = 1024
GROUP_WIDTH = 256
CONV_WIDTH = 31
POOL_WINDOWS = (2, 4, 8, 16)
POOL_GROUP = 64
SGU_HEADS = 4
SGU_CHUNK = 128
ATT_HEADS = 4
HEAD_DIM = 64
ROPE_DIM = 16
ROPE_HALF = 8
ROPE_THETA = 500000.0
CMP_BLOCK = 32
CMP_STRIDE = 16
SLC_BLOCK = 64
N_SELECT = 8
N_LOCAL = 2
WINDOW = 512
N_BRANCH = 3
D_FF = 4096
NORM_EPS = 1e-6
NEG_INF = -1e30
FORCE_SCORE = 1e9

LANES = 128
SUBLANES = 8
VMEM_LIMIT_BYTES = 56 * 1024 * 1024

PROJ_ROWS = 512
OUT_ROWS = 512
FFN_ROWS = 1024
FFN_COLS = 1024
CONV_ROWS = 64
POOL_ROWS = 64
ATT_Q = 256
ATT_K = 256
LOG2_E = 1.4426950408889634

COL_CVAL, COL_CGATE, COL_POOL, COL_SGU_U, COL_SGU_V = 0, 1, 2, 3, 4
COL_KCVC, COL_KS, COL_KW = 10, 11, 12
ZA_WIDTH = 1664
ROW_Q, ROW_VS, ROW_VW, ROW_G = 0, 256, 320, 384
ZB_ROWS = 400


def _cparams(*sem):
    return pltpu.CompilerParams(dimension_semantics=sem, vmem_limit_bytes=VMEM_LIMIT_BYTES)


def _gelu(x):
    return 0.5 * x * (1.0 + jnp.tanh(0.7978845608028654 * (x + 0.044715 * (x * x * x))))


def _sigmoid(x):
    return 1.0 / (1.0 + jnp.exp(-x))


def _layer_norm(x, g, b):
    mu = jnp.mean(x, axis=-1, keepdims=True)
    d = x - mu
    var = jnp.mean(d * d, axis=-1, keepdims=True)
    return d * lax.rsqrt(var + NORM_EPS) * g + b


def _rms_norm(x, g):
    return x * lax.rsqrt(jnp.mean(x * x, axis=-1, keepdims=True) + NORM_EPS) * g


def _rope_tok_kernel(pos_ref, inv_ref, sgn_ref, cs_ref):
    ang = pos_ref[0] * inv_ref[...]
    lane = lax.broadcasted_iota(jnp.int32, ang.shape, 1)
    cs_ref[0] = jnp.where(lane < HEAD_DIM, jnp.cos(ang), jnp.sin(ang) * sgn_ref[...])


def _rope_tok_table(pos_col, inv128, sgn128):
    b, n, _ = pos_col.shape
    return pl.pallas_call(
        _rope_tok_kernel,
        grid=(b,),
        in_specs=[pl.BlockSpec((1, n, 1), lambda i: (i, 0, 0)),
                  pl.BlockSpec((1, LANES), lambda i: (0, 0)),
                  pl.BlockSpec((1, LANES), lambda i: (0, 0))],
        out_specs=pl.BlockSpec((1, n, LANES), lambda i: (i, 0, 0)),
        out_shape=jax.ShapeDtypeStruct((b, n, LANES), F32),
        compiler_params=_cparams("parallel"),
        name="rope_tok_table",
    )(pos_col, inv128, sgn128)


def _rope_chan_kernel(pos_ref, inv_ref, cos_ref, sin_ref):
    ang = inv_ref[...] * pos_ref[0]
    cos_ref[0] = jnp.cos(ang)
    sin_ref[0] = jnp.sin(ang)


def _rope_chan_table(pos_row, inv_col):
    b, _, s = pos_row.shape
    spec = pl.BlockSpec((1, ROPE_HALF, s), lambda i: (i, 0, 0))
    return pl.pallas_call(
        _rope_chan_kernel,
        grid=(b,),
        in_specs=[pl.BlockSpec((1, 1, s), lambda i: (i, 0, 0)),
                  pl.BlockSpec((ROPE_HALF, 1), lambda i: (0, 0))],
        out_specs=[spec, spec],
        out_shape=[jax.ShapeDtypeStruct((b, ROPE_HALF, s), F32)] * 2,
        compiler_params=_cparams("parallel"),
        name="rope_chan_table",
    )(pos_row, inv_col)


def _proj_kernel(x_ref, g_ref, wa_ref, wb_ref, za_ref, zb_ref):
    h = _rms_norm(x_ref[0], g_ref[...]).astype(BF16)
    za_ref[0] = jnp.dot(h, wa_ref[...], preferred_element_type=F32)
    zb_ref[0] = lax.dot_general(wb_ref[...], h, (((1,), (1,)), ((), ())),
                                preferred_element_type=F32)


def _proj(x, g, wa, wb):
    b, s, d = x.shape
    tm = min(PROJ_ROWS, s)
    return pl.pallas_call(
        _proj_kernel,
        grid=(b, s // tm),
        in_specs=[pl.BlockSpec((1, tm, d), lambda i, j: (i, j, 0)),
                  pl.BlockSpec((1, d), lambda i, j: (0, 0)),
                  pl.BlockSpec((d, ZA_WIDTH), lambda i, j: (0, 0)),
                  pl.BlockSpec((ZB_ROWS, d), lambda i, j: (0, 0))],
        out_specs=[pl.BlockSpec((1, tm, ZA_WIDTH), lambda i, j: (i, j, 0)),
                   pl.BlockSpec((1, ZB_ROWS, tm), lambda i, j: (i, 0, j))],
        out_shape=[jax.ShapeDtypeStruct((b, s, ZA_WIDTH), F32),
                   jax.ShapeDtypeStruct((b, ZB_ROWS, s), F32)],
        compiler_params=_cparams("parallel", "parallel"),
        name="in_proj",
    )(x, g, wa, wb)


CONV_PAD = 32


def _conv_kernel(a_ref, gate_ref, w_ref, b_ref, lg_ref, lb_ref, o_ref, hp_ref):
    s = a_ref.shape[1]
    hp_ref[0:CONV_PAD, :] = jnp.zeros((CONV_PAD, GROUP_WIDTH), F32)
    hp_ref[CONV_PAD:CONV_PAD + s, :] = a_ref[0] * _sigmoid(gate_ref[0])
    shift = CONV_PAD - (CONV_WIDTH - 1)

    def body(i, carry):
        t0 = pl.multiple_of(i * CONV_ROWS, CONV_ROWS)
        acc = jnp.zeros((CONV_ROWS, GROUP_WIDTH), F32) + b_ref[...]
        win = hp_ref[pl.ds(t0, CONV_ROWS + CONV_PAD), :]
        for r in range(SUBLANES):
            taps = [k for k in range(CONV_WIDTH) if (k + shift) % SUBLANES == r]
            span = max(k + shift - r for k in taps) + CONV_ROWS
            wr = win[r:r + span, :]
            for k in taps:
                off = k + shift - r
                acc = acc + wr[off:off + CONV_ROWS, :] * w_ref[k:k + 1, :]
        y = _layer_norm(acc, lg_ref[...], lb_ref[...])
        o_ref[0, pl.ds(t0, CONV_ROWS), :] = y * _sigmoid(y)
        return carry

    lax.fori_loop(0, s // CONV_ROWS, body, 0)


def _conv_mixer(za, w, b, lg, lb):
    bsz, s, _ = za.shape
    vec = pl.BlockSpec((1, GROUP_WIDTH), lambda i: (0, 0))
    return pl.pallas_call(
        _conv_kernel,
        grid=(bsz,),
        in_specs=[pl.BlockSpec((1, s, GROUP_WIDTH), lambda i: (i, 0, COL_CVAL)),
                  pl.BlockSpec((1, s, GROUP_WIDTH), lambda i: (i, 0, COL_CGATE)),
                  pl.BlockSpec((CONV_WIDTH, GROUP_WIDTH), lambda i: (0, 0)),
                  vec, vec, vec],
        out_specs=pl.BlockSpec((1, s, GROUP_WIDTH), lambda i: (i, 0, 0)),
        out_shape=jax.ShapeDtypeStruct((bsz, s, GROUP_WIDTH), F32),
        scratch_shapes=[pltpu.VMEM((CONV_PAD + s, GROUP_WIDTH), F32)],
        compiler_params=_cparams("parallel"),
        name="conv_mixer",
    )(za, za, w, b, lg, lb)


POOL_PAD = 16


def _pool_kernel(p_ref, w_ref, sc_ref, o_ref, pp_ref):
    s = p_ref.shape[1]
    pp_ref[0:POOL_PAD, :] = jnp.zeros((POOL_PAD, GROUP_WIDTH), F32)
    pp_ref[POOL_PAD:POOL_PAD + s, :] = p_ref[0]
    lane = lax.broadcasted_iota(jnp.int32, (1, GROUP_WIDTH), 1)
    grp = lane // POOL_GROUP
    win = jnp.where(grp == 0, POOL_WINDOWS[0],
                    jnp.where(grp == 1, POOL_WINDOWS[1],
                              jnp.where(grp == 2, POOL_WINDOWS[2], POOL_WINDOWS[3])))

    def body(i, carry):
        t0 = pl.multiple_of(i * POOL_ROWS, POOL_ROWS)
        rows = pp_ref[pl.ds(t0, POOL_ROWS + POOL_PAD), :]
        p = rows[POOL_PAD:POOL_PAD + POOL_ROWS, :]
        acc = p
        sums = []
        back = 1
        for w in POOL_WINDOWS:
            while back < w:
                acc = acc + rows[POOL_PAD - back:POOL_PAD - back + POOL_ROWS, :]
                back += 1
            sums.append(acc)
        total = jnp.where(grp == 0, sums[0],
                          jnp.where(grp == 1, sums[1],
                                    jnp.where(grp == 2, sums[2], sums[3])))
        t = t0 + lax.broadcasted_iota(jnp.int32, (POOL_ROWS, 1), 0)
        count = jnp.minimum(t + 1, win).astype(F32)
        mixed = total / count - p
        y = jnp.dot(mixed.astype(BF16), w_ref[...], preferred_element_type=F32)
        o_ref[0, pl.ds(t0, POOL_ROWS), :] = y * sc_ref[...]
        return carry

    lax.fori_loop(0, s // POOL_ROWS, body, 0)


def _pool_mixer(za, w_bd, scale):
    bsz, s, _ = za.shape
    return pl.pallas_call(
        _pool_kernel,
        grid=(bsz,),
        in_specs=[pl.BlockSpec((1, s, GROUP_WIDTH), lambda i: (i, 0, COL_POOL)),
                  pl.BlockSpec((GROUP_WIDTH, GROUP_WIDTH), lambda i: (0, 0)),
                  pl.BlockSpec((1, GROUP_WIDTH), lambda i: (0, 0))],
        out_specs=pl.BlockSpec((1, s, GROUP_WIDTH), lambda i: (i, 0, 0)),
        out_shape=jax.ShapeDtypeStruct((bsz, s, GROUP_WIDTH), F32),
        scratch_shapes=[pltpu.VMEM((POOL_PAD + s, GROUP_WIDTH), F32)],
        compiler_params=_cparams("parallel"),
        name="pool_mixer",
    )(za, w_bd, scale)


def _sgu_kernel(u_ref, v_ref, lg_ref, lb_ref, w_ref, bias_ref, o_ref):
    s = u_ref.shape[1]
    row = lax.broadcasted_iota(jnp.int32, (SGU_CHUNK, SGU_CHUNK), 0)
    col = lax.broadcasted_iota(jnp.int32, (SGU_CHUNK, SGU_CHUNK), 1)
    head = lax.broadcasted_iota(jnp.int32, (1, GROUP_WIDTH), 1) // (GROUP_WIDTH // SGU_HEADS)

    def body(c, carry):
        t0 = pl.multiple_of(c * SGU_CHUNK, SGU_CHUNK)
        v = _layer_norm(_gelu(v_ref[0, pl.ds(t0, SGU_CHUNK), :]), lg_ref[...], lb_ref[...])
        vb = v.astype(BF16)
        mixed = jnp.zeros((SGU_CHUNK, GROUP_WIDTH), F32)
        for h in range(SGU_HEADS):
            w = jnp.where(row >= col, w_ref[h], 0.0).astype(BF16)
            r = jnp.dot(w, vb, preferred_element_type=F32)
            mixed = jnp.where(head == h, r, mixed)
        u = _gelu(u_ref[0, pl.ds(t0, SGU_CHUNK), :])
        o_ref[0, pl.ds(t0, SGU_CHUNK), :] = u * (mixed + bias_ref[...])
        return carry

    lax.fori_loop(0, s // SGU_CHUNK, body, 0)


def _sgu_mixer(za, lg, lb, w, bias):
    bsz, s, _ = za.shape
    vec = pl.BlockSpec((1, GROUP_WIDTH), lambda i: (0, 0))
    return pl.pallas_call(
        _sgu_kernel,
        grid=(bsz,),
        in_specs=[pl.BlockSpec((1, s, GROUP_WIDTH), lambda i: (i, 0, COL_SGU_U)),
                  pl.BlockSpec((1, s, GROUP_WIDTH), lambda i: (i, 0, COL_SGU_V)),
                  vec, vec,
                  pl.BlockSpec((SGU_HEADS, SGU_CHUNK, SGU_CHUNK), lambda i: (0, 0, 0)),
                  pl.BlockSpec((SGU_CHUNK, GROUP_WIDTH), lambda i: (0, 0))],
        out_specs=pl.BlockSpec((1, s, GROUP_WIDTH), lambda i: (i, 0, 0)),
        out_shape=jax.ShapeDtypeStruct((bsz, s, GROUP_WIDTH), F32),
        compiler_params=_cparams("parallel"),
        name="sgu_mixer",
    )(za, za, lg, lb, w, bias)


def _att_prep_kernel(kcvc_ref, ksx_ref, kwx_ref, zb_ref, cs_ref, cscmp_ref, cos_ref, sin_ref,
                     wa_ref, wb_ref, pe_ref, w1_ref, w2k_ref, w2vt_ref,
                     q_ref, ks_ref, kw_ref, vs_ref, vw_ref, kc_ref, vct_ref, g_ref, sh_ref):
    s = ksx_ref.shape[1]
    n_blk = s // CMP_STRIDE

    def rope_tok(x, cs):
        r = x * cs
        return (r + pltpu.roll(r, HEAD_DIM, axis=1))[:, :HEAD_DIM]

    cs = cs_ref[0]
    ks_ref[0] = rope_tok(ksx_ref[0], cs).astype(BF16)
    kw_ref[0] = rope_tok(kwx_ref[0], cs).astype(BF16)

    cos = cos_ref[0]
    sin = sin_ref[0]
    scale = HEAD_DIM ** -0.5 * LOG2_E
    parts = []
    for h in range(ATT_HEADS):
        r0 = ROW_Q + h * HEAD_DIM
        x1 = zb_ref[0, r0:r0 + ROPE_HALF, :]
        x2 = zb_ref[0, r0 + ROPE_HALF:r0 + ROPE_DIM, :]
        parts += [x1 * cos - x2 * sin, x2 * cos + x1 * sin, zb_ref[0, r0 + ROPE_DIM:r0 + HEAD_DIM, :]]
    q_ref[0] = (jnp.concatenate(parts, axis=0) * scale).astype(BF16)

    for c in range(s // ATT_K):
        vs_ref[0, c] = zb_ref[0, ROW_VS:ROW_VS + HEAD_DIM, c * ATT_K:(c + 1) * ATT_K].astype(BF16)
        vw_ref[0, c] = zb_ref[0, ROW_VW:ROW_VW + HEAD_DIM, c * ATT_K:(c + 1) * ATT_K].astype(BF16)
    g_ref[0] = _sigmoid(zb_ref[0, ROW_G:ROW_G + 16, :])

    acc_a = jnp.zeros((n_blk, LANES), F32)
    acc_b = jnp.zeros((n_blk, LANES), F32)
    for l in range(CMP_STRIDE):
        xl = kcvc_ref[0, pl.ds(l, n_blk, stride=CMP_STRIDE), :].astype(BF16)
        acc_a = acc_a + jnp.dot(xl, wa_ref[l], preferred_element_type=F32)
        acc_b = acc_b + jnp.dot(xl, wb_ref[l], preferred_element_type=F32)
    sh_ref[0:n_blk, :] = acc_b
    sh_ref[n_blk:n_blk + SUBLANES, :] = jnp.zeros((SUBLANES, LANES), F32)
    pe_term = jnp.sum(pe_ref[...] * w1_ref[...], axis=0, keepdims=True)
    hid = _gelu(acc_a + sh_ref[1:n_blk + 1, :] + pe_term).astype(BF16)
    kk = jnp.dot(hid, w2k_ref[...], preferred_element_type=F32)
    kc_ref[0] = rope_tok(kk, cscmp_ref[0]).astype(BF16)
    vct_ref[0] = lax.dot_general(w2vt_ref[...], hid, (((1,), (1,)), ((), ())),
                                 preferred_element_type=F32).astype(BF16)


def _att_prep(za, zb, cs_tok, cs_cmp, cos_t, sin_t, wa, wb, pe, w1, w2k, w2vt):
    bsz, s, _ = za.shape
    n_blk = s // CMP_STRIDE
    nt = s // ATT_K
    tok = lambda c: pl.BlockSpec((1, s, LANES), lambda i: (i, 0, c))
    full3 = lambda shp: pl.BlockSpec(shp, lambda i: (0, 0, 0))
    full2 = lambda shp: pl.BlockSpec(shp, lambda i: (0, 0))
    per_b3 = lambda shp: pl.BlockSpec((1,) + shp, lambda i: (i, 0, 0))
    per_b4 = lambda shp: pl.BlockSpec((1,) + shp, lambda i: (i, 0, 0, 0))
    return pl.pallas_call(
        _att_prep_kernel,
        grid=(bsz,),
        in_specs=[tok(COL_KCVC), tok(COL_KS), tok(COL_KW),
                  per_b3((ZB_ROWS, s)), per_b3((s, LANES)), per_b3((n_blk, LANES)),
                  per_b3((ROPE_HALF, s)), per_b3((ROPE_HALF, s)),
                  full3((CMP_STRIDE, LANES, LANES)), full3((CMP_STRIDE, LANES, LANES)),
                  full2((CMP_BLOCK * HEAD_DIM, LANES)), full2((CMP_BLOCK * HEAD_DIM, LANES)),
                  full2((LANES, LANES)), full2((HEAD_DIM, LANES))],
        out_specs=[per_b3((ATT_HEADS * HEAD_DIM, s)), per_b3((s, HEAD_DIM)), per_b3((s, HEAD_DIM)),
                   per_b4((nt, HEAD_DIM, ATT_K)), per_b4((nt, HEAD_DIM, ATT_K)),
                   per_b3((n_blk, HEAD_DIM)), per_b3((HEAD_DIM, n_blk)), per_b3((16, s))],
        out_shape=[jax.ShapeDtypeStruct((bsz, ATT_HEADS * HEAD_DIM, s), BF16),
                   jax.ShapeDtypeStruct((bsz, s, HEAD_DIM), BF16),
                   jax.ShapeDtypeStruct((bsz, s, HEAD_DIM), BF16),
                   jax.ShapeDtypeStruct((bsz, nt, HEAD_DIM, ATT_K), BF16),
                   jax.ShapeDtypeStruct((bsz, nt, HEAD_DIM, ATT_K), BF16),
                   jax.ShapeDtypeStruct((bsz, n_blk, HEAD_DIM), BF16),
                   jax.ShapeDtypeStruct((bsz, HEAD_DIM, n_blk), BF16),
                   jax.ShapeDtypeStruct((bsz, 16, s), F32)],
        scratch_shapes=[pltpu.VMEM((n_blk + SUBLANES, LANES), F32)],
        compiler_params=_cparams("parallel"),
        name="att_prep",
    )(za, za, za, zb, cs_tok, cs_cmp, cos_t, sin_t, wa, wb, pe, w1, w2k, w2vt)


def _att_kernel(q_ref, ks_ref, kw_ref, vs_ref, vw_ref, kc_ref, vct_ref, g_ref, ov_ref,
                o_ref, sel_ref):
    n_blk = kc_ref.shape[1]
    n_slc = ov_ref.shape[0]
    tq = ATT_Q
    wide = ATT_HEADS * tq
    i = pl.program_id(1)
    t0 = i * tq

    q = q_ref[0]
    qs = jnp.concatenate([q[h * HEAD_DIM:(h + 1) * HEAD_DIM] for h in range(ATT_HEADS)], axis=1)
    lane_w = lax.broadcasted_iota(jnp.int32, (1, wide), 1)
    t_w = t0 + (lane_w & (tq - 1))
    t_q = t0 + lax.broadcasted_iota(jnp.int32, (1, tq), 1)
    heads = lambda a: jnp.concatenate([a] * ATT_HEADS, axis=1)

    s_c = jnp.dot(kc_ref[0], qs, preferred_element_type=F32)
    blk_end = lax.broadcasted_iota(jnp.int32, (n_blk, 1), 0) * CMP_STRIDE + (CMP_BLOCK - 1)
    cmask = blk_end <= t_w
    s_m = jnp.where(cmask, s_c, NEG_INF)
    e = jnp.exp2(s_m - jnp.max(s_m, axis=0, keepdims=True))
    p_c = jnp.where(cmask, e * (1.0 / jnp.sum(e, axis=0, keepdims=True)), 0.0)
    o_cmp = jnp.dot(vct_ref[0], p_c.astype(BF16), preferred_element_type=F32)

    p_sum = p_c[:, 0:tq]
    for h in range(1, ATT_HEADS):
        p_sum = p_sum + p_c[:, h * tq:(h + 1) * tq]
    p_hi = p_sum.astype(BF16)
    p_lo = (p_sum - p_hi.astype(F32)).astype(BF16)
    imp = (jnp.dot(ov_ref[...], p_hi, preferred_element_type=F32)
           + jnp.dot(ov_ref[...], p_lo, preferred_element_type=F32))
    j = lax.broadcasted_iota(jnp.int32, (n_slc, 1), 0)
    back = t_q // SLC_BLOCK - j
    forced = (j == 0) | ((back >= 0) & (back < N_LOCAL))
    imp = jnp.where(forced, FORCE_SCORE, jnp.where(back < 0, -1.0, imp))
    rank = jnp.zeros((n_slc, tq), F32)
    for r in range(n_slc):
        row = imp[r:r + 1, :]
        ahead = (row > imp) | ((row == imp) & (j > r))
        rank = rank + jnp.where(ahead, 1.0, 0.0)
    sel_bias = jnp.where(rank < min(N_SELECT, n_slc), 0.0, -jnp.inf)
    per_tile = ATT_K // SLC_BLOCK
    for kt in range(n_slc // per_tile):
        sel_ref[kt, 0:per_tile, :] = sel_bias[kt * per_tile:(kt + 1) * per_tile, :]

    k_iota = lax.broadcasted_iota(jnp.int32, (ATT_K, 1), 0)

    def block_bias(kt):
        st = sel_ref[kt, 0:per_tile, :]
        rows = [jnp.broadcast_to(st[b:b + 1, :], (SLC_BLOCK, tq)) for b in range(per_tile)]
        return jnp.concatenate(rows, axis=0)

    def attend(k_tile, v_tile, bias, carry):
        m, l, acc = carry
        s_t = jnp.dot(k_tile, qs, preferred_element_type=F32) + heads(bias)
        m_new = jnp.maximum(m, jnp.max(s_t, axis=0, keepdims=True))
        alpha = jnp.exp2(m - m_new)
        p = jnp.exp2(s_t - m_new)
        l = alpha * l + jnp.sum(p, axis=0, keepdims=True)
        acc = alpha * acc + jnp.dot(v_tile, p.astype(BF16), preferred_element_type=F32)
        return m_new, l, acc

    init = (jnp.full((1, wide), NEG_INF, F32), jnp.zeros((1, wide), F32),
            jnp.zeros((HEAD_DIM, wide), F32))

    def slc_body(kt, carry):
        k0 = pl.multiple_of(kt * ATT_K, ATT_K)
        return attend(ks_ref[0, pl.ds(k0, ATT_K), :], vs_ref[0, kt], block_bias(kt), carry)

    carry = lax.fori_loop(0, i, slc_body, init)
    kd = pl.multiple_of(i * ATT_K, ATT_K)
    diag_bias = jnp.where((kd + k_iota) <= t_q, block_bias(i), -jnp.inf)
    _, l_s, acc_s = attend(ks_ref[0, pl.ds(kd, ATT_K), :], vs_ref[0, i], diag_bias, carry)

    n_win = WINDOW // ATT_K + 1
    first = jnp.maximum(i - (n_win - 1), 0)
    kw0 = pl.multiple_of(first * ATT_K, ATT_K)
    s_w = jnp.dot(kw_ref[0, pl.ds(kw0, n_win * ATT_K), :], qs, preferred_element_type=F32)
    diff = t_q - (kw0 + lax.broadcasted_iota(jnp.int32, (n_win * ATT_K, 1), 0))
    s_w = s_w + heads(jnp.where((diff >= 0) & (diff < WINDOW), 0.0, -jnp.inf))
    p_w = jnp.exp2(s_w - jnp.max(s_w, axis=0, keepdims=True))
    l_w = jnp.sum(p_w, axis=0, keepdims=True)
    p_w = p_w.astype(BF16)
    acc_w = None
    for n in range(n_win):
        part = jnp.dot(vw_ref[0, first + n], p_w[n * ATT_K:(n + 1) * ATT_K], preferred_element_type=F32)
        acc_w = part if acc_w is None else acc_w + part

    g = g_ref[0]
    gate = lambda br: jnp.concatenate(
        [g[h * N_BRANCH + br:h * N_BRANCH + br + 1, :] for h in range(ATT_HEADS)], axis=1)
    o = gate(0) * o_cmp + (gate(1) * (1.0 / l_s)) * acc_s + (gate(2) * (1.0 / l_w)) * acc_w
    o = jnp.concatenate([o[:, h * tq:(h + 1) * tq] for h in range(ATT_HEADS)], axis=0)
    o_ref[0] = o.T


def _att(q_t, ks, kw, vs_t, vw_t, kc, vc_t, g_t, ov_t):
    bsz, _, s = q_t.shape
    n_blk = kc.shape[1]
    nt = s // ATT_K
    n_slc = s // SLC_BLOCK
    per_b3 = lambda shp: pl.BlockSpec((1,) + shp, lambda b, i: (b, 0, 0))
    per_b4 = lambda shp: pl.BlockSpec((1,) + shp, lambda b, i: (b, 0, 0, 0))
    return pl.pallas_call(
        _att_kernel,
        grid=(bsz, s // ATT_Q),
        in_specs=[pl.BlockSpec((1, ATT_HEADS * HEAD_DIM, ATT_Q), lambda b, i: (b, 0, i)),
                  per_b3((s, HEAD_DIM)), per_b3((s, HEAD_DIM)),
                  per_b4((nt, HEAD_DIM, ATT_K)), per_b4((nt, HEAD_DIM, ATT_K)),
                  per_b3((n_blk, HEAD_DIM)), per_b3((HEAD_DIM, n_blk)),
                  pl.BlockSpec((1, 16, ATT_Q), lambda b, i: (b, 0, i)),
                  pl.BlockSpec((n_slc, n_blk), lambda b, i: (0, 0))],
        out_specs=pl.BlockSpec((1, ATT_Q, GROUP_WIDTH), lambda b, i: (b, i, 0)),
        out_shape=jax.ShapeDtypeStruct((bsz, s, GROUP_WIDTH), F32),
        scratch_shapes=[pltpu.VMEM((nt, SUBLANES, ATT_Q), F32)],
        compiler_params=_cparams("parallel", "arbitrary"),
        name="sparse_attention",
    )(q_t, ks, kw, vs_t, vw_t, kc, vc_t, g_t, ov_t)


def _out_kernel(x_ref, y0_ref, y1_ref, y2_ref, y3_ref, w_ref, g_ref, o_ref):
    acc = None
    for n, y_ref in enumerate((y0_ref, y1_ref, y2_ref, y3_ref)):
        part = jnp.dot(y_ref[...].astype(BF16), w_ref[n * GROUP_WIDTH:(n + 1) * GROUP_WIDTH, :],
                       preferred_element_type=F32)
        acc = part if acc is None else acc + part
    o_ref[...] = x_ref[...] + _rms_norm(acc, g_ref[...])


def _out_proj(x2, ys, w, g):
    t, d = x2.shape
    tm = min(OUT_ROWS, t)
    yspec = pl.BlockSpec((tm, GROUP_WIDTH), lambda i: (i, 0))
    return pl.pallas_call(
        _out_kernel,
        grid=(t // tm,),
        in_specs=[pl.BlockSpec((tm, d), lambda i: (i, 0)), yspec, yspec, yspec, yspec,
                  pl.BlockSpec((d, d), lambda i: (0, 0)),
                  pl.BlockSpec((1, d), lambda i: (0, 0))],
        out_specs=pl.BlockSpec((tm, d), lambda i: (i, 0)),
        out_shape=jax.ShapeDtypeStruct((t, d), F32),
        compiler_params=_cparams("parallel"),
        name="out_proj",
    )(x2, *ys, w, g)


def _ffn_kernel(x_ref, g1_ref, w1_ref, w2_ref, g2_ref, o_ref, h_ref, acc_ref):
    k = pl.program_id(1)

    @pl.when(k == 0)
    def _():
        h_ref[...] = _rms_norm(x_ref[...], g1_ref[...]).astype(BF16)
        acc_ref[...] = jnp.zeros_like(acc_ref)

    f = jnp.maximum(jnp.dot(h_ref[...], w1_ref[...], preferred_element_type=F32), 0.0)
    acc_ref[...] += jnp.dot((f * f).astype(BF16), w2_ref[...], preferred_element_type=F32)

    @pl.when(k == pl.num_programs(1) - 1)
    def _():
        o_ref[...] = x_ref[...] + _rms_norm(acc_ref[...], g2_ref[...])


def _ffn(x2, g1, w1, w2, g2):
    t, d = x2.shape
    dff = w1.shape[1]
    tm = min(FFN_ROWS, t)
    tf = FFN_COLS
    return pl.pallas_call(
        _ffn_kernel,
        grid=(t // tm, dff // tf),
        in_specs=[pl.BlockSpec((tm, d), lambda i, k: (i, 0)),
                  pl.BlockSpec((1, d), lambda i, k: (0, 0)),
                  pl.BlockSpec((d, tf), lambda i, k: (0, k)),
                  pl.BlockSpec((tf, d), lambda i, k: (k, 0)),
                  pl.BlockSpec((1, d), lambda i, k: (0, 0))],
        out_specs=pl.BlockSpec((tm, d), lambda i, k: (i, 0)),
        out_shape=jax.ShapeDtypeStruct((t, d), F32),
        scratch_shapes=[pltpu.VMEM((tm, d), BF16), pltpu.VMEM((tm, d), F32)],
        compiler_params=_cparams("parallel", "arbitrary"),
        name="ffn",
    )(x2, g1, w1, w2, g2)


def _rope_perm():
    idx = list(range(HEAD_DIM))
    for c in range(ROPE_HALF):
        idx[c], idx[c + ROPE_HALF] = c + ROPE_HALF, c
    return jnp.array(idx, jnp.int32)


def _split_w_in(w_in):
    gw = GROUP_WIDTH
    edges = [0, gw, 2 * gw, 3 * gw, 4 * gw, 5 * gw, 6 * gw]
    for _ in range(6):
        edges.append(edges[-1] + HEAD_DIM)
    edges.append(edges[-1] + ATT_HEADS * N_BRANCH)
    names = ("cval", "cgate", "pool", "su", "sv", "q", "kc", "vc", "ks", "vs", "kw", "vw", "g")
    return {n: w_in[:, edges[k]:edges[k + 1]] for k, n in enumerate(names)}


def _layer_weights(p):
    perm = _rope_perm()
    w = _split_w_in(p["w_in"])
    wa = jnp.concatenate([w["cval"], w["cgate"], w["pool"], w["su"], w["sv"], w["kc"], w["vc"],
                          w["ks"], w["ks"][:, perm], w["kw"], w["kw"][:, perm]], axis=1).astype(BF16)
    pad = jnp.zeros((D_MODEL, ZB_ROWS - ROW_G - ATT_HEADS * N_BRANCH), F32)
    wb = jnp.concatenate([w["q"], w["vs"], w["vw"], w["g"], pad], axis=1).T.astype(BF16)

    pool_bd = jnp.zeros((GROUP_WIDTH, GROUP_WIDTH), F32)
    for gi in range(len(POOL_WINDOWS)):
        sl = slice(gi * POOL_GROUP, (gi + 1) * POOL_GROUP)
        pool_bd = pool_bd.at[sl, sl].set(p["pool_w"][gi])
    sgu_bias = jnp.repeat(jnp.transpose(p["sgu_b"]), GROUP_WIDTH // SGU_HEADS, axis=1)

    w1k = p["cmp_k_w1"].reshape(CMP_BLOCK, HEAD_DIM, HEAD_DIM)
    w1v = p["cmp_v_w1"].reshape(CMP_BLOCK, HEAD_DIM, HEAD_DIM)
    z = jnp.zeros((CMP_STRIDE, HEAD_DIM, HEAD_DIM), F32)

    def kv_diag(a, b):
        return jnp.concatenate([jnp.concatenate([a, z], axis=2), jnp.concatenate([z, b], axis=2)], axis=1)

    cmp_wa = kv_diag(w1k[:CMP_STRIDE], w1v[:CMP_STRIDE]).astype(BF16)
    cmp_wb = kv_diag(w1k[CMP_STRIDE:], w1v[CMP_STRIDE:]).astype(BF16)
    ones = jnp.ones((1, HEAD_DIM), F32)
    cmp_pe = jnp.concatenate([p["cmp_k_pe"].reshape(-1, 1) * ones, p["cmp_v_pe"].reshape(-1, 1) * ones], axis=1)
    cmp_w1 = jnp.concatenate([p["cmp_k_w1"], p["cmp_v_w1"]], axis=1)
    zk = jnp.zeros((HEAD_DIM, LANES), F32)
    cmp_w2k = jnp.concatenate([jnp.concatenate([p["cmp_k_w2"], p["cmp_k_w2"][:, perm]], axis=1), zk], axis=0).astype(BF16)
    cmp_w2vt = jnp.concatenate([jnp.zeros((HEAD_DIM, HEAD_DIM), F32), p["cmp_v_w2"].T], axis=1).astype(BF16)

    row = lambda v: v.reshape(1, -1)
    return dict(
        wa=wa, wb=wb, pre_mix=row(p["pre_mix_norm"]), post_mix=row(p["post_mix_norm"]),
        pre_ffn=row(p["pre_ffn_norm"]), post_ffn=row(p["post_ffn_norm"]),
        conv_w=p["conv_w"], conv_b=row(p["conv_b"]), conv_lg=row(p["conv_ln_g"]), conv_lb=row(p["conv_ln_b"]),
        pool_bd=pool_bd.astype(BF16), pool_scale=row(p["pool_scale"]),
        sgu_lg=row(p["sgu_ln_g"]), sgu_lb=row(p["sgu_ln_b"]), sgu_w=p["sgu_w"], sgu_bias=sgu_bias,
        cmp_wa=cmp_wa, cmp_wb=cmp_wb, cmp_pe=cmp_pe, cmp_w1=cmp_w1, cmp_w2k=cmp_w2k, cmp_w2vt=cmp_w2vt,
        w_out=p["w_out"].astype(BF16), ffn_w1=p["ffn_w1"].astype(BF16), ffn_w2=p["ffn_w2"].astype(BF16),
    )


def _overlap_t(s):
    n_blk = s // CMP_STRIDE
    n_slc = s // SLC_BLOCK
    bs = jnp.arange(n_blk)[None, :] * CMP_STRIDE
    ss = jnp.arange(n_slc)[:, None] * SLC_BLOCK
    ov = jnp.clip(jnp.minimum(bs + CMP_BLOCK, ss + SLC_BLOCK) - jnp.maximum(bs, ss), 0)
    return (ov.astype(F32) / CMP_STRIDE).astype(BF16)


def _rope_tables(positions):
    bsz, s = positions.shape
    inv = ROPE_THETA ** (-jnp.arange(ROPE_HALF, dtype=F32) * 2.0 / ROPE_DIM)
    inv64 = jnp.concatenate([inv, inv, jnp.zeros((HEAD_DIM - ROPE_DIM,), F32)])
    inv128 = jnp.concatenate([inv64, inv64]).reshape(1, LANES)
    sgn64 = jnp.concatenate([-jnp.ones((ROPE_HALF,), F32), jnp.ones((ROPE_HALF,), F32),
                             jnp.zeros((HEAD_DIM - ROPE_DIM,), F32)])
    sgn128 = jnp.concatenate([sgn64, sgn64]).reshape(1, LANES)
    posf = positions.astype(F32)
    n_blk = s // CMP_STRIDE
    pos_end = posf[:, CMP_BLOCK - 1::CMP_STRIDE]
    pos_end = jnp.pad(pos_end, ((0, 0), (0, n_blk - pos_end.shape[1])))
    cs_tok = _rope_tok_table(posf[:, :, None], inv128, sgn128)
    cs_cmp = _rope_tok_table(pos_end[:, :, None], inv128, sgn128)
    cos_t, sin_t = _rope_chan_table(posf[:, None, :], inv.reshape(ROPE_HALF, 1))
    return cs_tok, cs_cmp, cos_t, sin_t


def _mixers(x, lw, tables, ov_t):
    cs_tok, cs_cmp, cos_t, sin_t = tables
    za, zb = _proj(x, lw["pre_mix"], lw["wa"], lw["wb"])
    y_conv = _conv_mixer(za, lw["conv_w"], lw["conv_b"], lw["conv_lg"], lw["conv_lb"])
    y_pool = _pool_mixer(za, lw["pool_bd"], lw["pool_scale"])
    y_sgu = _sgu_mixer(za, lw["sgu_lg"], lw["sgu_lb"], lw["sgu_w"], lw["sgu_bias"])
    prep = _att_prep(za, zb, cs_tok, cs_cmp, cos_t, sin_t, lw["cmp_wa"], lw["cmp_wb"],
                     lw["cmp_pe"], lw["cmp_w1"], lw["cmp_w2k"], lw["cmp_w2vt"])
    y_att = _att(*prep, ov_t)
    return y_conv, y_pool, y_sgu, y_att


def kernel(x, positions, pre_mix_norm, post_mix_norm, pre_ffn_norm, post_ffn_norm, w_in, conv_w, conv_b, conv_ln_g, conv_ln_b, pool_w, pool_scale, sgu_ln_g, sgu_ln_b, sgu_w, sgu_b, cmp_k_pe, cmp_k_w1, cmp_k_w2, cmp_v_pe, cmp_v_w1, cmp_v_w2, w_out, ffn_w1, ffn_w2):
    params = dict(pre_mix_norm=pre_mix_norm, post_mix_norm=post_mix_norm, pre_ffn_norm=pre_ffn_norm,
                  post_ffn_norm=post_ffn_norm, w_in=w_in, conv_w=conv_w, conv_b=conv_b,
                  conv_ln_g=conv_ln_g, conv_ln_b=conv_ln_b, pool_w=pool_w, pool_scale=pool_scale,
                  sgu_ln_g=sgu_ln_g, sgu_ln_b=sgu_ln_b, sgu_w=sgu_w, sgu_b=sgu_b,
                  cmp_k_pe=cmp_k_pe, cmp_k_w1=cmp_k_w1, cmp_k_w2=cmp_k_w2, cmp_v_pe=cmp_v_pe,
                  cmp_v_w1=cmp_v_w1, cmp_v_w2=cmp_v_w2, w_out=w_out, ffn_w1=ffn_w1, ffn_w2=ffn_w2)
    bsz, s, d = x.shape
    depth = w_in.shape[0]
    tables = _rope_tables(positions)
    ov_t = _overlap_t(s)
    for l in range(depth):
        lw = _layer_weights({k: v[l] for k, v in params.items()})
        ys = _mixers(x, lw, tables, ov_t)
        x2 = _out_proj(x.reshape(bsz * s, d), [y.reshape(bsz * s, GROUP_WIDTH) for y in ys],
                       lw["w_out"], lw["post_mix"])
        x2 = _ffn(x2, lw["pre_ffn"], lw["ffn_w1"], lw["ffn_w2"], lw["post_ffn"])
        x = x2.reshape(bsz, s, d)
    return x
```

```python
import functools

import jax
import jax.numpy as jnp
from jax import lax
from jax.experimental import pallas as pl
from jax.experimental.pallas import tpu as pltpu

F32 = jnp.float32
BF16 = jnp.bfloat16

D_MODEL = 1024
GROUP_WIDTH = 256
CONV_WIDTH = 31
POOL_WINDOWS = (2, 4, 8, 16)
POOL_GROUP = 64
SGU_HEADS = 4
SGU_CHUNK = 128
ATT_HEADS = 4
HEAD_DIM = 64
ROPE_DIM = 16
ROPE_HALF = 8
ROPE_THETA = 500000.0
CMP_BLOCK = 32
CMP_STRIDE = 16
SLC_BLOCK = 64
N_SELECT = 8
N_LOCAL = 2
WINDOW = 512
N_BRANCH = 3
D_FF = 4096
NORM_EPS = 1e-6
NEG_INF = -1e30
FORCE_SCORE = 1e9

LANES = 128
SUBLANES = 8
VMEM_LIMIT_BYTES = 56 * 1024 * 1024

PROJ_ROWS = 512
FFN_ROWS = 512
FFN_COLS = 512
CONV_ROWS = 64
POOL_ROWS = 64
ATT_Q = 256
ATT_K = 256
LOG2_E = 1.4426950408889634

COL_CVAL, COL_CGATE, COL_POOL, COL_SGU_U, COL_SGU_V = 0, 1, 2, 3, 4
COL_KCVC, COL_KS, COL_KW = 10, 11, 12
ZA_WIDTH = 1664
ROW_Q, ROW_VS, ROW_VW, ROW_G = 0, 256, 320, 384
ZB_ROWS = 400


def _cparams(*sem):
    return pltpu.CompilerParams(dimension_semantics=sem, vmem_limit_bytes=VMEM_LIMIT_BYTES)


def _gelu(x):
    return 0.5 * x * (1.0 + jnp.tanh(0.7978845608028654 * (x + 0.044715 * (x * x * x))))


def _sigmoid(x):
    return 1.0 / (1.0 + jnp.exp(-x))


def _layer_norm(x, g, b):
    mu = jnp.mean(x, axis=-1, keepdims=True)
    d = x - mu
    var = jnp.mean(d * d, axis=-1, keepdims=True)
    return d * lax.rsqrt(var + NORM_EPS) * g + b


def _rms_norm(x, g):
    return x * lax.rsqrt(jnp.mean(x * x, axis=-1, keepdims=True) + NORM_EPS) * g


def _rope_tok_kernel(pos_ref, inv_ref, sgn_ref, cs_ref):
    ang = pos_ref[0] * inv_ref[...]
    lane = lax.broadcasted_iota(jnp.int32, ang.shape, 1)
    cs_ref[0] = jnp.where(lane < HEAD_DIM, jnp.cos(ang), jnp.sin(ang) * sgn_ref[...])


def _rope_tok_table(pos_col, inv128, sgn128):
    b, n, _ = pos_col.shape
    return pl.pallas_call(
        _rope_tok_kernel,
        grid=(b,),
        in_specs=[pl.BlockSpec((1, n, 1), lambda i: (i, 0, 0)),
                  pl.BlockSpec((1, LANES), lambda i: (0, 0)),
                  pl.BlockSpec((1, LANES), lambda i: (0, 0))],
        out_specs=pl.BlockSpec((1, n, LANES), lambda i: (i, 0, 0)),
        out_shape=jax.ShapeDtypeStruct((b, n, LANES), F32),
        compiler_params=_cparams("parallel"),
        name="rope_tok_table",
    )(pos_col, inv128, sgn128)


def _rope_chan_kernel(pos_ref, inv_ref, cos_ref, sin_ref):
    ang = inv_ref[...] * pos_ref[0]
    cos_ref[0] = jnp.cos(ang)
    sin_ref[0] = jnp.sin(ang)


def _rope_chan_table(pos_row, inv_col):
    b, _, s = pos_row.shape
    spec = pl.BlockSpec((1, ROPE_HALF, s), lambda i: (i, 0, 0))
    return pl.pallas_call(
        _rope_chan_kernel,
        grid=(b,),
        in_specs=[pl.BlockSpec((1, 1, s), lambda i: (i, 0, 0)),
                  pl.BlockSpec((ROPE_HALF, 1), lambda i: (0, 0))],
        out_specs=[spec, spec],
        out_shape=[jax.ShapeDtypeStruct((b, ROPE_HALF, s), F32)] * 2,
        compiler_params=_cparams("parallel"),
        name="rope_chan_table",
    )(pos_row, inv_col)


def _proj_kernel(x_ref, g_ref, wa_ref, wb_ref, za_ref, zb_ref):
    h = _rms_norm(x_ref[0], g_ref[...]).astype(BF16)
    za_ref[0] = jnp.dot(h, wa_ref[...], preferred_element_type=F32)
    zb_ref[0] = lax.dot_general(wb_ref[...], h, (((1,), (1,)), ((), ())),
                                preferred_element_type=F32)


def _proj(x, g, wa, wb):
    b, s, d = x.shape
    tm = min(PROJ_ROWS, s)
    return pl.pallas_call(
        _proj_kernel,
        grid=(b, s // tm),
        in_specs=[pl.BlockSpec((1, tm, d), lambda i, j: (i, j, 0)),
                  pl.BlockSpec((1, d), lambda i, j: (0, 0)),
                  pl.BlockSpec((d, ZA_WIDTH), lambda i, j: (0, 0)),
                  pl.BlockSpec((ZB_ROWS, d), lambda i, j: (0, 0))],
        out_specs=[pl.BlockSpec((1, tm, ZA_WIDTH), lambda i, j: (i, j, 0)),
                   pl.BlockSpec((1, ZB_ROWS, tm), lambda i, j: (i, 0, j))],
        out_shape=[jax.ShapeDtypeStruct((b, s, ZA_WIDTH), F32),
                   jax.ShapeDtypeStruct((b, ZB_ROWS, s), F32)],
        compiler_params=_cparams("parallel", "parallel"),
        name="in_proj",
    )(x, g, wa, wb)


CONV_PAD = 32


def _conv_kernel(a_ref, gate_ref, w_ref, b_ref, lg_ref, lb_ref, o_ref, hp_ref):
    s = a_ref.shape[1]
    hp_ref[0:CONV_PAD, :] = jnp.zeros((CONV_PAD, GROUP_WIDTH), F32)
    hp_ref[CONV_PAD:CONV_PAD + s, :] = a_ref[0] * _sigmoid(gate_ref[0])
    shift = CONV_PAD - (CONV_WIDTH - 1)

    def body(i, carry):
        t0 = pl.multiple_of(i * CONV_ROWS, CONV_ROWS)
        acc = jnp.zeros((CONV_ROWS, GROUP_WIDTH), F32) + b_ref[...]
        win = hp_ref[pl.ds(t0, CONV_ROWS + CONV_PAD), :]
        for r in range(SUBLANES):
            taps = [k for k in range(CONV_WIDTH) if (k + shift) % SUBLANES == r]
            span = max(k + shift - r for k in taps) + CONV_ROWS
            wr = win[r:r + span, :]
            for k in taps:
                off = k + shift - r
                acc = acc + wr[off:off + CONV_ROWS, :] * w_ref[k:k + 1, :]
        y = _layer_norm(acc, lg_ref[...], lb_ref[...])
        o_ref[0, pl.ds(t0, CONV_ROWS), :] = y * _sigmoid(y)
        return carry

    lax.fori_loop(0, s // CONV_ROWS, body, 0)


def _conv_mixer(za, w, b, lg, lb):
    bsz, s, _ = za.shape
    vec = pl.BlockSpec((1, GROUP_WIDTH), lambda i: (0, 0))
    return pl.pallas_call(
        _conv_kernel,
        grid=(bsz,),
        in_specs=[pl.BlockSpec((1, s, GROUP_WIDTH), lambda i: (i, 0, COL_CVAL)),
                  pl.BlockSpec((1, s, GROUP_WIDTH), lambda i: (i, 0, COL_CGATE)),
                  pl.BlockSpec((CONV_WIDTH, GROUP_WIDTH), lambda i: (0, 0)),
                  vec, vec, vec],
        out_specs=pl.BlockSpec((1, s, GROUP_WIDTH), lambda i: (i, 0, 0)),
        out_shape=jax.ShapeDtypeStruct((bsz, s, GROUP_WIDTH), F32),
        scratch_shapes=[pltpu.VMEM((CONV_PAD + s, GROUP_WIDTH), F32)],
        compiler_params=_cparams("parallel"),
        name="conv_mixer",
    )(za, za, w, b, lg, lb)


POOL_PAD = 16


def _pool_kernel(p_ref, w_ref, sc_ref, o_ref, pp_ref):
    s = p_ref.shape[1]
    pp_ref[0:POOL_PAD, :] = jnp.zeros((POOL_PAD, GROUP_WIDTH), F32)
    pp_ref[POOL_PAD:POOL_PAD + s, :] = p_ref[0]
    lane = lax.broadcasted_iota(jnp.int32, (1, GROUP_WIDTH), 1)
    grp = lane // POOL_GROUP
    win = jnp.where(grp == 0, POOL_WINDOWS[0],
                    jnp.where(grp == 1, POOL_WINDOWS[1],
                              jnp.where(grp == 2, POOL_WINDOWS[2], POOL_WINDOWS[3])))

    def body(i, carry):
        t0 = pl.multiple_of(i * POOL_ROWS, POOL_ROWS)
        rows = pp_ref[pl.ds(t0, POOL_ROWS + POOL_PAD), :]
        p = rows[POOL_PAD:POOL_PAD + POOL_ROWS, :]
        acc = p
        sums = []
        back = 1
        for w in POOL_WINDOWS:
            while back < w:
                acc = acc + rows[POOL_PAD - back:POOL_PAD - back + POOL_ROWS, :]
                back += 1
            sums.append(acc)
        total = jnp.where(grp == 0, sums[0],
                          jnp.where(grp == 1, sums[1],
                                    jnp.where(grp == 2, sums[2], sums[3])))
        t = t0 + lax.broadcasted_iota(jnp.int32, (POOL_ROWS, 1), 0)
        count = jnp.minimum(t + 1, win).astype(F32)
        mixed = total / count - p
        y = jnp.dot(mixed.astype(BF16), w_ref[...], preferred_element_type=F32)
        o_ref[0, pl.ds(t0, POOL_ROWS), :] = y * sc_ref[...]
        return carry

    lax.fori_loop(0, s // POOL_ROWS, body, 0)


def _pool_mixer(za, w_bd, scale):
    bsz, s, _ = za.shape
    return pl.pallas_call(
        _pool_kernel,
        grid=(bsz,),
        in_specs=[pl.BlockSpec((1, s, GROUP_WIDTH), lambda i: (i, 0, COL_POOL)),
                  pl.BlockSpec((GROUP_WIDTH, GROUP_WIDTH), lambda i: (0, 0)),
                  pl.BlockSpec((1, GROUP_WIDTH), lambda i: (0, 0))],
        out_specs=pl.BlockSpec((1, s, GROUP_WIDTH), lambda i: (i, 0, 0)),
        out_shape=jax.ShapeDtypeStruct((bsz, s, GROUP_WIDTH), F32),
        scratch_shapes=[pltpu.VMEM((POOL_PAD + s, GROUP_WIDTH), F32)],
        compiler_params=_cparams("parallel"),
        name="pool_mixer",
    )(za, w_bd, scale)


def _sgu_kernel(u_ref, v_ref, lg_ref, lb_ref, w_ref, bias_ref, o_ref):
    s = u_ref.shape[1]
    row = lax.broadcasted_iota(jnp.int32, (SGU_CHUNK, SGU_CHUNK), 0)
    col = lax.broadcasted_iota(jnp.int32, (SGU_CHUNK, SGU_CHUNK), 1)
    head = lax.broadcasted_iota(jnp.int32, (1, GROUP_WIDTH), 1) // (GROUP_WIDTH // SGU_HEADS)

    def body(c, carry):
        t0 = pl.multiple_of(c * SGU_CHUNK, SGU_CHUNK)
        v = _layer_norm(_gelu(v_ref[0, pl.ds(t0, SGU_CHUNK), :]), lg_ref[...], lb_ref[...])
        vb = v.astype(BF16)
        mixed = jnp.zeros((SGU_CHUNK, GROUP_WIDTH), F32)
        for h in range(SGU_HEADS):
            w = jnp.where(row >= col, w_ref[h], 0.0).astype(BF16)
            r = jnp.dot(w, vb, preferred_element_type=F32)
            mixed = jnp.where(head == h, r, mixed)
        u = _gelu(u_ref[0, pl.ds(t0, SGU_CHUNK), :])
        o_ref[0, pl.ds(t0, SGU_CHUNK), :] = u * (mixed + bias_ref[...])
        return carry

    lax.fori_loop(0, s // SGU_CHUNK, body, 0)


def _sgu_mixer(za, lg, lb, w, bias):
    bsz, s, _ = za.shape
    vec = pl.BlockSpec((1, GROUP_WIDTH), lambda i: (0, 0))
    return pl.pallas_call(
        _sgu_kernel,
        grid=(bsz,),
        in_specs=[pl.BlockSpec((1, s, GROUP_WIDTH), lambda i: (i, 0, COL_SGU_U)),
                  pl.BlockSpec((1, s, GROUP_WIDTH), lambda i: (i, 0, COL_SGU_V)),
                  vec, vec,
                  pl.BlockSpec((SGU_HEADS, SGU_CHUNK, SGU_CHUNK), lambda i: (0, 0, 0)),
                  pl.BlockSpec((SGU_CHUNK, GROUP_WIDTH), lambda i: (0, 0))],
        out_specs=pl.BlockSpec((1, s, GROUP_WIDTH), lambda i: (i, 0, 0)),
        out_shape=jax.ShapeDtypeStruct((bsz, s, GROUP_WIDTH), F32),
        compiler_params=_cparams("parallel"),
        name="sgu_mixer",
    )(za, za, lg, lb, w, bias)


def _att_prep_kernel(kcvc_ref, ksx_ref, kwx_ref, zb_ref, cs_ref, cscmp_ref, cos_ref, sin_ref,
                     wa_ref, wb_ref, pe_ref, w1_ref, w2k_ref, w2vt_ref,
                     q_ref, ks_ref, kw_ref, vs_ref, vw_ref, kc_ref, vct_ref, g_ref, sh_ref):
    s = ksx_ref.shape[1]
    n_blk = s // CMP_STRIDE

    def rope_tok(x, cs):
        r = x * cs
        return (r + pltpu.roll(r, HEAD_DIM, axis=1))[:, :HEAD_DIM]

    cs = cs_ref[0]
    ks_ref[0] = rope_tok(ksx_ref[0], cs).astype(BF16)
    kw_ref[0] = rope_tok(kwx_ref[0], cs).astype(BF16)

    cos = cos_ref[0]
    sin = sin_ref[0]
    scale = HEAD_DIM ** -0.5 * LOG2_E
    parts = []
    for h in range(ATT_HEADS):
        r0 = ROW_Q + h * HEAD_DIM
        x1 = zb_ref[0, r0:r0 + ROPE_HALF, :]
        x2 = zb_ref[0, r0 + ROPE_HALF:r0 + ROPE_DIM, :]
        parts += [x1 * cos - x2 * sin, x2 * cos + x1 * sin, zb_ref[0, r0 + ROPE_DIM:r0 + HEAD_DIM, :]]
    q_ref[0] = (jnp.concatenate(parts, axis=0) * scale).astype(BF16)

    for c in range(s // ATT_K):
        vs_ref[0, c] = zb_ref[0, ROW_VS:ROW_VS + HEAD_DIM, c * ATT_K:(c + 1) * ATT_K].astype(BF16)
        vw_ref[0, c] = zb_ref[0, ROW_VW:ROW_VW + HEAD_DIM, c * ATT_K:(c + 1) * ATT_K].astype(BF16)
    g_ref[0] = _sigmoid(zb_ref[0, ROW_G:ROW_G + 16, :])

    acc_a = jnp.zeros((n_blk, LANES), F32)
    acc_b = jnp.zeros((n_blk, LANES), F32)
    for l in range(CMP_STRIDE):
        xl = kcvc_ref[0, pl.ds(l, n_blk, stride=CMP_STRIDE), :].astype(BF16)
        acc_a = acc_a + jnp.dot(xl, wa_ref[l], preferred_element_type=F32)
        acc_b = acc_b + jnp.dot(xl, wb_ref[l], preferred_element_type=F32)
    sh_ref[0:n_blk, :] = acc_b
    sh_ref[n_blk:n_blk + SUBLANES, :] = jnp.zeros((SUBLANES, LANES), F32)
    pe_term = jnp.sum(pe_ref[...] * w1_ref[...], axis=0, keepdims=True)
    hid = _gelu(acc_a + sh_ref[1:n_blk + 1, :] + pe_term).astype(BF16)
    kk = jnp.dot(hid, w2k_ref[...], preferred_element_type=F32)
    kc_ref[0] = rope_tok(kk, cscmp_ref[0]).astype(BF16)
    vct_ref[0] = lax.dot_general(w2vt_ref[...], hid, (((1,), (1,)), ((), ())),
                                 preferred_element_type=F32).astype(BF16)


def _att_prep(za, zb, cs_tok, cs_cmp, cos_t, sin_t, wa, wb, pe, w1, w2k, w2vt):
    bsz, s, _ = za.shape
    n_blk = s // CMP_STRIDE
    nt = s // ATT_K
    tok = lambda c: pl.BlockSpec((1, s, LANES), lambda i: (i, 0, c))
    full3 = lambda shp: pl.BlockSpec(shp, lambda i: (0, 0, 0))
    full2 = lambda shp: pl.BlockSpec(shp, lambda i: (0, 0))
    per_b3 = lambda shp: pl.BlockSpec((1,) + shp, lambda i: (i, 0, 0))
    per_b4 = lambda shp: pl.BlockSpec((1,) + shp, lambda i: (i, 0, 0, 0))
    return pl.pallas_call(
        _att_prep_kernel,
        grid=(bsz,),
        in_specs=[tok(COL_KCVC), tok(COL_KS), tok(COL_KW),
                  per_b3((ZB_ROWS, s)), per_b3((s, LANES)), per_b3((n_blk, LANES)),
                  per_b3((ROPE_HALF, s)), per_b3((ROPE_HALF, s)),
                  full3((CMP_STRIDE, LANES, LANES)), full3((CMP_STRIDE, LANES, LANES)),
                  full2((CMP_BLOCK * HEAD_DIM, LANES)), full2((CMP_BLOCK * HEAD_DIM, LANES)),
                  full2((LANES, LANES)), full2((HEAD_DIM, LANES))],
        out_specs=[per_b3((ATT_HEADS * HEAD_DIM, s)), per_b3((s, HEAD_DIM)), per_b3((s, HEAD_DIM)),
                   per_b4((nt, HEAD_DIM, ATT_K)), per_b4((nt, HEAD_DIM, ATT_K)),
                   per_b3((n_blk, HEAD_DIM)), per_b3((HEAD_DIM, n_blk)), per_b3((16, s))],
        out_shape=[jax.ShapeDtypeStruct((bsz, ATT_HEADS * HEAD_DIM, s), BF16),
                   jax.ShapeDtypeStruct((bsz, s, HEAD_DIM), BF16),
                   jax.ShapeDtypeStruct((bsz, s, HEAD_DIM), BF16),
                   jax.ShapeDtypeStruct((bsz, nt, HEAD_DIM, ATT_K), BF16),
                   jax.ShapeDtypeStruct((bsz, nt, HEAD_DIM, ATT_K), BF16),
                   jax.ShapeDtypeStruct((bsz, n_blk, HEAD_DIM), BF16),
                   jax.ShapeDtypeStruct((bsz, HEAD_DIM, n_blk), BF16),
                   jax.ShapeDtypeStruct((bsz, 16, s), F32)],
        scratch_shapes=[pltpu.VMEM((n_blk + SUBLANES, LANES), F32)],
        compiler_params=_cparams("parallel"),
        name="att_prep",
    )(za, za, za, zb, cs_tok, cs_cmp, cos_t, sin_t, wa, wb, pe, w1, w2k, w2vt)


def _att_kernel(q_ref, ks_ref, kw_ref, vs_ref, vw_ref, kc_ref, vct_ref, g_ref, ov_ref,
                o_ref, sel_ref):
    n_blk = kc_ref.shape[1]
    n_slc = ov_ref.shape[0]
    tq = ATT_Q
    wide = ATT_HEADS * tq
    i = pl.program_id(1)
    t0 = i * tq

    q = q_ref[0]
    qs = jnp.concatenate([q[h * HEAD_DIM:(h + 1) * HEAD_DIM] for h in range(ATT_HEADS)], axis=1)
    lane_w = lax.broadcasted_iota(jnp.int32, (1, wide), 1)
    t_w = t0 + (lane_w & (tq - 1))
    t_q = t0 + lax.broadcasted_iota(jnp.int32, (1, tq), 1)
    heads = lambda a: jnp.concatenate([a] * ATT_HEADS, axis=1)

    s_c = jnp.dot(kc_ref[0], qs, preferred_element_type=F32)
    blk_end = lax.broadcasted_iota(jnp.int32, (n_blk, 1), 0) * CMP_STRIDE + (CMP_BLOCK - 1)
    cmask = blk_end <= t_w
    s_m = jnp.where(cmask, s_c, NEG_INF)
    e = jnp.exp2(s_m - jnp.max(s_m, axis=0, keepdims=True))
    p_c = jnp.where(cmask, e * (1.0 / jnp.sum(e, axis=0, keepdims=True)), 0.0)
    o_cmp = jnp.dot(vct_ref[0], p_c.astype(BF16), preferred_element_type=F32)

    p_sum = p_c[:, 0:tq]
    for h in range(1, ATT_HEADS):
        p_sum = p_sum + p_c[:, h * tq:(h + 1) * tq]
    p_hi = p_sum.astype(BF16)
    p_lo = (p_sum - p_hi.astype(F32)).astype(BF16)
    imp = (jnp.dot(ov_ref[...], p_hi, preferred_element_type=F32)
           + jnp.dot(ov_ref[...], p_lo, preferred_element_type=F32))
    j = lax.broadcasted_iota(jnp.int32, (n_slc, 1), 0)
    back = t_q // SLC_BLOCK - j
    forced = (j == 0) | ((back >= 0) & (back < N_LOCAL))
    imp = jnp.where(forced, FORCE_SCORE, jnp.where(back < 0, -1.0, imp))
    rank = jnp.zeros((n_slc, tq), F32)
    for r in range(n_slc):
        row = imp[r:r + 1, :]
        ahead = (row > imp) | ((row == imp) & (j > r))
        rank = rank + jnp.where(ahead, 1.0, 0.0)
    sel_bias = jnp.where(rank < min(N_SELECT, n_slc), 0.0, -jnp.inf)
    per_tile = ATT_K // SLC_BLOCK
    for kt in range(n_slc // per_tile):
        sel_ref[kt, 0:per_tile, :] = sel_bias[kt * per_tile:(kt + 1) * per_tile, :]

    k_iota = lax.broadcasted_iota(jnp.int32, (ATT_K, 1), 0)

    def block_bias(kt):
        st = sel_ref[kt, 0:per_tile, :]
        rows = [jnp.broadcast_to(st[b:b + 1, :], (SLC_BLOCK, tq)) for b in range(per_tile)]
        return jnp.concatenate(rows, axis=0)

    def attend(k_tile, v_tile, bias, carry):
        m, l, acc = carry
        s_t = jnp.dot(k_tile, qs, preferred_element_type=F32) + heads(bias)
        m_new = jnp.maximum(m, jnp.max(s_t, axis=0, keepdims=True))
        alpha = jnp.exp2(m - m_new)
        p = jnp.exp2(s_t - m_new)
        l = alpha * l + jnp.sum(p, axis=0, keepdims=True)
        acc = alpha * acc + jnp.dot(v_tile, p.astype(BF16), preferred_element_type=F32)
        return m_new, l, acc

    init = (jnp.full((1, wide), NEG_INF, F32), jnp.zeros((1, wide), F32),
            jnp.zeros((HEAD_DIM, wide), F32))

    def slc_body(kt, carry):
        k0 = pl.multiple_of(kt * ATT_K, ATT_K)
        return attend(ks_ref[0, pl.ds(k0, ATT_K), :], vs_ref[0, kt], block_bias(kt), carry)

    carry = lax.fori_loop(0, i, slc_body, init)
    kd = pl.multiple_of(i * ATT_K, ATT_K)
    diag_bias = jnp.where((kd + k_iota) <= t_q, block_bias(i), -jnp.inf)
    _, l_s, acc_s = attend(ks_ref[0, pl.ds(kd, ATT_K), :], vs_ref[0, i], diag_bias, carry)

    n_win = WINDOW // ATT_K + 1
    first = jnp.maximum(i - (n_win - 1), 0)
    kw0 = pl.multiple_of(first * ATT_K, ATT_K)
    s_w = jnp.dot(kw_ref[0, pl.ds(kw0, n_win * ATT_K), :], qs, preferred_element_type=F32)
    diff = t_q - (kw0 + lax.broadcasted_iota(jnp.int32, (n_win * ATT_K, 1), 0))
    s_w = s_w + heads(jnp.where((diff >= 0) & (diff < WINDOW), 0.0, -jnp.inf))
    p_w = jnp.exp2(s_w - jnp.max(s_w, axis=0, keepdims=True))
    l_w = jnp.sum(p_w, axis=0, keepdims=True)
    p_w = p_w.astype(BF16)
    acc_w = None
    for n in range(n_win):
        part = jnp.dot(vw_ref[0, first + n], p_w[n * ATT_K:(n + 1) * ATT_K], preferred_element_type=F32)
        acc_w = part if acc_w is None else acc_w + part

    g = g_ref[0]
    gate = lambda br: jnp.concatenate(
        [g[h * N_BRANCH + br:h * N_BRANCH + br + 1, :] for h in range(ATT_HEADS)], axis=1)
    o = gate(0) * o_cmp + (gate(1) * (1.0 / l_s)) * acc_s + (gate(2) * (1.0 / l_w)) * acc_w
    o = jnp.concatenate([o[:, h * tq:(h + 1) * tq] for h in range(ATT_HEADS)], axis=0)
    o_ref[0] = o.T


def _att(q_t, ks, kw, vs_t, vw_t, kc, vc_t, g_t, ov_t):
    bsz, _, s = q_t.shape
    n_blk = kc.shape[1]
    nt = s // ATT_K
    n_slc = s // SLC_BLOCK
    per_b3 = lambda shp: pl.BlockSpec((1,) + shp, lambda b, i: (b, 0, 0))
    per_b4 = lambda shp: pl.BlockSpec((1,) + shp, lambda b, i: (b, 0, 0, 0))
    return pl.pallas_call(
        _att_kernel,
        grid=(bsz, s // ATT_Q),
        in_specs=[pl.BlockSpec((1, ATT_HEADS * HEAD_DIM, ATT_Q), lambda b, i: (b, 0, i)),
                  per_b3((s, HEAD_DIM)), per_b3((s, HEAD_DIM)),
                  per_b4((nt, HEAD_DIM, ATT_K)), per_b4((nt, HEAD_DIM, ATT_K)),
                  per_b3((n_blk, HEAD_DIM)), per_b3((HEAD_DIM, n_blk)),
                  pl.BlockSpec((1, 16, ATT_Q), lambda b, i: (b, 0, i)),
                  pl.BlockSpec((n_slc, n_blk), lambda b, i: (0, 0))],
        out_specs=pl.BlockSpec((1, ATT_Q, GROUP_WIDTH), lambda b, i: (b, i, 0)),
        out_shape=jax.ShapeDtypeStruct((bsz, s, GROUP_WIDTH), F32),
        scratch_shapes=[pltpu.VMEM((nt, SUBLANES, ATT_Q), F32)],
        compiler_params=_cparams("parallel", "arbitrary"),
        name="sparse_attention",
    )(q_t, ks, kw, vs_t, vw_t, kc, vc_t, g_t, ov_t)


def _out_ffn_kernel(x_ref, y0_ref, y1_ref, y2_ref, y3_ref, wo_ref, gm_ref, g1_ref, w1_ref, w2_ref,
                    g2_ref, o_ref):
    acc = None
    for n, y_ref in enumerate((y0_ref, y1_ref, y2_ref, y3_ref)):
        part = jnp.dot(y_ref[...].astype(BF16), wo_ref[n * GROUP_WIDTH:(n + 1) * GROUP_WIDTH, :],
                       preferred_element_type=F32)
        acc = part if acc is None else acc + part
    x1 = x_ref[...] + _rms_norm(acc, gm_ref[...])
    h = _rms_norm(x1, g1_ref[...]).astype(BF16)
    acc = None
    for c in range(w1_ref.shape[1] // FFN_COLS):
        cols = slice(c * FFN_COLS, (c + 1) * FFN_COLS)
        f = jnp.maximum(jnp.dot(h, w1_ref[:, cols], preferred_element_type=F32), 0.0)
        part = jnp.dot((f * f).astype(BF16), w2_ref[cols, :], preferred_element_type=F32)
        acc = part if acc is None else acc + part
    o_ref[...] = x1 + _rms_norm(acc, g2_ref[...])


def _out_ffn(x2, ys, wo, gm, g1, w1, w2, g2):
    t, d = x2.shape
    dff = w1.shape[1]
    tm = min(FFN_ROWS, t)
    yspec = pl.BlockSpec((tm, GROUP_WIDTH), lambda i: (i, 0))
    vec = pl.BlockSpec((1, d), lambda i: (0, 0))
    once = lambda shp: pl.BlockSpec(shp, lambda i: (0, 0), pipeline_mode=pl.Buffered(1))
    return pl.pallas_call(
        _out_ffn_kernel,
        grid=(t // tm,),
        in_specs=[pl.BlockSpec((tm, d), lambda i: (i, 0)), yspec, yspec, yspec, yspec,
                  once((d, d)), vec, vec, once((d, dff)), once((dff, d)), vec],
        out_specs=pl.BlockSpec((tm, d), lambda i: (i, 0)),
        out_shape=jax.ShapeDtypeStruct((t, d), F32),
        compiler_params=_cparams("parallel"),
        name="out_ffn",
    )(x2, *ys, wo, gm, g1, w1, w2, g2)


def _rope_perm():
    idx = list(range(HEAD_DIM))
    for c in range(ROPE_HALF):
        idx[c], idx[c + ROPE_HALF] = c + ROPE_HALF, c
    return jnp.array(idx, jnp.int32)


def _split_w_in(w_in):
    gw = GROUP_WIDTH
    edges = [0, gw, 2 * gw, 3 * gw, 4 * gw, 5 * gw, 6 * gw]
    for _ in range(6):
        edges.append(edges[-1] + HEAD_DIM)
    edges.append(edges[-1] + ATT_HEADS * N_BRANCH)
    names = ("cval", "cgate", "pool", "su", "sv", "q", "kc", "vc", "ks", "vs", "kw", "vw", "g")
    return {n: w_in[:, edges[k]:edges[k + 1]] for k, n in enumerate(names)}


def _layer_weights(p):
    perm = _rope_perm()
    w = _split_w_in(p["w_in"])
    wa = jnp.concatenate([w["cval"], w["cgate"], w["pool"], w["su"], w["sv"], w["kc"], w["vc"],
                          w["ks"], w["ks"][:, perm], w["kw"], w["kw"][:, perm]], axis=1).astype(BF16)
    pad = jnp.zeros((D_MODEL, ZB_ROWS - ROW_G - ATT_HEADS * N_BRANCH), F32)
    wb = jnp.concatenate([w["q"], w["vs"], w["vw"], w["g"], pad], axis=1).T.astype(BF16)

    pool_bd = jnp.zeros((GROUP_WIDTH, GROUP_WIDTH), F32)
    for gi in range(len(POOL_WINDOWS)):
        sl = slice(gi * POOL_GROUP, (gi + 1) * POOL_GROUP)
        pool_bd = pool_bd.at[sl, sl].set(p["pool_w"][gi])
    sgu_bias = jnp.repeat(jnp.transpose(p["sgu_b"]), GROUP_WIDTH // SGU_HEADS, axis=1)

    w1k = p["cmp_k_w1"].reshape(CMP_BLOCK, HEAD_DIM, HEAD_DIM)
    w1v = p["cmp_v_w1"].reshape(CMP_BLOCK, HEAD_DIM, HEAD_DIM)
    z = jnp.zeros((CMP_STRIDE, HEAD_DIM, HEAD_DIM), F32)

    def kv_diag(a, b):
        return jnp.concatenate([jnp.concatenate([a, z], axis=2), jnp.concatenate([z, b], axis=2)], axis=1)

    cmp_wa = kv_diag(w1k[:CMP_STRIDE], w1v[:CMP_STRIDE]).astype(BF16)
    cmp_wb = kv_diag(w1k[CMP_STRIDE:], w1v[CMP_STRIDE:]).astype(BF16)
    ones = jnp.ones((1, HEAD_DIM), F32)
    cmp_pe = jnp.concatenate([p["cmp_k_pe"].reshape(-1, 1) * ones, p["cmp_v_pe"].reshape(-1, 1) * ones], axis=1)
    cmp_w1 = jnp.concatenate([p["cmp_k_w1"], p["cmp_v_w1"]], axis=1)
    zk = jnp.zeros((HEAD_DIM, LANES), F32)
    cmp_w2k = jnp.concatenate([jnp.concatenate([p["cmp_k_w2"], p["cmp_k_w2"][:, perm]], axis=1), zk], axis=0).astype(BF16)
    cmp_w2vt = jnp.concatenate([jnp.zeros((HEAD_DIM, HEAD_DIM), F32), p["cmp_v_w2"].T], axis=1).astype(BF16)

    row = lambda v: v.reshape(1, -1)
    return dict(
        wa=wa, wb=wb, pre_mix=row(p["pre_mix_norm"]), post_mix=row(p["post_mix_norm"]),
        pre_ffn=row(p["pre_ffn_norm"]), post_ffn=row(p["post_ffn_norm"]),
        conv_w=p["conv_w"], conv_b=row(p["conv_b"]), conv_lg=row(p["conv_ln_g"]), conv_lb=row(p["conv_ln_b"]),
        pool_bd=pool_bd.astype(BF16), pool_scale=row(p["pool_scale"]),
        sgu_lg=row(p["sgu_ln_g"]), sgu_lb=row(p["sgu_ln_b"]), sgu_w=p["sgu_w"], sgu_bias=sgu_bias,
        cmp_wa=cmp_wa, cmp_wb=cmp_wb, cmp_pe=cmp_pe, cmp_w1=cmp_w1, cmp_w2k=cmp_w2k, cmp_w2vt=cmp_w2vt,
        w_out=p["w_out"].astype(BF16), ffn_w1=p["ffn_w1"].astype(BF16), ffn_w2=p["ffn_w2"].astype(BF16),
    )


def _overlap_t(s):
    n_blk = s // CMP_STRIDE
    n_slc = s // SLC_BLOCK
    bs = jnp.arange(n_blk)[None, :] * CMP_STRIDE
    ss = jnp.arange(n_slc)[:, None] * SLC_BLOCK
    ov = jnp.clip(jnp.minimum(bs + CMP_BLOCK, ss + SLC_BLOCK) - jnp.maximum(bs, ss), 0)
    return (ov.astype(F32) / CMP_STRIDE).astype(BF16)


def _rope_tables(positions):
    bsz, s = positions.shape
    inv = ROPE_THETA ** (-jnp.arange(ROPE_HALF, dtype=F32) * 2.0 / ROPE_DIM)
    inv64 = jnp.concatenate([inv, inv, jnp.zeros((HEAD_DIM - ROPE_DIM,), F32)])
    inv128 = jnp.concatenate([inv64, inv64]).reshape(1, LANES)
    sgn64 = jnp.concatenate([-jnp.ones((ROPE_HALF,), F32), jnp.ones((ROPE_HALF,), F32),
                             jnp.zeros((HEAD_DIM - ROPE_DIM,), F32)])
    sgn128 = jnp.concatenate([sgn64, sgn64]).reshape(1, LANES)
    posf = positions.astype(F32)
    n_blk = s // CMP_STRIDE
    pos_end = posf[:, CMP_BLOCK - 1::CMP_STRIDE]
    pos_end = jnp.pad(pos_end, ((0, 0), (0, n_blk - pos_end.shape[1])))
    cs_tok = _rope_tok_table(posf[:, :, None], inv128, sgn128)
    cs_cmp = _rope_tok_table(pos_end[:, :, None], inv128, sgn128)
    cos_t, sin_t = _rope_chan_table(posf[:, None, :], inv.reshape(ROPE_HALF, 1))
    return cs_tok, cs_cmp, cos_t, sin_t


def _mixers(x, lw, tables, ov_t):
    cs_tok, cs_cmp, cos_t, sin_t = tables
    za, zb = _proj(x, lw["pre_mix"], lw["wa"], lw["wb"])
    y_conv = _conv_mixer(za, lw["conv_w"], lw["conv_b"], lw["conv_lg"], lw["conv_lb"])
    y_pool = _pool_mixer(za, lw["pool_bd"], lw["pool_scale"])
    y_sgu = _sgu_mixer(za, lw["sgu_lg"], lw["sgu_lb"], lw["sgu_w"], lw["sgu_bias"])
    prep = _att_prep(za, zb, cs_tok, cs_cmp, cos_t, sin_t, lw["cmp_wa"], lw["cmp_wb"],
                     lw["cmp_pe"], lw["cmp_w1"], lw["cmp_w2k"], lw["cmp_w2vt"])
    y_att = _att(*prep, ov_t)
    return y_conv, y_pool, y_sgu, y_att


def kernel(x, positions, pre_mix_norm, post_mix_norm, pre_ffn_norm, post_ffn_norm, w_in, conv_w, conv_b, conv_ln_g, conv_ln_b, pool_w, pool_scale, sgu_ln_g, sgu_ln_b, sgu_w, sgu_b, cmp_k_pe, cmp_k_w1, cmp_k_w2, cmp_v_pe, cmp_v_w1, cmp_v_w2, w_out, ffn_w1, ffn_w2):
    params = dict(pre_mix_norm=pre_mix_norm, post_mix_norm=post_mix_norm, pre_ffn_norm=pre_ffn_norm,
                  post_ffn_norm=post_ffn_norm, w_in=w_in, conv_w=conv_w, conv_b=conv_b,
                  conv_ln_g=conv_ln_g, conv_ln_b=conv_ln_b, pool_w=pool_w, pool_scale=pool_scale,
                  sgu_ln_g=sgu_ln_g, sgu_ln_b=sgu_ln_b, sgu_w=sgu_w, sgu_b=sgu_b,
                  cmp_k_pe=cmp_k_pe, cmp_k_w1=cmp_k_w1, cmp_k_w2=cmp_k_w2, cmp_v_pe=cmp_v_pe,
                  cmp_v_w1=cmp_v_w1, cmp_v_w2=cmp_v_w2, w_out=w_out, ffn_w1=ffn_w1, ffn_w2=ffn_w2)
    bsz, s, d = x.shape
    depth = w_in.shape[0]
    tables = _rope_tables(positions)
    ov_t = _overlap_t(s)
    for l in range(depth):
        lw = _layer_weights({k: v[l] for k, v in params.items()})
        ys = _mixers(x, lw, tables, ov_t)
        x2 = _out_ffn(x.reshape(bsz * s, d), [y.reshape(bsz * s, GROUP_WIDTH) for y in ys],
                      lw["w_out"], lw["post_mix"], lw["pre_ffn"], lw["ffn_w1"], lw["ffn_w2"], lw["post_ffn"])
        x = x2.reshape(bsz, s, d)
    return x
```

```python
import functools

import jax
import jax.numpy as jnp
from jax import lax
from jax.experimental import pallas as pl
from jax.experimental.pallas import tpu as pltpu

F32 = jnp.float32
BF16 = jnp.bfloat16

D_MODEL = 1024
GROUP_WIDTH = 256
CONV_WIDTH = 31
POOL_WINDOWS = (2, 4, 8, 16)
POOL_GROUP = 64
SGU_HEADS = 4
SGU_CHUNK = 128
ATT_HEADS = 4
HEAD_DIM = 64
ROPE_DIM = 16
ROPE_HALF = 8
ROPE_THETA = 500000.0
CMP_BLOCK = 32
CMP_STRIDE = 16
SLC_BLOCK = 64
N_SELECT = 8
N_LOCAL = 2
WINDOW = 512
N_BRANCH = 3
D_FF = 4096
NORM_EPS = 1e-6
NEG_INF = -1e30
FORCE_SCORE = 1e9

LANES = 128
SUBLANES = 8
VMEM_LIMIT_BYTES = 56 * 1024 * 1024

PROJ_ROWS = 512
FFN_ROWS = 512
FFN_COLS = 512
CONV_ROWS = 64
POOL_ROWS = 64
ATT_Q = 256
ATT_K = 256
LOG2_E = 1.4426950408889634

COL_CVAL, COL_CGATE, COL_POOL, COL_SGU_U, COL_SGU_V = 0, 1, 2, 3, 4
COL_KCVC, COL_KS, COL_KW = 10, 11, 12
ZA_WIDTH = 1664
ROW_Q, ROW_VS, ROW_VW, ROW_G = 0, 256, 320, 384
ZB_ROWS = 400


def _cparams(*sem):
    return pltpu.CompilerParams(dimension_semantics=sem, vmem_limit_bytes=VMEM_LIMIT_BYTES)


def _gelu(x):
    return 0.5 * x * (1.0 + jnp.tanh(0.7978845608028654 * (x + 0.044715 * (x * x * x))))


def _sigmoid(x):
    return 1.0 / (1.0 + jnp.exp(-x))


def _layer_norm(x, g, b):
    mu = jnp.mean(x, axis=-1, keepdims=True)
    d = x - mu
    var = jnp.mean(d * d, axis=-1, keepdims=True)
    return d * lax.rsqrt(var + NORM_EPS) * g + b


def _rms_norm(x, g):
    return x * lax.rsqrt(jnp.mean(x * x, axis=-1, keepdims=True) + NORM_EPS) * g


def _rope_tok_kernel(pos_ref, inv_ref, sgn_ref, cs_ref):
    ang = pos_ref[0] * inv_ref[...]
    lane = lax.broadcasted_iota(jnp.int32, ang.shape, 1)
    cs_ref[0] = jnp.where(lane < HEAD_DIM, jnp.cos(ang), jnp.sin(ang) * sgn_ref[...])


def _rope_tok_table(pos_col, inv128, sgn128):
    b, n, _ = pos_col.shape
    return pl.pallas_call(
        _rope_tok_kernel,
        grid=(b,),
        in_specs=[pl.BlockSpec((1, n, 1), lambda i: (i, 0, 0)),
                  pl.BlockSpec((1, LANES), lambda i: (0, 0)),
                  pl.BlockSpec((1, LANES), lambda i: (0, 0))],
        out_specs=pl.BlockSpec((1, n, LANES), lambda i: (i, 0, 0)),
        out_shape=jax.ShapeDtypeStruct((b, n, LANES), F32),
        compiler_params=_cparams("parallel"),
        name="rope_tok_table",
    )(pos_col, inv128, sgn128)


def _rope_chan_kernel(pos_ref, inv_ref, cos_ref, sin_ref):
    ang = inv_ref[...] * pos_ref[0]
    cos_ref[0] = jnp.cos(ang)
    sin_ref[0] = jnp.sin(ang)


def _rope_chan_table(pos_row, inv_col):
    b, _, s = pos_row.shape
    spec = pl.BlockSpec((1, ROPE_HALF, s), lambda i: (i, 0, 0))
    return pl.pallas_call(
        _rope_chan_kernel,
        grid=(b,),
        in_specs=[pl.BlockSpec((1, 1, s), lambda i: (i, 0, 0)),
                  pl.BlockSpec((ROPE_HALF, 1), lambda i: (0, 0))],
        out_specs=[spec, spec],
        out_shape=[jax.ShapeDtypeStruct((b, ROPE_HALF, s), F32)] * 2,
        compiler_params=_cparams("parallel"),
        name="rope_chan_table",
    )(pos_row, inv_col)


def _proj_kernel(x_ref, g_ref, wa_ref, wb_ref, za_ref, zb_ref):
    h = _rms_norm(x_ref[0], g_ref[...]).astype(BF16)
    za_ref[0] = jnp.dot(h, wa_ref[...], preferred_element_type=F32)
    zb_ref[0] = lax.dot_general(wb_ref[...], h, (((1,), (1,)), ((), ())),
                                preferred_element_type=F32)


def _proj(x, g, wa, wb):
    b, s, d = x.shape
    tm = min(PROJ_ROWS, s)
    return pl.pallas_call(
        _proj_kernel,
        grid=(b, s // tm),
        in_specs=[pl.BlockSpec((1, tm, d), lambda i, j: (i, j, 0)),
                  pl.BlockSpec((1, d), lambda i, j: (0, 0)),
                  pl.BlockSpec((d, ZA_WIDTH), lambda i, j: (0, 0)),
                  pl.BlockSpec((ZB_ROWS, d), lambda i, j: (0, 0))],
        out_specs=[pl.BlockSpec((1, tm, ZA_WIDTH), lambda i, j: (i, j, 0)),
                   pl.BlockSpec((1, ZB_ROWS, tm), lambda i, j: (i, 0, j))],
        out_shape=[jax.ShapeDtypeStruct((b, s, ZA_WIDTH), F32),
                   jax.ShapeDtypeStruct((b, ZB_ROWS, s), F32)],
        compiler_params=_cparams("parallel", "parallel"),
        name="in_proj",
    )(x, g, wa, wb)


CONV_PAD = 32


def _conv_kernel(a_ref, gate_ref, w_ref, b_ref, lg_ref, lb_ref, o_ref, hp_ref):
    s = a_ref.shape[1]
    hp_ref[0:CONV_PAD, :] = jnp.zeros((CONV_PAD, GROUP_WIDTH), F32)
    hp_ref[CONV_PAD:CONV_PAD + s, :] = a_ref[0] * _sigmoid(gate_ref[0])
    shift = CONV_PAD - (CONV_WIDTH - 1)

    def body(i, carry):
        t0 = pl.multiple_of(i * CONV_ROWS, CONV_ROWS)
        acc = jnp.zeros((CONV_ROWS, GROUP_WIDTH), F32) + b_ref[...]
        win = hp_ref[pl.ds(t0, CONV_ROWS + CONV_PAD), :]
        n_win = CONV_ROWS + CONV_PAD
        for r in range(SUBLANES):
            taps = [k for k in range(CONV_WIDTH) if (k + shift) % SUBLANES == r]
            wr = win if r == 0 else pltpu.roll(win, n_win - r, axis=0)
            for k in taps:
                off = k + shift - r
                acc = acc + wr[off:off + CONV_ROWS, :] * w_ref[k:k + 1, :]
        y = _layer_norm(acc, lg_ref[...], lb_ref[...])
        o_ref[0, pl.ds(t0, CONV_ROWS), :] = y * _sigmoid(y)
        return carry

    lax.fori_loop(0, s // CONV_ROWS, body, 0)


def _conv_mixer(za, w, b, lg, lb):
    bsz, s, _ = za.shape
    vec = pl.BlockSpec((1, GROUP_WIDTH), lambda i: (0, 0))
    return pl.pallas_call(
        _conv_kernel,
        grid=(bsz,),
        in_specs=[pl.BlockSpec((1, s, GROUP_WIDTH), lambda i: (i, 0, COL_CVAL)),
                  pl.BlockSpec((1, s, GROUP_WIDTH), lambda i: (i, 0, COL_CGATE)),
                  pl.BlockSpec((CONV_WIDTH, GROUP_WIDTH), lambda i: (0, 0)),
                  vec, vec, vec],
        out_specs=pl.BlockSpec((1, s, GROUP_WIDTH), lambda i: (i, 0, 0)),
        out_shape=jax.ShapeDtypeStruct((bsz, s, GROUP_WIDTH), F32),
        scratch_shapes=[pltpu.VMEM((CONV_PAD + s, GROUP_WIDTH), F32)],
        compiler_params=_cparams("parallel"),
        name="conv_mixer",
    )(za, za, w, b, lg, lb)


POOL_PAD = 16


def _pool_kernel(p_ref, w_ref, sc_ref, o_ref, pp_ref):
    s = p_ref.shape[1]
    pp_ref[0:POOL_PAD, :] = jnp.zeros((POOL_PAD, GROUP_WIDTH), F32)
    pp_ref[POOL_PAD:POOL_PAD + s, :] = p_ref[0]
    lane = lax.broadcasted_iota(jnp.int32, (1, GROUP_WIDTH), 1)
    grp = lane // POOL_GROUP
    win = jnp.where(grp == 0, POOL_WINDOWS[0],
                    jnp.where(grp == 1, POOL_WINDOWS[1],
                              jnp.where(grp == 2, POOL_WINDOWS[2], POOL_WINDOWS[3])))

    def body(i, carry):
        t0 = pl.multiple_of(i * POOL_ROWS, POOL_ROWS)
        rows = pp_ref[pl.ds(t0, POOL_ROWS + POOL_PAD), :]
        p = rows[POOL_PAD:POOL_PAD + POOL_ROWS, :]
        acc = rows
        sums = []
        width = 1
        for w in POOL_WINDOWS:
            while width < w:
                acc = acc + pltpu.roll(acc, width, axis=0)
                width *= 2
            sums.append(acc[POOL_PAD:POOL_PAD + POOL_ROWS, :])
        total = jnp.where(grp == 0, sums[0],
                          jnp.where(grp == 1, sums[1],
                                    jnp.where(grp == 2, sums[2], sums[3])))
        t = t0 + lax.broadcasted_iota(jnp.int32, (POOL_ROWS, 1), 0)
        count = jnp.minimum(t + 1, win).astype(F32)
        mixed = total / count - p
        y = jnp.dot(mixed.astype(BF16), w_ref[...], preferred_element_type=F32)
        o_ref[0, pl.ds(t0, POOL_ROWS), :] = y * sc_ref[...]
        return carry

    lax.fori_loop(0, s // POOL_ROWS, body, 0, unroll=2)


def _pool_mixer(za, w_bd, scale):
    bsz, s, _ = za.shape
    return pl.pallas_call(
        _pool_kernel,
        grid=(bsz,),
        in_specs=[pl.BlockSpec((1, s, GROUP_WIDTH), lambda i: (i, 0, COL_POOL)),
                  pl.BlockSpec((GROUP_WIDTH, GROUP_WIDTH), lambda i: (0, 0)),
                  pl.BlockSpec((1, GROUP_WIDTH), lambda i: (0, 0))],
        out_specs=pl.BlockSpec((1, s, GROUP_WIDTH), lambda i: (i, 0, 0)),
        out_shape=jax.ShapeDtypeStruct((bsz, s, GROUP_WIDTH), F32),
        scratch_shapes=[pltpu.VMEM((POOL_PAD + s, GROUP_WIDTH), F32)],
        compiler_params=_cparams("parallel"),
        name="pool_mixer",
    )(za, w_bd, scale)


def _sgu_kernel(u_ref, v_ref, lg_ref, lb_ref, w_ref, bias_ref, o_ref):
    s = u_ref.shape[1]
    row = lax.broadcasted_iota(jnp.int32, (SGU_CHUNK, SGU_CHUNK), 0)
    col = lax.broadcasted_iota(jnp.int32, (SGU_CHUNK, SGU_CHUNK), 1)
    head = lax.broadcasted_iota(jnp.int32, (1, GROUP_WIDTH), 1) // (GROUP_WIDTH // SGU_HEADS)

    def body(c, carry):
        t0 = pl.multiple_of(c * SGU_CHUNK, SGU_CHUNK)
        v = _layer_norm(_gelu(v_ref[0, pl.ds(t0, SGU_CHUNK), :]), lg_ref[...], lb_ref[...])
        vb = v.astype(BF16)
        mixed = jnp.zeros((SGU_CHUNK, GROUP_WIDTH), F32)
        for h in range(SGU_HEADS):
            w = jnp.where(row >= col, w_ref[h], 0.0).astype(BF16)
            r = jnp.dot(w, vb, preferred_element_type=F32)
            mixed = jnp.where(head == h, r, mixed)
        u = _gelu(u_ref[0, pl.ds(t0, SGU_CHUNK), :])
        o_ref[0, pl.ds(t0, SGU_CHUNK), :] = u * (mixed + bias_ref[...])
        return carry

    lax.fori_loop(0, s // SGU_CHUNK, body, 0, unroll=2)


def _sgu_mixer(za, lg, lb, w, bias):
    bsz, s, _ = za.shape
    vec = pl.BlockSpec((1, GROUP_WIDTH), lambda i: (0, 0))
    return pl.pallas_call(
        _sgu_kernel,
        grid=(bsz,),
        in_specs=[pl.BlockSpec((1, s, GROUP_WIDTH), lambda i: (i, 0, COL_SGU_U)),
                  pl.BlockSpec((1, s, GROUP_WIDTH), lambda i: (i, 0, COL_SGU_V)),
                  vec, vec,
                  pl.BlockSpec((SGU_HEADS, SGU_CHUNK, SGU_CHUNK), lambda i: (0, 0, 0)),
                  pl.BlockSpec((SGU_CHUNK, GROUP_WIDTH), lambda i: (0, 0))],
        out_specs=pl.BlockSpec((1, s, GROUP_WIDTH), lambda i: (i, 0, 0)),
        out_shape=jax.ShapeDtypeStruct((bsz, s, GROUP_WIDTH), F32),
        compiler_params=_cparams("parallel"),
        name="sgu_mixer",
    )(za, za, lg, lb, w, bias)


def _att_prep_kernel(kcvc_ref, ksx_ref, kwx_ref, zb_ref, cs_ref, cscmp_ref, cos_ref, sin_ref,
                     wa_ref, wb_ref, pe_ref, w1_ref, w2k_ref, w2vt_ref,
                     q_ref, ks_ref, kw_ref, vs_ref, vw_ref, kc_ref, vct_ref, g_ref, sh_ref):
    s = ksx_ref.shape[1]
    n_blk = s // CMP_STRIDE

    def rope_tok(x, cs):
        r = x * cs
        return (r + pltpu.roll(r, HEAD_DIM, axis=1))[:, :HEAD_DIM]

    cs = cs_ref[0]
    ks_ref[0] = rope_tok(ksx_ref[0], cs).astype(BF16)
    kw_ref[0] = rope_tok(kwx_ref[0], cs).astype(BF16)

    cos = cos_ref[0]
    sin = sin_ref[0]
    scale = HEAD_DIM ** -0.5 * LOG2_E
    parts = []
    for h in range(ATT_HEADS):
        r0 = ROW_Q + h * HEAD_DIM
        x1 = zb_ref[0, r0:r0 + ROPE_HALF, :]
        x2 = zb_ref[0, r0 + ROPE_HALF:r0 + ROPE_DIM, :]
        parts += [x1 * cos - x2 * sin, x2 * cos + x1 * sin, zb_ref[0, r0 + ROPE_DIM:r0 + HEAD_DIM, :]]
    q_ref[0] = (jnp.concatenate(parts, axis=0) * scale).astype(BF16)

    for c in range(s // ATT_K):
        vs_ref[0, c] = zb_ref[0, ROW_VS:ROW_VS + HEAD_DIM, c * ATT_K:(c + 1) * ATT_K].astype(BF16)
        vw_ref[0, c] = zb_ref[0, ROW_VW:ROW_VW + HEAD_DIM, c * ATT_K:(c + 1) * ATT_K].astype(BF16)
    g_ref[0] = _sigmoid(zb_ref[0, ROW_G:ROW_G + 16, :])

    acc_a = jnp.zeros((n_blk, LANES), F32)
    acc_b = jnp.zeros((n_blk, LANES), F32)
    for l in range(CMP_STRIDE):
        xl = kcvc_ref[0, pl.ds(l, n_blk, stride=CMP_STRIDE), :].astype(BF16)
        acc_a = acc_a + jnp.dot(xl, wa_ref[l], preferred_element_type=F32)
        acc_b = acc_b + jnp.dot(xl, wb_ref[l], preferred_element_type=F32)
    sh_ref[0:n_blk, :] = acc_b
    sh_ref[n_blk:n_blk + SUBLANES, :] = jnp.zeros((SUBLANES, LANES), F32)
    pe_term = jnp.sum(pe_ref[...] * w1_ref[...], axis=0, keepdims=True)
    hid = _gelu(acc_a + sh_ref[1:n_blk + 1, :] + pe_term).astype(BF16)
    kk = jnp.dot(hid, w2k_ref[...], preferred_element_type=F32)
    kc_ref[0] = rope_tok(kk, cscmp_ref[0]).astype(BF16)
    vct_ref[0] = lax.dot_general(w2vt_ref[...], hid, (((1,), (1,)), ((), ())),
                                 preferred_element_type=F32).astype(BF16)


def _att_prep(za, zb, cs_tok, cs_cmp, cos_t, sin_t, wa, wb, pe, w1, w2k, w2vt):
    bsz, s, _ = za.shape
    n_blk = s // CMP_STRIDE
    nt = s // ATT_K
    tok = lambda c: pl.BlockSpec((1, s, LANES), lambda i: (i, 0, c))
    full3 = lambda shp: pl.BlockSpec(shp, lambda i: (0, 0, 0))
    full2 = lambda shp: pl.BlockSpec(shp, lambda i: (0, 0))
    per_b3 = lambda shp: pl.BlockSpec((1,) + shp, lambda i: (i, 0, 0))
    per_b4 = lambda shp: pl.BlockSpec((1,) + shp, lambda i: (i, 0, 0, 0))
    return pl.pallas_call(
        _att_prep_kernel,
        grid=(bsz,),
        in_specs=[tok(COL_KCVC), tok(COL_KS), tok(COL_KW),
                  per_b3((ZB_ROWS, s)), per_b3((s, LANES)), per_b3((n_blk, LANES)),
                  per_b3((ROPE_HALF, s)), per_b3((ROPE_HALF, s)),
                  full3((CMP_STRIDE, LANES, LANES)), full3((CMP_STRIDE, LANES, LANES)),
                  full2((CMP_BLOCK * HEAD_DIM, LANES)), full2((CMP_BLOCK * HEAD_DIM, LANES)),
                  full2((LANES, LANES)), full2((HEAD_DIM, LANES))],
        out_specs=[per_b3((ATT_HEADS * HEAD_DIM, s)), per_b3((s, HEAD_DIM)), per_b3((s, HEAD_DIM)),
                   per_b4((nt, HEAD_DIM, ATT_K)), per_b4((nt, HEAD_DIM, ATT_K)),
                   per_b3((n_blk, HEAD_DIM)), per_b3((HEAD_DIM, n_blk)), per_b3((16, s))],
        out_shape=[jax.ShapeDtypeStruct((bsz, ATT_HEADS * HEAD_DIM, s), BF16),
                   jax.ShapeDtypeStruct((bsz, s, HEAD_DIM), BF16),
                   jax.ShapeDtypeStruct((bsz, s, HEAD_DIM), BF16),
                   jax.ShapeDtypeStruct((bsz, nt, HEAD_DIM, ATT_K), BF16),
                   jax.ShapeDtypeStruct((bsz, nt, HEAD_DIM, ATT_K), BF16),
                   jax.ShapeDtypeStruct((bsz, n_blk, HEAD_DIM), BF16),
                   jax.ShapeDtypeStruct((bsz, HEAD_DIM, n_blk), BF16),
                   jax.ShapeDtypeStruct((bsz, 16, s), F32)],
        scratch_shapes=[pltpu.VMEM((n_blk + SUBLANES, LANES), F32)],
        compiler_params=_cparams("parallel"),
        name="att_prep",
    )(za, za, za, zb, cs_tok, cs_cmp, cos_t, sin_t, wa, wb, pe, w1, w2k, w2vt)


def _att_kernel(q_ref, ks_ref, kw_ref, vs_ref, vw_ref, kc_ref, vct_ref, g_ref, ov_ref,
                o_ref, sel_ref):
    n_blk = kc_ref.shape[1]
    n_slc = ov_ref.shape[0]
    tq = ATT_Q
    wide = ATT_HEADS * tq
    i = pl.program_id(1)
    t0 = i * tq

    q = q_ref[0]
    qs = jnp.concatenate([q[h * HEAD_DIM:(h + 1) * HEAD_DIM] for h in range(ATT_HEADS)], axis=1)
    lane_w = lax.broadcasted_iota(jnp.int32, (1, wide), 1)
    t_w = t0 + (lane_w & (tq - 1))
    t_q = t0 + lax.broadcasted_iota(jnp.int32, (1, tq), 1)
    heads = lambda a: jnp.concatenate([a] * ATT_HEADS, axis=1)

    s_c = jnp.dot(kc_ref[0], qs, preferred_element_type=F32)
    blk_end = lax.broadcasted_iota(jnp.int32, (n_blk, 1), 0) * CMP_STRIDE + (CMP_BLOCK - 1)
    cmask = blk_end <= t_w
    s_m = jnp.where(cmask, s_c, NEG_INF)
    e = jnp.exp2(s_m - jnp.max(s_m, axis=0, keepdims=True))
    p_c = jnp.where(cmask, e * (1.0 / jnp.sum(e, axis=0, keepdims=True)), 0.0)
    o_cmp = jnp.dot(vct_ref[0], p_c.astype(BF16), preferred_element_type=F32)

    p_sum = p_c[:, 0:tq]
    for h in range(1, ATT_HEADS):
        p_sum = p_sum + p_c[:, h * tq:(h + 1) * tq]
    p_hi = p_sum.astype(BF16)
    p_lo = (p_sum - p_hi.astype(F32)).astype(BF16)
    imp = (jnp.dot(ov_ref[...], p_hi, preferred_element_type=F32)
           + jnp.dot(ov_ref[...], p_lo, preferred_element_type=F32))
    j = lax.broadcasted_iota(jnp.int32, (n_slc, 1), 0)
    back = t_q // SLC_BLOCK - j
    forced = (j == 0) | ((back >= 0) & (back < N_LOCAL))
    imp = jnp.where(forced, FORCE_SCORE, jnp.where(back < 0, -1.0, imp))
    rank = jnp.zeros((n_slc, tq), F32)
    for r in range(n_slc):
        row = imp[r:r + 1, :]
        ahead = (row > imp) | ((row == imp) & (j > r))
        rank = rank + jnp.where(ahead, 1.0, 0.0)
    sel_bias = jnp.where(rank < min(N_SELECT, n_slc), 0.0, -jnp.inf)
    per_tile = ATT_K // SLC_BLOCK
    for kt in range(n_slc // per_tile):
        sel_ref[kt, 0:per_tile, :] = sel_bias[kt * per_tile:(kt + 1) * per_tile, :]

    k_iota = lax.broadcasted_iota(jnp.int32, (ATT_K, 1), 0)

    def block_bias(kt):
        st = sel_ref[kt, 0:per_tile, :]
        rows = [jnp.broadcast_to(st[b:b + 1, :], (SLC_BLOCK, tq)) for b in range(per_tile)]
        return jnp.concatenate(rows, axis=0)

    def attend(k_tile, v_tile, bias, carry):
        m, l, acc = carry
        s_t = jnp.dot(k_tile, qs, preferred_element_type=F32) + heads(bias)
        m_new = jnp.maximum(m, jnp.max(s_t, axis=0, keepdims=True))
        alpha = jnp.exp2(m - m_new)
        p = jnp.exp2(s_t - m_new)
        l = alpha * l + jnp.sum(p, axis=0, keepdims=True)
        acc = alpha * acc + jnp.dot(v_tile, p.astype(BF16), preferred_element_type=F32)
        return m_new, l, acc

    init = (jnp.full((1, wide), NEG_INF, F32), jnp.zeros((1, wide), F32),
            jnp.zeros((HEAD_DIM, wide), F32))

    def slc_body(kt, carry):
        k0 = pl.multiple_of(kt * ATT_K, ATT_K)
        return attend(ks_ref[0, pl.ds(k0, ATT_K), :], vs_ref[0, kt], block_bias(kt), carry)

    carry = lax.fori_loop(0, i, slc_body, init)
    kd = pl.multiple_of(i * ATT_K, ATT_K)
    diag_bias = jnp.where((kd + k_iota) <= t_q, block_bias(i), -jnp.inf)
    _, l_s, acc_s = attend(ks_ref[0, pl.ds(kd, ATT_K), :], vs_ref[0, i], diag_bias, carry)

    n_win = WINDOW // ATT_K + 1
    first = jnp.maximum(i - (n_win - 1), 0)
    kw0 = pl.multiple_of(first * ATT_K, ATT_K)
    s_w = jnp.dot(kw_ref[0, pl.ds(kw0, n_win * ATT_K), :], qs, preferred_element_type=F32)
    diff = t_q - (kw0 + lax.broadcasted_iota(jnp.int32, (n_win * ATT_K, 1), 0))
    s_w = s_w + heads(jnp.where((diff >= 0) & (diff < WINDOW), 0.0, -jnp.inf))
    p_w = jnp.exp2(s_w - jnp.max(s_w, axis=0, keepdims=True))
    l_w = jnp.sum(p_w, axis=0, keepdims=True)
    p_w = p_w.astype(BF16)
    acc_w = None
    for n in range(n_win):
        part = jnp.dot(vw_ref[0, first + n], p_w[n * ATT_K:(n + 1) * ATT_K], preferred_element_type=F32)
        acc_w = part if acc_w is None else acc_w + part

    g = g_ref[0]
    gate = lambda br: jnp.concatenate(
        [g[h * N_BRANCH + br:h * N_BRANCH + br + 1, :] for h in range(ATT_HEADS)], axis=1)
    o = gate(0) * o_cmp + (gate(1) * (1.0 / l_s)) * acc_s + (gate(2) * (1.0 / l_w)) * acc_w
    o = jnp.concatenate([o[:, h * tq:(h + 1) * tq] for h in range(ATT_HEADS)], axis=0)
    o_ref[0] = o.T


def _att(q_t, ks, kw, vs_t, vw_t, kc, vc_t, g_t, ov_t):
    bsz, _, s = q_t.shape
    n_blk = kc.shape[1]
    nt = s // ATT_K
    n_slc = s // SLC_BLOCK
    per_b3 = lambda shp: pl.BlockSpec((1,) + shp, lambda b, i: (b, 0, 0))
    per_b4 = lambda shp: pl.BlockSpec((1,) + shp, lambda b, i: (b, 0, 0, 0))
    return pl.pallas_call(
        _att_kernel,
        grid=(bsz, s // ATT_Q),
        in_specs=[pl.BlockSpec((1, ATT_HEADS * HEAD_DIM, ATT_Q), lambda b, i: (b, 0, i)),
                  per_b3((s, HEAD_DIM)), per_b3((s, HEAD_DIM)),
                  per_b4((nt, HEAD_DIM, ATT_K)), per_b4((nt, HEAD_DIM, ATT_K)),
                  per_b3((n_blk, HEAD_DIM)), per_b3((HEAD_DIM, n_blk)),
                  pl.BlockSpec((1, 16, ATT_Q), lambda b, i: (b, 0, i)),
                  pl.BlockSpec((n_slc, n_blk), lambda b, i: (0, 0))],
        out_specs=pl.BlockSpec((1, ATT_Q, GROUP_WIDTH), lambda b, i: (b, i, 0)),
        out_shape=jax.ShapeDtypeStruct((bsz, s, GROUP_WIDTH), F32),
        scratch_shapes=[pltpu.VMEM((nt, SUBLANES, ATT_Q), F32)],
        compiler_params=_cparams("parallel", "arbitrary"),
        name="sparse_attention",
    )(q_t, ks, kw, vs_t, vw_t, kc, vc_t, g_t, ov_t)


def _out_ffn_kernel(x_ref, y0_ref, y1_ref, y2_ref, y3_ref, wo_ref, gm_ref, g1_ref, w1_ref, w2_ref,
                    g2_ref, o_ref):
    acc = None
    for n, y_ref in enumerate((y0_ref, y1_ref, y2_ref, y3_ref)):
        part = jnp.dot(y_ref[...].astype(BF16), wo_ref[n * GROUP_WIDTH:(n + 1) * GROUP_WIDTH, :],
                       preferred_element_type=F32)
        acc = part if acc is None else acc + part
    x1 = x_ref[...] + _rms_norm(acc, gm_ref[...])
    h = _rms_norm(x1, g1_ref[...]).astype(BF16)
    acc = None
    for c in range(w1_ref.shape[1] // FFN_COLS):
        cols = slice(c * FFN_COLS, (c + 1) * FFN_COLS)
        f = jnp.maximum(jnp.dot(h, w1_ref[:, cols], preferred_element_type=F32), 0.0)
        part = jnp.dot((f * f).astype(BF16), w2_ref[cols, :], preferred_element_type=F32)
        acc = part if acc is None else acc + part
    o_ref[...] = x1 + _rms_norm(acc, g2_ref[...])


def _out_ffn(x2, ys, wo, gm, g1, w1, w2, g2):
    t, d = x2.shape
    dff = w1.shape[1]
    tm = min(FFN_ROWS, t)
    yspec = pl.BlockSpec((tm, GROUP_WIDTH), lambda i: (i, 0))
    vec = pl.BlockSpec((1, d), lambda i: (0, 0))
    once = lambda shp: pl.BlockSpec(shp, lambda i: (0, 0), pipeline_mode=pl.Buffered(1))
    return pl.pallas_call(
        _out_ffn_kernel,
        grid=(t // tm,),
        in_specs=[pl.BlockSpec((tm, d), lambda i: (i, 0)), yspec, yspec, yspec, yspec,
                  once((d, d)), vec, vec, once((d, dff)), once((dff, d)), vec],
        out_specs=pl.BlockSpec((tm, d), lambda i: (i, 0)),
        out_shape=jax.ShapeDtypeStruct((t, d), F32),
        compiler_params=_cparams("parallel"),
        name="out_ffn",
    )(x2, *ys, wo, gm, g1, w1, w2, g2)


def _rope_perm():
    idx = list(range(HEAD_DIM))
    for c in range(ROPE_HALF):
        idx[c], idx[c + ROPE_HALF] = c + ROPE_HALF, c
    return jnp.array(idx, jnp.int32)


def _split_w_in(w_in):
    gw = GROUP_WIDTH
    edges = [0, gw, 2 * gw, 3 * gw, 4 * gw, 5 * gw, 6 * gw]
    for _ in range(6):
        edges.append(edges[-1] + HEAD_DIM)
    edges.append(edges[-1] + ATT_HEADS * N_BRANCH)
    names = ("cval", "cgate", "pool", "su", "sv", "q", "kc", "vc", "ks", "vs", "kw", "vw", "g")
    return {n: w_in[:, edges[k]:edges[k + 1]] for k, n in enumerate(names)}


def _layer_weights(p):
    perm = _rope_perm()
    w = _split_w_in(p["w_in"])
    wa = jnp.concatenate([w["cval"], w["cgate"], w["pool"], w["su"], w["sv"], w["kc"], w["vc"],
                          w["ks"], w["ks"][:, perm], w["kw"], w["kw"][:, perm]], axis=1).astype(BF16)
    pad = jnp.zeros((D_MODEL, ZB_ROWS - ROW_G - ATT_HEADS * N_BRANCH), F32)
    wb = jnp.concatenate([w["q"], w["vs"], w["vw"], w["g"], pad], axis=1).T.astype(BF16)

    pool_bd = jnp.zeros((GROUP_WIDTH, GROUP_WIDTH), F32)
    for gi in range(len(POOL_WINDOWS)):
        sl = slice(gi * POOL_GROUP, (gi + 1) * POOL_GROUP)
        pool_bd = pool_bd.at[sl, sl].set(p["pool_w"][gi])
    sgu_bias = jnp.repeat(jnp.transpose(p["sgu_b"]), GROUP_WIDTH // SGU_HEADS, axis=1)

    w1k = p["cmp_k_w1"].reshape(CMP_BLOCK, HEAD_DIM, HEAD_DIM)
    w1v = p["cmp_v_w1"].reshape(CMP_BLOCK, HEAD_DIM, HEAD_DIM)
    z = jnp.zeros((CMP_STRIDE, HEAD_DIM, HEAD_DIM), F32)

    def kv_diag(a, b):
        return jnp.concatenate([jnp.concatenate([a, z], axis=2), jnp.concatenate([z, b], axis=2)], axis=1)

    cmp_wa = kv_diag(w1k[:CMP_STRIDE], w1v[:CMP_STRIDE]).astype(BF16)
    cmp_wb = kv_diag(w1k[CMP_STRIDE:], w1v[CMP_STRIDE:]).astype(BF16)
    ones = jnp.ones((1, HEAD_DIM), F32)
    cmp_pe = jnp.concatenate([p["cmp_k_pe"].reshape(-1, 1) * ones, p["cmp_v_pe"].reshape(-1, 1) * ones], axis=1)
    cmp_w1 = jnp.concatenate([p["cmp_k_w1"], p["cmp_v_w1"]], axis=1)
    zk = jnp.zeros((HEAD_DIM, LANES), F32)
    cmp_w2k = jnp.concatenate([jnp.concatenate([p["cmp_k_w2"], p["cmp_k_w2"][:, perm]], axis=1), zk], axis=0).astype(BF16)
    cmp_w2vt = jnp.concatenate([jnp.zeros((HEAD_DIM, HEAD_DIM), F32), p["cmp_v_w2"].T], axis=1).astype(BF16)

    row = lambda v: v.reshape(1, -1)
    return dict(
        wa=wa, wb=wb, pre_mix=row(p["pre_mix_norm"]), post_mix=row(p["post_mix_norm"]),
        pre_ffn=row(p["pre_ffn_norm"]), post_ffn=row(p["post_ffn_norm"]),
        conv_w=p["conv_w"], conv_b=row(p["conv_b"]), conv_lg=row(p["conv_ln_g"]), conv_lb=row(p["conv_ln_b"]),
        pool_bd=pool_bd.astype(BF16), pool_scale=row(p["pool_scale"]),
        sgu_lg=row(p["sgu_ln_g"]), sgu_lb=row(p["sgu_ln_b"]), sgu_w=p["sgu_w"], sgu_bias=sgu_bias,
        cmp_wa=cmp_wa, cmp_wb=cmp_wb, cmp_pe=cmp_pe, cmp_w1=cmp_w1, cmp_w2k=cmp_w2k, cmp_w2vt=cmp_w2vt,
        w_out=p["w_out"].astype(BF16), ffn_w1=p["ffn_w1"].astype(BF16), ffn_w2=p["ffn_w2"].astype(BF16),
    )


def _overlap_t(s):
    n_blk = s // CMP_STRIDE
    n_slc = s // SLC_BLOCK
    bs = jnp.arange(n_blk)[None, :] * CMP_STRIDE
    ss = jnp.arange(n_slc)[:, None] * SLC_BLOCK
    ov = jnp.clip(jnp.minimum(bs + CMP_BLOCK, ss + SLC_BLOCK) - jnp.maximum(bs, ss), 0)
    return (ov.astype(F32) / CMP_STRIDE).astype(BF16)


def _rope_tables(positions):
    bsz, s = positions.shape
    inv = ROPE_THETA ** (-jnp.arange(ROPE_HALF, dtype=F32) * 2.0 / ROPE_DIM)
    inv64 = jnp.concatenate([inv, inv, jnp.zeros((HEAD_DIM - ROPE_DIM,), F32)])
    inv128 = jnp.concatenate([inv64, inv64]).reshape(1, LANES)
    sgn64 = jnp.concatenate([-jnp.ones((ROPE_HALF,), F32), jnp.ones((ROPE_HALF,), F32),
                             jnp.zeros((HEAD_DIM - ROPE_DIM,), F32)])
    sgn128 = jnp.concatenate([sgn64, sgn64]).reshape(1, LANES)
    posf = positions.astype(F32)
    n_blk = s // CMP_STRIDE
    pos_end = posf[:, CMP_BLOCK - 1::CMP_STRIDE]
    pos_end = jnp.pad(pos_end, ((0, 0), (0, n_blk - pos_end.shape[1])))
    cs_tok = _rope_tok_table(posf[:, :, None], inv128, sgn128)
    cs_cmp = _rope_tok_table(pos_end[:, :, None], inv128, sgn128)
    cos_t, sin_t = _rope_chan_table(posf[:, None, :], inv.reshape(ROPE_HALF, 1))
    return cs_tok, cs_cmp, cos_t, sin_t


def _mixers(x, lw, tables, ov_t):
    cs_tok, cs_cmp, cos_t, sin_t = tables
    za, zb = _proj(x, lw["pre_mix"], lw["wa"], lw["wb"])
    y_conv = _conv_mixer(za, lw["conv_w"], lw["conv_b"], lw["conv_lg"], lw["conv_lb"])
    y_pool = _pool_mixer(za, lw["pool_bd"], lw["pool_scale"])
    y_sgu = _sgu_mixer(za, lw["sgu_lg"], lw["sgu_lb"], lw["sgu_w"], lw["sgu_bias"])
    prep = _att_prep(za, zb, cs_tok, cs_cmp, cos_t, sin_t, lw["cmp_wa"], lw["cmp_wb"],
                     lw["cmp_pe"], lw["cmp_w1"], lw["cmp_w2k"], lw["cmp_w2vt"])
    y_att = _att(*prep, ov_t)
    return y_conv, y_pool, y_sgu, y_att


def kernel(x, positions, pre_mix_norm, post_mix_norm, pre_ffn_norm, post_ffn_norm, w_in, conv_w, conv_b, conv_ln_g, conv_ln_b, pool_w, pool_scale, sgu_ln_g, sgu_ln_b, sgu_w, sgu_b, cmp_k_pe, cmp_k_w1, cmp_k_w2, cmp_v_pe, cmp_v_w1, cmp_v_w2, w_out, ffn_w1, ffn_w2):
    params = dict(pre_mix_norm=pre_mix_norm, post_mix_norm=post_mix_norm, pre_ffn_norm=pre_ffn_norm,
                  post_ffn_norm=post_ffn_norm, w_in=w_in, conv_w=conv_w, conv_b=conv_b,
                  conv_ln_g=conv_ln_g, conv_ln_b=conv_ln_b, pool_w=pool_w, pool_scale=pool_scale,
                  sgu_ln_g=sgu_ln_g, sgu_ln_b=sgu_ln_b, sgu_w=sgu_w, sgu_b=sgu_b,
                  cmp_k_pe=cmp_k_pe, cmp_k_w1=cmp_k_w1, cmp_k_w2=cmp_k_w2, cmp_v_pe=cmp_v_pe,
                  cmp_v_w1=cmp_v_w1, cmp_v_w2=cmp_v_w2, w_out=w_out, ffn_w1=ffn_w1, ffn_w2=ffn_w2)
    bsz, s, d = x.shape
    depth = w_in.shape[0]
    tables = _rope_tables(positions)
    ov_t = _overlap_t(s)
    for l in range(depth):
        lw = _layer_weights({k: v[l] for k, v in params.items()})
        ys = _mixers(x, lw, tables, ov_t)
        x2 = _out_ffn(x.reshape(bsz * s, d), [y.reshape(bsz * s, GROUP_WIDTH) for y in ys],
                      lw["w_out"], lw["post_mix"], lw["pre_ffn"], lw["ffn_w1"], lw["ffn_w2"], lw["post_ffn"])
        x = x2.reshape(bsz, s, d)
    return x
```

```python
import functools

import jax
import jax.numpy as jnp
from jax import lax
from jax.experimental import pallas as pl
from jax.experimental.pallas import tpu as pltpu

F32 = jnp.float32
BF16 = jnp.bfloat16

D_MODEL = 1024
GROUP_WIDTH = 256
CONV_WIDTH = 31
POOL_WINDOWS = (2, 4, 8, 16)
POOL_GROUP = 64
SGU_HEADS = 4
SGU_CHUNK = 128
ATT_HEADS = 4
HEAD_DIM = 64
ROPE_DIM = 16
ROPE_HALF = 8
ROPE_THETA = 500000.0
CMP_BLOCK = 32
CMP_STRIDE = 16
SLC_BLOCK = 64
N_SELECT = 8
N_LOCAL = 2
WINDOW = 512
N_BRANCH = 3
D_FF = 4096
NORM_EPS = 1e-6
NEG_INF = -1e30
FORCE_SCORE = 1e9

LANES = 128
SUBLANES = 8
VMEM_LIMIT_BYTES = 56 * 1024 * 1024

PROJ_ROWS = 512
FFN_ROWS = 512
FFN_COLS = 512
CONV_ROWS = 64
POOL_ROWS = 64
ATT_Q = 256
ATT_K = 256
LOG2_E = 1.4426950408889634
V_ROWS = 80
MASK_BIAS = -2e30

COL_CVAL, COL_CGATE, COL_POOL, COL_SGU_U, COL_SGU_V = 0, 1, 2, 3, 4
COL_KCVC, COL_KS, COL_KW = 10, 11, 12
ZA_WIDTH = 1664
ROW_Q, ROW_VS, ROW_VW, ROW_G = 0, 256, 320, 384
ZB_ROWS = 400


def _cparams(*sem):
    return pltpu.CompilerParams(dimension_semantics=sem, vmem_limit_bytes=VMEM_LIMIT_BYTES)


def _gelu(x):
    return 0.5 * x * (1.0 + jnp.tanh(0.7978845608028654 * (x + 0.044715 * (x * x * x))))


def _sigmoid(x):
    return 1.0 / (1.0 + jnp.exp(-x))


def _layer_norm(x, g, b):
    mu = jnp.mean(x, axis=-1, keepdims=True)
    d = x - mu
    var = jnp.mean(d * d, axis=-1, keepdims=True)
    return d * lax.rsqrt(var + NORM_EPS) * g + b


def _rms_norm(x, g):
    return x * lax.rsqrt(jnp.mean(x * x, axis=-1, keepdims=True) + NORM_EPS) * g


def _rope_tok_kernel(pos_ref, inv_ref, sgn_ref, cs_ref):
    ang = pos_ref[0] * inv_ref[...]
    lane = lax.broadcasted_iota(jnp.int32, ang.shape, 1)
    cs_ref[0] = jnp.where(lane < HEAD_DIM, jnp.cos(ang), jnp.sin(ang) * sgn_ref[...])


def _rope_tok_table(pos_col, inv128, sgn128):
    b, n, _ = pos_col.shape
    return pl.pallas_call(
        _rope_tok_kernel,
        grid=(b,),
        in_specs=[pl.BlockSpec((1, n, 1), lambda i: (i, 0, 0)),
                  pl.BlockSpec((1, LANES), lambda i: (0, 0)),
                  pl.BlockSpec((1, LANES), lambda i: (0, 0))],
        out_specs=pl.BlockSpec((1, n, LANES), lambda i: (i, 0, 0)),
        out_shape=jax.ShapeDtypeStruct((b, n, LANES), F32),
        compiler_params=_cparams("parallel"),
        name="rope_tok_table",
    )(pos_col, inv128, sgn128)


def _rope_chan_kernel(pos_ref, inv_ref, cos_ref, sin_ref):
    ang = inv_ref[...] * pos_ref[0]
    cos_ref[0] = jnp.cos(ang)
    sin_ref[0] = jnp.sin(ang)


def _rope_chan_table(pos_row, inv_col):
    b, _, s = pos_row.shape
    spec = pl.BlockSpec((1, ROPE_HALF, s), lambda i: (i, 0, 0))
    return pl.pallas_call(
        _rope_chan_kernel,
        grid=(b,),
        in_specs=[pl.BlockSpec((1, 1, s), lambda i: (i, 0, 0)),
                  pl.BlockSpec((ROPE_HALF, 1), lambda i: (0, 0))],
        out_specs=[spec, spec],
        out_shape=[jax.ShapeDtypeStruct((b, ROPE_HALF, s), F32)] * 2,
        compiler_params=_cparams("parallel"),
        name="rope_chan_table",
    )(pos_row, inv_col)


def _proj_kernel(x_ref, g_ref, wa_ref, wb_ref, za_ref, zb_ref):
    h = _rms_norm(x_ref[0], g_ref[...]).astype(BF16)
    za_ref[0] = jnp.dot(h, wa_ref[...], preferred_element_type=F32)
    zb_ref[0] = lax.dot_general(wb_ref[...], h, (((1,), (1,)), ((), ())),
                                preferred_element_type=F32)


def _proj(x, g, wa, wb):
    b, s, d = x.shape
    tm = min(PROJ_ROWS, s)
    return pl.pallas_call(
        _proj_kernel,
        grid=(b, s // tm),
        in_specs=[pl.BlockSpec((1, tm, d), lambda i, j: (i, j, 0)),
                  pl.BlockSpec((1, d), lambda i, j: (0, 0)),
                  pl.BlockSpec((d, ZA_WIDTH), lambda i, j: (0, 0)),
                  pl.BlockSpec((ZB_ROWS, d), lambda i, j: (0, 0))],
        out_specs=[pl.BlockSpec((1, tm, ZA_WIDTH), lambda i, j: (i, j, 0)),
                   pl.BlockSpec((1, ZB_ROWS, tm), lambda i, j: (i, 0, j))],
        out_shape=[jax.ShapeDtypeStruct((b, s, ZA_WIDTH), F32),
                   jax.ShapeDtypeStruct((b, ZB_ROWS, s), F32)],
        compiler_params=_cparams("parallel", "parallel"),
        name="in_proj",
    )(x, g, wa, wb)


CONV_PAD = 32


def _conv_kernel(a_ref, gate_ref, w_ref, b_ref, lg_ref, lb_ref, o_ref, hp_ref):
    s = a_ref.shape[1]
    hp_ref[0:CONV_PAD, :] = jnp.zeros((CONV_PAD, GROUP_WIDTH), F32)
    hp_ref[CONV_PAD:CONV_PAD + s, :] = a_ref[0] * _sigmoid(gate_ref[0])
    shift = CONV_PAD - (CONV_WIDTH - 1)

    def body(i, carry):
        t0 = pl.multiple_of(i * CONV_ROWS, CONV_ROWS)
        acc = jnp.zeros((CONV_ROWS, GROUP_WIDTH), F32) + b_ref[...]
        win = hp_ref[pl.ds(t0, CONV_ROWS + CONV_PAD), :]
        n_win = CONV_ROWS + CONV_PAD
        for r in range(SUBLANES):
            taps = [k for k in range(CONV_WIDTH) if (k + shift) % SUBLANES == r]
            wr = win if r == 0 else pltpu.roll(win, n_win - r, axis=0)
            for k in taps:
                off = k + shift - r
                acc = acc + wr[off:off + CONV_ROWS, :] * w_ref[k:k + 1, :]
        y = _layer_norm(acc, lg_ref[...], lb_ref[...])
        o_ref[0, pl.ds(t0, CONV_ROWS), :] = y * _sigmoid(y)
        return carry

    lax.fori_loop(0, s // CONV_ROWS, body, 0)


def _conv_mixer(za, w, b, lg, lb):
    bsz, s, _ = za.shape
    vec = pl.BlockSpec((1, GROUP_WIDTH), lambda i: (0, 0))
    return pl.pallas_call(
        _conv_kernel,
        grid=(bsz,),
        in_specs=[pl.BlockSpec((1, s, GROUP_WIDTH), lambda i: (i, 0, COL_CVAL)),
                  pl.BlockSpec((1, s, GROUP_WIDTH), lambda i: (i, 0, COL_CGATE)),
                  pl.BlockSpec((CONV_WIDTH, GROUP_WIDTH), lambda i: (0, 0)),
                  vec, vec, vec],
        out_specs=pl.BlockSpec((1, s, GROUP_WIDTH), lambda i: (i, 0, 0)),
        out_shape=jax.ShapeDtypeStruct((bsz, s, GROUP_WIDTH), F32),
        scratch_shapes=[pltpu.VMEM((CONV_PAD + s, GROUP_WIDTH), F32)],
        compiler_params=_cparams("parallel"),
        name="conv_mixer",
    )(za, za, w, b, lg, lb)


POOL_PAD = 16


def _pool_kernel(p_ref, w_ref, sc_ref, o_ref, pp_ref):
    s = p_ref.shape[1]
    pp_ref[0:POOL_PAD, :] = jnp.zeros((POOL_PAD, GROUP_WIDTH), F32)
    pp_ref[POOL_PAD:POOL_PAD + s, :] = p_ref[0]
    lane = lax.broadcasted_iota(jnp.int32, (1, GROUP_WIDTH), 1)
    grp = lane // POOL_GROUP
    win = jnp.where(grp == 0, POOL_WINDOWS[0],
                    jnp.where(grp == 1, POOL_WINDOWS[1],
                              jnp.where(grp == 2, POOL_WINDOWS[2], POOL_WINDOWS[3])))

    def body(i, carry):
        t0 = pl.multiple_of(i * POOL_ROWS, POOL_ROWS)
        rows = pp_ref[pl.ds(t0, POOL_ROWS + POOL_PAD), :]
        p = rows[POOL_PAD:POOL_PAD + POOL_ROWS, :]
        acc = rows
        sums = []
        width = 1
        for w in POOL_WINDOWS:
            while width < w:
                acc = acc + pltpu.roll(acc, width, axis=0)
                width *= 2
            sums.append(acc[POOL_PAD:POOL_PAD + POOL_ROWS, :])
        total = jnp.where(grp == 0, sums[0],
                          jnp.where(grp == 1, sums[1],
                                    jnp.where(grp == 2, sums[2], sums[3])))
        t = t0 + lax.broadcasted_iota(jnp.int32, (POOL_ROWS, 1), 0)
        count = jnp.minimum(t + 1, win).astype(F32)
        mixed = total / count - p
        y = jnp.dot(mixed.astype(BF16), w_ref[...], preferred_element_type=F32)
        o_ref[0, pl.ds(t0, POOL_ROWS), :] = y * sc_ref[...]
        return carry

    lax.fori_loop(0, s // POOL_ROWS, body, 0, unroll=2)


def _pool_mixer(za, w_bd, scale):
    bsz, s, _ = za.shape
    return pl.pallas_call(
        _pool_kernel,
        grid=(bsz,),
        in_specs=[pl.BlockSpec((1, s, GROUP_WIDTH), lambda i: (i, 0, COL_POOL)),
                  pl.BlockSpec((GROUP_WIDTH, GROUP_WIDTH), lambda i: (0, 0)),
                  pl.BlockSpec((1, GROUP_WIDTH), lambda i: (0, 0))],
        out_specs=pl.BlockSpec((1, s, GROUP_WIDTH), lambda i: (i, 0, 0)),
        out_shape=jax.ShapeDtypeStruct((bsz, s, GROUP_WIDTH), F32),
        scratch_shapes=[pltpu.VMEM((POOL_PAD + s, GROUP_WIDTH), F32)],
        compiler_params=_cparams("parallel"),
        name="pool_mixer",
    )(za, w_bd, scale)


def _sgu_kernel(u_ref, v_ref, lg_ref, lb_ref, w_ref, bias_ref, o_ref):
    s = u_ref.shape[1]
    row = lax.broadcasted_iota(jnp.int32, (SGU_CHUNK, SGU_CHUNK), 0)
    col = lax.broadcasted_iota(jnp.int32, (SGU_CHUNK, SGU_CHUNK), 1)
    head = lax.broadcasted_iota(jnp.int32, (1, GROUP_WIDTH), 1) // (GROUP_WIDTH // SGU_HEADS)

    def body(c, carry):
        t0 = pl.multiple_of(c * SGU_CHUNK, SGU_CHUNK)
        v = _layer_norm(_gelu(v_ref[0, pl.ds(t0, SGU_CHUNK), :]), lg_ref[...], lb_ref[...])
        vb = v.astype(BF16)
        mixed = jnp.zeros((SGU_CHUNK, GROUP_WIDTH), F32)
        for h in range(SGU_HEADS):
            w = jnp.where(row >= col, w_ref[h], 0.0).astype(BF16)
            r = jnp.dot(w, vb, preferred_element_type=F32)
            mixed = jnp.where(head == h, r, mixed)
        u = _gelu(u_ref[0, pl.ds(t0, SGU_CHUNK), :])
        o_ref[0, pl.ds(t0, SGU_CHUNK), :] = u * (mixed + bias_ref[...])
        return carry

    lax.fori_loop(0, s // SGU_CHUNK, body, 0, unroll=2)


def _sgu_mixer(za, lg, lb, w, bias):
    bsz, s, _ = za.shape
    vec = pl.BlockSpec((1, GROUP_WIDTH), lambda i: (0, 0))
    return pl.pallas_call(
        _sgu_kernel,
        grid=(bsz,),
        in_specs=[pl.BlockSpec((1, s, GROUP_WIDTH), lambda i: (i, 0, COL_SGU_U)),
                  pl.BlockSpec((1, s, GROUP_WIDTH), lambda i: (i, 0, COL_SGU_V)),
                  vec, vec,
                  pl.BlockSpec((SGU_HEADS, SGU_CHUNK, SGU_CHUNK), lambda i: (0, 0, 0)),
                  pl.BlockSpec((SGU_CHUNK, GROUP_WIDTH), lambda i: (0, 0))],
        out_specs=pl.BlockSpec((1, s, GROUP_WIDTH), lambda i: (i, 0, 0)),
        out_shape=jax.ShapeDtypeStruct((bsz, s, GROUP_WIDTH), F32),
        compiler_params=_cparams("parallel"),
        name="sgu_mixer",
    )(za, za, lg, lb, w, bias)


def _att_prep_kernel(kcvc_ref, ksx_ref, kwx_ref, zb_ref, cs_ref, cscmp_ref, cos_ref, sin_ref,
                     wa_ref, wb_ref, pe_ref, w1_ref, w2k_ref, w2vt_ref,
                     q_ref, ks_ref, kw_ref, vs_ref, vw_ref, kc_ref, vct_ref, g_ref, sh_ref):
    s = ksx_ref.shape[1]
    n_blk = s // CMP_STRIDE

    def rope_wide(x, cs):
        r = x * cs
        return r + pltpu.roll(r, HEAD_DIM, axis=1)

    def rope_tok(x, cs):
        return rope_wide(x, cs)[:, :HEAD_DIM]

    cs = cs_ref[0]
    lane = lax.broadcasted_iota(jnp.int32, (s, LANES), 1)
    blk = lax.broadcasted_iota(jnp.int32, (s, LANES), 0) // SLC_BLOCK
    ks_ref[0] = jnp.where(lane < HEAD_DIM, rope_wide(ksx_ref[0], cs),
                          jnp.where(lane - HEAD_DIM == blk, 1.0, 0.0)).astype(BF16)
    kw_ref[0] = rope_tok(kwx_ref[0], cs).astype(BF16)

    cos = cos_ref[0]
    sin = sin_ref[0]
    scale = HEAD_DIM ** -0.5 * LOG2_E
    parts = []
    for h in range(ATT_HEADS):
        r0 = ROW_Q + h * HEAD_DIM
        x1 = zb_ref[0, r0:r0 + ROPE_HALF, :]
        x2 = zb_ref[0, r0 + ROPE_HALF:r0 + ROPE_DIM, :]
        parts += [x1 * cos - x2 * sin, x2 * cos + x1 * sin, zb_ref[0, r0 + ROPE_DIM:r0 + HEAD_DIM, :]]
    q_ref[0] = (jnp.concatenate(parts, axis=0) * scale).astype(BF16)

    ones_row = jnp.where(lax.broadcasted_iota(jnp.int32, (V_ROWS - HEAD_DIM, ATT_K), 0) == 0, 1.0, 0.0)
    for c in range(s // ATT_K):
        cols = slice(c * ATT_K, (c + 1) * ATT_K)
        vs_ref[0, c] = jnp.concatenate([zb_ref[0, ROW_VS:ROW_VS + HEAD_DIM, cols], ones_row], axis=0).astype(BF16)
        vw_ref[0, c] = jnp.concatenate([zb_ref[0, ROW_VW:ROW_VW + HEAD_DIM, cols], ones_row], axis=0).astype(BF16)
    g_ref[0] = _sigmoid(zb_ref[0, ROW_G:ROW_G + 16, :])

    acc_a = jnp.zeros((n_blk, LANES), F32)
    acc_b = jnp.zeros((n_blk, LANES), F32)
    for l in range(CMP_STRIDE):
        xl = kcvc_ref[0, pl.ds(l, n_blk, stride=CMP_STRIDE), :].astype(BF16)
        acc_a = acc_a + jnp.dot(xl, wa_ref[l], preferred_element_type=F32)
        acc_b = acc_b + jnp.dot(xl, wb_ref[l], preferred_element_type=F32)
    sh_ref[0:n_blk, :] = acc_b
    sh_ref[n_blk:n_blk + SUBLANES, :] = jnp.zeros((SUBLANES, LANES), F32)
    pe_term = jnp.sum(pe_ref[...] * w1_ref[...], axis=0, keepdims=True)
    hid = _gelu(acc_a + sh_ref[1:n_blk + 1, :] + pe_term).astype(BF16)
    kk = jnp.dot(hid, w2k_ref[...], preferred_element_type=F32)
    kc_ref[0] = rope_tok(kk, cscmp_ref[0]).astype(BF16)
    vct_ref[0] = lax.dot_general(w2vt_ref[...], hid, (((1,), (1,)), ((), ())),
                                 preferred_element_type=F32).astype(BF16)


def _att_prep(za, zb, cs_tok, cs_cmp, cos_t, sin_t, wa, wb, pe, w1, w2k, w2vt):
    bsz, s, _ = za.shape
    n_blk = s // CMP_STRIDE
    nt = s // ATT_K
    tok = lambda c: pl.BlockSpec((1, s, LANES), lambda i: (i, 0, c))
    full3 = lambda shp: pl.BlockSpec(shp, lambda i: (0, 0, 0))
    full2 = lambda shp: pl.BlockSpec(shp, lambda i: (0, 0))
    per_b3 = lambda shp: pl.BlockSpec((1,) + shp, lambda i: (i, 0, 0))
    per_b4 = lambda shp: pl.BlockSpec((1,) + shp, lambda i: (i, 0, 0, 0))
    return pl.pallas_call(
        _att_prep_kernel,
        grid=(bsz,),
        in_specs=[tok(COL_KCVC), tok(COL_KS), tok(COL_KW),
                  per_b3((ZB_ROWS, s)), per_b3((s, LANES)), per_b3((n_blk, LANES)),
                  per_b3((ROPE_HALF, s)), per_b3((ROPE_HALF, s)),
                  full3((CMP_STRIDE, LANES, LANES)), full3((CMP_STRIDE, LANES, LANES)),
                  full2((CMP_BLOCK * HEAD_DIM, LANES)), full2((CMP_BLOCK * HEAD_DIM, LANES)),
                  full2((LANES, LANES)), full2((HEAD_DIM, LANES))],
        out_specs=[per_b3((ATT_HEADS * HEAD_DIM, s)), per_b3((s, LANES)), per_b3((s, HEAD_DIM)),
                   per_b4((nt, V_ROWS, ATT_K)), per_b4((nt, V_ROWS, ATT_K)),
                   per_b3((n_blk, HEAD_DIM)), per_b3((HEAD_DIM, n_blk)), per_b3((16, s))],
        out_shape=[jax.ShapeDtypeStruct((bsz, ATT_HEADS * HEAD_DIM, s), BF16),
                   jax.ShapeDtypeStruct((bsz, s, LANES), BF16),
                   jax.ShapeDtypeStruct((bsz, s, HEAD_DIM), BF16),
                   jax.ShapeDtypeStruct((bsz, nt, V_ROWS, ATT_K), BF16),
                   jax.ShapeDtypeStruct((bsz, nt, V_ROWS, ATT_K), BF16),
                   jax.ShapeDtypeStruct((bsz, n_blk, HEAD_DIM), BF16),
                   jax.ShapeDtypeStruct((bsz, HEAD_DIM, n_blk), BF16),
                   jax.ShapeDtypeStruct((bsz, 16, s), F32)],
        scratch_shapes=[pltpu.VMEM((n_blk + SUBLANES, LANES), F32)],
        compiler_params=_cparams("parallel"),
        name="att_prep",
    )(za, za, za, zb, cs_tok, cs_cmp, cos_t, sin_t, wa, wb, pe, w1, w2k, w2vt)


def _att_kernel(q_ref, ks_ref, kw_ref, vs_ref, vw_ref, kc_ref, vct_ref, g_ref, ov_ref,
                o_ref, qa_ref, sa_ref, sb_ref, m_ref, acc_ref):
    n_blk = kc_ref.shape[1]
    n_slc = ov_ref.shape[0]
    tq = ATT_Q
    wide = ATT_HEADS * tq
    i = pl.program_id(1)
    t0 = i * tq

    q = q_ref[0]
    qs = jnp.concatenate([q[h * HEAD_DIM:(h + 1) * HEAD_DIM] for h in range(ATT_HEADS)], axis=1)
    lane_w = lax.broadcasted_iota(jnp.int32, (1, wide), 1)
    t_w = t0 + (lane_w & (tq - 1))
    t_q = t0 + lax.broadcasted_iota(jnp.int32, (1, tq), 1)
    heads = lambda a: jnp.concatenate([a] * ATT_HEADS, axis=1)

    s_c = jnp.dot(kc_ref[0], qs, preferred_element_type=F32)
    blk_end = lax.broadcasted_iota(jnp.int32, (n_blk, 1), 0) * CMP_STRIDE + (CMP_BLOCK - 1)
    cmask = blk_end <= t_w
    s_m = jnp.where(cmask, s_c, NEG_INF)
    e = jnp.exp2(s_m - jnp.max(s_m, axis=0, keepdims=True))
    p_c = jnp.where(cmask, e * (1.0 / jnp.sum(e, axis=0, keepdims=True)), 0.0)
    o_cmp = jnp.dot(vct_ref[0], p_c.astype(BF16), preferred_element_type=F32)

    p_sum = p_c[:, 0:tq]
    for h in range(1, ATT_HEADS):
        p_sum = p_sum + p_c[:, h * tq:(h + 1) * tq]
    p_hi = p_sum.astype(BF16)
    p_lo = (p_sum - p_hi.astype(F32)).astype(BF16)
    imp = (jnp.dot(ov_ref[...], p_hi, preferred_element_type=F32)
           + jnp.dot(ov_ref[...], p_lo, preferred_element_type=F32))
    j = lax.broadcasted_iota(jnp.int32, (n_slc, 1), 0)
    back = t_q // SLC_BLOCK - j
    forced = (j == 0) | ((back >= 0) & (back < N_LOCAL))
    imp = jnp.where(forced, FORCE_SCORE, jnp.where(back < 0, -1.0, imp))
    rank = jnp.zeros((n_slc, tq), F32)
    for r in range(n_slc):
        row = imp[r:r + 1, :]
        ahead = (row > imp) | ((row == imp) & (j > r))
        rank = rank + jnp.where(ahead, 1.0, 0.0)
    sel_bias = jnp.where((rank < min(N_SELECT, n_slc)) & (back >= 0), 0.0, MASK_BIAS)
    qa_ref[0:HEAD_DIM, :] = qs
    qa_ref[HEAD_DIM:HEAD_DIM + n_slc, :] = heads(sel_bias).astype(BF16)
    qa_ref[HEAD_DIM + n_slc:, :] = jnp.zeros((LANES - HEAD_DIM - n_slc, wide), BF16)

    k_iota = lax.broadcasted_iota(jnp.int32, (ATT_K, 1), 0)

    n_win = (WINDOW + tq) // ATT_K
    first = jnp.maximum(i - WINDOW // ATT_K, 0)
    kw0 = pl.multiple_of(first * ATT_K, ATT_K)
    s_w = jnp.dot(kw_ref[0, pl.ds(kw0, n_win * ATT_K), :], qs, preferred_element_type=F32)
    diff = t_q - (kw0 + lax.broadcasted_iota(jnp.int32, (n_win * ATT_K, 1), 0))
    s_w = s_w + heads(jnp.where((diff >= 0) & (diff < WINDOW), 0.0, -jnp.inf))
    p_w = jnp.exp2(s_w - jnp.max(s_w, axis=0, keepdims=True)).astype(BF16)
    acc_w = None
    for n in range(n_win):
        part = jnp.dot(vw_ref[0, first + n], p_w[n * ATT_K:(n + 1) * ATT_K], preferred_element_type=F32)
        acc_w = part if acc_w is None else acc_w + part

    m_ref[...] = jnp.full((1, wide), NEG_INF, F32)
    acc_ref[...] = jnp.zeros((V_ROWS, wide), F32)

    def scores(kt, causal=False, live=None):
        k0 = pl.multiple_of(kt * ATT_K, ATT_K)
        s_t = jnp.dot(ks_ref[0, pl.ds(k0, ATT_K), :], qa_ref[...], preferred_element_type=F32)
        if causal:
            visible = (k0 + k_iota) <= t_q
            if live is not None:
                visible = visible & live
            s_t = jnp.where(heads(visible), s_t, -jnp.inf)
        return s_t

    def consume(s_ref, kt):
        m_old = m_ref[...]
        m_new = jnp.maximum(m_old, jnp.max(s_ref[...], axis=0, keepdims=True))
        alpha = jnp.exp2(m_old - m_new)
        p = jnp.exp2(s_ref[...] - m_new).astype(BF16)
        m_ref[...] = m_new
        acc_ref[...] = alpha * acc_ref[...] + jnp.dot(vs_ref[0, kt], p, preferred_element_type=F32)

    single = 1 - i % 2
    sa_ref[...] = scores(0, causal=True, live=single == 1)
    consume(sa_ref, 0)
    sa_ref[...] = scores(single)

    def pair_body(n, carry):
        a = single + 2 * n
        sb_ref[...] = scores(a + 1, causal=True)
        consume(sa_ref, a)
        sa_ref[...] = scores(jnp.minimum(a + 2, i))
        consume(sb_ref, a + 1)
        return carry

    lax.fori_loop(0, (i + 1 - single) // 2, pair_body, 0)
    acc_s = acc_ref[...]

    g = g_ref[0]
    gate = lambda br: jnp.concatenate(
        [g[h * N_BRANCH + br:h * N_BRANCH + br + 1, :] for h in range(ATT_HEADS)], axis=1)
    l_s = acc_s[HEAD_DIM:HEAD_DIM + 1, :]
    l_w = acc_w[HEAD_DIM:HEAD_DIM + 1, :]
    o = (gate(0) * o_cmp + (gate(1) * (1.0 / l_s)) * acc_s[:HEAD_DIM]
         + (gate(2) * (1.0 / l_w)) * acc_w[:HEAD_DIM])
    o = jnp.concatenate([o[:, h * tq:(h + 1) * tq] for h in range(ATT_HEADS)], axis=0)
    o_ref[0] = o.T


def _att(q_t, ks, kw, vs_t, vw_t, kc, vc_t, g_t, ov_t):
    bsz, _, s = q_t.shape
    n_blk = kc.shape[1]
    nt = s // ATT_K
    n_slc = s // SLC_BLOCK
    wide = ATT_HEADS * ATT_Q
    assert ATT_Q == ATT_K, "the selected-branch tile pairing assumes one diagonal key tile per query tile"
    per_b3 = lambda shp: pl.BlockSpec((1,) + shp, lambda b, i: (b, 0, 0))
    per_b4 = lambda shp: pl.BlockSpec((1,) + shp, lambda b, i: (b, 0, 0, 0))
    return pl.pallas_call(
        _att_kernel,
        grid=(bsz, s // ATT_Q),
        in_specs=[pl.BlockSpec((1, ATT_HEADS * HEAD_DIM, ATT_Q), lambda b, i: (b, 0, i)),
                  per_b3((s, LANES)), per_b3((s, HEAD_DIM)),
                  per_b4((nt, V_ROWS, ATT_K)), per_b4((nt, V_ROWS, ATT_K)),
                  per_b3((n_blk, HEAD_DIM)), per_b3((HEAD_DIM, n_blk)),
                  pl.BlockSpec((1, 16, ATT_Q), lambda b, i: (b, 0, i)),
                  pl.BlockSpec((n_slc, n_blk), lambda b, i: (0, 0))],
        out_specs=pl.BlockSpec((1, ATT_Q, GROUP_WIDTH), lambda b, i: (b, i, 0)),
        out_shape=jax.ShapeDtypeStruct((bsz, s, GROUP_WIDTH), F32),
        scratch_shapes=[pltpu.VMEM((LANES, wide), BF16),
                        pltpu.VMEM((ATT_K, wide), F32), pltpu.VMEM((ATT_K, wide), F32),
                        pltpu.VMEM((1, wide), F32), pltpu.VMEM((V_ROWS, wide), F32)],
        compiler_params=_cparams("parallel", "arbitrary"),
        name="sparse_attention",
    )(q_t, ks, kw, vs_t, vw_t, kc, vc_t, g_t, ov_t)


def _out_ffn_kernel(x_ref, y0_ref, y1_ref, y2_ref, y3_ref, wo_ref, gm_ref, g1_ref, w1_ref, w2_ref,
                    g2_ref, o_ref):
    acc = None
    for n, y_ref in enumerate((y0_ref, y1_ref, y2_ref, y3_ref)):
        part = jnp.dot(y_ref[...].astype(BF16), wo_ref[n * GROUP_WIDTH:(n + 1) * GROUP_WIDTH, :],
                       preferred_element_type=F32)
        acc = part if acc is None else acc + part
    x1 = x_ref[...] + _rms_norm(acc, gm_ref[...])
    h = _rms_norm(x1, g1_ref[...]).astype(BF16)
    acc = None
    for c in range(w1_ref.shape[1] // FFN_COLS):
        cols = slice(c * FFN_COLS, (c + 1) * FFN_COLS)
        f = jnp.maximum(jnp.dot(h, w1_ref[:, cols], preferred_element_type=F32), 0.0)
        part = jnp.dot((f * f).astype(BF16), w2_ref[cols, :], preferred_element_type=F32)
        acc = part if acc is None else acc + part
    o_ref[...] = x1 + _rms_norm(acc, g2_ref[...])


def _out_ffn(x2, ys, wo, gm, g1, w1, w2, g2):
    t, d = x2.shape
    dff = w1.shape[1]
    tm = min(FFN_ROWS, t)
    yspec = pl.BlockSpec((tm, GROUP_WIDTH), lambda i: (i, 0))
    vec = pl.BlockSpec((1, d), lambda i: (0, 0))
    once = lambda shp: pl.BlockSpec(shp, lambda i: (0, 0), pipeline_mode=pl.Buffered(1))
    return pl.pallas_call(
        _out_ffn_kernel,
        grid=(t // tm,),
        in_specs=[pl.BlockSpec((tm, d), lambda i: (i, 0)), yspec, yspec, yspec, yspec,
                  once((d, d)), vec, vec, once((d, dff)), once((dff, d)), vec],
        out_specs=pl.BlockSpec((tm, d), lambda i: (i, 0)),
        out_shape=jax.ShapeDtypeStruct((t, d), F32),
        compiler_params=_cparams("parallel"),
        name="out_ffn",
    )(x2, *ys, wo, gm, g1, w1, w2, g2)


def _rope_perm():
    idx = list(range(HEAD_DIM))
    for c in range(ROPE_HALF):
        idx[c], idx[c + ROPE_HALF] = c + ROPE_HALF, c
    return jnp.array(idx, jnp.int32)


def _split_w_in(w_in):
    gw = GROUP_WIDTH
    edges = [0, gw, 2 * gw, 3 * gw, 4 * gw, 5 * gw, 6 * gw]
    for _ in range(6):
        edges.append(edges[-1] + HEAD_DIM)
    edges.append(edges[-1] + ATT_HEADS * N_BRANCH)
    names = ("cval", "cgate", "pool", "su", "sv", "q", "kc", "vc", "ks", "vs", "kw", "vw", "g")
    return {n: w_in[:, edges[k]:edges[k + 1]] for k, n in enumerate(names)}


def _layer_weights(p):
    perm = _rope_perm()
    w = _split_w_in(p["w_in"])
    wa = jnp.concatenate([w["cval"], w["cgate"], w["pool"], w["su"], w["sv"], w["kc"], w["vc"],
                          w["ks"], w["ks"][:, perm], w["kw"], w["kw"][:, perm]], axis=1).astype(BF16)
    pad = jnp.zeros((D_MODEL, ZB_ROWS - ROW_G - ATT_HEADS * N_BRANCH), F32)
    wb = jnp.concatenate([w["q"], w["vs"], w["vw"], w["g"], pad], axis=1).T.astype(BF16)

    pool_bd = jnp.zeros((GROUP_WIDTH, GROUP_WIDTH), F32)
    for gi in range(len(POOL_WINDOWS)):
        sl = slice(gi * POOL_GROUP, (gi + 1) * POOL_GROUP)
        pool_bd = pool_bd.at[sl, sl].set(p["pool_w"][gi])
    sgu_bias = jnp.repeat(jnp.transpose(p["sgu_b"]), GROUP_WIDTH // SGU_HEADS, axis=1)

    w1k = p["cmp_k_w1"].reshape(CMP_BLOCK, HEAD_DIM, HEAD_DIM)
    w1v = p["cmp_v_w1"].reshape(CMP_BLOCK, HEAD_DIM, HEAD_DIM)
    z = jnp.zeros((CMP_STRIDE, HEAD_DIM, HEAD_DIM), F32)

    def kv_diag(a, b):
        return jnp.concatenate([jnp.concatenate([a, z], axis=2), jnp.concatenate([z, b], axis=2)], axis=1)

    cmp_wa = kv_diag(w1k[:CMP_STRIDE], w1v[:CMP_STRIDE]).astype(BF16)
    cmp_wb = kv_diag(w1k[CMP_STRIDE:], w1v[CMP_STRIDE:]).astype(BF16)
    ones = jnp.ones((1, HEAD_DIM), F32)
    cmp_pe = jnp.concatenate([p["cmp_k_pe"].reshape(-1, 1) * ones, p["cmp_v_pe"].reshape(-1, 1) * ones], axis=1)
    cmp_w1 = jnp.concatenate([p["cmp_k_w1"], p["cmp_v_w1"]], axis=1)
    zk = jnp.zeros((HEAD_DIM, LANES), F32)
    cmp_w2k = jnp.concatenate([jnp.concatenate([p["cmp_k_w2"], p["cmp_k_w2"][:, perm]], axis=1), zk], axis=0).astype(BF16)
    cmp_w2vt = jnp.concatenate([jnp.zeros((HEAD_DIM, HEAD_DIM), F32), p["cmp_v_w2"].T], axis=1).astype(BF16)

    row = lambda v: v.reshape(1, -1)
    return dict(
        wa=wa, wb=wb, pre_mix=row(p["pre_mix_norm"]), post_mix=row(p["post_mix_norm"]),
        pre_ffn=row(p["pre_ffn_norm"]), post_ffn=row(p["post_ffn_norm"]),
        conv_w=p["conv_w"], conv_b=row(p["conv_b"]), conv_lg=row(p["conv_ln_g"]), conv_lb=row(p["conv_ln_b"]),
        pool_bd=pool_bd.astype(BF16), pool_scale=row(p["pool_scale"]),
        sgu_lg=row(p["sgu_ln_g"]), sgu_lb=row(p["sgu_ln_b"]), sgu_w=p["sgu_w"], sgu_bias=sgu_bias,
        cmp_wa=cmp_wa, cmp_wb=cmp_wb, cmp_pe=cmp_pe, cmp_w1=cmp_w1, cmp_w2k=cmp_w2k, cmp_w2vt=cmp_w2vt,
        w_out=p["w_out"].astype(BF16), ffn_w1=p["ffn_w1"].astype(BF16), ffn_w2=p["ffn_w2"].astype(BF16),
    )


def _overlap_t(s):
    n_blk = s // CMP_STRIDE
    n_slc = s // SLC_BLOCK
    bs = jnp.arange(n_blk)[None, :] * CMP_STRIDE
    ss = jnp.arange(n_slc)[:, None] * SLC_BLOCK
    ov = jnp.clip(jnp.minimum(bs + CMP_BLOCK, ss + SLC_BLOCK) - jnp.maximum(bs, ss), 0)
    return (ov.astype(F32) / CMP_STRIDE).astype(BF16)


def _rope_tables(positions):
    bsz, s = positions.shape
    inv = ROPE_THETA ** (-jnp.arange(ROPE_HALF, dtype=F32) * 2.0 / ROPE_DIM)
    inv64 = jnp.concatenate([inv, inv, jnp.zeros((HEAD_DIM - ROPE_DIM,), F32)])
    inv128 = jnp.concatenate([inv64, inv64]).reshape(1, LANES)
    sgn64 = jnp.concatenate([-jnp.ones((ROPE_HALF,), F32), jnp.ones((ROPE_HALF,), F32),
                             jnp.zeros((HEAD_DIM - ROPE_DIM,), F32)])
    sgn128 = jnp.concatenate([sgn64, sgn64]).reshape(1, LANES)
    posf = positions.astype(F32)
    n_blk = s // CMP_STRIDE
    pos_end = posf[:, CMP_BLOCK - 1::CMP_STRIDE]
    pos_end = jnp.pad(pos_end, ((0, 0), (0, n_blk - pos_end.shape[1])))
    cs_tok = _rope_tok_table(posf[:, :, None], inv128, sgn128)
    cs_cmp = _rope_tok_table(pos_end[:, :, None], inv128, sgn128)
    cos_t, sin_t = _rope_chan_table(posf[:, None, :], inv.reshape(ROPE_HALF, 1))
    return cs_tok, cs_cmp, cos_t, sin_t


def _mixers(x, lw, tables, ov_t):
    cs_tok, cs_cmp, cos_t, sin_t = tables
    za, zb = _proj(x, lw["pre_mix"], lw["wa"], lw["wb"])
    y_conv = _conv_mixer(za, lw["conv_w"], lw["conv_b"], lw["conv_lg"], lw["conv_lb"])
    y_pool = _pool_mixer(za, lw["pool_bd"], lw["pool_scale"])
    y_sgu = _sgu_mixer(za, lw["sgu_lg"], lw["sgu_lb"], lw["sgu_w"], lw["sgu_bias"])
    prep = _att_prep(za, zb, cs_tok, cs_cmp, cos_t, sin_t, lw["cmp_wa"], lw["cmp_wb"],
                     lw["cmp_pe"], lw["cmp_w1"], lw["cmp_w2k"], lw["cmp_w2vt"])
    y_att = _att(*prep, ov_t)
    return y_conv, y_pool, y_sgu, y_att


def kernel(x, positions, pre_mix_norm, post_mix_norm, pre_ffn_norm, post_ffn_norm, w_in, conv_w, conv_b, conv_ln_g, conv_ln_b, pool_w, pool_scale, sgu_ln_g, sgu_ln_b, sgu_w, sgu_b, cmp_k_pe, cmp_k_w1, cmp_k_w2, cmp_v_pe, cmp_v_w1, cmp_v_w2, w_out, ffn_w1, ffn_w2):
    params = dict(pre_mix_norm=pre_mix_norm, post_mix_norm=post_mix_norm, pre_ffn_norm=pre_ffn_norm,
                  post_ffn_norm=post_ffn_norm, w_in=w_in, conv_w=conv_w, conv_b=conv_b,
                  conv_ln_g=conv_ln_g, conv_ln_b=conv_ln_b, pool_w=pool_w, pool_scale=pool_scale,
                  sgu_ln_g=sgu_ln_g, sgu_ln_b=sgu_ln_b, sgu_w=sgu_w, sgu_b=sgu_b,
                  cmp_k_pe=cmp_k_pe, cmp_k_w1=cmp_k_w1, cmp_k_w2=cmp_k_w2, cmp_v_pe=cmp_v_pe,
                  cmp_v_w1=cmp_v_w1, cmp_v_w2=cmp_v_w2, w_out=w_out, ffn_w1=ffn_w1, ffn_w2=ffn_w2)
    bsz, s, d = x.shape
    depth = w_in.shape[0]
    tables = _rope_tables(positions)
    ov_t = _overlap_t(s)
    for l in range(depth):
        lw = _layer_weights({k: v[l] for k, v in params.items()})
        ys = _mixers(x, lw, tables, ov_t)
        x2 = _out_ffn(x.reshape(bsz * s, d), [y.reshape(bsz * s, GROUP_WIDTH) for y in ys],
                      lw["w_out"], lw["post_mix"], lw["pre_ffn"], lw["ffn_w1"], lw["ffn_w2"], lw["post_ffn"])
        x = x2.reshape(bsz, s, d)
    return x
```

```python
import functools

import jax
import jax.numpy as jnp
from jax import lax
from jax.experimental import pallas as pl
from jax.experimental.pallas import tpu as pltpu

F32 = jnp.float32
BF16 = jnp.bfloat16

D_MODEL = 1024
GROUP_WIDTH = 256
CONV_WIDTH = 31
POOL_WINDOWS = (2, 4, 8, 16)
POOL_GROUP = 64
SGU_HEADS = 4
SGU_CHUNK = 128
ATT_HEADS = 4
HEAD_DIM = 64
ROPE_DIM = 16
ROPE_HALF = 8
ROPE_THETA = 500000.0
CMP_BLOCK = 32
CMP_STRIDE = 16
SLC_BLOCK = 64
N_SELECT = 8
N_LOCAL = 2
WINDOW = 512
N_BRANCH = 3
D_FF = 4096
NORM_EPS = 1e-6
NEG_INF = -1e30
FORCE_SCORE = 1e9

LANES = 128
SUBLANES = 8
VMEM_LIMIT_BYTES = 56 * 1024 * 1024

PROJ_ROWS = 512
FFN_ROWS = 512
FFN_COLS = 512
CONV_ROWS = 64
POOL_ROWS = 64
ATT_Q = 256
ATT_K = 256
LOG2_E = 1.4426950408889634
V_ROWS = 80
MASK_BIAS = -2e30

COL_CVAL, COL_CGATE, COL_POOL, COL_SGU_U, COL_SGU_V = 0, 1, 2, 3, 4
COL_KCVC, COL_KS, COL_KW = 10, 11, 12
ZA_WIDTH = 1664
ROW_Q, ROW_VS, ROW_VW, ROW_G = 0, 256, 320, 384
ZB_ROWS = 400


def _cparams(*sem):
    return pltpu.CompilerParams(dimension_semantics=sem, vmem_limit_bytes=VMEM_LIMIT_BYTES)


def _per_layer(l, *shape, pipeline_mode=None):
    zeros = (0,) * len(shape)
    extra = {} if pipeline_mode is None else {"pipeline_mode": pipeline_mode}
    return pl.BlockSpec((None,) + shape, lambda *_: (l,) + zeros, **extra)


def _gelu(x):
    return 0.5 * x * (1.0 + jnp.tanh(0.7978845608028654 * (x + 0.044715 * (x * x * x))))


def _sigmoid(x):
    return 1.0 / (1.0 + jnp.exp(-x))


def _layer_norm(x, g, b):
    mu = jnp.mean(x, axis=-1, keepdims=True)
    d = x - mu
    var = jnp.mean(d * d, axis=-1, keepdims=True)
    return d * lax.rsqrt(var + NORM_EPS) * g + b


def _rms_norm(x, g):
    return x * lax.rsqrt(jnp.mean(x * x, axis=-1, keepdims=True) + NORM_EPS) * g


def _rope_chan_kernel(pos_ref, inv_ref, cos_ref, sin_ref):
    ang = inv_ref[...] * pos_ref[0]
    cos_ref[0] = jnp.cos(ang)
    sin_ref[0] = jnp.sin(ang)


def _rope_chan_table(pos_row, inv_col):
    b, _, s = pos_row.shape
    spec = pl.BlockSpec((1, ROPE_HALF, s), lambda i: (i, 0, 0))
    return pl.pallas_call(
        _rope_chan_kernel,
        grid=(b,),
        in_specs=[pl.BlockSpec((1, 1, s), lambda i: (i, 0, 0)),
                  pl.BlockSpec((ROPE_HALF, 1), lambda i: (0, 0))],
        out_specs=[spec, spec],
        out_shape=[jax.ShapeDtypeStruct((b, ROPE_HALF, s), F32)] * 2,
        compiler_params=_cparams("parallel"),
        name="rope_chan_table",
    )(pos_row, inv_col)


def _proj_kernel(x_ref, g_ref, wa_ref, wb_ref, za_ref, zb_ref):
    h = _rms_norm(x_ref[0], g_ref[...]).astype(BF16)
    za_ref[0] = jnp.dot(h, wa_ref[...], preferred_element_type=F32)
    zb_ref[0] = lax.dot_general(wb_ref[...], h, (((1,), (1,)), ((), ())),
                                preferred_element_type=F32)


def _proj(l, x, g, wa, wb):
    b, s, d = x.shape
    tm = min(PROJ_ROWS, s)
    return pl.pallas_call(
        _proj_kernel,
        grid=(b, s // tm),
        in_specs=[pl.BlockSpec((1, tm, d), lambda i, j: (i, j, 0)),
                  _per_layer(l, 1, d), _per_layer(l, d, ZA_WIDTH), _per_layer(l, ZB_ROWS, d)],
        out_specs=[pl.BlockSpec((1, tm, ZA_WIDTH), lambda i, j: (i, j, 0)),
                   pl.BlockSpec((1, ZB_ROWS, tm), lambda i, j: (i, 0, j))],
        out_shape=[jax.ShapeDtypeStruct((b, s, ZA_WIDTH), F32),
                   jax.ShapeDtypeStruct((b, ZB_ROWS, s), F32)],
        compiler_params=_cparams("parallel", "parallel"),
        name="in_proj",
    )(x, g, wa, wb)


CONV_PAD = 32


def _conv_kernel(a_ref, gate_ref, w_ref, b_ref, lg_ref, lb_ref, o_ref, hp_ref):
    s = a_ref.shape[1]
    hp_ref[0:CONV_PAD, :] = jnp.zeros((CONV_PAD, GROUP_WIDTH), F32)
    hp_ref[CONV_PAD:CONV_PAD + s, :] = a_ref[0] * _sigmoid(gate_ref[0])
    shift = CONV_PAD - (CONV_WIDTH - 1)

    def body(i, carry):
        t0 = pl.multiple_of(i * CONV_ROWS, CONV_ROWS)
        acc = jnp.zeros((CONV_ROWS, GROUP_WIDTH), F32) + b_ref[...]
        win = hp_ref[pl.ds(t0, CONV_ROWS + CONV_PAD), :]
        n_win = CONV_ROWS + CONV_PAD
        for r in range(SUBLANES):
            taps = [k for k in range(CONV_WIDTH) if (k + shift) % SUBLANES == r]
            wr = win if r == 0 else pltpu.roll(win, n_win - r, axis=0)
            for k in taps:
                off = k + shift - r
                acc = acc + wr[off:off + CONV_ROWS, :] * w_ref[k:k + 1, :]
        y = _layer_norm(acc, lg_ref[...], lb_ref[...])
        o_ref[0, pl.ds(t0, CONV_ROWS), :] = y * _sigmoid(y)
        return carry

    lax.fori_loop(0, s // CONV_ROWS, body, 0)


def _conv_mixer(l, za, w, b, lg, lb):
    bsz, s, _ = za.shape
    vec = _per_layer(l, 1, GROUP_WIDTH)
    return pl.pallas_call(
        _conv_kernel,
        grid=(bsz,),
        in_specs=[pl.BlockSpec((1, s, GROUP_WIDTH), lambda i: (i, 0, COL_CVAL)),
                  pl.BlockSpec((1, s, GROUP_WIDTH), lambda i: (i, 0, COL_CGATE)),
                  _per_layer(l, CONV_WIDTH, GROUP_WIDTH), vec, vec, vec],
        out_specs=pl.BlockSpec((1, s, GROUP_WIDTH), lambda i: (i, 0, 0)),
        out_shape=jax.ShapeDtypeStruct((bsz, s, GROUP_WIDTH), F32),
        scratch_shapes=[pltpu.VMEM((CONV_PAD + s, GROUP_WIDTH), F32)],
        compiler_params=_cparams("parallel"),
        name="conv_mixer",
    )(za, za, w, b, lg, lb)


POOL_PAD = 16


def _pool_kernel(p_ref, w_ref, sc_ref, o_ref, pp_ref):
    s = p_ref.shape[1]
    pp_ref[0:POOL_PAD, :] = jnp.zeros((POOL_PAD, GROUP_WIDTH), F32)
    pp_ref[POOL_PAD:POOL_PAD + s, :] = p_ref[0]
    lane = lax.broadcasted_iota(jnp.int32, (1, GROUP_WIDTH), 1)
    grp = lane // POOL_GROUP
    win = jnp.where(grp == 0, POOL_WINDOWS[0],
                    jnp.where(grp == 1, POOL_WINDOWS[1],
                              jnp.where(grp == 2, POOL_WINDOWS[2], POOL_WINDOWS[3])))

    def body(i, carry):
        t0 = pl.multiple_of(i * POOL_ROWS, POOL_ROWS)
        rows = pp_ref[pl.ds(t0, POOL_ROWS + POOL_PAD), :]
        p = rows[POOL_PAD:POOL_PAD + POOL_ROWS, :]
        acc = rows
        sums = []
        width = 1
        for w in POOL_WINDOWS:
            while width < w:
                acc = acc + pltpu.roll(acc, width, axis=0)
                width *= 2
            sums.append(acc[POOL_PAD:POOL_PAD + POOL_ROWS, :])
        total = jnp.where(grp == 0, sums[0],
                          jnp.where(grp == 1, sums[1],
                                    jnp.where(grp == 2, sums[2], sums[3])))
        t = t0 + lax.broadcasted_iota(jnp.int32, (POOL_ROWS, 1), 0)
        count = jnp.minimum(t + 1, win).astype(F32)
        mixed = total / count - p
        y = jnp.dot(mixed.astype(BF16), w_ref[...], preferred_element_type=F32)
        o_ref[0, pl.ds(t0, POOL_ROWS), :] = y * sc_ref[...]
        return carry

    lax.fori_loop(0, s // POOL_ROWS, body, 0, unroll=2)


def _pool_mixer(l, za, w_bd, scale):
    bsz, s, _ = za.shape
    return pl.pallas_call(
        _pool_kernel,
        grid=(bsz,),
        in_specs=[pl.BlockSpec((1, s, GROUP_WIDTH), lambda i: (i, 0, COL_POOL)),
                  _per_layer(l, GROUP_WIDTH, GROUP_WIDTH), _per_layer(l, 1, GROUP_WIDTH)],
        out_specs=pl.BlockSpec((1, s, GROUP_WIDTH), lambda i: (i, 0, 0)),
        out_shape=jax.ShapeDtypeStruct((bsz, s, GROUP_WIDTH), F32),
        scratch_shapes=[pltpu.VMEM((POOL_PAD + s, GROUP_WIDTH), F32)],
        compiler_params=_cparams("parallel"),
        name="pool_mixer",
    )(za, w_bd, scale)


def _sgu_kernel(u_ref, v_ref, lg_ref, lb_ref, w_ref, bias_ref, o_ref):
    s = u_ref.shape[1]
    row = lax.broadcasted_iota(jnp.int32, (SGU_CHUNK, SGU_CHUNK), 0)
    col = lax.broadcasted_iota(jnp.int32, (SGU_CHUNK, SGU_CHUNK), 1)
    head = lax.broadcasted_iota(jnp.int32, (1, GROUP_WIDTH), 1) // (GROUP_WIDTH // SGU_HEADS)

    def body(c, carry):
        t0 = pl.multiple_of(c * SGU_CHUNK, SGU_CHUNK)
        v = _layer_norm(_gelu(v_ref[0, pl.ds(t0, SGU_CHUNK), :]), lg_ref[...], lb_ref[...])
        vb = v.astype(BF16)
        mixed = jnp.zeros((SGU_CHUNK, GROUP_WIDTH), F32)
        for h in range(SGU_HEADS):
            w = jnp.where(row >= col, w_ref[h], 0.0).astype(BF16)
            r = jnp.dot(w, vb, preferred_element_type=F32)
            mixed = jnp.where(head == h, r, mixed)
        u = _gelu(u_ref[0, pl.ds(t0, SGU_CHUNK), :])
        o_ref[0, pl.ds(t0, SGU_CHUNK), :] = u * (mixed + bias_ref[...])
        return carry

    lax.fori_loop(0, s // SGU_CHUNK, body, 0, unroll=2)


def _sgu_mixer(l, za, lg, lb, w, bias):
    bsz, s, _ = za.shape
    vec = _per_layer(l, 1, GROUP_WIDTH)
    return pl.pallas_call(
        _sgu_kernel,
        grid=(bsz,),
        in_specs=[pl.BlockSpec((1, s, GROUP_WIDTH), lambda i: (i, 0, COL_SGU_U)),
                  pl.BlockSpec((1, s, GROUP_WIDTH), lambda i: (i, 0, COL_SGU_V)),
                  vec, vec,
                  _per_layer(l, SGU_HEADS, SGU_CHUNK, SGU_CHUNK), _per_layer(l, SGU_CHUNK, GROUP_WIDTH)],
        out_specs=pl.BlockSpec((1, s, GROUP_WIDTH), lambda i: (i, 0, 0)),
        out_shape=jax.ShapeDtypeStruct((bsz, s, GROUP_WIDTH), F32),
        compiler_params=_cparams("parallel"),
        name="sgu_mixer",
    )(za, za, lg, lb, w, bias)


def _att_prep_kernel(kcvc_ref, ksx_ref, kwx_ref, zb_ref, cos_ref, sin_ref, cosb_ref, sinb_ref,
                     wa_ref, wb_ref, pe_ref, w1_ref, w2k_ref, w2vt_ref,
                     q_ref, ks_ref, kw_ref, vs_ref, vw_ref, kc_ref, vct_ref, g_ref, sh_ref):
    s = ksx_ref.shape[1]
    n_blk = s // CMP_STRIDE

    def tok_table(cos, sin):
        n = cos.shape[1]
        rest = HEAD_DIM - ROPE_DIM
        rows = [cos, cos, jnp.ones((rest, n), F32), -sin, sin, jnp.zeros((rest, n), F32)]
        return jnp.concatenate(rows, axis=0).T

    def rope_wide(x, cs):
        r = x * cs
        return r + pltpu.roll(r, HEAD_DIM, axis=1)

    def rope_tok(x, cs):
        return rope_wide(x, cs)[:, :HEAD_DIM]

    cos = cos_ref[0]
    sin = sin_ref[0]
    cs = tok_table(cos, sin)
    lane = lax.broadcasted_iota(jnp.int32, (s, LANES), 1)
    blk = lax.broadcasted_iota(jnp.int32, (s, LANES), 0) // SLC_BLOCK
    ks_ref[0] = jnp.where(lane < HEAD_DIM, rope_wide(ksx_ref[0], cs),
                          jnp.where(lane - HEAD_DIM == blk, 1.0, 0.0)).astype(BF16)
    kw_ref[0] = rope_tok(kwx_ref[0], cs).astype(BF16)

    scale = HEAD_DIM ** -0.5 * LOG2_E
    parts = []
    for h in range(ATT_HEADS):
        r0 = ROW_Q + h * HEAD_DIM
        x1 = zb_ref[0, r0:r0 + ROPE_HALF, :]
        x2 = zb_ref[0, r0 + ROPE_HALF:r0 + ROPE_DIM, :]
        parts += [x1 * cos - x2 * sin, x2 * cos + x1 * sin, zb_ref[0, r0 + ROPE_DIM:r0 + HEAD_DIM, :]]
    q_ref[0] = (jnp.concatenate(parts, axis=0) * scale).astype(BF16)

    ones_row = jnp.where(lax.broadcasted_iota(jnp.int32, (V_ROWS - HEAD_DIM, ATT_K), 0) == 0, 1.0, 0.0)
    for c in range(s // ATT_K):
        cols = slice(c * ATT_K, (c + 1) * ATT_K)
        vs_ref[0, c] = jnp.concatenate([zb_ref[0, ROW_VS:ROW_VS + HEAD_DIM, cols], ones_row], axis=0).astype(BF16)
        vw_ref[0, c] = jnp.concatenate([zb_ref[0, ROW_VW:ROW_VW + HEAD_DIM, cols], ones_row], axis=0).astype(BF16)
    g_ref[0] = _sigmoid(zb_ref[0, ROW_G:ROW_G + 16, :])

    acc_a = jnp.zeros((n_blk, LANES), F32)
    acc_b = jnp.zeros((n_blk, LANES), F32)
    for l in range(CMP_STRIDE):
        xl = kcvc_ref[0, pl.ds(l, n_blk, stride=CMP_STRIDE), :].astype(BF16)
        acc_a = acc_a + jnp.dot(xl, wa_ref[l], preferred_element_type=F32)
        acc_b = acc_b + jnp.dot(xl, wb_ref[l], preferred_element_type=F32)
    sh_ref[0:n_blk, :] = acc_b
    sh_ref[n_blk:n_blk + SUBLANES, :] = jnp.zeros((SUBLANES, LANES), F32)
    pe_term = jnp.sum(pe_ref[...] * w1_ref[...], axis=0, keepdims=True)
    hid = _gelu(acc_a + sh_ref[1:n_blk + 1, :] + pe_term).astype(BF16)
    kk = jnp.dot(hid, w2k_ref[...], preferred_element_type=F32)
    kc_ref[0] = rope_tok(kk, tok_table(cosb_ref[0], sinb_ref[0])).astype(BF16)
    vct_ref[0] = lax.dot_general(w2vt_ref[...], hid, (((1,), (1,)), ((), ())),
                                 preferred_element_type=F32).astype(BF16)


def _att_prep(l, za, zb, cos_t, sin_t, cos_b, sin_b, wa, wb, pe, w1, w2k, w2vt):
    bsz, s, _ = za.shape
    n_blk = s // CMP_STRIDE
    nt = s // ATT_K
    tok = lambda c: pl.BlockSpec((1, s, LANES), lambda i: (i, 0, c))
    per_b3 = lambda shp: pl.BlockSpec((1,) + shp, lambda i: (i, 0, 0))
    per_b4 = lambda shp: pl.BlockSpec((1,) + shp, lambda i: (i, 0, 0, 0))
    return pl.pallas_call(
        _att_prep_kernel,
        grid=(bsz,),
        in_specs=[tok(COL_KCVC), tok(COL_KS), tok(COL_KW),
                  per_b3((ZB_ROWS, s)),
                  per_b3((ROPE_HALF, s)), per_b3((ROPE_HALF, s)),
                  per_b3((ROPE_HALF, n_blk)), per_b3((ROPE_HALF, n_blk)),
                  _per_layer(l, CMP_STRIDE, LANES, LANES), _per_layer(l, CMP_STRIDE, LANES, LANES),
                  _per_layer(l, CMP_BLOCK * HEAD_DIM, LANES), _per_layer(l, CMP_BLOCK * HEAD_DIM, LANES),
                  _per_layer(l, LANES, LANES), _per_layer(l, HEAD_DIM, LANES)],
        out_specs=[per_b3((ATT_HEADS * HEAD_DIM, s)), per_b3((s, LANES)), per_b3((s, HEAD_DIM)),
                   per_b4((nt, V_ROWS, ATT_K)), per_b4((nt, V_ROWS, ATT_K)),
                   per_b3((n_blk, HEAD_DIM)), per_b3((HEAD_DIM, n_blk)), per_b3((16, s))],
        out_shape=[jax.ShapeDtypeStruct((bsz, ATT_HEADS * HEAD_DIM, s), BF16),
                   jax.ShapeDtypeStruct((bsz, s, LANES), BF16),
                   jax.ShapeDtypeStruct((bsz, s, HEAD_DIM), BF16),
                   jax.ShapeDtypeStruct((bsz, nt, V_ROWS, ATT_K), BF16),
                   jax.ShapeDtypeStruct((bsz, nt, V_ROWS, ATT_K), BF16),
                   jax.ShapeDtypeStruct((bsz, n_blk, HEAD_DIM), BF16),
                   jax.ShapeDtypeStruct((bsz, HEAD_DIM, n_blk), BF16),
                   jax.ShapeDtypeStruct((bsz, 16, s), F32)],
        scratch_shapes=[pltpu.VMEM((n_blk + SUBLANES, LANES), F32)],
        compiler_params=_cparams("parallel"),
        name="att_prep",
    )(za, za, za, zb, cos_t, sin_t, cos_b, sin_b, wa, wb, pe, w1, w2k, w2vt)


def _att_kernel(q_ref, ks_ref, kw_ref, vs_ref, vw_ref, kc_ref, vct_ref, g_ref, ov_ref,
                o_ref, qa_ref, sa_ref, sb_ref, m_ref, acc_ref):
    n_blk = kc_ref.shape[1]
    n_slc = ov_ref.shape[0]
    tq = ATT_Q
    wide = ATT_HEADS * tq
    i = pl.program_id(1)
    t0 = i * tq

    q = q_ref[0]
    qs = jnp.concatenate([q[h * HEAD_DIM:(h + 1) * HEAD_DIM] for h in range(ATT_HEADS)], axis=1)
    lane_w = lax.broadcasted_iota(jnp.int32, (1, wide), 1)
    t_w = t0 + (lane_w & (tq - 1))
    t_q = t0 + lax.broadcasted_iota(jnp.int32, (1, tq), 1)
    heads = lambda a: jnp.concatenate([a] * ATT_HEADS, axis=1)

    s_c = jnp.dot(kc_ref[0], qs, preferred_element_type=F32)
    blk_end = lax.broadcasted_iota(jnp.int32, (n_blk, 1), 0) * CMP_STRIDE + (CMP_BLOCK - 1)
    cmask = blk_end <= t_w
    s_m = jnp.where(cmask, s_c, NEG_INF)
    e = jnp.exp2(s_m - jnp.max(s_m, axis=0, keepdims=True))
    p_c = jnp.where(cmask, e * (1.0 / jnp.sum(e, axis=0, keepdims=True)), 0.0)
    o_cmp = jnp.dot(vct_ref[0], p_c.astype(BF16), preferred_element_type=F32)

    p_sum = p_c[:, 0:tq]
    for h in range(1, ATT_HEADS):
        p_sum = p_sum + p_c[:, h * tq:(h + 1) * tq]
    p_hi = p_sum.astype(BF16)
    p_lo = (p_sum - p_hi.astype(F32)).astype(BF16)
    imp = (jnp.dot(ov_ref[...], p_hi, preferred_element_type=F32)
           + jnp.dot(ov_ref[...], p_lo, preferred_element_type=F32))
    j = lax.broadcasted_iota(jnp.int32, (n_slc, 1), 0)
    back = t_q // SLC_BLOCK - j
    forced = (j == 0) | ((back >= 0) & (back < N_LOCAL))
    imp = jnp.where(forced, FORCE_SCORE, jnp.where(back < 0, -1.0, imp))
    rank = jnp.zeros((n_slc, tq), F32)
    for r in range(n_slc):
        row = imp[r:r + 1, :]
        ahead = (row > imp) | ((row == imp) & (j > r))
        rank = rank + jnp.where(ahead, 1.0, 0.0)
    sel_bias = jnp.where((rank < min(N_SELECT, n_slc)) & (back >= 0), 0.0, MASK_BIAS)
    qa_ref[0:HEAD_DIM, :] = qs
    qa_ref[HEAD_DIM:HEAD_DIM + n_slc, :] = heads(sel_bias).astype(BF16)
    qa_ref[HEAD_DIM + n_slc:, :] = jnp.zeros((LANES - HEAD_DIM - n_slc, wide), BF16)

    k_iota = lax.broadcasted_iota(jnp.int32, (ATT_K, 1), 0)

    n_win = (WINDOW + tq) // ATT_K
    first = jnp.maximum(i - WINDOW // ATT_K, 0)
    kw0 = pl.multiple_of(first * ATT_K, ATT_K)
    s_w = jnp.dot(kw_ref[0, pl.ds(kw0, n_win * ATT_K), :], qs, preferred_element_type=F32)
    diff = t_q - (kw0 + lax.broadcasted_iota(jnp.int32, (n_win * ATT_K, 1), 0))
    s_w = s_w + heads(jnp.where((diff >= 0) & (diff < WINDOW), 0.0, -jnp.inf))
    p_w = jnp.exp2(s_w - jnp.max(s_w, axis=0, keepdims=True)).astype(BF16)
    acc_w = None
    for n in range(n_win):
        part = jnp.dot(vw_ref[0, first + n], p_w[n * ATT_K:(n + 1) * ATT_K], preferred_element_type=F32)
        acc_w = part if acc_w is None else acc_w + part

    m_ref[...] = jnp.full((1, wide), NEG_INF, F32)
    acc_ref[...] = jnp.zeros((V_ROWS, wide), F32)

    def scores(kt, causal=False, live=None):
        k0 = pl.multiple_of(kt * ATT_K, ATT_K)
        s_t = jnp.dot(ks_ref[0, pl.ds(k0, ATT_K), :], qa_ref[...], preferred_element_type=F32)
        if causal:
            visible = (k0 + k_iota) <= t_q
            if live is not None:
                visible = visible & live
            s_t = jnp.where(heads(visible), s_t, -jnp.inf)
        return s_t

    def consume(s_ref, kt):
        m_old = m_ref[...]
        m_new = jnp.maximum(m_old, jnp.max(s_ref[...], axis=0, keepdims=True))
        alpha = jnp.exp2(m_old - m_new)
        p = jnp.exp2(s_ref[...] - m_new).astype(BF16)
        m_ref[...] = m_new
        acc_ref[...] = alpha * acc_ref[...] + jnp.dot(vs_ref[0, kt], p, preferred_element_type=F32)

    single = 1 - i % 2
    sa_ref[...] = scores(0, causal=True, live=single == 1)
    consume(sa_ref, 0)
    sa_ref[...] = scores(single)

    def pair_body(n, carry):
        a = single + 2 * n
        sb_ref[...] = scores(a + 1, causal=True)
        consume(sa_ref, a)
        sa_ref[...] = scores(jnp.minimum(a + 2, i))
        consume(sb_ref, a + 1)
        return carry

    lax.fori_loop(0, (i + 1 - single) // 2, pair_body, 0)
    acc_s = acc_ref[...]

    g = g_ref[0]
    gate = lambda br: jnp.concatenate(
        [g[h * N_BRANCH + br:h * N_BRANCH + br + 1, :] for h in range(ATT_HEADS)], axis=1)
    l_s = acc_s[HEAD_DIM:HEAD_DIM + 1, :]
    l_w = acc_w[HEAD_DIM:HEAD_DIM + 1, :]
    o = (gate(0) * o_cmp + (gate(1) * (1.0 / l_s)) * acc_s[:HEAD_DIM]
         + (gate(2) * (1.0 / l_w)) * acc_w[:HEAD_DIM])
    o = jnp.concatenate([o[:, h * tq:(h + 1) * tq] for h in range(ATT_HEADS)], axis=0)
    o_ref[0] = o.T


def _att(q_t, ks, kw, vs_t, vw_t, kc, vc_t, g_t, ov_t):
    bsz, _, s = q_t.shape
    n_blk = kc.shape[1]
    nt = s // ATT_K
    n_slc = s // SLC_BLOCK
    wide = ATT_HEADS * ATT_Q
    assert ATT_Q == ATT_K, "the selected-branch tile pairing assumes one diagonal key tile per query tile"
    per_b3 = lambda shp: pl.BlockSpec((1,) + shp, lambda b, i: (b, 0, 0))
    per_b4 = lambda shp: pl.BlockSpec((1,) + shp, lambda b, i: (b, 0, 0, 0))
    return pl.pallas_call(
        _att_kernel,
        grid=(bsz, s // ATT_Q),
        in_specs=[pl.BlockSpec((1, ATT_HEADS * HEAD_DIM, ATT_Q), lambda b, i: (b, 0, i)),
                  per_b3((s, LANES)), per_b3((s, HEAD_DIM)),
                  per_b4((nt, V_ROWS, ATT_K)), per_b4((nt, V_ROWS, ATT_K)),
                  per_b3((n_blk, HEAD_DIM)), per_b3((HEAD_DIM, n_blk)),
                  pl.BlockSpec((1, 16, ATT_Q), lambda b, i: (b, 0, i)),
                  pl.BlockSpec((n_slc, n_blk), lambda b, i: (0, 0))],
        out_specs=pl.BlockSpec((1, ATT_Q, GROUP_WIDTH), lambda b, i: (b, i, 0)),
        out_shape=jax.ShapeDtypeStruct((bsz, s, GROUP_WIDTH), F32),
        scratch_shapes=[pltpu.VMEM((LANES, wide), BF16),
                        pltpu.VMEM((ATT_K, wide), F32), pltpu.VMEM((ATT_K, wide), F32),
                        pltpu.VMEM((1, wide), F32), pltpu.VMEM((V_ROWS, wide), F32)],
        compiler_params=_cparams("parallel", "arbitrary"),
        name="sparse_attention",
    )(q_t, ks, kw, vs_t, vw_t, kc, vc_t, g_t, ov_t)


def _out_ffn_kernel(x_ref, y0_ref, y1_ref, y2_ref, y3_ref, wo_ref, gm_ref, g1_ref, w1_ref, w2_ref,
                    g2_ref, o_ref):
    acc = None
    for n, y_ref in enumerate((y0_ref, y1_ref, y2_ref, y3_ref)):
        part = jnp.dot(y_ref[...].astype(BF16), wo_ref[n * GROUP_WIDTH:(n + 1) * GROUP_WIDTH, :],
                       preferred_element_type=F32)
        acc = part if acc is None else acc + part
    x1 = x_ref[...] + _rms_norm(acc, gm_ref[...])
    h = _rms_norm(x1, g1_ref[...]).astype(BF16)
    acc = None
    for c in range(w1_ref.shape[1] // FFN_COLS):
        cols = slice(c * FFN_COLS, (c + 1) * FFN_COLS)
        f = jnp.maximum(jnp.dot(h, w1_ref[:, cols], preferred_element_type=F32), 0.0)
        part = jnp.dot((f * f).astype(BF16), w2_ref[cols, :], preferred_element_type=F32)
        acc = part if acc is None else acc + part
    o_ref[...] = x1 + _rms_norm(acc, g2_ref[...])


def _out_ffn(l, x2, ys, wo, gm, g1, w1, w2, g2):
    t, d = x2.shape
    dff = w1.shape[2]
    tm = min(FFN_ROWS, t)
    yspec = pl.BlockSpec((tm, GROUP_WIDTH), lambda i: (i, 0))
    vec = _per_layer(l, 1, d)
    once = lambda *shp: _per_layer(l, *shp, pipeline_mode=pl.Buffered(1))
    return pl.pallas_call(
        _out_ffn_kernel,
        grid=(t // tm,),
        in_specs=[pl.BlockSpec((tm, d), lambda i: (i, 0)), yspec, yspec, yspec, yspec,
                  once(d, d), vec, vec, once(d, dff), once(dff, d), vec],
        out_specs=pl.BlockSpec((tm, d), lambda i: (i, 0)),
        out_shape=jax.ShapeDtypeStruct((t, d), F32),
        compiler_params=_cparams("parallel"),
        name="out_ffn",
    )(x2, *ys, wo, gm, g1, w1, w2, g2)


def _rope_perm():
    idx = list(range(HEAD_DIM))
    for c in range(ROPE_HALF):
        idx[c], idx[c + ROPE_HALF] = c + ROPE_HALF, c
    return jnp.array(idx, jnp.int32)


def _split_w_in(w_in):
    gw = GROUP_WIDTH
    edges = [0, gw, 2 * gw, 3 * gw, 4 * gw, 5 * gw, 6 * gw]
    for _ in range(6):
        edges.append(edges[-1] + HEAD_DIM)
    edges.append(edges[-1] + ATT_HEADS * N_BRANCH)
    names = ("cval", "cgate", "pool", "su", "sv", "q", "kc", "vc", "ks", "vs", "kw", "vw", "g")
    return {n: w_in[..., edges[k]:edges[k + 1]] for k, n in enumerate(names)}


def _prep_weights(p):
    depth = p["w_in"].shape[0]
    perm = _rope_perm()
    w = _split_w_in(p["w_in"])
    wa = jnp.concatenate([w["cval"], w["cgate"], w["pool"], w["su"], w["sv"], w["kc"], w["vc"],
                          w["ks"], w["ks"][..., perm], w["kw"], w["kw"][..., perm]], axis=-1).astype(BF16)
    pad = jnp.zeros((depth, D_MODEL, ZB_ROWS - ROW_G - ATT_HEADS * N_BRANCH), F32)
    wb = jnp.swapaxes(jnp.concatenate([w["q"], w["vs"], w["vw"], w["g"], pad], axis=-1), 1, 2).astype(BF16)

    n_grp = len(POOL_WINDOWS)
    pool_bd = (p["pool_w"][:, :, :, None, :] * jnp.eye(n_grp, dtype=F32)[None, :, None, :, None])
    pool_bd = pool_bd.reshape(depth, GROUP_WIDTH, GROUP_WIDTH)
    sgu_bias = jnp.repeat(jnp.swapaxes(p["sgu_b"], 1, 2), GROUP_WIDTH // SGU_HEADS, axis=2)

    w1k = p["cmp_k_w1"].reshape(depth, CMP_BLOCK, HEAD_DIM, HEAD_DIM)
    w1v = p["cmp_v_w1"].reshape(depth, CMP_BLOCK, HEAD_DIM, HEAD_DIM)
    z = jnp.zeros((depth, CMP_STRIDE, HEAD_DIM, HEAD_DIM), F32)

    def kv_diag(a, b):
        return jnp.concatenate([jnp.concatenate([a, z], axis=3), jnp.concatenate([z, b], axis=3)], axis=2)

    cmp_wa = kv_diag(w1k[:, :CMP_STRIDE], w1v[:, :CMP_STRIDE]).astype(BF16)
    cmp_wb = kv_diag(w1k[:, CMP_STRIDE:], w1v[:, CMP_STRIDE:]).astype(BF16)
    ones = jnp.ones((1, 1, HEAD_DIM), F32)
    cmp_pe = jnp.concatenate([p["cmp_k_pe"].reshape(depth, -1, 1) * ones,
                              p["cmp_v_pe"].reshape(depth, -1, 1) * ones], axis=2)
    cmp_w1 = jnp.concatenate([p["cmp_k_w1"], p["cmp_v_w1"]], axis=2)
    zk = jnp.zeros((depth, HEAD_DIM, LANES), F32)
    cmp_w2k = jnp.concatenate([jnp.concatenate([p["cmp_k_w2"], p["cmp_k_w2"][..., perm]], axis=2), zk],
                              axis=1).astype(BF16)
    cmp_w2vt = jnp.concatenate([jnp.zeros((depth, HEAD_DIM, HEAD_DIM), F32),
                                jnp.swapaxes(p["cmp_v_w2"], 1, 2)], axis=2).astype(BF16)

    row = lambda v: v[:, None, :]
    return dict(
        wa=wa, wb=wb, pre_mix=row(p["pre_mix_norm"]), post_mix=row(p["post_mix_norm"]),
        pre_ffn=row(p["pre_ffn_norm"]), post_ffn=row(p["post_ffn_norm"]),
        conv_w=p["conv_w"], conv_b=row(p["conv_b"]), conv_lg=row(p["conv_ln_g"]), conv_lb=row(p["conv_ln_b"]),
        pool_bd=pool_bd.astype(BF16), pool_scale=row(p["pool_scale"]),
        sgu_lg=row(p["sgu_ln_g"]), sgu_lb=row(p["sgu_ln_b"]), sgu_w=p["sgu_w"], sgu_bias=sgu_bias,
        cmp_wa=cmp_wa, cmp_wb=cmp_wb, cmp_pe=cmp_pe, cmp_w1=cmp_w1, cmp_w2k=cmp_w2k, cmp_w2vt=cmp_w2vt,
        w_out=p["w_out"].astype(BF16), ffn_w1=p["ffn_w1"].astype(BF16), ffn_w2=p["ffn_w2"].astype(BF16),
    )


def _overlap_t(s):
    n_blk = s // CMP_STRIDE
    n_slc = s // SLC_BLOCK
    bs = jnp.arange(n_blk)[None, :] * CMP_STRIDE
    ss = jnp.arange(n_slc)[:, None] * SLC_BLOCK
    ov = jnp.clip(jnp.minimum(bs + CMP_BLOCK, ss + SLC_BLOCK) - jnp.maximum(bs, ss), 0)
    return (ov.astype(F32) / CMP_STRIDE).astype(BF16)


def _rope_tables(positions):
    bsz, s = positions.shape
    inv = (ROPE_THETA ** (-jnp.arange(ROPE_HALF, dtype=F32) * 2.0 / ROPE_DIM)).reshape(ROPE_HALF, 1)
    posf = positions.astype(F32)
    n_blk = s // CMP_STRIDE
    pos_end = posf[:, CMP_BLOCK - 1::CMP_STRIDE]
    pos_end = jnp.pad(pos_end, ((0, 0), (0, n_blk - pos_end.shape[1])))
    cos_t, sin_t = _rope_chan_table(posf[:, None, :], inv)
    cos_b, sin_b = _rope_chan_table(pos_end[:, None, :], inv)
    return cos_t, sin_t, cos_b, sin_b


def _mixers(l, x, lw, tables, ov_t):
    za, zb = _proj(l, x, lw["pre_mix"], lw["wa"], lw["wb"])
    y_conv = _conv_mixer(l, za, lw["conv_w"], lw["conv_b"], lw["conv_lg"], lw["conv_lb"])
    y_pool = _pool_mixer(l, za, lw["pool_bd"], lw["pool_scale"])
    y_sgu = _sgu_mixer(l, za, lw["sgu_lg"], lw["sgu_lb"], lw["sgu_w"], lw["sgu_bias"])
    prep = _att_prep(l, za, zb, *tables, lw["cmp_wa"], lw["cmp_wb"],
                     lw["cmp_pe"], lw["cmp_w1"], lw["cmp_w2k"], lw["cmp_w2vt"])
    y_att = _att(*prep, ov_t)
    return y_conv, y_pool, y_sgu, y_att


def kernel(x, positions, pre_mix_norm, post_mix_norm, pre_ffn_norm, post_ffn_norm, w_in, conv_w, conv_b, conv_ln_g, conv_ln_b, pool_w, pool_scale, sgu_ln_g, sgu_ln_b, sgu_w, sgu_b, cmp_k_pe, cmp_k_w1, cmp_k_w2, cmp_v_pe, cmp_v_w1, cmp_v_w2, w_out, ffn_w1, ffn_w2):
    params = dict(pre_mix_norm=pre_mix_norm, post_mix_norm=post_mix_norm, pre_ffn_norm=pre_ffn_norm,
                  post_ffn_norm=post_ffn_norm, w_in=w_in, conv_w=conv_w, conv_b=conv_b,
                  conv_ln_g=conv_ln_g, conv_ln_b=conv_ln_b, pool_w=pool_w, pool_scale=pool_scale,
                  sgu_ln_g=sgu_ln_g, sgu_ln_b=sgu_ln_b, sgu_w=sgu_w, sgu_b=sgu_b,
                  cmp_k_pe=cmp_k_pe, cmp_k_w1=cmp_k_w1, cmp_k_w2=cmp_k_w2, cmp_v_pe=cmp_v_pe,
                  cmp_v_w1=cmp_v_w1, cmp_v_w2=cmp_v_w2, w_out=w_out, ffn_w1=ffn_w1, ffn_w2=ffn_w2)
    bsz, s, d = x.shape
    depth = w_in.shape[0]
    tables = _rope_tables(positions)
    ov_t = _overlap_t(s)
    lw = _prep_weights(params)
    for l in range(depth):
        ys = _mixers(l, x, lw, tables, ov_t)
        x2 = _out_ffn(l, x.reshape(bsz * s, d), [y.reshape(bsz * s, GROUP_WIDTH) for y in ys],
                      lw["w_out"], lw["post_mix"], lw["pre_ffn"], lw["ffn_w1"], lw["ffn_w2"], lw["post_ffn"])
        x = x2.reshape(bsz, s, d)
    return x
```

```python
import functools

import jax
import jax.numpy as jnp
from jax import lax
from jax.experimental import pallas as pl
from jax.experimental.pallas import tpu as pltpu

F32 = jnp.float32
BF16 = jnp.bfloat16

D_MODEL = 1024
GROUP_WIDTH = 256
CONV_WIDTH = 31
POOL_WINDOWS = (2, 4, 8, 16)
POOL_GROUP = 64
SGU_HEADS = 4
SGU_CHUNK = 128
ATT_HEADS = 4
HEAD_DIM = 64
ROPE_DIM = 16
ROPE_HALF = 8
ROPE_THETA = 500000.0
CMP_BLOCK = 32
CMP_STRIDE = 16
SLC_BLOCK = 64
N_SELECT = 8
N_LOCAL = 2
WINDOW = 512
N_BRANCH = 3
D_FF = 4096
NORM_EPS = 1e-6
NEG_INF = -1e30
FORCE_SCORE = 1e9

LANES = 128
SUBLANES = 8
VMEM_LIMIT_BYTES = 56 * 1024 * 1024

PROJ_ROWS = 512
FFN_ROWS = 512
FFN_COLS = 512
CONV_ROWS = 64
POOL_ROWS = 64
ATT_Q = 256
ATT_K = 256
LOG2_E = 1.4426950408889634
V_ROWS = 80
MASK_BIAS = -2e30

COL_CVAL, COL_CGATE, COL_POOL, COL_SGU_U, COL_SGU_V = 0, 1, 2, 3, 4
COL_KCVC, COL_KS, COL_KW = 10, 11, 12
ZA_WIDTH = 1664
ROW_Q, ROW_VS, ROW_VW, ROW_G = 0, 256, 320, 384
ZB_ROWS = 400


def _cparams(*sem):
    return pltpu.CompilerParams(dimension_semantics=sem, vmem_limit_bytes=VMEM_LIMIT_BYTES)


def _per_layer(l, *shape, pipeline_mode=None):
    zeros = (0,) * len(shape)
    extra = {} if pipeline_mode is None else {"pipeline_mode": pipeline_mode}
    return pl.BlockSpec((None,) + shape, lambda *_: (l,) + zeros, **extra)


def _gelu(x):
    return 0.5 * x * (1.0 + jnp.tanh(0.7978845608028654 * (x + 0.044715 * (x * x * x))))


def _sigmoid(x):
    return 0.5 * jnp.tanh(0.5 * x) + 0.5


def _layer_norm(x, g, b):
    mu = jnp.mean(x, axis=-1, keepdims=True)
    d = x - mu
    var = jnp.mean(d * d, axis=-1, keepdims=True)
    return d * lax.rsqrt(var + NORM_EPS) * g + b


def _rms_norm(x, g):
    return x * lax.rsqrt(jnp.mean(x * x, axis=-1, keepdims=True) + NORM_EPS) * g


def _rope_chan_kernel(pos_ref, inv_ref, cos_ref, sin_ref):
    ang = inv_ref[...] * pos_ref[0]
    cos_ref[0] = jnp.cos(ang)
    sin_ref[0] = jnp.sin(ang)


def _rope_chan_table(pos_row, inv_col):
    b, _, s = pos_row.shape
    spec = pl.BlockSpec((1, ROPE_HALF, s), lambda i: (i, 0, 0))
    return pl.pallas_call(
        _rope_chan_kernel,
        grid=(b,),
        in_specs=[pl.BlockSpec((1, 1, s), lambda i: (i, 0, 0)),
                  pl.BlockSpec((ROPE_HALF, 1), lambda i: (0, 0))],
        out_specs=[spec, spec],
        out_shape=[jax.ShapeDtypeStruct((b, ROPE_HALF, s), F32)] * 2,
        compiler_params=_cparams("parallel"),
        name="rope_chan_table",
    )(pos_row, inv_col)


def _proj_kernel(x_ref, g_ref, wa_ref, wb_ref, za_ref, zb_ref):
    h = _rms_norm(x_ref[0], g_ref[...]).astype(BF16)
    za_ref[0] = jnp.dot(h, wa_ref[...], preferred_element_type=F32)
    zb_ref[0] = lax.dot_general(wb_ref[...], h, (((1,), (1,)), ((), ())),
                                preferred_element_type=F32)


def _proj(l, x, g, wa, wb):
    b, s, d = x.shape
    tm = min(PROJ_ROWS, s)
    return pl.pallas_call(
        _proj_kernel,
        grid=(b, s // tm),
        in_specs=[pl.BlockSpec((1, tm, d), lambda i, j: (i, j, 0)),
                  _per_layer(l, 1, d), _per_layer(l, d, ZA_WIDTH), _per_layer(l, ZB_ROWS, d)],
        out_specs=[pl.BlockSpec((1, tm, ZA_WIDTH), lambda i, j: (i, j, 0)),
                   pl.BlockSpec((1, ZB_ROWS, tm), lambda i, j: (i, 0, j))],
        out_shape=[jax.ShapeDtypeStruct((b, s, ZA_WIDTH), F32),
                   jax.ShapeDtypeStruct((b, ZB_ROWS, s), F32)],
        compiler_params=_cparams("parallel", "parallel"),
        name="in_proj",
    )(x, g, wa, wb)


CONV_PAD = 32


def _conv_kernel(a_ref, gate_ref, w_ref, b_ref, lg_ref, lb_ref, o_ref, hp_ref):
    s = a_ref.shape[1]
    hp_ref[0:CONV_PAD, :] = jnp.zeros((CONV_PAD, GROUP_WIDTH), F32)
    hp_ref[CONV_PAD:CONV_PAD + s, :] = a_ref[0] * _sigmoid(gate_ref[0])
    shift = CONV_PAD - (CONV_WIDTH - 1)

    def body(i, carry):
        t0 = pl.multiple_of(i * CONV_ROWS, CONV_ROWS)
        acc = jnp.zeros((CONV_ROWS, GROUP_WIDTH), F32) + b_ref[...]
        win = hp_ref[pl.ds(t0, CONV_ROWS + CONV_PAD), :]
        n_win = CONV_ROWS + CONV_PAD
        for r in range(SUBLANES):
            taps = [k for k in range(CONV_WIDTH) if (k + shift) % SUBLANES == r]
            wr = win if r == 0 else pltpu.roll(win, n_win - r, axis=0)
            for k in taps:
                off = k + shift - r
                acc = acc + wr[off:off + CONV_ROWS, :] * w_ref[k:k + 1, :]
        y = _layer_norm(acc, lg_ref[...], lb_ref[...])
        o_ref[0, pl.ds(t0, CONV_ROWS), :] = y * _sigmoid(y)
        return carry

    lax.fori_loop(0, s // CONV_ROWS, body, 0, unroll=2)


def _conv_mixer(l, za, w, b, lg, lb):
    bsz, s, _ = za.shape
    vec = _per_layer(l, 1, GROUP_WIDTH)
    return pl.pallas_call(
        _conv_kernel,
        grid=(bsz,),
        in_specs=[pl.BlockSpec((1, s, GROUP_WIDTH), lambda i: (i, 0, COL_CVAL)),
                  pl.BlockSpec((1, s, GROUP_WIDTH), lambda i: (i, 0, COL_CGATE)),
                  _per_layer(l, CONV_WIDTH, GROUP_WIDTH), vec, vec, vec],
        out_specs=pl.BlockSpec((1, s, GROUP_WIDTH), lambda i: (i, 0, 0)),
        out_shape=jax.ShapeDtypeStruct((bsz, s, GROUP_WIDTH), F32),
        scratch_shapes=[pltpu.VMEM((CONV_PAD + s, GROUP_WIDTH), F32)],
        compiler_params=_cparams("parallel"),
        name="conv_mixer",
    )(za, za, w, b, lg, lb)


POOL_PAD = 16


def _pool_kernel(p_ref, w_ref, sc_ref, o_ref, pp_ref):
    s = p_ref.shape[1]
    pp_ref[0:POOL_PAD, :] = jnp.zeros((POOL_PAD, GROUP_WIDTH), F32)
    pp_ref[POOL_PAD:POOL_PAD + s, :] = p_ref[0]
    lane = lax.broadcasted_iota(jnp.int32, (1, GROUP_WIDTH), 1)
    grp = lane // POOL_GROUP
    win = jnp.where(grp == 0, POOL_WINDOWS[0],
                    jnp.where(grp == 1, POOL_WINDOWS[1],
                              jnp.where(grp == 2, POOL_WINDOWS[2], POOL_WINDOWS[3])))

    def body(i, carry):
        t0 = pl.multiple_of(i * POOL_ROWS, POOL_ROWS)
        rows = pp_ref[pl.ds(t0, POOL_ROWS + POOL_PAD), :]
        p = rows[POOL_PAD:POOL_PAD + POOL_ROWS, :]
        acc = rows
        sums = []
        width = 1
        for w in POOL_WINDOWS:
            while width < w:
                acc = acc + pltpu.roll(acc, width, axis=0)
                width *= 2
            sums.append(acc[POOL_PAD:POOL_PAD + POOL_ROWS, :])
        total = jnp.where(grp == 0, sums[0],
                          jnp.where(grp == 1, sums[1],
                                    jnp.where(grp == 2, sums[2], sums[3])))
        t = t0 + lax.broadcasted_iota(jnp.int32, (POOL_ROWS, 1), 0)
        count = jnp.minimum(t + 1, win).astype(F32)
        mixed = total / count - p
        y = jnp.dot(mixed.astype(BF16), w_ref[...], preferred_element_type=F32)
        o_ref[0, pl.ds(t0, POOL_ROWS), :] = y * sc_ref[...]
        return carry

    lax.fori_loop(0, s // POOL_ROWS, body, 0, unroll=2)


def _pool_mixer(l, za, w_bd, scale):
    bsz, s, _ = za.shape
    return pl.pallas_call(
        _pool_kernel,
        grid=(bsz,),
        in_specs=[pl.BlockSpec((1, s, GROUP_WIDTH), lambda i: (i, 0, COL_POOL)),
                  _per_layer(l, GROUP_WIDTH, GROUP_WIDTH), _per_layer(l, 1, GROUP_WIDTH)],
        out_specs=pl.BlockSpec((1, s, GROUP_WIDTH), lambda i: (i, 0, 0)),
        out_shape=jax.ShapeDtypeStruct((bsz, s, GROUP_WIDTH), F32),
        scratch_shapes=[pltpu.VMEM((POOL_PAD + s, GROUP_WIDTH), F32)],
        compiler_params=_cparams("parallel"),
        name="pool_mixer",
    )(za, w_bd, scale)


def _sgu_kernel(u_ref, v_ref, lg_ref, lb_ref, w_ref, bias_ref, o_ref):
    s = u_ref.shape[1]
    row = lax.broadcasted_iota(jnp.int32, (SGU_CHUNK, SGU_CHUNK), 0)
    col = lax.broadcasted_iota(jnp.int32, (SGU_CHUNK, SGU_CHUNK), 1)
    head = lax.broadcasted_iota(jnp.int32, (1, GROUP_WIDTH), 1) // (GROUP_WIDTH // SGU_HEADS)

    def body(c, carry):
        t0 = pl.multiple_of(c * SGU_CHUNK, SGU_CHUNK)
        v = _layer_norm(_gelu(v_ref[0, pl.ds(t0, SGU_CHUNK), :]), lg_ref[...], lb_ref[...])
        vb = v.astype(BF16)
        mixed = jnp.zeros((SGU_CHUNK, GROUP_WIDTH), F32)
        for h in range(SGU_HEADS):
            w = jnp.where(row >= col, w_ref[h], 0.0).astype(BF16)
            r = jnp.dot(w, vb, preferred_element_type=F32)
            mixed = jnp.where(head == h, r, mixed)
        u = _gelu(u_ref[0, pl.ds(t0, SGU_CHUNK), :])
        o_ref[0, pl.ds(t0, SGU_CHUNK), :] = u * (mixed + bias_ref[...])
        return carry

    lax.fori_loop(0, s // SGU_CHUNK, body, 0, unroll=2)


def _sgu_mixer(l, za, lg, lb, w, bias):
    bsz, s, _ = za.shape
    vec = _per_layer(l, 1, GROUP_WIDTH)
    return pl.pallas_call(
        _sgu_kernel,
        grid=(bsz,),
        in_specs=[pl.BlockSpec((1, s, GROUP_WIDTH), lambda i: (i, 0, COL_SGU_U)),
                  pl.BlockSpec((1, s, GROUP_WIDTH), lambda i: (i, 0, COL_SGU_V)),
                  vec, vec,
                  _per_layer(l, SGU_HEADS, SGU_CHUNK, SGU_CHUNK), _per_layer(l, SGU_CHUNK, GROUP_WIDTH)],
        out_specs=pl.BlockSpec((1, s, GROUP_WIDTH), lambda i: (i, 0, 0)),
        out_shape=jax.ShapeDtypeStruct((bsz, s, GROUP_WIDTH), F32),
        compiler_params=_cparams("parallel"),
        name="sgu_mixer",
    )(za, za, lg, lb, w, bias)


def _att_prep_kernel(kcvc_ref, ksx_ref, kwx_ref, zb_ref, cos_ref, sin_ref, cosb_ref, sinb_ref,
                     wa_ref, wb_ref, pe_ref, w1_ref, w2k_ref, w2vt_ref,
                     q_ref, ks_ref, kw_ref, vs_ref, vw_ref, kc_ref, vct_ref, g_ref, sh_ref):
    s = ksx_ref.shape[1]
    n_blk = s // CMP_STRIDE

    def tok_table(cos, sin):
        n = cos.shape[1]
        rest = HEAD_DIM - ROPE_DIM
        rows = [cos, cos, jnp.ones((rest, n), F32), -sin, sin, jnp.zeros((rest, n), F32)]
        return jnp.concatenate(rows, axis=0).T

    def rope_wide(x, cs):
        r = x * cs
        return r + pltpu.roll(r, HEAD_DIM, axis=1)

    def rope_tok(x, cs):
        return rope_wide(x, cs)[:, :HEAD_DIM]

    cos = cos_ref[0]
    sin = sin_ref[0]
    cs = tok_table(cos, sin)
    lane = lax.broadcasted_iota(jnp.int32, (s, LANES), 1)
    blk = lax.broadcasted_iota(jnp.int32, (s, LANES), 0) // SLC_BLOCK
    ks_ref[0] = jnp.where(lane < HEAD_DIM, rope_wide(ksx_ref[0], cs),
                          jnp.where(lane - HEAD_DIM == blk, 1.0, 0.0)).astype(BF16)
    kw_ref[0] = rope_tok(kwx_ref[0], cs).astype(BF16)

    scale = HEAD_DIM ** -0.5 * LOG2_E
    parts = []
    for h in range(ATT_HEADS):
        r0 = ROW_Q + h * HEAD_DIM
        x1 = zb_ref[0, r0:r0 + ROPE_HALF, :]
        x2 = zb_ref[0, r0 + ROPE_HALF:r0 + ROPE_DIM, :]
        parts += [x1 * cos - x2 * sin, x2 * cos + x1 * sin, zb_ref[0, r0 + ROPE_DIM:r0 + HEAD_DIM, :]]
    q_ref[0] = (jnp.concatenate(parts, axis=0) * scale).astype(BF16)

    ones_row = jnp.where(lax.broadcasted_iota(jnp.int32, (V_ROWS - HEAD_DIM, ATT_K), 0) == 0, 1.0, 0.0)
    for c in range(s // ATT_K):
        cols = slice(c * ATT_K, (c + 1) * ATT_K)
        vs_ref[0, c] = jnp.concatenate([zb_ref[0, ROW_VS:ROW_VS + HEAD_DIM, cols], ones_row], axis=0).astype(BF16)
        vw_ref[0, c] = jnp.concatenate([zb_ref[0, ROW_VW:ROW_VW + HEAD_DIM, cols], ones_row], axis=0).astype(BF16)
    g_ref[0] = _sigmoid(zb_ref[0, ROW_G:ROW_G + 16, :])

    acc_a = jnp.zeros((n_blk, LANES), F32)
    acc_b = jnp.zeros((n_blk, LANES), F32)
    for l in range(CMP_STRIDE):
        xl = kcvc_ref[0, pl.ds(l, n_blk, stride=CMP_STRIDE), :].astype(BF16)
        acc_a = acc_a + jnp.dot(xl, wa_ref[l], preferred_element_type=F32)
        acc_b = acc_b + jnp.dot(xl, wb_ref[l], preferred_element_type=F32)
    sh_ref[0:n_blk, :] = acc_b
    sh_ref[n_blk:n_blk + SUBLANES, :] = jnp.zeros((SUBLANES, LANES), F32)
    pe_term = jnp.sum(pe_ref[...] * w1_ref[...], axis=0, keepdims=True)
    hid = _gelu(acc_a + sh_ref[1:n_blk + 1, :] + pe_term).astype(BF16)
    kk = jnp.dot(hid, w2k_ref[...], preferred_element_type=F32)
    kc_ref[0] = rope_tok(kk, tok_table(cosb_ref[0], sinb_ref[0])).astype(BF16)
    vct_ref[0] = lax.dot_general(w2vt_ref[...], hid, (((1,), (1,)), ((), ())),
                                 preferred_element_type=F32).astype(BF16)


def _att_prep(l, za, zb, cos_t, sin_t, cos_b, sin_b, wa, wb, pe, w1, w2k, w2vt):
    bsz, s, _ = za.shape
    n_blk = s // CMP_STRIDE
    nt = s // ATT_K
    tok = lambda c: pl.BlockSpec((1, s, LANES), lambda i: (i, 0, c))
    per_b3 = lambda shp: pl.BlockSpec((1,) + shp, lambda i: (i, 0, 0))
    per_b4 = lambda shp: pl.BlockSpec((1,) + shp, lambda i: (i, 0, 0, 0))
    return pl.pallas_call(
        _att_prep_kernel,
        grid=(bsz,),
        in_specs=[tok(COL_KCVC), tok(COL_KS), tok(COL_KW),
                  per_b3((ZB_ROWS, s)),
                  per_b3((ROPE_HALF, s)), per_b3((ROPE_HALF, s)),
                  per_b3((ROPE_HALF, n_blk)), per_b3((ROPE_HALF, n_blk)),
                  _per_layer(l, CMP_STRIDE, LANES, LANES), _per_layer(l, CMP_STRIDE, LANES, LANES),
                  _per_layer(l, CMP_BLOCK * HEAD_DIM, LANES), _per_layer(l, CMP_BLOCK * HEAD_DIM, LANES),
                  _per_layer(l, LANES, LANES), _per_layer(l, HEAD_DIM, LANES)],
        out_specs=[per_b3((ATT_HEADS * HEAD_DIM, s)), per_b3((s, LANES)), per_b3((s, HEAD_DIM)),
                   per_b4((nt, V_ROWS, ATT_K)), per_b4((nt, V_ROWS, ATT_K)),
                   per_b3((n_blk, HEAD_DIM)), per_b3((HEAD_DIM, n_blk)), per_b3((16, s))],
        out_shape=[jax.ShapeDtypeStruct((bsz, ATT_HEADS * HEAD_DIM, s), BF16),
                   jax.ShapeDtypeStruct((bsz, s, LANES), BF16),
                   jax.ShapeDtypeStruct((bsz, s, HEAD_DIM), BF16),
                   jax.ShapeDtypeStruct((bsz, nt, V_ROWS, ATT_K), BF16),
                   jax.ShapeDtypeStruct((bsz, nt, V_ROWS, ATT_K), BF16),
                   jax.ShapeDtypeStruct((bsz, n_blk, HEAD_DIM), BF16),
                   jax.ShapeDtypeStruct((bsz, HEAD_DIM, n_blk), BF16),
                   jax.ShapeDtypeStruct((bsz, 16, s), F32)],
        scratch_shapes=[pltpu.VMEM((n_blk + SUBLANES, LANES), F32)],
        compiler_params=_cparams("parallel"),
        name="att_prep",
    )(za, za, za, zb, cos_t, sin_t, cos_b, sin_b, wa, wb, pe, w1, w2k, w2vt)


def _att_kernel(q_ref, ks_ref, kw_ref, vs_ref, vw_ref, kc_ref, vct_ref, g_ref, ov_ref,
                o_ref, qa_ref, sa_ref, sb_ref, m_ref, acc_ref):
    n_blk = kc_ref.shape[1]
    n_slc = ov_ref.shape[0]
    tq = ATT_Q
    wide = ATT_HEADS * tq
    i = pl.program_id(1)
    t0 = i * tq

    q = q_ref[0]
    qs = jnp.concatenate([q[h * HEAD_DIM:(h + 1) * HEAD_DIM] for h in range(ATT_HEADS)], axis=1)
    lane_w = lax.broadcasted_iota(jnp.int32, (1, wide), 1)
    t_w = t0 + (lane_w & (tq - 1))
    t_q = t0 + lax.broadcasted_iota(jnp.int32, (1, tq), 1)
    heads = lambda a: jnp.concatenate([a] * ATT_HEADS, axis=1)

    s_c = jnp.dot(kc_ref[0], qs, preferred_element_type=F32)
    blk_end = lax.broadcasted_iota(jnp.int32, (n_blk, 1), 0) * CMP_STRIDE + (CMP_BLOCK - 1)
    cmask = blk_end <= t_w
    s_m = jnp.where(cmask, s_c, NEG_INF)
    e = jnp.exp2(s_m - jnp.max(s_m, axis=0, keepdims=True))
    p_c = jnp.where(cmask, e * (1.0 / jnp.sum(e, axis=0, keepdims=True)), 0.0)
    o_cmp = jnp.dot(vct_ref[0], p_c.astype(BF16), preferred_element_type=F32)

    p_sum = p_c[:, 0:tq]
    for h in range(1, ATT_HEADS):
        p_sum = p_sum + p_c[:, h * tq:(h + 1) * tq]
    p_hi = p_sum.astype(BF16)
    p_lo = (p_sum - p_hi.astype(F32)).astype(BF16)
    imp = (jnp.dot(ov_ref[...], p_hi, preferred_element_type=F32)
           + jnp.dot(ov_ref[...], p_lo, preferred_element_type=F32))
    j = lax.broadcasted_iota(jnp.int32, (n_slc, 1), 0)
    back = t_q // SLC_BLOCK - j
    forced = (j == 0) | ((back >= 0) & (back < N_LOCAL))
    imp = jnp.where(forced, FORCE_SCORE, jnp.where(back < 0, -1.0, imp))
    rank = jnp.zeros((n_slc, tq), F32)
    for r in range(n_slc):
        row = imp[r:r + 1, :]
        ahead = (row > imp) | ((row == imp) & (j > r))
        rank = rank + jnp.where(ahead, 1.0, 0.0)
    sel_bias = jnp.where((rank < min(N_SELECT, n_slc)) & (back >= 0), 0.0, MASK_BIAS)
    qa_ref[0:HEAD_DIM, :] = qs
    qa_ref[HEAD_DIM:HEAD_DIM + n_slc, :] = heads(sel_bias).astype(BF16)
    qa_ref[HEAD_DIM + n_slc:, :] = jnp.zeros((LANES - HEAD_DIM - n_slc, wide), BF16)

    k_iota = lax.broadcasted_iota(jnp.int32, (ATT_K, 1), 0)

    n_win = (WINDOW + tq) // ATT_K
    first = jnp.maximum(i - WINDOW // ATT_K, 0)
    kw0 = pl.multiple_of(first * ATT_K, ATT_K)
    s_w = jnp.dot(kw_ref[0, pl.ds(kw0, n_win * ATT_K), :], qs, preferred_element_type=F32)
    diff = t_q - (kw0 + lax.broadcasted_iota(jnp.int32, (n_win * ATT_K, 1), 0))
    s_w = s_w + heads(jnp.where((diff >= 0) & (diff < WINDOW), 0.0, -jnp.inf))
    p_w = jnp.exp2(s_w - jnp.max(s_w, axis=0, keepdims=True)).astype(BF16)
    acc_w = None
    for n in range(n_win):
        part = jnp.dot(vw_ref[0, first + n], p_w[n * ATT_K:(n + 1) * ATT_K], preferred_element_type=F32)
        acc_w = part if acc_w is None else acc_w + part

    m_ref[...] = jnp.full((1, wide), NEG_INF, F32)
    acc_ref[...] = jnp.zeros((V_ROWS, wide), F32)

    def scores(kt, causal=False, live=None):
        k0 = pl.multiple_of(kt * ATT_K, ATT_K)
        s_t = jnp.dot(ks_ref[0, pl.ds(k0, ATT_K), :], qa_ref[...], preferred_element_type=F32)
        if causal:
            visible = (k0 + k_iota) <= t_q
            if live is not None:
                visible = visible & live
            s_t = jnp.where(heads(visible), s_t, -jnp.inf)
        return s_t

    def consume(s_ref, kt):
        m_old = m_ref[...]
        m_new = jnp.maximum(m_old, jnp.max(s_ref[...], axis=0, keepdims=True))
        alpha = jnp.exp2(m_old - m_new)
        p = jnp.exp2(s_ref[...] - m_new).astype(BF16)
        m_ref[...] = m_new
        acc_ref[...] = alpha * acc_ref[...] + jnp.dot(vs_ref[0, kt], p, preferred_element_type=F32)

    single = 1 - i % 2
    sa_ref[...] = scores(0, causal=True, live=single == 1)
    consume(sa_ref, 0)
    sa_ref[...] = scores(single)

    def pair_body(n, carry):
        a = single + 2 * n
        sb_ref[...] = scores(a + 1, causal=True)
        consume(sa_ref, a)
        sa_ref[...] = scores(jnp.minimum(a + 2, i))
        consume(sb_ref, a + 1)
        return carry

    lax.fori_loop(0, (i + 1 - single) // 2, pair_body, 0)
    acc_s = acc_ref[...]

    g = g_ref[0]
    gate = lambda br: jnp.concatenate(
        [g[h * N_BRANCH + br:h * N_BRANCH + br + 1, :] for h in range(ATT_HEADS)], axis=1)
    l_s = acc_s[HEAD_DIM:HEAD_DIM + 1, :]
    l_w = acc_w[HEAD_DIM:HEAD_DIM + 1, :]
    o = (gate(0) * o_cmp + (gate(1) * (1.0 / l_s)) * acc_s[:HEAD_DIM]
         + (gate(2) * (1.0 / l_w)) * acc_w[:HEAD_DIM])
    o = jnp.concatenate([o[:, h * tq:(h + 1) * tq] for h in range(ATT_HEADS)], axis=0)
    o_ref[0] = o.T


def _att(q_t, ks, kw, vs_t, vw_t, kc, vc_t, g_t, ov_t):
    bsz, _, s = q_t.shape
    n_blk = kc.shape[1]
    nt = s // ATT_K
    n_slc = s // SLC_BLOCK
    wide = ATT_HEADS * ATT_Q
    assert ATT_Q == ATT_K, "the selected-branch tile pairing assumes one diagonal key tile per query tile"
    per_b3 = lambda shp: pl.BlockSpec((1,) + shp, lambda b, i: (b, 0, 0))
    per_b4 = lambda shp: pl.BlockSpec((1,) + shp, lambda b, i: (b, 0, 0, 0))
    return pl.pallas_call(
        _att_kernel,
        grid=(bsz, s // ATT_Q),
        in_specs=[pl.BlockSpec((1, ATT_HEADS * HEAD_DIM, ATT_Q), lambda b, i: (b, 0, i)),
                  per_b3((s, LANES)), per_b3((s, HEAD_DIM)),
                  per_b4((nt, V_ROWS, ATT_K)), per_b4((nt, V_ROWS, ATT_K)),
                  per_b3((n_blk, HEAD_DIM)), per_b3((HEAD_DIM, n_blk)),
                  pl.BlockSpec((1, 16, ATT_Q), lambda b, i: (b, 0, i)),
                  pl.BlockSpec((n_slc, n_blk), lambda b, i: (0, 0))],
        out_specs=pl.BlockSpec((1, ATT_Q, GROUP_WIDTH), lambda b, i: (b, i, 0)),
        out_shape=jax.ShapeDtypeStruct((bsz, s, GROUP_WIDTH), F32),
        scratch_shapes=[pltpu.VMEM((LANES, wide), BF16),
                        pltpu.VMEM((ATT_K, wide), F32), pltpu.VMEM((ATT_K, wide), F32),
                        pltpu.VMEM((1, wide), F32), pltpu.VMEM((V_ROWS, wide), F32)],
        compiler_params=_cparams("parallel", "arbitrary"),
        name="sparse_attention",
    )(q_t, ks, kw, vs_t, vw_t, kc, vc_t, g_t, ov_t)


def _out_ffn_kernel(x_ref, y0_ref, y1_ref, y2_ref, y3_ref, wo_ref, gm_ref, g1_ref, w1_ref, w2_ref,
                    g2_ref, o_ref):
    acc = None
    for n, y_ref in enumerate((y0_ref, y1_ref, y2_ref, y3_ref)):
        part = jnp.dot(y_ref[...].astype(BF16), wo_ref[n * GROUP_WIDTH:(n + 1) * GROUP_WIDTH, :],
                       preferred_element_type=F32)
        acc = part if acc is None else acc + part
    x1 = x_ref[...] + _rms_norm(acc, gm_ref[...])
    h = _rms_norm(x1, g1_ref[...]).astype(BF16)
    acc = None
    for c in range(w1_ref.shape[1] // FFN_COLS):
        cols = slice(c * FFN_COLS, (c + 1) * FFN_COLS)
        f = jnp.maximum(jnp.dot(h, w1_ref[:, cols], preferred_element_type=F32), 0.0)
        part = jnp.dot((f * f).astype(BF16), w2_ref[cols, :], preferred_element_type=F32)
        acc = part if acc is None else acc + part
    o_ref[...] = x1 + _rms_norm(acc, g2_ref[...])


def _out_ffn(l, x2, ys, wo, gm, g1, w1, w2, g2):
    t, d = x2.shape
    dff = w1.shape[2]
    tm = min(FFN_ROWS, t)
    yspec = pl.BlockSpec((tm, GROUP_WIDTH), lambda i: (i, 0))
    vec = _per_layer(l, 1, d)
    once = lambda *shp: _per_layer(l, *shp, pipeline_mode=pl.Buffered(1))
    return pl.pallas_call(
        _out_ffn_kernel,
        grid=(t // tm,),
        in_specs=[pl.BlockSpec((tm, d), lambda i: (i, 0)), yspec, yspec, yspec, yspec,
                  once(d, d), vec, vec, once(d, dff), once(dff, d), vec],
        out_specs=pl.BlockSpec((tm, d), lambda i: (i, 0)),
        out_shape=jax.ShapeDtypeStruct((t, d), F32),
        compiler_params=_cparams("parallel"),
        name="out_ffn",
    )(x2, *ys, wo, gm, g1, w1, w2, g2)


def _rope_perm():
    idx = list(range(HEAD_DIM))
    for c in range(ROPE_HALF):
        idx[c], idx[c + ROPE_HALF] = c + ROPE_HALF, c
    return jnp.array(idx, jnp.int32)


def _split_w_in(w_in):
    gw = GROUP_WIDTH
    edges = [0, gw, 2 * gw, 3 * gw, 4 * gw, 5 * gw, 6 * gw]
    for _ in range(6):
        edges.append(edges[-1] + HEAD_DIM)
    edges.append(edges[-1] + ATT_HEADS * N_BRANCH)
    names = ("cval", "cgate", "pool", "su", "sv", "q", "kc", "vc", "ks", "vs", "kw", "vw", "g")
    return {n: w_in[..., edges[k]:edges[k + 1]] for k, n in enumerate(names)}


def _prep_weights(p):
    depth = p["w_in"].shape[0]
    perm = _rope_perm()
    w = _split_w_in(p["w_in"])
    wa = jnp.concatenate([w["cval"], w["cgate"], w["pool"], w["su"], w["sv"], w["kc"], w["vc"],
                          w["ks"], w["ks"][..., perm], w["kw"], w["kw"][..., perm]], axis=-1).astype(BF16)
    pad = jnp.zeros((depth, D_MODEL, ZB_ROWS - ROW_G - ATT_HEADS * N_BRANCH), F32)
    wb = jnp.swapaxes(jnp.concatenate([w["q"], w["vs"], w["vw"], w["g"], pad], axis=-1), 1, 2).astype(BF16)

    n_grp = len(POOL_WINDOWS)
    pool_bd = (p["pool_w"][:, :, :, None, :] * jnp.eye(n_grp, dtype=F32)[None, :, None, :, None])
    pool_bd = pool_bd.reshape(depth, GROUP_WIDTH, GROUP_WIDTH)
    sgu_bias = jnp.repeat(jnp.swapaxes(p["sgu_b"], 1, 2), GROUP_WIDTH // SGU_HEADS, axis=2)

    w1k = p["cmp_k_w1"].reshape(depth, CMP_BLOCK, HEAD_DIM, HEAD_DIM)
    w1v = p["cmp_v_w1"].reshape(depth, CMP_BLOCK, HEAD_DIM, HEAD_DIM)
    z = jnp.zeros((depth, CMP_STRIDE, HEAD_DIM, HEAD_DIM), F32)

    def kv_diag(a, b):
        return jnp.concatenate([jnp.concatenate([a, z], axis=3), jnp.concatenate([z, b], axis=3)], axis=2)

    cmp_wa = kv_diag(w1k[:, :CMP_STRIDE], w1v[:, :CMP_STRIDE]).astype(BF16)
    cmp_wb = kv_diag(w1k[:, CMP_STRIDE:], w1v[:, CMP_STRIDE:]).astype(BF16)
    ones = jnp.ones((1, 1, HEAD_DIM), F32)
    cmp_pe = jnp.concatenate([p["cmp_k_pe"].reshape(depth, -1, 1) * ones,
                              p["cmp_v_pe"].reshape(depth, -1, 1) * ones], axis=2)
    cmp_w1 = jnp.concatenate([p["cmp_k_w1"], p["cmp_v_w1"]], axis=2)
    zk = jnp.zeros((depth, HEAD_DIM, LANES), F32)
    cmp_w2k = jnp.concatenate([jnp.concatenate([p["cmp_k_w2"], p["cmp_k_w2"][..., perm]], axis=2), zk],
                              axis=1).astype(BF16)
    cmp_w2vt = jnp.concatenate([jnp.zeros((depth, HEAD_DIM, HEAD_DIM), F32),
                                jnp.swapaxes(p["cmp_v_w2"], 1, 2)], axis=2).astype(BF16)

    row = lambda v: v[:, None, :]
    return dict(
        wa=wa, wb=wb, pre_mix=row(p["pre_mix_norm"]), post_mix=row(p["post_mix_norm"]),
        pre_ffn=row(p["pre_ffn_norm"]), post_ffn=row(p["post_ffn_norm"]),
        conv_w=p["conv_w"], conv_b=row(p["conv_b"]), conv_lg=row(p["conv_ln_g"]), conv_lb=row(p["conv_ln_b"]),
        pool_bd=pool_bd.astype(BF16), pool_scale=row(p["pool_scale"]),
        sgu_lg=row(p["sgu_ln_g"]), sgu_lb=row(p["sgu_ln_b"]), sgu_w=p["sgu_w"], sgu_bias=sgu_bias,
        cmp_wa=cmp_wa, cmp_wb=cmp_wb, cmp_pe=cmp_pe, cmp_w1=cmp_w1, cmp_w2k=cmp_w2k, cmp_w2vt=cmp_w2vt,
        w_out=p["w_out"].astype(BF16), ffn_w1=p["ffn_w1"].astype(BF16), ffn_w2=p["ffn_w2"].astype(BF16),
    )


def _overlap_t(s):
    n_blk = s // CMP_STRIDE
    n_slc = s // SLC_BLOCK
    bs = jnp.arange(n_blk)[None, :] * CMP_STRIDE
    ss = jnp.arange(n_slc)[:, None] * SLC_BLOCK
    ov = jnp.clip(jnp.minimum(bs + CMP_BLOCK, ss + SLC_BLOCK) - jnp.maximum(bs, ss), 0)
    return (ov.astype(F32) / CMP_STRIDE).astype(BF16)


def _rope_tables(positions):
    bsz, s = positions.shape
    inv = (ROPE_THETA ** (-jnp.arange(ROPE_HALF, dtype=F32) * 2.0 / ROPE_DIM)).reshape(ROPE_HALF, 1)
    posf = positions.astype(F32)
    n_blk = s // CMP_STRIDE
    pos_end = posf[:, CMP_BLOCK - 1::CMP_STRIDE]
    pos_end = jnp.pad(pos_end, ((0, 0), (0, n_blk - pos_end.shape[1])))
    cos_t, sin_t = _rope_chan_table(posf[:, None, :], inv)
    cos_b, sin_b = _rope_chan_table(pos_end[:, None, :], inv)
    return cos_t, sin_t, cos_b, sin_b


def _mixers(l, x, lw, tables, ov_t):
    za, zb = _proj(l, x, lw["pre_mix"], lw["wa"], lw["wb"])
    y_conv = _conv_mixer(l, za, lw["conv_w"], lw["conv_b"], lw["conv_lg"], lw["conv_lb"])
    y_pool = _pool_mixer(l, za, lw["pool_bd"], lw["pool_scale"])
    y_sgu = _sgu_mixer(l, za, lw["sgu_lg"], lw["sgu_lb"], lw["sgu_w"], lw["sgu_bias"])
    prep = _att_prep(l, za, zb, *tables, lw["cmp_wa"], lw["cmp_wb"],
                     lw["cmp_pe"], lw["cmp_w1"], lw["cmp_w2k"], lw["cmp_w2vt"])
    y_att = _att(*prep, ov_t)
    return y_conv, y_pool, y_sgu, y_att


def kernel(x, positions, pre_mix_norm, post_mix_norm, pre_ffn_norm, post_ffn_norm, w_in, conv_w, conv_b, conv_ln_g, conv_ln_b, pool_w, pool_scale, sgu_ln_g, sgu_ln_b, sgu_w, sgu_b, cmp_k_pe, cmp_k_w1, cmp_k_w2, cmp_v_pe, cmp_v_w1, cmp_v_w2, w_out, ffn_w1, ffn_w2):
    params = dict(pre_mix_norm=pre_mix_norm, post_mix_norm=post_mix_norm, pre_ffn_norm=pre_ffn_norm,
                  post_ffn_norm=post_ffn_norm, w_in=w_in, conv_w=conv_w, conv_b=conv_b,
                  conv_ln_g=conv_ln_g, conv_ln_b=conv_ln_b, pool_w=pool_w, pool_scale=pool_scale,
                  sgu_ln_g=sgu_ln_g, sgu_ln_b=sgu_ln_b, sgu_w=sgu_w, sgu_b=sgu_b,
                  cmp_k_pe=cmp_k_pe, cmp_k_w1=cmp_k_w1, cmp_k_w2=cmp_k_w2, cmp_v_pe=cmp_v_pe,
                  cmp_v_w1=cmp_v_w1, cmp_v_w2=cmp_v_w2, w_out=w_out, ffn_w1=ffn_w1, ffn_w2=ffn_w2)
    bsz, s, d = x.shape
    depth = w_in.shape[0]
    tables = _rope_tables(positions)
    ov_t = _overlap_t(s)
    lw = _prep_weights(params)
    for l in range(depth):
        ys = _mixers(l, x, lw, tables, ov_t)
        x2 = _out_ffn(l, x.reshape(bsz * s, d), [y.reshape(bsz * s, GROUP_WIDTH) for y in ys],
                      lw["w_out"], lw["post_mix"], lw["pre_ffn"], lw["ffn_w1"], lw["ffn_w2"], lw["post_ffn"])
        x = x2.reshape(bsz, s, d)
    return x
```

```python
import functools

import jax
import jax.numpy as jnp
from jax import lax
from jax.experimental import pallas as pl
from jax.experimental.pallas import tpu as pltpu

F32 = jnp.float32
BF16 = jnp.bfloat16

D_MODEL = 1024
GROUP_WIDTH = 256
CONV_WIDTH = 31
POOL_WINDOWS = (2, 4, 8, 16)
POOL_GROUP = 64
SGU_HEADS = 4
SGU_CHUNK = 128
ATT_HEADS = 4
HEAD_DIM = 64
ROPE_DIM = 16
ROPE_HALF = 8
ROPE_THETA = 500000.0
CMP_BLOCK = 32
CMP_STRIDE = 16
SLC_BLOCK = 64
N_SELECT = 8
N_LOCAL = 2
WINDOW = 512
N_BRANCH = 3
D_FF = 4096
NORM_EPS = 1e-6
NEG_INF = -1e30
FORCE_SCORE = 1e9

LANES = 128
SUBLANES = 8
VMEM_LIMIT_BYTES = 56 * 1024 * 1024

PROJ_ROWS = 512
FFN_ROWS = 512
FFN_COLS = 512
CONV_ROWS = 64
POOL_ROWS = 64
ATT_Q = 256
ATT_K = 256
LOG2_E = 1.4426950408889634
V_ROWS = 80
MASK_BIAS = -2e30

COL_CVAL, COL_CGATE, COL_POOL, COL_SGU_U, COL_SGU_V = 0, 1, 2, 3, 4
COL_KCVC, COL_KS, COL_KW = 10, 11, 12
ZA_WIDTH = 1664
ROW_Q, ROW_VS, ROW_VW, ROW_G = 0, 256, 320, 384
ZB_ROWS = 400


def _cparams(*sem):
    return pltpu.CompilerParams(dimension_semantics=sem, vmem_limit_bytes=VMEM_LIMIT_BYTES)


def _per_layer(l, *shape, pipeline_mode=None):
    zeros = (0,) * len(shape)
    extra = {} if pipeline_mode is None else {"pipeline_mode": pipeline_mode}
    return pl.BlockSpec((None,) + shape, lambda *_: (l,) + zeros, **extra)


def _gelu(x):
    return 0.5 * x * (1.0 + jnp.tanh(0.7978845608028654 * (x + 0.044715 * (x * x * x))))


def _sigmoid(x):
    return 0.5 * jnp.tanh(0.5 * x) + 0.5


def _layer_norm(x, g, b):
    mu = jnp.mean(x, axis=-1, keepdims=True)
    d = x - mu
    var = jnp.mean(d * d, axis=-1, keepdims=True)
    return d * lax.rsqrt(var + NORM_EPS) * g + b


def _rms_norm(x, g):
    return x * lax.rsqrt(jnp.mean(x * x, axis=-1, keepdims=True) + NORM_EPS) * g


def _rope_chan_kernel(pos_ref, inv_ref, cos_ref, sin_ref):
    ang = inv_ref[...] * pos_ref[0]
    cos_ref[0] = jnp.cos(ang)
    sin_ref[0] = jnp.sin(ang)


def _rope_chan_table(pos_row, inv_col):
    b, _, s = pos_row.shape
    spec = pl.BlockSpec((1, ROPE_HALF, s), lambda i: (i, 0, 0))
    return pl.pallas_call(
        _rope_chan_kernel,
        grid=(b,),
        in_specs=[pl.BlockSpec((1, 1, s), lambda i: (i, 0, 0)),
                  pl.BlockSpec((ROPE_HALF, 1), lambda i: (0, 0))],
        out_specs=[spec, spec],
        out_shape=[jax.ShapeDtypeStruct((b, ROPE_HALF, s), F32)] * 2,
        compiler_params=_cparams("parallel"),
        name="rope_chan_table",
    )(pos_row, inv_col)


def _proj_kernel(x_ref, g_ref, wa_ref, wb_ref, za_ref, zb_ref):
    h = _rms_norm(x_ref[0], g_ref[...]).astype(BF16)
    za_ref[0] = jnp.dot(h, wa_ref[...], preferred_element_type=F32)
    zb_ref[0] = lax.dot_general(wb_ref[...], h, (((1,), (1,)), ((), ())),
                                preferred_element_type=F32)


def _proj(l, x, g, wa, wb):
    b, s, d = x.shape
    tm = min(PROJ_ROWS, s)
    return pl.pallas_call(
        _proj_kernel,
        grid=(b, s // tm),
        in_specs=[pl.BlockSpec((1, tm, d), lambda i, j: (i, j, 0)),
                  _per_layer(l, 1, d), _per_layer(l, d, ZA_WIDTH), _per_layer(l, ZB_ROWS, d)],
        out_specs=[pl.BlockSpec((1, tm, ZA_WIDTH), lambda i, j: (i, j, 0)),
                   pl.BlockSpec((1, ZB_ROWS, tm), lambda i, j: (i, 0, j))],
        out_shape=[jax.ShapeDtypeStruct((b, s, ZA_WIDTH), F32),
                   jax.ShapeDtypeStruct((b, ZB_ROWS, s), F32)],
        compiler_params=_cparams("parallel", "parallel"),
        name="in_proj",
    )(x, g, wa, wb)


MIX_HALO = 32


def _conv_rows(hp_ref, r0, w_ref, b_ref, lg_ref, lb_ref):
    shift = MIX_HALO - (CONV_WIDTH - 1)
    n_win = CONV_ROWS + MIX_HALO
    acc = jnp.zeros((CONV_ROWS, GROUP_WIDTH), F32) + b_ref[...]
    win = hp_ref[r0:r0 + n_win, :]
    for r in range(SUBLANES):
        taps = [k for k in range(CONV_WIDTH) if (k + shift) % SUBLANES == r]
        wr = win if r == 0 else pltpu.roll(win, n_win - r, axis=0)
        for k in taps:
            off = k + shift - r
            acc = acc + wr[off:off + CONV_ROWS, :] * w_ref[k:k + 1, :]
    y = _layer_norm(acc, lg_ref[...], lb_ref[...])
    return y * _sigmoid(y)


POOL_PAD = 16


def _pool_rows(pp_ref, r0, seq_t0, w_ref, sc_ref):
    lane = lax.broadcasted_iota(jnp.int32, (1, GROUP_WIDTH), 1)
    grp = lane // POOL_GROUP
    win = jnp.where(grp == 0, POOL_WINDOWS[0],
                    jnp.where(grp == 1, POOL_WINDOWS[1],
                              jnp.where(grp == 2, POOL_WINDOWS[2], POOL_WINDOWS[3])))
    base = r0 + MIX_HALO - POOL_PAD
    rows = pp_ref[base:base + POOL_ROWS + POOL_PAD, :]
    p = rows[POOL_PAD:POOL_PAD + POOL_ROWS, :]
    acc = rows
    sums = []
    width = 1
    for w in POOL_WINDOWS:
        while width < w:
            acc = acc + pltpu.roll(acc, width, axis=0)
            width *= 2
        sums.append(acc[POOL_PAD:POOL_PAD + POOL_ROWS, :])
    total = jnp.where(grp == 0, sums[0],
                      jnp.where(grp == 1, sums[1],
                                jnp.where(grp == 2, sums[2], sums[3])))
    t = seq_t0 + r0 + lax.broadcasted_iota(jnp.int32, (POOL_ROWS, 1), 0)
    count = jnp.minimum(t + 1, win).astype(F32)
    mixed = total / count - p
    y = jnp.dot(mixed.astype(BF16), w_ref[...], preferred_element_type=F32)
    return y * sc_ref[...]


def _sgu_weights(w_ref):
    row = lax.broadcasted_iota(jnp.int32, (SGU_CHUNK, SGU_CHUNK), 0)
    col = lax.broadcasted_iota(jnp.int32, (SGU_CHUNK, SGU_CHUNK), 1)
    return [jnp.where(row >= col, w_ref[h], 0.0).astype(BF16) for h in range(SGU_HEADS)]


def _sgu_rows(u, v, ws, lg_ref, lb_ref, bias_ref):
    head = lax.broadcasted_iota(jnp.int32, (1, GROUP_WIDTH), 1) // (GROUP_WIDTH // SGU_HEADS)
    vb = _layer_norm(_gelu(v), lg_ref[...], lb_ref[...]).astype(BF16)
    mixed = jnp.zeros((SGU_CHUNK, GROUP_WIDTH), F32)
    for h in range(SGU_HEADS):
        mixed = jnp.where(head == h, jnp.dot(ws[h], vb, preferred_element_type=F32), mixed)
    return _gelu(u) * (mixed + bias_ref[...])


def _att_prep_kernel(kcvc_ref, ksx_ref, kwx_ref, zb_ref, cos_ref, sin_ref, cosb_ref, sinb_ref,
                     wa_ref, wb_ref, pe_ref, w1_ref, w2k_ref, w2vt_ref,
                     q_ref, ks_ref, kw_ref, vs_ref, vw_ref, kc_ref, vct_ref, g_ref, sh_ref):
    s = ksx_ref.shape[1]
    n_blk = s // CMP_STRIDE

    def tok_table(cos, sin):
        n = cos.shape[1]
        rest = HEAD_DIM - ROPE_DIM
        rows = [cos, cos, jnp.ones((rest, n), F32), -sin, sin, jnp.zeros((rest, n), F32)]
        return jnp.concatenate(rows, axis=0).T

    def rope_wide(x, cs):
        r = x * cs
        return r + pltpu.roll(r, HEAD_DIM, axis=1)

    def rope_tok(x, cs):
        return rope_wide(x, cs)[:, :HEAD_DIM]

    cos = cos_ref[0]
    sin = sin_ref[0]
    cs = tok_table(cos, sin)
    lane = lax.broadcasted_iota(jnp.int32, (s, LANES), 1)
    blk = lax.broadcasted_iota(jnp.int32, (s, LANES), 0) // SLC_BLOCK
    ks_ref[0] = jnp.where(lane < HEAD_DIM, rope_wide(ksx_ref[0], cs),
                          jnp.where(lane - HEAD_DIM == blk, 1.0, 0.0)).astype(BF16)
    kw_ref[0] = rope_tok(kwx_ref[0], cs).astype(BF16)

    scale = HEAD_DIM ** -0.5 * LOG2_E
    parts = []
    for h in range(ATT_HEADS):
        r0 = ROW_Q + h * HEAD_DIM
        x1 = zb_ref[0, r0:r0 + ROPE_HALF, :]
        x2 = zb_ref[0, r0 + ROPE_HALF:r0 + ROPE_DIM, :]
        parts += [x1 * cos - x2 * sin, x2 * cos + x1 * sin, zb_ref[0, r0 + ROPE_DIM:r0 + HEAD_DIM, :]]
    q_ref[0] = (jnp.concatenate(parts, axis=0) * scale).astype(BF16)

    ones_row = jnp.where(lax.broadcasted_iota(jnp.int32, (V_ROWS - HEAD_DIM, ATT_K), 0) == 0, 1.0, 0.0)
    for c in range(s // ATT_K):
        cols = slice(c * ATT_K, (c + 1) * ATT_K)
        vs_ref[0, c] = jnp.concatenate([zb_ref[0, ROW_VS:ROW_VS + HEAD_DIM, cols], ones_row], axis=0).astype(BF16)
        vw_ref[0, c] = jnp.concatenate([zb_ref[0, ROW_VW:ROW_VW + HEAD_DIM, cols], ones_row], axis=0).astype(BF16)
    g_ref[0] = _sigmoid(zb_ref[0, ROW_G:ROW_G + 16, :])

    acc_a = jnp.zeros((n_blk, LANES), F32)
    acc_b = jnp.zeros((n_blk, LANES), F32)
    for l in range(CMP_STRIDE):
        xl = kcvc_ref[0, pl.ds(l, n_blk, stride=CMP_STRIDE), :].astype(BF16)
        acc_a = acc_a + jnp.dot(xl, wa_ref[l], preferred_element_type=F32)
        acc_b = acc_b + jnp.dot(xl, wb_ref[l], preferred_element_type=F32)
    sh_ref[0:n_blk, :] = acc_b
    sh_ref[n_blk:n_blk + SUBLANES, :] = jnp.zeros((SUBLANES, LANES), F32)
    pe_term = jnp.sum(pe_ref[...] * w1_ref[...], axis=0, keepdims=True)
    hid = _gelu(acc_a + sh_ref[1:n_blk + 1, :] + pe_term).astype(BF16)
    kk = jnp.dot(hid, w2k_ref[...], preferred_element_type=F32)
    kc_ref[0] = rope_tok(kk, tok_table(cosb_ref[0], sinb_ref[0])).astype(BF16)
    vct_ref[0] = lax.dot_general(w2vt_ref[...], hid, (((1,), (1,)), ((), ())),
                                 preferred_element_type=F32).astype(BF16)


def _att_prep(l, za, zb, cos_t, sin_t, cos_b, sin_b, wa, wb, pe, w1, w2k, w2vt):
    bsz, s, _ = za.shape
    n_blk = s // CMP_STRIDE
    nt = s // ATT_K
    tok = lambda c: pl.BlockSpec((1, s, LANES), lambda i: (i, 0, c))
    per_b3 = lambda shp: pl.BlockSpec((1,) + shp, lambda i: (i, 0, 0))
    per_b4 = lambda shp: pl.BlockSpec((1,) + shp, lambda i: (i, 0, 0, 0))
    return pl.pallas_call(
        _att_prep_kernel,
        grid=(bsz,),
        in_specs=[tok(COL_KCVC), tok(COL_KS), tok(COL_KW),
                  per_b3((ZB_ROWS, s)),
                  per_b3((ROPE_HALF, s)), per_b3((ROPE_HALF, s)),
                  per_b3((ROPE_HALF, n_blk)), per_b3((ROPE_HALF, n_blk)),
                  _per_layer(l, CMP_STRIDE, LANES, LANES), _per_layer(l, CMP_STRIDE, LANES, LANES),
                  _per_layer(l, CMP_BLOCK * HEAD_DIM, LANES), _per_layer(l, CMP_BLOCK * HEAD_DIM, LANES),
                  _per_layer(l, LANES, LANES), _per_layer(l, HEAD_DIM, LANES)],
        out_specs=[per_b3((ATT_HEADS * HEAD_DIM, s)), per_b3((s, LANES)), per_b3((s, HEAD_DIM)),
                   per_b4((nt, V_ROWS, ATT_K)), per_b4((nt, V_ROWS, ATT_K)),
                   per_b3((n_blk, HEAD_DIM)), per_b3((HEAD_DIM, n_blk)), per_b3((16, s))],
        out_shape=[jax.ShapeDtypeStruct((bsz, ATT_HEADS * HEAD_DIM, s), BF16),
                   jax.ShapeDtypeStruct((bsz, s, LANES), BF16),
                   jax.ShapeDtypeStruct((bsz, s, HEAD_DIM), BF16),
                   jax.ShapeDtypeStruct((bsz, nt, V_ROWS, ATT_K), BF16),
                   jax.ShapeDtypeStruct((bsz, nt, V_ROWS, ATT_K), BF16),
                   jax.ShapeDtypeStruct((bsz, n_blk, HEAD_DIM), BF16),
                   jax.ShapeDtypeStruct((bsz, HEAD_DIM, n_blk), BF16),
                   jax.ShapeDtypeStruct((bsz, 16, s), F32)],
        scratch_shapes=[pltpu.VMEM((n_blk + SUBLANES, LANES), F32)],
        compiler_params=_cparams("parallel"),
        name="att_prep",
    )(za, za, za, zb, cos_t, sin_t, cos_b, sin_b, wa, wb, pe, w1, w2k, w2vt)


def _att_kernel(q_ref, ks_ref, kw_ref, vs_ref, vw_ref, kc_ref, vct_ref, g_ref, ov_ref,
                o_ref, qa_ref, sa_ref, sb_ref, m_ref, acc_ref):
    n_blk = kc_ref.shape[1]
    n_slc = ov_ref.shape[0]
    tq = ATT_Q
    wide = ATT_HEADS * tq
    i = pl.program_id(1)
    t0 = i * tq

    q = q_ref[0]
    qs = jnp.concatenate([q[h * HEAD_DIM:(h + 1) * HEAD_DIM] for h in range(ATT_HEADS)], axis=1)
    lane_w = lax.broadcasted_iota(jnp.int32, (1, wide), 1)
    t_w = t0 + (lane_w & (tq - 1))
    t_q = t0 + lax.broadcasted_iota(jnp.int32, (1, tq), 1)
    heads = lambda a: jnp.concatenate([a] * ATT_HEADS, axis=1)

    s_c = jnp.dot(kc_ref[0], qs, preferred_element_type=F32)
    blk_end = lax.broadcasted_iota(jnp.int32, (n_blk, 1), 0) * CMP_STRIDE + (CMP_BLOCK - 1)
    cmask = blk_end <= t_w
    s_m = jnp.where(cmask, s_c, NEG_INF)
    e = jnp.exp2(s_m - jnp.max(s_m, axis=0, keepdims=True))
    p_c = jnp.where(cmask, e * (1.0 / jnp.sum(e, axis=0, keepdims=True)), 0.0)
    o_cmp = jnp.dot(vct_ref[0], p_c.astype(BF16), preferred_element_type=F32)

    p_sum = p_c[:, 0:tq]
    for h in range(1, ATT_HEADS):
        p_sum = p_sum + p_c[:, h * tq:(h + 1) * tq]
    p_hi = p_sum.astype(BF16)
    p_lo = (p_sum - p_hi.astype(F32)).astype(BF16)
    imp = (jnp.dot(ov_ref[...], p_hi, preferred_element_type=F32)
           + jnp.dot(ov_ref[...], p_lo, preferred_element_type=F32))
    j = lax.broadcasted_iota(jnp.int32, (n_slc, 1), 0)
    back = t_q // SLC_BLOCK - j
    forced = (j == 0) | ((back >= 0) & (back < N_LOCAL))
    imp = jnp.where(forced, FORCE_SCORE, jnp.where(back < 0, -1.0, imp))
    rank = jnp.zeros((n_slc, tq), F32)
    for r in range(n_slc):
        row = imp[r:r + 1, :]
        ahead = (row > imp) | ((row == imp) & (j > r))
        rank = rank + jnp.where(ahead, 1.0, 0.0)
    sel_bias = jnp.where((rank < min(N_SELECT, n_slc)) & (back >= 0), 0.0, MASK_BIAS)
    qa_ref[0:HEAD_DIM, :] = qs
    qa_ref[HEAD_DIM:HEAD_DIM + n_slc, :] = heads(sel_bias).astype(BF16)
    qa_ref[HEAD_DIM + n_slc:, :] = jnp.zeros((LANES - HEAD_DIM - n_slc, wide), BF16)

    k_iota = lax.broadcasted_iota(jnp.int32, (ATT_K, 1), 0)

    n_win = (WINDOW + tq) // ATT_K
    first = jnp.maximum(i - WINDOW // ATT_K, 0)
    kw0 = pl.multiple_of(first * ATT_K, ATT_K)
    s_w = jnp.dot(kw_ref[0, pl.ds(kw0, n_win * ATT_K), :], qs, preferred_element_type=F32)
    diff = t_q - (kw0 + lax.broadcasted_iota(jnp.int32, (n_win * ATT_K, 1), 0))
    s_w = s_w + heads(jnp.where((diff >= 0) & (diff < WINDOW), 0.0, -jnp.inf))
    p_w = jnp.exp2(s_w - jnp.max(s_w, axis=0, keepdims=True)).astype(BF16)
    acc_w = None
    for n in range(n_win):
        part = jnp.dot(vw_ref[0, first + n], p_w[n * ATT_K:(n + 1) * ATT_K], preferred_element_type=F32)
        acc_w = part if acc_w is None else acc_w + part

    m_ref[...] = jnp.full((1, wide), NEG_INF, F32)
    acc_ref[...] = jnp.zeros((V_ROWS, wide), F32)

    def scores(kt, causal=False, live=None):
        k0 = pl.multiple_of(kt * ATT_K, ATT_K)
        s_t = jnp.dot(ks_ref[0, pl.ds(k0, ATT_K), :], qa_ref[...], preferred_element_type=F32)
        if causal:
            visible = (k0 + k_iota) <= t_q
            if live is not None:
                visible = visible & live
            s_t = jnp.where(heads(visible), s_t, -jnp.inf)
        return s_t

    def consume(s_ref, kt):
        m_old = m_ref[...]
        m_new = jnp.maximum(m_old, jnp.max(s_ref[...], axis=0, keepdims=True))
        alpha = jnp.exp2(m_old - m_new)
        p = jnp.exp2(s_ref[...] - m_new).astype(BF16)
        m_ref[...] = m_new
        acc_ref[...] = alpha * acc_ref[...] + jnp.dot(vs_ref[0, kt], p, preferred_element_type=F32)

    single = 1 - i % 2
    sa_ref[...] = scores(0, causal=True, live=single == 1)
    consume(sa_ref, 0)
    sa_ref[...] = scores(single)

    def pair_body(n, carry):
        a = single + 2 * n
        sb_ref[...] = scores(a + 1, causal=True)
        consume(sa_ref, a)
        sa_ref[...] = scores(jnp.minimum(a + 2, i))
        consume(sb_ref, a + 1)
        return carry

    lax.fori_loop(0, (i + 1 - single) // 2, pair_body, 0)
    acc_s = acc_ref[...]

    g = g_ref[0]
    gate = lambda br: jnp.concatenate(
        [g[h * N_BRANCH + br:h * N_BRANCH + br + 1, :] for h in range(ATT_HEADS)], axis=1)
    l_s = acc_s[HEAD_DIM:HEAD_DIM + 1, :]
    l_w = acc_w[HEAD_DIM:HEAD_DIM + 1, :]
    o = (gate(0) * o_cmp + (gate(1) * (1.0 / l_s)) * acc_s[:HEAD_DIM]
         + (gate(2) * (1.0 / l_w)) * acc_w[:HEAD_DIM])
    o = jnp.concatenate([o[:, h * tq:(h + 1) * tq] for h in range(ATT_HEADS)], axis=0)
    o_ref[0] = o.T


def _att(q_t, ks, kw, vs_t, vw_t, kc, vc_t, g_t, ov_t):
    bsz, _, s = q_t.shape
    n_blk = kc.shape[1]
    nt = s // ATT_K
    n_slc = s // SLC_BLOCK
    wide = ATT_HEADS * ATT_Q
    assert ATT_Q == ATT_K, "the selected-branch tile pairing assumes one diagonal key tile per query tile"
    per_b3 = lambda shp: pl.BlockSpec((1,) + shp, lambda b, i: (b, 0, 0))
    per_b4 = lambda shp: pl.BlockSpec((1,) + shp, lambda b, i: (b, 0, 0, 0))
    return pl.pallas_call(
        _att_kernel,
        grid=(bsz, s // ATT_Q),
        in_specs=[pl.BlockSpec((1, ATT_HEADS * HEAD_DIM, ATT_Q), lambda b, i: (b, 0, i)),
                  per_b3((s, LANES)), per_b3((s, HEAD_DIM)),
                  per_b4((nt, V_ROWS, ATT_K)), per_b4((nt, V_ROWS, ATT_K)),
                  per_b3((n_blk, HEAD_DIM)), per_b3((HEAD_DIM, n_blk)),
                  pl.BlockSpec((1, 16, ATT_Q), lambda b, i: (b, 0, i)),
                  pl.BlockSpec((n_slc, n_blk), lambda b, i: (0, 0))],
        out_specs=pl.BlockSpec((1, ATT_Q, GROUP_WIDTH), lambda b, i: (b, i, 0)),
        out_shape=jax.ShapeDtypeStruct((bsz, s, GROUP_WIDTH), F32),
        scratch_shapes=[pltpu.VMEM((LANES, wide), BF16),
                        pltpu.VMEM((ATT_K, wide), F32), pltpu.VMEM((ATT_K, wide), F32),
                        pltpu.VMEM((1, wide), F32), pltpu.VMEM((V_ROWS, wide), F32)],
        compiler_params=_cparams("parallel", "arbitrary"),
        name="sparse_attention",
    )(q_t, ks, kw, vs_t, vw_t, kc, vc_t, g_t, ov_t)


def _mix_ffn_kernel(x_ref, yatt_ref, ca_ref, cg_ref, pin_ref, su_ref, sv_ref, cah_ref, cgh_ref, ph_ref,
                    cw_ref, cb_ref, clg_ref, clb_ref, pw_ref, psc_ref, slg_ref, slb_ref, sw_ref, sbias_ref,
                    wo_ref, gm_ref, g1_ref, w1_ref, w2_ref, g2_ref,
                    o_ref, y_ref, hp_ref, pp_ref, acc_ref, *, tiles_per_seq):
    k = pl.program_id(0)
    tm = x_ref.shape[0]

    @pl.when(k == 0)
    def _():
        y_ref[...] = jnp.zeros_like(y_ref)

    kk = jnp.minimum(k, pl.num_programs(0) - 2)
    seq_tile = kk % tiles_per_seq
    has_prev = seq_tile > 0

    def conv_piece(r0):
        y_ref[0, r0:r0 + CONV_ROWS, :] = _conv_rows(hp_ref, r0, cw_ref, cb_ref, clg_ref, clb_ref)
        return 0, r0

    def pool_piece(r0):
        y_ref[1, r0:r0 + POOL_ROWS, :] = _pool_rows(pp_ref, r0, seq_tile * tm, pw_ref, psc_ref)
        return 1, r0

    def sgu_piece(r0):
        rows = slice(r0, r0 + SGU_CHUNK)
        y_ref[2, rows, :] = _sgu_rows(su_ref[rows, :], sv_ref[rows, :], _sgu_weights(sw_ref),
                                      slg_ref, slb_ref, sbias_ref)
        return 2, r0

    def order_before_next_accumulate(n, r0):
        bits = pltpu.bitcast(y_ref[n, r0:r0 + SUBLANES, 0:LANES], jnp.uint32)
        zero = pltpu.bitcast(lax.shift_right_logical(bits, jnp.uint32(32)), F32)
        acc_ref[0:SUBLANES, 0:LANES] = acc_ref[0:SUBLANES, 0:LANES] + zero

    pieces = []
    for r0 in range(0, tm, CONV_ROWS):
        pieces += [functools.partial(conv_piece, r0), functools.partial(pool_piece, r0)]
        if r0 % SGU_CHUNK == 0:
            pieces.append(functools.partial(sgu_piece, r0))
    n_chunks = w1_ref.shape[1] // FFN_COLS
    per_chunk = -(-len(pieces) // (n_chunks - 1))

    acc = jnp.dot(yatt_ref[...].astype(BF16), wo_ref[3 * GROUP_WIDTH:4 * GROUP_WIDTH, :],
                  preferred_element_type=F32)
    for n in range(3):
        acc = acc + jnp.dot(y_ref[n].astype(BF16), wo_ref[n * GROUP_WIDTH:(n + 1) * GROUP_WIDTH, :],
                            preferred_element_type=F32)
    x1 = x_ref[...] + _rms_norm(acc, gm_ref[...])
    h = _rms_norm(x1, g1_ref[...]).astype(BF16)
    hp_ref[0:MIX_HALO, :] = jnp.where(has_prev, cah_ref[...] * _sigmoid(cgh_ref[...]), 0.0)
    hp_ref[MIX_HALO:, :] = ca_ref[...] * _sigmoid(cg_ref[...])
    pp_ref[0:MIX_HALO, :] = jnp.where(has_prev, ph_ref[...], 0.0)
    pp_ref[MIX_HALO:, :] = pin_ref[...]
    for c in range(n_chunks):
        cols = slice(c * FFN_COLS, (c + 1) * FFN_COLS)
        f = jnp.maximum(jnp.dot(h, w1_ref[:, cols], preferred_element_type=F32), 0.0)
        part = jnp.dot((f * f).astype(BF16), w2_ref[cols, :], preferred_element_type=F32)
        acc_ref[...] = part if c == 0 else acc_ref[...] + part
        for piece in pieces[c * per_chunk:(c + 1) * per_chunk]:
            order_before_next_accumulate(*piece())
    o_ref[...] = x1 + _rms_norm(acc_ref[...], g2_ref[...])


def _mix_ffn(l, x2, y_att, za2, seq_len, lw):
    t, d = x2.shape
    dff = lw["ffn_w1"].shape[2]
    tm = min(FFN_ROWS, seq_len)
    assert CONV_ROWS == POOL_ROWS and SGU_CHUNK % CONV_ROWS == 0 and tm % SGU_CHUNK == 0
    n_tiles = t // tm
    halo_per_tile = tm // MIX_HALO
    cur = lambda k: jnp.maximum(k - 1, 0)
    nxt = lambda k: jnp.minimum(k, n_tiles - 1)
    main = lambda col: pl.BlockSpec((tm, GROUP_WIDTH), lambda k: (nxt(k), col))
    halo = lambda col: pl.BlockSpec((MIX_HALO, GROUP_WIDTH),
                                    lambda k: (jnp.maximum(nxt(k) * halo_per_tile - 1, 0), col))
    vec = lambda n: _per_layer(l, 1, n)
    once = lambda *shp: _per_layer(l, *shp, pipeline_mode=pl.Buffered(1))
    return pl.pallas_call(
        functools.partial(_mix_ffn_kernel, tiles_per_seq=seq_len // tm),
        grid=(n_tiles + 1,),
        in_specs=[pl.BlockSpec((tm, d), lambda k: (cur(k), 0)),
                  pl.BlockSpec((tm, GROUP_WIDTH), lambda k: (cur(k), 0)),
                  main(COL_CVAL), main(COL_CGATE), main(COL_POOL), main(COL_SGU_U), main(COL_SGU_V),
                  halo(COL_CVAL), halo(COL_CGATE), halo(COL_POOL),
                  _per_layer(l, CONV_WIDTH, GROUP_WIDTH), vec(GROUP_WIDTH), vec(GROUP_WIDTH), vec(GROUP_WIDTH),
                  _per_layer(l, GROUP_WIDTH, GROUP_WIDTH), vec(GROUP_WIDTH),
                  vec(GROUP_WIDTH), vec(GROUP_WIDTH),
                  _per_layer(l, SGU_HEADS, SGU_CHUNK, SGU_CHUNK), _per_layer(l, SGU_CHUNK, GROUP_WIDTH),
                  once(d, d), vec(d), vec(d), once(d, dff), once(dff, d), vec(d)],
        out_specs=pl.BlockSpec((tm, d), lambda k: (cur(k), 0)),
        out_shape=jax.ShapeDtypeStruct((t, d), F32),
        scratch_shapes=[pltpu.VMEM((3, tm, GROUP_WIDTH), F32),
                        pltpu.VMEM((MIX_HALO + tm, GROUP_WIDTH), F32),
                        pltpu.VMEM((MIX_HALO + tm, GROUP_WIDTH), F32),
                        pltpu.VMEM((tm, d), F32)],
        compiler_params=_cparams("arbitrary"),
        name="mix_ffn",
    )(x2, y_att, za2, za2, za2, za2, za2, za2, za2, za2,
      lw["conv_w"], lw["conv_b"], lw["conv_lg"], lw["conv_lb"], lw["pool_bd"], lw["pool_scale"],
      lw["sgu_lg"], lw["sgu_lb"], lw["sgu_w"], lw["sgu_bias"],
      lw["w_out"], lw["post_mix"], lw["pre_ffn"], lw["ffn_w1"], lw["ffn_w2"], lw["post_ffn"])


def _rope_perm():
    idx = list(range(HEAD_DIM))
    for c in range(ROPE_HALF):
        idx[c], idx[c + ROPE_HALF] = c + ROPE_HALF, c
    return jnp.array(idx, jnp.int32)


def _split_w_in(w_in):
    gw = GROUP_WIDTH
    edges = [0, gw, 2 * gw, 3 * gw, 4 * gw, 5 * gw, 6 * gw]
    for _ in range(6):
        edges.append(edges[-1] + HEAD_DIM)
    edges.append(edges[-1] + ATT_HEADS * N_BRANCH)
    names = ("cval", "cgate", "pool", "su", "sv", "q", "kc", "vc", "ks", "vs", "kw", "vw", "g")
    return {n: w_in[..., edges[k]:edges[k + 1]] for k, n in enumerate(names)}


def _prep_weights(p):
    depth = p["w_in"].shape[0]
    perm = _rope_perm()
    w = _split_w_in(p["w_in"])
    wa = jnp.concatenate([w["cval"], w["cgate"], w["pool"], w["su"], w["sv"], w["kc"], w["vc"],
                          w["ks"], w["ks"][..., perm], w["kw"], w["kw"][..., perm]], axis=-1).astype(BF16)
    pad = jnp.zeros((depth, D_MODEL, ZB_ROWS - ROW_G - ATT_HEADS * N_BRANCH), F32)
    wb = jnp.swapaxes(jnp.concatenate([w["q"], w["vs"], w["vw"], w["g"], pad], axis=-1), 1, 2).astype(BF16)

    n_grp = len(POOL_WINDOWS)
    pool_bd = (p["pool_w"][:, :, :, None, :] * jnp.eye(n_grp, dtype=F32)[None, :, None, :, None])
    pool_bd = pool_bd.reshape(depth, GROUP_WIDTH, GROUP_WIDTH)
    sgu_bias = jnp.repeat(jnp.swapaxes(p["sgu_b"], 1, 2), GROUP_WIDTH // SGU_HEADS, axis=2)

    w1k = p["cmp_k_w1"].reshape(depth, CMP_BLOCK, HEAD_DIM, HEAD_DIM)
    w1v = p["cmp_v_w1"].reshape(depth, CMP_BLOCK, HEAD_DIM, HEAD_DIM)
    z = jnp.zeros((depth, CMP_STRIDE, HEAD_DIM, HEAD_DIM), F32)

    def kv_diag(a, b):
        return jnp.concatenate([jnp.concatenate([a, z], axis=3), jnp.concatenate([z, b], axis=3)], axis=2)

    cmp_wa = kv_diag(w1k[:, :CMP_STRIDE], w1v[:, :CMP_STRIDE]).astype(BF16)
    cmp_wb = kv_diag(w1k[:, CMP_STRIDE:], w1v[:, CMP_STRIDE:]).astype(BF16)
    ones = jnp.ones((1, 1, HEAD_DIM), F32)
    cmp_pe = jnp.concatenate([p["cmp_k_pe"].reshape(depth, -1, 1) * ones,
                              p["cmp_v_pe"].reshape(depth, -1, 1) * ones], axis=2)
    cmp_w1 = jnp.concatenate([p["cmp_k_w1"], p["cmp_v_w1"]], axis=2)
    zk = jnp.zeros((depth, HEAD_DIM, LANES), F32)
    cmp_w2k = jnp.concatenate([jnp.concatenate([p["cmp_k_w2"], p["cmp_k_w2"][..., perm]], axis=2), zk],
                              axis=1).astype(BF16)
    cmp_w2vt = jnp.concatenate([jnp.zeros((depth, HEAD_DIM, HEAD_DIM), F32),
                                jnp.swapaxes(p["cmp_v_w2"], 1, 2)], axis=2).astype(BF16)

    row = lambda v: v[:, None, :]
    return dict(
        wa=wa, wb=wb, pre_mix=row(p["pre_mix_norm"]), post_mix=row(p["post_mix_norm"]),
        pre_ffn=row(p["pre_ffn_norm"]), post_ffn=row(p["post_ffn_norm"]),
        conv_w=p["conv_w"], conv_b=row(p["conv_b"]), conv_lg=row(p["conv_ln_g"]), conv_lb=row(p["conv_ln_b"]),
        pool_bd=pool_bd.astype(BF16), pool_scale=row(p["pool_scale"]),
        sgu_lg=row(p["sgu_ln_g"]), sgu_lb=row(p["sgu_ln_b"]), sgu_w=p["sgu_w"], sgu_bias=sgu_bias,
        cmp_wa=cmp_wa, cmp_wb=cmp_wb, cmp_pe=cmp_pe, cmp_w1=cmp_w1, cmp_w2k=cmp_w2k, cmp_w2vt=cmp_w2vt,
        w_out=p["w_out"].astype(BF16), ffn_w1=p["ffn_w1"].astype(BF16), ffn_w2=p["ffn_w2"].astype(BF16),
    )


def _overlap_t(s):
    n_blk = s // CMP_STRIDE
    n_slc = s // SLC_BLOCK
    bs = jnp.arange(n_blk)[None, :] * CMP_STRIDE
    ss = jnp.arange(n_slc)[:, None] * SLC_BLOCK
    ov = jnp.clip(jnp.minimum(bs + CMP_BLOCK, ss + SLC_BLOCK) - jnp.maximum(bs, ss), 0)
    return (ov.astype(F32) / CMP_STRIDE).astype(BF16)


def _rope_tables(positions):
    bsz, s = positions.shape
    inv = (ROPE_THETA ** (-jnp.arange(ROPE_HALF, dtype=F32) * 2.0 / ROPE_DIM)).reshape(ROPE_HALF, 1)
    posf = positions.astype(F32)
    n_blk = s // CMP_STRIDE
    pos_end = posf[:, CMP_BLOCK - 1::CMP_STRIDE]
    pos_end = jnp.pad(pos_end, ((0, 0), (0, n_blk - pos_end.shape[1])))
    cos_t, sin_t = _rope_chan_table(posf[:, None, :], inv)
    cos_b, sin_b = _rope_chan_table(pos_end[:, None, :], inv)
    return cos_t, sin_t, cos_b, sin_b


def _layer(l, x, lw, tables, ov_t):
    bsz, s, d = x.shape
    za, zb = _proj(l, x, lw["pre_mix"], lw["wa"], lw["wb"])
    prep = _att_prep(l, za, zb, *tables, lw["cmp_wa"], lw["cmp_wb"],
                     lw["cmp_pe"], lw["cmp_w1"], lw["cmp_w2k"], lw["cmp_w2vt"])
    y_att = _att(*prep, ov_t)
    x2 = _mix_ffn(l, x.reshape(bsz * s, d), y_att.reshape(bsz * s, GROUP_WIDTH),
                  za.reshape(bsz * s, ZA_WIDTH), s, lw)
    return x2.reshape(bsz, s, d)


def kernel(x, positions, pre_mix_norm, post_mix_norm, pre_ffn_norm, post_ffn_norm, w_in, conv_w, conv_b, conv_ln_g, conv_ln_b, pool_w, pool_scale, sgu_ln_g, sgu_ln_b, sgu_w, sgu_b, cmp_k_pe, cmp_k_w1, cmp_k_w2, cmp_v_pe, cmp_v_w1, cmp_v_w2, w_out, ffn_w1, ffn_w2):
    params = dict(pre_mix_norm=pre_mix_norm, post_mix_norm=post_mix_norm, pre_ffn_norm=pre_ffn_norm,
                  post_ffn_norm=post_ffn_norm, w_in=w_in, conv_w=conv_w, conv_b=conv_b,
                  conv_ln_g=conv_ln_g, conv_ln_b=conv_ln_b, pool_w=pool_w, pool_scale=pool_scale,
                  sgu_ln_g=sgu_ln_g, sgu_ln_b=sgu_ln_b, sgu_w=sgu_w, sgu_b=sgu_b,
                  cmp_k_pe=cmp_k_pe, cmp_k_w1=cmp_k_w1, cmp_k_w2=cmp_k_w2, cmp_v_pe=cmp_v_pe,
                  cmp_v_w1=cmp_v_w1, cmp_v_w2=cmp_v_w2, w_out=w_out, ffn_w1=ffn_w1, ffn_w2=ffn_w2)
    bsz, s, d = x.shape
    depth = w_in.shape[0]
    tables = _rope_tables(positions)
    ov_t = _overlap_t(s)
    lw = _prep_weights(params)
    for l in range(depth):
        x = _layer(l, x, lw, tables, ov_t)
    return x
```

```python
import functools

import jax
import jax.numpy as jnp
from jax import lax
from jax.experimental import pallas as pl
from jax.experimental.pallas import tpu as pltpu

F32 = jnp.float32
BF16 = jnp.bfloat16

D_MODEL = 1024
GROUP_WIDTH = 256
CONV_WIDTH = 31
POOL_WINDOWS = (2, 4, 8, 16)
POOL_GROUP = 64
SGU_HEADS = 4
SGU_CHUNK = 128
ATT_HEADS = 4
HEAD_DIM = 64
ROPE_DIM = 16
ROPE_HALF = 8
ROPE_THETA = 500000.0
CMP_BLOCK = 32
CMP_STRIDE = 16
SLC_BLOCK = 64
N_SELECT = 8
N_LOCAL = 2
WINDOW = 512
N_BRANCH = 3
D_FF = 4096
NORM_EPS = 1e-6
NEG_INF = -1e30
FORCE_SCORE = 1e9

LANES = 128
SUBLANES = 8
VMEM_LIMIT_BYTES = 56 * 1024 * 1024

PROJ_ROWS = 512
FFN_ROWS = 512
FFN_COLS = 512
CONV_ROWS = 64
POOL_ROWS = 64
ATT_Q = 256
ATT_K = 256
LOG2_E = 1.4426950408889634
V_ROWS = 80
MASK_BIAS = -2e30

COL_CVAL, COL_CGATE, COL_POOL, COL_SGU_U, COL_SGU_V = 0, 1, 2, 3, 4
COL_KCVC, COL_KS, COL_KW = 10, 11, 12
ZA_WIDTH = 1664
ROW_Q, ROW_VS, ROW_VW, ROW_G = 0, 256, 320, 384
ZB_ROWS = 400


def _cparams(*sem):
    return pltpu.CompilerParams(dimension_semantics=sem, vmem_limit_bytes=VMEM_LIMIT_BYTES)


def _per_layer(l, *shape, pipeline_mode=None):
    zeros = (0,) * len(shape)
    extra = {} if pipeline_mode is None else {"pipeline_mode": pipeline_mode}
    return pl.BlockSpec((None,) + shape, lambda *_: (l,) + zeros, **extra)


def _gelu(x):
    return 0.5 * x * (1.0 + jnp.tanh(0.7978845608028654 * (x + 0.044715 * (x * x * x))))


def _sigmoid(x):
    return 0.5 * jnp.tanh(0.5 * x) + 0.5


def _layer_norm(x, g, b):
    mu = jnp.mean(x, axis=-1, keepdims=True)
    d = x - mu
    var = jnp.mean(d * d, axis=-1, keepdims=True)
    return d * lax.rsqrt(var + NORM_EPS) * g + b


def _rms_norm(x, g):
    return x * lax.rsqrt(jnp.mean(x * x, axis=-1, keepdims=True) + NORM_EPS) * g


def _rope_chan_kernel(pos_ref, inv_ref, cos_ref, sin_ref):
    ang = inv_ref[...] * pos_ref[0]
    cos_ref[0] = jnp.cos(ang)
    sin_ref[0] = jnp.sin(ang)


def _rope_chan_table(pos_row, inv_col):
    b, _, s = pos_row.shape
    spec = pl.BlockSpec((1, ROPE_HALF, s), lambda i: (i, 0, 0))
    return pl.pallas_call(
        _rope_chan_kernel,
        grid=(b,),
        in_specs=[pl.BlockSpec((1, 1, s), lambda i: (i, 0, 0)),
                  pl.BlockSpec((ROPE_HALF, 1), lambda i: (0, 0))],
        out_specs=[spec, spec],
        out_shape=[jax.ShapeDtypeStruct((b, ROPE_HALF, s), F32)] * 2,
        compiler_params=_cparams("parallel"),
        name="rope_chan_table",
    )(pos_row, inv_col)


def _proj_kernel(x_ref, g_ref, wa_ref, wb_ref, za_ref, zb_ref):
    h = _rms_norm(x_ref[0], g_ref[...]).astype(BF16)
    za_ref[0] = jnp.dot(h, wa_ref[...], preferred_element_type=F32)
    zb_ref[0] = lax.dot_general(wb_ref[...], h, (((1,), (1,)), ((), ())),
                                preferred_element_type=F32)


def _proj(l, x, g, wa, wb):
    b, s, d = x.shape
    tm = min(PROJ_ROWS, s)
    return pl.pallas_call(
        _proj_kernel,
        grid=(b, s // tm),
        in_specs=[pl.BlockSpec((1, tm, d), lambda i, j: (i, j, 0)),
                  _per_layer(l, 1, d), _per_layer(l, d, ZA_WIDTH), _per_layer(l, ZB_ROWS, d)],
        out_specs=[pl.BlockSpec((1, tm, ZA_WIDTH), lambda i, j: (i, j, 0)),
                   pl.BlockSpec((1, ZB_ROWS, tm), lambda i, j: (i, 0, j))],
        out_shape=[jax.ShapeDtypeStruct((b, s, ZA_WIDTH), F32),
                   jax.ShapeDtypeStruct((b, ZB_ROWS, s), F32)],
        compiler_params=_cparams("parallel", "parallel"),
        name="in_proj",
    )(x, g, wa, wb)


MIX_HALO = 32


def _conv_rows(hp_ref, r0, w_ref, b_ref, lg_ref, lb_ref):
    shift = MIX_HALO - (CONV_WIDTH - 1)
    n_win = CONV_ROWS + MIX_HALO
    acc = jnp.zeros((CONV_ROWS, GROUP_WIDTH), F32) + b_ref[...]
    win = hp_ref[r0:r0 + n_win, :]
    for r in range(SUBLANES):
        taps = [k for k in range(CONV_WIDTH) if (k + shift) % SUBLANES == r]
        wr = win if r == 0 else pltpu.roll(win, n_win - r, axis=0)
        for k in taps:
            off = k + shift - r
            acc = acc + wr[off:off + CONV_ROWS, :] * w_ref[k:k + 1, :]
    y = _layer_norm(acc, lg_ref[...], lb_ref[...])
    return y * _sigmoid(y)


POOL_PAD = 16


def _pool_rows(pp_ref, r0, seq_t0, w_ref, sc_ref):
    lane = lax.broadcasted_iota(jnp.int32, (1, GROUP_WIDTH), 1)
    grp = lane // POOL_GROUP
    win = jnp.where(grp == 0, POOL_WINDOWS[0],
                    jnp.where(grp == 1, POOL_WINDOWS[1],
                              jnp.where(grp == 2, POOL_WINDOWS[2], POOL_WINDOWS[3])))
    base = r0 + MIX_HALO - POOL_PAD
    rows = pp_ref[base:base + POOL_ROWS + POOL_PAD, :]
    p = rows[POOL_PAD:POOL_PAD + POOL_ROWS, :]
    acc = rows
    sums = []
    width = 1
    for w in POOL_WINDOWS:
        while width < w:
            acc = acc + pltpu.roll(acc, width, axis=0)
            width *= 2
        sums.append(acc[POOL_PAD:POOL_PAD + POOL_ROWS, :])
    total = jnp.where(grp == 0, sums[0],
                      jnp.where(grp == 1, sums[1],
                                jnp.where(grp == 2, sums[2], sums[3])))
    t = seq_t0 + r0 + lax.broadcasted_iota(jnp.int32, (POOL_ROWS, 1), 0)
    count = jnp.minimum(t + 1, win).astype(F32)
    mixed = total / count - p
    y = jnp.dot(mixed.astype(BF16), w_ref[...], preferred_element_type=F32)
    return y * sc_ref[...]


def _sgu_weights(w_ref):
    row = lax.broadcasted_iota(jnp.int32, (SGU_CHUNK, SGU_CHUNK), 0)
    col = lax.broadcasted_iota(jnp.int32, (SGU_CHUNK, SGU_CHUNK), 1)
    return [jnp.where(row >= col, w_ref[h], 0.0).astype(BF16) for h in range(SGU_HEADS)]


def _sgu_rows(u, v, ws, lg_ref, lb_ref, bias_ref):
    head = lax.broadcasted_iota(jnp.int32, (1, GROUP_WIDTH), 1) // (GROUP_WIDTH // SGU_HEADS)
    vb = _layer_norm(_gelu(v), lg_ref[...], lb_ref[...]).astype(BF16)
    mixed = jnp.zeros((SGU_CHUNK, GROUP_WIDTH), F32)
    for h in range(SGU_HEADS):
        mixed = jnp.where(head == h, jnp.dot(ws[h], vb, preferred_element_type=F32), mixed)
    return _gelu(u) * (mixed + bias_ref[...])


def _att_prep_kernel(kcvc_ref, ksx_ref, kwx_ref, zb_ref, cos_ref, sin_ref, cosb_ref, sinb_ref,
                     wa_ref, wb_ref, pe_ref, w1_ref, w2k_ref, w2vt_ref,
                     q_ref, ks_ref, kw_ref, vs_ref, vw_ref, kc_ref, vct_ref, g_ref, sh_ref):
    s = ksx_ref.shape[1]
    n_blk = s // CMP_STRIDE

    def tok_table(cos, sin):
        n = cos.shape[1]
        rest = HEAD_DIM - ROPE_DIM
        rows = [cos, cos, jnp.ones((rest, n), F32), -sin, sin, jnp.zeros((rest, n), F32)]
        return jnp.concatenate(rows, axis=0).T

    def rope_wide(x, cs):
        r = x * cs
        return r + pltpu.roll(r, HEAD_DIM, axis=1)

    def rope_tok(x, cs):
        return rope_wide(x, cs)[:, :HEAD_DIM]

    cos = cos_ref[0]
    sin = sin_ref[0]
    cs = tok_table(cos, sin)
    lane = lax.broadcasted_iota(jnp.int32, (s, LANES), 1)
    blk = lax.broadcasted_iota(jnp.int32, (s, LANES), 0) // SLC_BLOCK
    ks_ref[0] = jnp.where(lane < HEAD_DIM, rope_wide(ksx_ref[0], cs),
                          jnp.where(lane - HEAD_DIM == blk, 1.0, 0.0)).astype(BF16)
    kw_ref[0] = rope_tok(kwx_ref[0], cs).astype(BF16)

    scale = HEAD_DIM ** -0.5 * LOG2_E
    parts = []
    for h in range(ATT_HEADS):
        r0 = ROW_Q + h * HEAD_DIM
        x1 = zb_ref[0, r0:r0 + ROPE_HALF, :]
        x2 = zb_ref[0, r0 + ROPE_HALF:r0 + ROPE_DIM, :]
        parts += [x1 * cos - x2 * sin, x2 * cos + x1 * sin, zb_ref[0, r0 + ROPE_DIM:r0 + HEAD_DIM, :]]
    q_ref[0] = (jnp.concatenate(parts, axis=0) * scale).astype(BF16)

    ones_row = jnp.where(lax.broadcasted_iota(jnp.int32, (V_ROWS - HEAD_DIM, ATT_K), 0) == 0, 1.0, 0.0)
    for c in range(s // ATT_K):
        cols = slice(c * ATT_K, (c + 1) * ATT_K)
        vs_ref[0, c] = jnp.concatenate([zb_ref[0, ROW_VS:ROW_VS + HEAD_DIM, cols], ones_row], axis=0).astype(BF16)
        vw_ref[0, c] = jnp.concatenate([zb_ref[0, ROW_VW:ROW_VW + HEAD_DIM, cols], ones_row], axis=0).astype(BF16)
    g_ref[0] = _sigmoid(zb_ref[0, ROW_G:ROW_G + 16, :])

    acc_a = jnp.zeros((n_blk, LANES), F32)
    acc_b = jnp.zeros((n_blk, LANES), F32)
    for l in range(CMP_STRIDE):
        xl = kcvc_ref[0, pl.ds(l, n_blk, stride=CMP_STRIDE), :].astype(BF16)
        acc_a = acc_a + jnp.dot(xl, wa_ref[l], preferred_element_type=F32)
        acc_b = acc_b + jnp.dot(xl, wb_ref[l], preferred_element_type=F32)
    sh_ref[0:n_blk, :] = acc_b
    sh_ref[n_blk:n_blk + SUBLANES, :] = jnp.zeros((SUBLANES, LANES), F32)
    pe_term = jnp.sum(pe_ref[...] * w1_ref[...], axis=0, keepdims=True)
    hid = _gelu(acc_a + sh_ref[1:n_blk + 1, :] + pe_term).astype(BF16)
    kk = jnp.dot(hid, w2k_ref[...], preferred_element_type=F32)
    kc_ref[0] = rope_tok(kk, tok_table(cosb_ref[0], sinb_ref[0])).astype(BF16)
    vct_ref[0] = lax.dot_general(w2vt_ref[...], hid, (((1,), (1,)), ((), ())),
                                 preferred_element_type=F32).astype(BF16)


def _att_prep(l, za, zb, cos_t, sin_t, cos_b, sin_b, wa, wb, pe, w1, w2k, w2vt):
    bsz, s, _ = za.shape
    n_blk = s // CMP_STRIDE
    nt = s // ATT_K
    tok = lambda c: pl.BlockSpec((1, s, LANES), lambda i: (i, 0, c))
    per_b3 = lambda shp: pl.BlockSpec((1,) + shp, lambda i: (i, 0, 0))
    per_b4 = lambda shp: pl.BlockSpec((1,) + shp, lambda i: (i, 0, 0, 0))
    return pl.pallas_call(
        _att_prep_kernel,
        grid=(bsz,),
        in_specs=[tok(COL_KCVC), tok(COL_KS), tok(COL_KW),
                  per_b3((ZB_ROWS, s)),
                  per_b3((ROPE_HALF, s)), per_b3((ROPE_HALF, s)),
                  per_b3((ROPE_HALF, n_blk)), per_b3((ROPE_HALF, n_blk)),
                  _per_layer(l, CMP_STRIDE, LANES, LANES), _per_layer(l, CMP_STRIDE, LANES, LANES),
                  _per_layer(l, CMP_BLOCK * HEAD_DIM, LANES), _per_layer(l, CMP_BLOCK * HEAD_DIM, LANES),
                  _per_layer(l, LANES, LANES), _per_layer(l, HEAD_DIM, LANES)],
        out_specs=[per_b3((ATT_HEADS * HEAD_DIM, s)), per_b3((s, LANES)), per_b3((s, HEAD_DIM)),
                   per_b4((nt, V_ROWS, ATT_K)), per_b4((nt, V_ROWS, ATT_K)),
                   per_b3((n_blk, HEAD_DIM)), per_b3((HEAD_DIM, n_blk)), per_b3((16, s))],
        out_shape=[jax.ShapeDtypeStruct((bsz, ATT_HEADS * HEAD_DIM, s), BF16),
                   jax.ShapeDtypeStruct((bsz, s, LANES), BF16),
                   jax.ShapeDtypeStruct((bsz, s, HEAD_DIM), BF16),
                   jax.ShapeDtypeStruct((bsz, nt, V_ROWS, ATT_K), BF16),
                   jax.ShapeDtypeStruct((bsz, nt, V_ROWS, ATT_K), BF16),
                   jax.ShapeDtypeStruct((bsz, n_blk, HEAD_DIM), BF16),
                   jax.ShapeDtypeStruct((bsz, HEAD_DIM, n_blk), BF16),
                   jax.ShapeDtypeStruct((bsz, 16, s), F32)],
        scratch_shapes=[pltpu.VMEM((n_blk + SUBLANES, LANES), F32)],
        compiler_params=_cparams("parallel"),
        name="att_prep",
    )(za, za, za, zb, cos_t, sin_t, cos_b, sin_b, wa, wb, pe, w1, w2k, w2vt)


def _att_tile(i, q_ref, g_ref, ks_ref, kw_ref, vs_ref, vw_ref, kc_ref, vct_ref, ov_ref,
              qa_ref, sa_ref, sb_ref, m_ref, acc_ref):
    n_blk = kc_ref.shape[1]
    n_slc = ov_ref.shape[0]
    tq = ATT_Q
    wide = ATT_HEADS * tq
    t0 = i * tq

    q = q_ref[0]
    qs = jnp.concatenate([q[h * HEAD_DIM:(h + 1) * HEAD_DIM] for h in range(ATT_HEADS)], axis=1)
    lane_w = lax.broadcasted_iota(jnp.int32, (1, wide), 1)
    t_w = t0 + (lane_w & (tq - 1))
    t_q = t0 + lax.broadcasted_iota(jnp.int32, (1, tq), 1)
    heads = lambda a: jnp.concatenate([a] * ATT_HEADS, axis=1)

    s_c = jnp.dot(kc_ref[0], qs, preferred_element_type=F32)
    blk_end = lax.broadcasted_iota(jnp.int32, (n_blk, 1), 0) * CMP_STRIDE + (CMP_BLOCK - 1)
    cmask = blk_end <= t_w
    s_m = jnp.where(cmask, s_c, NEG_INF)
    e = jnp.exp2(s_m - jnp.max(s_m, axis=0, keepdims=True))
    p_c = jnp.where(cmask, e * (1.0 / jnp.sum(e, axis=0, keepdims=True)), 0.0)
    o_cmp = jnp.dot(vct_ref[0], p_c.astype(BF16), preferred_element_type=F32)

    p_sum = p_c[:, 0:tq]
    for h in range(1, ATT_HEADS):
        p_sum = p_sum + p_c[:, h * tq:(h + 1) * tq]
    p_hi = p_sum.astype(BF16)
    p_lo = (p_sum - p_hi.astype(F32)).astype(BF16)
    imp = (jnp.dot(ov_ref[...], p_hi, preferred_element_type=F32)
           + jnp.dot(ov_ref[...], p_lo, preferred_element_type=F32))
    j = lax.broadcasted_iota(jnp.int32, (n_slc, 1), 0)
    back = t_q // SLC_BLOCK - j
    forced = (j == 0) | ((back >= 0) & (back < N_LOCAL))
    imp = jnp.where(forced, FORCE_SCORE, jnp.where(back < 0, -1.0, imp))
    rank = jnp.zeros((n_slc, tq), F32)
    for r in range(n_slc):
        row = imp[r:r + 1, :]
        ahead = (row > imp) | ((row == imp) & (j > r))
        rank = rank + jnp.where(ahead, 1.0, 0.0)
    sel_bias = jnp.where((rank < min(N_SELECT, n_slc)) & (back >= 0), 0.0, MASK_BIAS)
    qa_ref[0:HEAD_DIM, :] = qs
    qa_ref[HEAD_DIM:HEAD_DIM + n_slc, :] = heads(sel_bias).astype(BF16)
    qa_ref[HEAD_DIM + n_slc:, :] = jnp.zeros((LANES - HEAD_DIM - n_slc, wide), BF16)

    k_iota = lax.broadcasted_iota(jnp.int32, (ATT_K, 1), 0)

    n_win = (WINDOW + tq) // ATT_K
    first = jnp.maximum(i - WINDOW // ATT_K, 0)
    kw0 = pl.multiple_of(first * ATT_K, ATT_K)
    s_w = jnp.dot(kw_ref[0, pl.ds(kw0, n_win * ATT_K), :], qs, preferred_element_type=F32)
    diff = t_q - (kw0 + lax.broadcasted_iota(jnp.int32, (n_win * ATT_K, 1), 0))
    s_w = s_w + heads(jnp.where((diff >= 0) & (diff < WINDOW), 0.0, -jnp.inf))
    p_w = jnp.exp2(s_w - jnp.max(s_w, axis=0, keepdims=True)).astype(BF16)
    acc_w = None
    for n in range(n_win):
        part = jnp.dot(vw_ref[0, first + n], p_w[n * ATT_K:(n + 1) * ATT_K], preferred_element_type=F32)
        acc_w = part if acc_w is None else acc_w + part

    m_ref[...] = jnp.full((1, wide), NEG_INF, F32)
    acc_ref[...] = jnp.zeros((V_ROWS, wide), F32)

    def scores(kt, causal=False, live=None):
        k0 = pl.multiple_of(kt * ATT_K, ATT_K)
        s_t = jnp.dot(ks_ref[0, pl.ds(k0, ATT_K), :], qa_ref[...], preferred_element_type=F32)
        if causal:
            visible = (k0 + k_iota) <= t_q
            if live is not None:
                visible = visible & live
            s_t = jnp.where(heads(visible), s_t, -jnp.inf)
        return s_t

    def consume(s_ref, kt):
        m_old = m_ref[...]
        m_new = jnp.maximum(m_old, jnp.max(s_ref[...], axis=0, keepdims=True))
        alpha = jnp.exp2(m_old - m_new)
        p = jnp.exp2(s_ref[...] - m_new).astype(BF16)
        m_ref[...] = m_new
        acc_ref[...] = alpha * acc_ref[...] + jnp.dot(vs_ref[0, kt], p, preferred_element_type=F32)

    single = 1 - i % 2
    sa_ref[...] = scores(0, causal=True, live=single == 1)
    consume(sa_ref, 0)
    sa_ref[...] = scores(single)

    def pair(n):
        a = single + 2 * n
        sb_ref[...] = scores(a + 1, causal=True)
        consume(sa_ref, a)
        sa_ref[...] = scores(jnp.minimum(a + 2, i))
        consume(sb_ref, a + 1)

    def finish():
        acc_s = acc_ref[...]
        g = g_ref[0]
        gate = lambda br: jnp.concatenate(
            [g[h * N_BRANCH + br:h * N_BRANCH + br + 1, :] for h in range(ATT_HEADS)], axis=1)
        l_s = acc_s[HEAD_DIM:HEAD_DIM + 1, :]
        l_w = acc_w[HEAD_DIM:HEAD_DIM + 1, :]
        o = (gate(0) * o_cmp + (gate(1) * (1.0 / l_s)) * acc_s[:HEAD_DIM]
             + (gate(2) * (1.0 / l_w)) * acc_w[:HEAD_DIM])
        return jnp.concatenate([o[:, h * tq:(h + 1) * tq] for h in range(ATT_HEADS)], axis=0).T

    return pair, (i + 1 - single) // 2, finish


def _att_kernel(qa_in_ref, qb_in_ref, ks_ref, kw_ref, vs_ref, vw_ref, kc_ref, vct_ref, ga_ref, gb_ref,
                ov_ref, o_ref, *scratch, half):
    j = pl.program_id(1)
    shared = (ks_ref, kw_ref, vs_ref, vw_ref, kc_ref, vct_ref, ov_ref)
    n_scr = len(scratch) // 2
    pair_a, pairs_a, finish_a = _att_tile(j, qa_in_ref, ga_ref, *shared, *scratch[:n_scr])
    pair_b, _, finish_b = _att_tile(j + half, qb_in_ref, gb_ref, *shared, *scratch[n_scr:])

    def both(n, carry):
        pair_a(n)
        pair_b(n)
        return carry

    lax.fori_loop(0, pairs_a, both, 0)
    for extra in range(half // 2):
        pair_b(pairs_a + extra)
    o_ref[0, 0] = finish_a()
    o_ref[0, 1] = finish_b()


def _att(q_t, ks, kw, vs_t, vw_t, kc, vc_t, g_t, ov_t):
    bsz, _, s = q_t.shape
    n_blk = kc.shape[1]
    nt = s // ATT_K
    n_slc = s // SLC_BLOCK
    wide = ATT_HEADS * ATT_Q
    half = s // ATT_Q // 2
    assert ATT_Q == ATT_K, "the selected-branch tile pairing assumes one diagonal key tile per query tile"
    assert half % 2 == 0, "the two tiles of a step must need the same parity of key tiles"
    per_b3 = lambda shp: pl.BlockSpec((1,) + shp, lambda b, j: (b, 0, 0))
    per_b4 = lambda shp: pl.BlockSpec((1,) + shp, lambda b, j: (b, 0, 0, 0))
    q_spec = lambda off: pl.BlockSpec((1, ATT_HEADS * HEAD_DIM, ATT_Q), lambda b, j: (b, 0, j + off))
    g_spec = lambda off: pl.BlockSpec((1, 16, ATT_Q), lambda b, j: (b, 0, j + off))
    tile_scratch = [pltpu.VMEM((LANES, wide), BF16),
                    pltpu.VMEM((ATT_K, wide), F32), pltpu.VMEM((ATT_K, wide), F32),
                    pltpu.VMEM((1, wide), F32), pltpu.VMEM((V_ROWS, wide), F32)]
    out = pl.pallas_call(
        functools.partial(_att_kernel, half=half),
        grid=(bsz, half),
        in_specs=[q_spec(0), q_spec(half),
                  per_b3((s, LANES)), per_b3((s, HEAD_DIM)),
                  per_b4((nt, V_ROWS, ATT_K)), per_b4((nt, V_ROWS, ATT_K)),
                  per_b3((n_blk, HEAD_DIM)), per_b3((HEAD_DIM, n_blk)),
                  g_spec(0), g_spec(half),
                  pl.BlockSpec((n_slc, n_blk), lambda b, j: (0, 0))],
        out_specs=pl.BlockSpec((1, 2, ATT_Q, GROUP_WIDTH), lambda b, j: (b, 0, j, 0)),
        out_shape=jax.ShapeDtypeStruct((bsz, 2, s // 2, GROUP_WIDTH), F32),
        scratch_shapes=tile_scratch + tile_scratch,
        compiler_params=_cparams("parallel", "arbitrary"),
        name="sparse_attention",
    )(q_t, q_t, ks, kw, vs_t, vw_t, kc, vc_t, g_t, g_t, ov_t)
    return out.reshape(bsz, s, GROUP_WIDTH)


def _mix_ffn_kernel(x_ref, yatt_ref, ca_ref, cg_ref, pin_ref, su_ref, sv_ref, cah_ref, cgh_ref, ph_ref,
                    cw_ref, cb_ref, clg_ref, clb_ref, pw_ref, psc_ref, slg_ref, slb_ref, sw_ref, sbias_ref,
                    wo_ref, gm_ref, g1_ref, w1_ref, w2_ref, g2_ref,
                    o_ref, y_ref, hp_ref, pp_ref, acc_ref, *, tiles_per_seq):
    k = pl.program_id(0)
    tm = x_ref.shape[0]

    @pl.when(k == 0)
    def _():
        y_ref[...] = jnp.zeros_like(y_ref)

    kk = jnp.minimum(k, pl.num_programs(0) - 2)
    seq_tile = kk % tiles_per_seq
    has_prev = seq_tile > 0

    def conv_piece(r0):
        y_ref[0, r0:r0 + CONV_ROWS, :] = _conv_rows(hp_ref, r0, cw_ref, cb_ref, clg_ref, clb_ref)
        return 0, r0

    def pool_piece(r0):
        y_ref[1, r0:r0 + POOL_ROWS, :] = _pool_rows(pp_ref, r0, seq_tile * tm, pw_ref, psc_ref)
        return 1, r0

    def sgu_piece(r0):
        rows = slice(r0, r0 + SGU_CHUNK)
        y_ref[2, rows, :] = _sgu_rows(su_ref[rows, :], sv_ref[rows, :], _sgu_weights(sw_ref),
                                      slg_ref, slb_ref, sbias_ref)
        return 2, r0

    def order_before_next_accumulate(n, r0):
        bits = pltpu.bitcast(y_ref[n, r0:r0 + SUBLANES, 0:LANES], jnp.uint32)
        zero = pltpu.bitcast(lax.shift_right_logical(bits, jnp.uint32(32)), F32)
        acc_ref[0:SUBLANES, 0:LANES] = acc_ref[0:SUBLANES, 0:LANES] + zero

    pieces = []
    for r0 in range(0, tm, CONV_ROWS):
        pieces += [functools.partial(conv_piece, r0), functools.partial(pool_piece, r0)]
        if r0 % SGU_CHUNK == 0:
            pieces.append(functools.partial(sgu_piece, r0))
    n_chunks = w1_ref.shape[1] // FFN_COLS
    per_chunk = -(-len(pieces) // (n_chunks - 1))

    acc = jnp.dot(yatt_ref[...].astype(BF16), wo_ref[3 * GROUP_WIDTH:4 * GROUP_WIDTH, :],
                  preferred_element_type=F32)
    for n in range(3):
        acc = acc + jnp.dot(y_ref[n].astype(BF16), wo_ref[n * GROUP_WIDTH:(n + 1) * GROUP_WIDTH, :],
                            preferred_element_type=F32)
    x1 = x_ref[...] + _rms_norm(acc, gm_ref[...])
    h = _rms_norm(x1, g1_ref[...]).astype(BF16)
    hp_ref[0:MIX_HALO, :] = jnp.where(has_prev, cah_ref[...] * _sigmoid(cgh_ref[...]), 0.0)
    hp_ref[MIX_HALO:, :] = ca_ref[...] * _sigmoid(cg_ref[...])
    pp_ref[0:MIX_HALO, :] = jnp.where(has_prev, ph_ref[...], 0.0)
    pp_ref[MIX_HALO:, :] = pin_ref[...]
    for c in range(n_chunks):
        cols = slice(c * FFN_COLS, (c + 1) * FFN_COLS)
        f = jnp.maximum(jnp.dot(h, w1_ref[:, cols], preferred_element_type=F32), 0.0)
        part = jnp.dot((f * f).astype(BF16), w2_ref[cols, :], preferred_element_type=F32)
        acc_ref[...] = part if c == 0 else acc_ref[...] + part
        for piece in pieces[c * per_chunk:(c + 1) * per_chunk]:
            order_before_next_accumulate(*piece())
    o_ref[...] = x1 + _rms_norm(acc_ref[...], g2_ref[...])


def _mix_ffn(l, x2, y_att, za2, seq_len, lw):
    t, d = x2.shape
    dff = lw["ffn_w1"].shape[2]
    tm = min(FFN_ROWS, seq_len)
    assert CONV_ROWS == POOL_ROWS and SGU_CHUNK % CONV_ROWS == 0 and tm % SGU_CHUNK == 0
    n_tiles = t // tm
    halo_per_tile = tm // MIX_HALO
    cur = lambda k: jnp.maximum(k - 1, 0)
    nxt = lambda k: jnp.minimum(k, n_tiles - 1)
    main = lambda col: pl.BlockSpec((tm, GROUP_WIDTH), lambda k: (nxt(k), col))
    halo = lambda col: pl.BlockSpec((MIX_HALO, GROUP_WIDTH),
                                    lambda k: (jnp.maximum(nxt(k) * halo_per_tile - 1, 0), col))
    vec = lambda n: _per_layer(l, 1, n)
    once = lambda *shp: _per_layer(l, *shp, pipeline_mode=pl.Buffered(1))
    return pl.pallas_call(
        functools.partial(_mix_ffn_kernel, tiles_per_seq=seq_len // tm),
        grid=(n_tiles + 1,),
        in_specs=[pl.BlockSpec((tm, d), lambda k: (cur(k), 0)),
                  pl.BlockSpec((tm, GROUP_WIDTH), lambda k: (cur(k), 0)),
                  main(COL_CVAL), main(COL_CGATE), main(COL_POOL), main(COL_SGU_U), main(COL_SGU_V),
                  halo(COL_CVAL), halo(COL_CGATE), halo(COL_POOL),
                  _per_layer(l, CONV_WIDTH, GROUP_WIDTH), vec(GROUP_WIDTH), vec(GROUP_WIDTH), vec(GROUP_WIDTH),
                  _per_layer(l, GROUP_WIDTH, GROUP_WIDTH), vec(GROUP_WIDTH),
                  vec(GROUP_WIDTH), vec(GROUP_WIDTH),
                  _per_layer(l, SGU_HEADS, SGU_CHUNK, SGU_CHUNK), _per_layer(l, SGU_CHUNK, GROUP_WIDTH),
                  once(d, d), vec(d), vec(d), once(d, dff), once(dff, d), vec(d)],
        out_specs=pl.BlockSpec((tm, d), lambda k: (cur(k), 0)),
        out_shape=jax.ShapeDtypeStruct((t, d), F32),
        scratch_shapes=[pltpu.VMEM((3, tm, GROUP_WIDTH), F32),
                        pltpu.VMEM((MIX_HALO + tm, GROUP_WIDTH), F32),
                        pltpu.VMEM((MIX_HALO + tm, GROUP_WIDTH), F32),
                        pltpu.VMEM((tm, d), F32)],
        compiler_params=_cparams("arbitrary"),
        name="mix_ffn",
    )(x2, y_att, za2, za2, za2, za2, za2, za2, za2, za2,
      lw["conv_w"], lw["conv_b"], lw["conv_lg"], lw["conv_lb"], lw["pool_bd"], lw["pool_scale"],
      lw["sgu_lg"], lw["sgu_lb"], lw["sgu_w"], lw["sgu_bias"],
      lw["w_out"], lw["post_mix"], lw["pre_ffn"], lw["ffn_w1"], lw["ffn_w2"], lw["post_ffn"])


def _rope_perm():
    idx = list(range(HEAD_DIM))
    for c in range(ROPE_HALF):
        idx[c], idx[c + ROPE_HALF] = c + ROPE_HALF, c
    return jnp.array(idx, jnp.int32)


def _split_w_in(w_in):
    gw = GROUP_WIDTH
    edges = [0, gw, 2 * gw, 3 * gw, 4 * gw, 5 * gw, 6 * gw]
    for _ in range(6):
        edges.append(edges[-1] + HEAD_DIM)
    edges.append(edges[-1] + ATT_HEADS * N_BRANCH)
    names = ("cval", "cgate", "pool", "su", "sv", "q", "kc", "vc", "ks", "vs", "kw", "vw", "g")
    return {n: w_in[..., edges[k]:edges[k + 1]] for k, n in enumerate(names)}


def _prep_weights(p):
    depth = p["w_in"].shape[0]
    perm = _rope_perm()
    w = _split_w_in(p["w_in"])
    wa = jnp.concatenate([w["cval"], w["cgate"], w["pool"], w["su"], w["sv"], w["kc"], w["vc"],
                          w["ks"], w["ks"][..., perm], w["kw"], w["kw"][..., perm]], axis=-1).astype(BF16)
    pad = jnp.zeros((depth, D_MODEL, ZB_ROWS - ROW_G - ATT_HEADS * N_BRANCH), F32)
    wb = jnp.swapaxes(jnp.concatenate([w["q"], w["vs"], w["vw"], w["g"], pad], axis=-1), 1, 2).astype(BF16)

    n_grp = len(POOL_WINDOWS)
    pool_bd = (p["pool_w"][:, :, :, None, :] * jnp.eye(n_grp, dtype=F32)[None, :, None, :, None])
    pool_bd = pool_bd.reshape(depth, GROUP_WIDTH, GROUP_WIDTH)
    sgu_bias = jnp.repeat(jnp.swapaxes(p["sgu_b"], 1, 2), GROUP_WIDTH // SGU_HEADS, axis=2)

    w1k = p["cmp_k_w1"].reshape(depth, CMP_BLOCK, HEAD_DIM, HEAD_DIM)
    w1v = p["cmp_v_w1"].reshape(depth, CMP_BLOCK, HEAD_DIM, HEAD_DIM)
    z = jnp.zeros((depth, CMP_STRIDE, HEAD_DIM, HEAD_DIM), F32)

    def kv_diag(a, b):
        return jnp.concatenate([jnp.concatenate([a, z], axis=3), jnp.concatenate([z, b], axis=3)], axis=2)

    cmp_wa = kv_diag(w1k[:, :CMP_STRIDE], w1v[:, :CMP_STRIDE]).astype(BF16)
    cmp_wb = kv_diag(w1k[:, CMP_STRIDE:], w1v[:, CMP_STRIDE:]).astype(BF16)
    ones = jnp.ones((1, 1, HEAD_DIM), F32)
    cmp_pe = jnp.concatenate([p["cmp_k_pe"].reshape(depth, -1, 1) * ones,
                              p["cmp_v_pe"].reshape(depth, -1, 1) * ones], axis=2)
    cmp_w1 = jnp.concatenate([p["cmp_k_w1"], p["cmp_v_w1"]], axis=2)
    zk = jnp.zeros((depth, HEAD_DIM, LANES), F32)
    cmp_w2k = jnp.concatenate([jnp.concatenate([p["cmp_k_w2"], p["cmp_k_w2"][..., perm]], axis=2), zk],
                              axis=1).astype(BF16)
    cmp_w2vt = jnp.concatenate([jnp.zeros((depth, HEAD_DIM, HEAD_DIM), F32),
                                jnp.swapaxes(p["cmp_v_w2"], 1, 2)], axis=2).astype(BF16)

    row = lambda v: v[:, None, :]
    return dict(
        wa=wa, wb=wb, pre_mix=row(p["pre_mix_norm"]), post_mix=row(p["post_mix_norm"]),
        pre_ffn=row(p["pre_ffn_norm"]), post_ffn=row(p["post_ffn_norm"]),
        conv_w=p["conv_w"], conv_b=row(p["conv_b"]), conv_lg=row(p["conv_ln_g"]), conv_lb=row(p["conv_ln_b"]),
        pool_bd=pool_bd.astype(BF16), pool_scale=row(p["pool_scale"]),
        sgu_lg=row(p["sgu_ln_g"]), sgu_lb=row(p["sgu_ln_b"]), sgu_w=p["sgu_w"], sgu_bias=sgu_bias,
        cmp_wa=cmp_wa, cmp_wb=cmp_wb, cmp_pe=cmp_pe, cmp_w1=cmp_w1, cmp_w2k=cmp_w2k, cmp_w2vt=cmp_w2vt,
        w_out=p["w_out"].astype(BF16), ffn_w1=p["ffn_w1"].astype(BF16), ffn_w2=p["ffn_w2"].astype(BF16),
    )


def _overlap_t(s):
    n_blk = s // CMP_STRIDE
    n_slc = s // SLC_BLOCK
    bs = jnp.arange(n_blk)[None, :] * CMP_STRIDE
    ss = jnp.arange(n_slc)[:, None] * SLC_BLOCK
    ov = jnp.clip(jnp.minimum(bs + CMP_BLOCK, ss + SLC_BLOCK) - jnp.maximum(bs, ss), 0)
    return (ov.astype(F32) / CMP_STRIDE).astype(BF16)


def _rope_tables(positions):
    bsz, s = positions.shape
    inv = (ROPE_THETA ** (-jnp.arange(ROPE_HALF, dtype=F32) * 2.0 / ROPE_DIM)).reshape(ROPE_HALF, 1)
    posf = positions.astype(F32)
    n_blk = s // CMP_STRIDE
    pos_end = posf[:, CMP_BLOCK - 1::CMP_STRIDE]
    pos_end = jnp.pad(pos_end, ((0, 0), (0, n_blk - pos_end.shape[1])))
    cos_t, sin_t = _rope_chan_table(posf[:, None, :], inv)
    cos_b, sin_b = _rope_chan_table(pos_end[:, None, :], inv)
    return cos_t, sin_t, cos_b, sin_b


def _layer(l, x, lw, tables, ov_t):
    bsz, s, d = x.shape
    za, zb = _proj(l, x, lw["pre_mix"], lw["wa"], lw["wb"])
    prep = _att_prep(l, za, zb, *tables, lw["cmp_wa"], lw["cmp_wb"],
                     lw["cmp_pe"], lw["cmp_w1"], lw["cmp_w2k"], lw["cmp_w2vt"])
    y_att = _att(*prep, ov_t)
    x2 = _mix_ffn(l, x.reshape(bsz * s, d), y_att.reshape(bsz * s, GROUP_WIDTH),
                  za.reshape(bsz * s, ZA_WIDTH), s, lw)
    return x2.reshape(bsz, s, d)


def kernel(x, positions, pre_mix_norm, post_mix_norm, pre_ffn_norm, post_ffn_norm, w_in, conv_w, conv_b, conv_ln_g, conv_ln_b, pool_w, pool_scale, sgu_ln_g, sgu_ln_b, sgu_w, sgu_b, cmp_k_pe, cmp_k_w1, cmp_k_w2, cmp_v_pe, cmp_v_w1, cmp_v_w2, w_out, ffn_w1, ffn_w2):
    params = dict(pre_mix_norm=pre_mix_norm, post_mix_norm=post_mix_norm, pre_ffn_norm=pre_ffn_norm,
                  post_ffn_norm=post_ffn_norm, w_in=w_in, conv_w=conv_w, conv_b=conv_b,
                  conv_ln_g=conv_ln_g, conv_ln_b=conv_ln_b, pool_w=pool_w, pool_scale=pool_scale,
                  sgu_ln_g=sgu_ln_g, sgu_ln_b=sgu_ln_b, sgu_w=sgu_w, sgu_b=sgu_b,
                  cmp_k_pe=cmp_k_pe, cmp_k_w1=cmp_k_w1, cmp_k_w2=cmp_k_w2, cmp_v_pe=cmp_v_pe,
                  cmp_v_w1=cmp_v_w1, cmp_v_w2=cmp_v_w2, w_out=w_out, ffn_w1=ffn_w1, ffn_w2=ffn_w2)
    bsz, s, d = x.shape
    depth = w_in.shape[0]
    tables = _rope_tables(positions)
    ov_t = _overlap_t(s)
    lw = _prep_weights(params)
    for l in range(depth):
        x = _layer(l, x, lw, tables, ov_t)
    return x
```

```python
import functools

import jax
import jax.numpy as jnp
from jax import lax
from jax.experimental import pallas as pl
from jax.experimental.pallas import tpu as pltpu

F32 = jnp.float32
BF16 = jnp.bfloat16

D_MODEL = 1024
GROUP_WIDTH = 256
CONV_WIDTH = 31
POOL_WINDOWS = (2, 4, 8, 16)
POOL_GROUP = 64
SGU_HEADS = 4
SGU_CHUNK = 128
ATT_HEADS = 4
HEAD_DIM = 64
ROPE_DIM = 16
ROPE_HALF = 8
ROPE_THETA = 500000.0
CMP_BLOCK = 32
CMP_STRIDE = 16
SLC_BLOCK = 64
N_SELECT = 8
N_LOCAL = 2
WINDOW = 512
N_BRANCH = 3
D_FF = 4096
NORM_EPS = 1e-6
NEG_INF = -1e30
FORCE_SCORE = 1e9

LANES = 128
SUBLANES = 8
VMEM_LIMIT_BYTES = 56 * 1024 * 1024

PROJ_ROWS = 512
FFN_ROWS = 512
FFN_COLS = 512
CONV_ROWS = 64
POOL_ROWS = 64
ATT_Q = 256
ATT_K = 256
ATT_STREAMS = 4
LOG2_E = 1.4426950408889634
V_ROWS = 80
MASK_BIAS = -2e30

COL_CVAL, COL_CGATE, COL_POOL, COL_SGU_U, COL_SGU_V = 0, 1, 2, 3, 4
COL_KCVC, COL_KS, COL_KW = 10, 11, 12
ZA_WIDTH = 1664
ROW_Q, ROW_VS, ROW_VW, ROW_G = 0, 256, 320, 384
ZB_ROWS = 400


def _cparams(*sem):
    return pltpu.CompilerParams(dimension_semantics=sem, vmem_limit_bytes=VMEM_LIMIT_BYTES)


def _per_layer(l, *shape, pipeline_mode=None):
    zeros = (0,) * len(shape)
    extra = {} if pipeline_mode is None else {"pipeline_mode": pipeline_mode}
    return pl.BlockSpec((None,) + shape, lambda *_: (l,) + zeros, **extra)


def _gelu(x):
    return 0.5 * x * (1.0 + jnp.tanh(0.7978845608028654 * (x + 0.044715 * (x * x * x))))


def _sigmoid(x):
    return 0.5 * jnp.tanh(0.5 * x) + 0.5


def _layer_norm(x, g, b):
    mu = jnp.mean(x, axis=-1, keepdims=True)
    d = x - mu
    var = jnp.mean(d * d, axis=-1, keepdims=True)
    return d * lax.rsqrt(var + NORM_EPS) * g + b


def _rms_norm(x, g):
    return x * lax.rsqrt(jnp.mean(x * x, axis=-1, keepdims=True) + NORM_EPS) * g


def _rope_chan_kernel(pos_ref, inv_ref, cos_ref, sin_ref):
    ang = inv_ref[...] * pos_ref[0]
    cos_ref[0] = jnp.cos(ang)
    sin_ref[0] = jnp.sin(ang)


def _rope_chan_table(pos_row, inv_col):
    b, _, s = pos_row.shape
    spec = pl.BlockSpec((1, ROPE_HALF, s), lambda i: (i, 0, 0))
    return pl.pallas_call(
        _rope_chan_kernel,
        grid=(b,),
        in_specs=[pl.BlockSpec((1, 1, s), lambda i: (i, 0, 0)),
                  pl.BlockSpec((ROPE_HALF, 1), lambda i: (0, 0))],
        out_specs=[spec, spec],
        out_shape=[jax.ShapeDtypeStruct((b, ROPE_HALF, s), F32)] * 2,
        compiler_params=_cparams("parallel"),
        name="rope_chan_table",
    )(pos_row, inv_col)


def _proj_kernel(x_ref, g_ref, wa_ref, wb_ref, za_ref, zb_ref):
    h = _rms_norm(x_ref[0], g_ref[...]).astype(BF16)
    za_ref[0] = jnp.dot(h, wa_ref[...], preferred_element_type=F32)
    zb_ref[0] = lax.dot_general(wb_ref[...], h, (((1,), (1,)), ((), ())),
                                preferred_element_type=F32)


def _proj(l, x, g, wa, wb):
    b, s, d = x.shape
    tm = min(PROJ_ROWS, s)
    return pl.pallas_call(
        _proj_kernel,
        grid=(b, s // tm),
        in_specs=[pl.BlockSpec((1, tm, d), lambda i, j: (i, j, 0)),
                  _per_layer(l, 1, d), _per_layer(l, d, ZA_WIDTH), _per_layer(l, ZB_ROWS, d)],
        out_specs=[pl.BlockSpec((1, tm, ZA_WIDTH), lambda i, j: (i, j, 0)),
                   pl.BlockSpec((1, ZB_ROWS, tm), lambda i, j: (i, 0, j))],
        out_shape=[jax.ShapeDtypeStruct((b, s, ZA_WIDTH), F32),
                   jax.ShapeDtypeStruct((b, ZB_ROWS, s), F32)],
        compiler_params=_cparams("parallel", "parallel"),
        name="in_proj",
    )(x, g, wa, wb)


MIX_HALO = 32


def _conv_rows(hp_ref, r0, w_ref, b_ref, lg_ref, lb_ref):
    shift = MIX_HALO - (CONV_WIDTH - 1)
    n_win = CONV_ROWS + MIX_HALO
    acc = jnp.zeros((CONV_ROWS, GROUP_WIDTH), F32) + b_ref[...]
    win = hp_ref[r0:r0 + n_win, :]
    for r in range(SUBLANES):
        taps = [k for k in range(CONV_WIDTH) if (k + shift) % SUBLANES == r]
        wr = win if r == 0 else pltpu.roll(win, n_win - r, axis=0)
        for k in taps:
            off = k + shift - r
            acc = acc + wr[off:off + CONV_ROWS, :] * w_ref[k:k + 1, :]
    y = _layer_norm(acc, lg_ref[...], lb_ref[...])
    return y * _sigmoid(y)


POOL_PAD = 16


def _pool_rows(pp_ref, r0, seq_t0, w_ref, sc_ref):
    lane = lax.broadcasted_iota(jnp.int32, (1, GROUP_WIDTH), 1)
    grp = lane // POOL_GROUP
    win = jnp.where(grp == 0, POOL_WINDOWS[0],
                    jnp.where(grp == 1, POOL_WINDOWS[1],
                              jnp.where(grp == 2, POOL_WINDOWS[2], POOL_WINDOWS[3])))
    base = r0 + MIX_HALO - POOL_PAD
    rows = pp_ref[base:base + POOL_ROWS + POOL_PAD, :]
    p = rows[POOL_PAD:POOL_PAD + POOL_ROWS, :]
    acc = rows
    sums = []
    width = 1
    for w in POOL_WINDOWS:
        while width < w:
            acc = acc + pltpu.roll(acc, width, axis=0)
            width *= 2
        sums.append(acc[POOL_PAD:POOL_PAD + POOL_ROWS, :])
    total = jnp.where(grp == 0, sums[0],
                      jnp.where(grp == 1, sums[1],
                                jnp.where(grp == 2, sums[2], sums[3])))
    t = seq_t0 + r0 + lax.broadcasted_iota(jnp.int32, (POOL_ROWS, 1), 0)
    count = jnp.minimum(t + 1, win).astype(F32)
    mixed = total / count - p
    y = jnp.dot(mixed.astype(BF16), w_ref[...], preferred_element_type=F32)
    return y * sc_ref[...]


def _sgu_weights(w_ref):
    row = lax.broadcasted_iota(jnp.int32, (SGU_CHUNK, SGU_CHUNK), 0)
    col = lax.broadcasted_iota(jnp.int32, (SGU_CHUNK, SGU_CHUNK), 1)
    return [jnp.where(row >= col, w_ref[h], 0.0).astype(BF16) for h in range(SGU_HEADS)]


def _sgu_rows(u, v, ws, lg_ref, lb_ref, bias_ref):
    head = lax.broadcasted_iota(jnp.int32, (1, GROUP_WIDTH), 1) // (GROUP_WIDTH // SGU_HEADS)
    vb = _layer_norm(_gelu(v), lg_ref[...], lb_ref[...]).astype(BF16)
    mixed = jnp.zeros((SGU_CHUNK, GROUP_WIDTH), F32)
    for h in range(SGU_HEADS):
        mixed = jnp.where(head == h, jnp.dot(ws[h], vb, preferred_element_type=F32), mixed)
    return _gelu(u) * (mixed + bias_ref[...])


def _att_prep_kernel(kcvc_ref, ksx_ref, kwx_ref, zb_ref, cos_ref, sin_ref, cosb_ref, sinb_ref,
                     wa_ref, wb_ref, pe_ref, w1_ref, w2k_ref, w2vt_ref,
                     q_ref, ks_ref, kw_ref, vs_ref, vw_ref, kc_ref, vct_ref, g_ref, sh_ref):
    s = ksx_ref.shape[1]
    n_blk = s // CMP_STRIDE

    def tok_table(cos, sin):
        n = cos.shape[1]
        rest = HEAD_DIM - ROPE_DIM
        rows = [cos, cos, jnp.ones((rest, n), F32), -sin, sin, jnp.zeros((rest, n), F32)]
        return jnp.concatenate(rows, axis=0).T

    def rope_wide(x, cs):
        r = x * cs
        return r + pltpu.roll(r, HEAD_DIM, axis=1)

    def rope_tok(x, cs):
        return rope_wide(x, cs)[:, :HEAD_DIM]

    cos = cos_ref[0]
    sin = sin_ref[0]
    cs = tok_table(cos, sin)
    lane = lax.broadcasted_iota(jnp.int32, (s, LANES), 1)
    blk = lax.broadcasted_iota(jnp.int32, (s, LANES), 0) // SLC_BLOCK
    ks_ref[0] = jnp.where(lane < HEAD_DIM, rope_wide(ksx_ref[0], cs),
                          jnp.where(lane - HEAD_DIM == blk, 1.0, 0.0)).astype(BF16)
    kw_ref[0] = rope_tok(kwx_ref[0], cs).astype(BF16)

    scale = HEAD_DIM ** -0.5 * LOG2_E
    parts = []
    for h in range(ATT_HEADS):
        r0 = ROW_Q + h * HEAD_DIM
        x1 = zb_ref[0, r0:r0 + ROPE_HALF, :]
        x2 = zb_ref[0, r0 + ROPE_HALF:r0 + ROPE_DIM, :]
        parts += [x1 * cos - x2 * sin, x2 * cos + x1 * sin, zb_ref[0, r0 + ROPE_DIM:r0 + HEAD_DIM, :]]
    q_ref[0] = (jnp.concatenate(parts, axis=0) * scale).astype(BF16)

    ones_row = jnp.where(lax.broadcasted_iota(jnp.int32, (V_ROWS - HEAD_DIM, ATT_K), 0) == 0, 1.0, 0.0)
    for c in range(s // ATT_K):
        cols = slice(c * ATT_K, (c + 1) * ATT_K)
        vs_ref[0, c] = jnp.concatenate([zb_ref[0, ROW_VS:ROW_VS + HEAD_DIM, cols], ones_row], axis=0).astype(BF16)
        vw_ref[0, c] = jnp.concatenate([zb_ref[0, ROW_VW:ROW_VW + HEAD_DIM, cols], ones_row], axis=0).astype(BF16)
    g_ref[0] = _sigmoid(zb_ref[0, ROW_G:ROW_G + 16, :])

    acc_a = jnp.zeros((n_blk, LANES), F32)
    acc_b = jnp.zeros((n_blk, LANES), F32)
    for l in range(CMP_STRIDE):
        xl = kcvc_ref[0, pl.ds(l, n_blk, stride=CMP_STRIDE), :].astype(BF16)
        acc_a = acc_a + jnp.dot(xl, wa_ref[l], preferred_element_type=F32)
        acc_b = acc_b + jnp.dot(xl, wb_ref[l], preferred_element_type=F32)
    sh_ref[0:n_blk, :] = acc_b
    sh_ref[n_blk:n_blk + SUBLANES, :] = jnp.zeros((SUBLANES, LANES), F32)
    pe_term = jnp.sum(pe_ref[...] * w1_ref[...], axis=0, keepdims=True)
    hid = _gelu(acc_a + sh_ref[1:n_blk + 1, :] + pe_term).astype(BF16)
    kk = jnp.dot(hid, w2k_ref[...], preferred_element_type=F32)
    kc_ref[0] = rope_tok(kk, tok_table(cosb_ref[0], sinb_ref[0])).astype(BF16)
    vct_ref[0] = lax.dot_general(w2vt_ref[...], hid, (((1,), (1,)), ((), ())),
                                 preferred_element_type=F32).astype(BF16)


def _att_prep(l, za, zb, cos_t, sin_t, cos_b, sin_b, wa, wb, pe, w1, w2k, w2vt):
    bsz, s, _ = za.shape
    n_blk = s // CMP_STRIDE
    nt = s // ATT_K
    tok = lambda c: pl.BlockSpec((1, s, LANES), lambda i: (i, 0, c))
    per_b3 = lambda shp: pl.BlockSpec((1,) + shp, lambda i: (i, 0, 0))
    per_b4 = lambda shp: pl.BlockSpec((1,) + shp, lambda i: (i, 0, 0, 0))
    return pl.pallas_call(
        _att_prep_kernel,
        grid=(bsz,),
        in_specs=[tok(COL_KCVC), tok(COL_KS), tok(COL_KW),
                  per_b3((ZB_ROWS, s)),
                  per_b3((ROPE_HALF, s)), per_b3((ROPE_HALF, s)),
                  per_b3((ROPE_HALF, n_blk)), per_b3((ROPE_HALF, n_blk)),
                  _per_layer(l, CMP_STRIDE, LANES, LANES), _per_layer(l, CMP_STRIDE, LANES, LANES),
                  _per_layer(l, CMP_BLOCK * HEAD_DIM, LANES), _per_layer(l, CMP_BLOCK * HEAD_DIM, LANES),
                  _per_layer(l, LANES, LANES), _per_layer(l, HEAD_DIM, LANES)],
        out_specs=[per_b3((ATT_HEADS * HEAD_DIM, s)), per_b3((s, LANES)), per_b3((s, HEAD_DIM)),
                   per_b4((nt, V_ROWS, ATT_K)), per_b4((nt, V_ROWS, ATT_K)),
                   per_b3((n_blk, HEAD_DIM)), per_b3((HEAD_DIM, n_blk)), per_b3((16, s))],
        out_shape=[jax.ShapeDtypeStruct((bsz, ATT_HEADS * HEAD_DIM, s), BF16),
                   jax.ShapeDtypeStruct((bsz, s, LANES), BF16),
                   jax.ShapeDtypeStruct((bsz, s, HEAD_DIM), BF16),
                   jax.ShapeDtypeStruct((bsz, nt, V_ROWS, ATT_K), BF16),
                   jax.ShapeDtypeStruct((bsz, nt, V_ROWS, ATT_K), BF16),
                   jax.ShapeDtypeStruct((bsz, n_blk, HEAD_DIM), BF16),
                   jax.ShapeDtypeStruct((bsz, HEAD_DIM, n_blk), BF16),
                   jax.ShapeDtypeStruct((bsz, 16, s), F32)],
        scratch_shapes=[pltpu.VMEM((n_blk + SUBLANES, LANES), F32)],
        compiler_params=_cparams("parallel"),
        name="att_prep",
    )(za, za, za, zb, cos_t, sin_t, cos_b, sin_b, wa, wb, pe, w1, w2k, w2vt)


def _att_tile(i, q_ref, g_ref, ks_ref, kw_ref, vs_ref, vw_ref, kc_ref, vct_ref, ov_ref,
              qa_ref, sa_ref, sb_ref, m_ref, acc_ref):
    n_blk = kc_ref.shape[1]
    n_slc = ov_ref.shape[0]
    tq = ATT_Q
    wide = ATT_HEADS * tq
    t0 = i * tq

    q = q_ref[0]
    qs = jnp.concatenate([q[h * HEAD_DIM:(h + 1) * HEAD_DIM] for h in range(ATT_HEADS)], axis=1)
    lane_w = lax.broadcasted_iota(jnp.int32, (1, wide), 1)
    t_w = t0 + (lane_w & (tq - 1))
    t_q = t0 + lax.broadcasted_iota(jnp.int32, (1, tq), 1)
    heads = lambda a: jnp.concatenate([a] * ATT_HEADS, axis=1)

    s_c = jnp.dot(kc_ref[0], qs, preferred_element_type=F32)
    blk_end = lax.broadcasted_iota(jnp.int32, (n_blk, 1), 0) * CMP_STRIDE + (CMP_BLOCK - 1)
    cmask = blk_end <= t_w
    s_m = jnp.where(cmask, s_c, NEG_INF)
    e = jnp.exp2(s_m - jnp.max(s_m, axis=0, keepdims=True))
    p_c = jnp.where(cmask, e * (1.0 / jnp.sum(e, axis=0, keepdims=True)), 0.0)
    o_cmp = jnp.dot(vct_ref[0], p_c.astype(BF16), preferred_element_type=F32)

    p_sum = p_c[:, 0:tq]
    for h in range(1, ATT_HEADS):
        p_sum = p_sum + p_c[:, h * tq:(h + 1) * tq]
    p_hi = p_sum.astype(BF16)
    p_lo = (p_sum - p_hi.astype(F32)).astype(BF16)
    imp = (jnp.dot(ov_ref[...], p_hi, preferred_element_type=F32)
           + jnp.dot(ov_ref[...], p_lo, preferred_element_type=F32))
    j = lax.broadcasted_iota(jnp.int32, (n_slc, 1), 0)
    back = t_q // SLC_BLOCK - j
    forced = (j == 0) | ((back >= 0) & (back < N_LOCAL))
    imp = jnp.where(forced, FORCE_SCORE, jnp.where(back < 0, -1.0, imp))
    rank = jnp.zeros((n_slc, tq), F32)
    for r in range(n_slc):
        row = imp[r:r + 1, :]
        ahead = (row > imp) | ((row == imp) & (j > r))
        rank = rank + jnp.where(ahead, 1.0, 0.0)
    sel_bias = jnp.where((rank < min(N_SELECT, n_slc)) & (back >= 0), 0.0, MASK_BIAS)
    qa_ref[0:HEAD_DIM, :] = qs
    qa_ref[HEAD_DIM:HEAD_DIM + n_slc, :] = heads(sel_bias).astype(BF16)
    qa_ref[HEAD_DIM + n_slc:, :] = jnp.zeros((LANES - HEAD_DIM - n_slc, wide), BF16)

    k_iota = lax.broadcasted_iota(jnp.int32, (ATT_K, 1), 0)

    n_win = (WINDOW + tq) // ATT_K
    first = jnp.maximum(i - WINDOW // ATT_K, 0)
    kw0 = pl.multiple_of(first * ATT_K, ATT_K)
    s_w = jnp.dot(kw_ref[0, pl.ds(kw0, n_win * ATT_K), :], qs, preferred_element_type=F32)
    diff = t_q - (kw0 + lax.broadcasted_iota(jnp.int32, (n_win * ATT_K, 1), 0))
    s_w = s_w + heads(jnp.where((diff >= 0) & (diff < WINDOW), 0.0, -jnp.inf))
    p_w = jnp.exp2(s_w - jnp.max(s_w, axis=0, keepdims=True)).astype(BF16)
    acc_w = None
    for n in range(n_win):
        part = jnp.dot(vw_ref[0, first + n], p_w[n * ATT_K:(n + 1) * ATT_K], preferred_element_type=F32)
        acc_w = part if acc_w is None else acc_w + part

    m_ref[...] = jnp.full((1, wide), NEG_INF, F32)
    acc_ref[...] = jnp.zeros((V_ROWS, wide), F32)

    def scores(kt, causal=False, live=None):
        k0 = pl.multiple_of(kt * ATT_K, ATT_K)
        s_t = jnp.dot(ks_ref[0, pl.ds(k0, ATT_K), :], qa_ref[...], preferred_element_type=F32)
        if causal:
            visible = (k0 + k_iota) <= t_q
            if live is not None:
                visible = visible & live
            s_t = jnp.where(heads(visible), s_t, -jnp.inf)
        return s_t

    def consume(s_ref, kt):
        m_old = m_ref[...]
        m_new = jnp.maximum(m_old, jnp.max(s_ref[...], axis=0, keepdims=True))
        alpha = jnp.exp2(m_old - m_new)
        p = jnp.exp2(s_ref[...] - m_new).astype(BF16)
        m_ref[...] = m_new
        acc_ref[...] = alpha * acc_ref[...] + jnp.dot(vs_ref[0, kt], p, preferred_element_type=F32)

    single = 1 - i % 2
    sa_ref[...] = scores(0, causal=True, live=single == 1)
    consume(sa_ref, 0)
    sa_ref[...] = scores(single)

    def pair(n):
        a = single + 2 * n
        sb_ref[...] = scores(a + 1, causal=True)
        consume(sa_ref, a)
        sa_ref[...] = scores(jnp.minimum(a + 2, i))
        consume(sb_ref, a + 1)

    def finish():
        acc_s = acc_ref[...]
        g = g_ref[0]
        gate = lambda br: jnp.concatenate(
            [g[h * N_BRANCH + br:h * N_BRANCH + br + 1, :] for h in range(ATT_HEADS)], axis=1)
        l_s = acc_s[HEAD_DIM:HEAD_DIM + 1, :]
        l_w = acc_w[HEAD_DIM:HEAD_DIM + 1, :]
        o = (gate(0) * o_cmp + (gate(1) * (1.0 / l_s)) * acc_s[:HEAD_DIM]
             + (gate(2) * (1.0 / l_w)) * acc_w[:HEAD_DIM])
        return jnp.concatenate([o[:, h * tq:(h + 1) * tq] for h in range(ATT_HEADS)], axis=0).T

    return pair, (i + 1 - single) // 2, finish


def _att_kernel(*refs, stride):
    n = ATT_STREAMS
    q_refs, refs = refs[:n], refs[n:]
    shared, refs = refs[:6], refs[6:]
    g_refs, refs = refs[:n], refs[n:]
    ov_ref, o_ref, scratch = refs[0], refs[1], refs[2:]
    n_scr = len(scratch) // n
    j = pl.program_id(1)
    tiles = [_att_tile(j + t * stride, q_refs[t], g_refs[t], *shared, ov_ref,
                       *scratch[t * n_scr:(t + 1) * n_scr]) for t in range(n)]
    common = tiles[0][1]

    def all_tiles(p, carry):
        for pair, _, _ in tiles:
            pair(p)
        return carry

    lax.fori_loop(0, common, all_tiles, 0)
    for extra in range((n - 1) * stride // 2):
        for t, (pair, _, _) in enumerate(tiles):
            if extra < t * stride // 2:
                pair(common + extra)
    for t, (_, _, finish) in enumerate(tiles):
        o_ref[0, t] = finish()


def _att(q_t, ks, kw, vs_t, vw_t, kc, vc_t, g_t, ov_t):
    bsz, _, s = q_t.shape
    n_blk = kc.shape[1]
    nt = s // ATT_K
    n_slc = s // SLC_BLOCK
    wide = ATT_HEADS * ATT_Q
    n = ATT_STREAMS
    stride = s // ATT_Q // n
    assert ATT_Q == ATT_K, "the selected-branch tile pairing assumes one diagonal key tile per query tile"
    assert stride % 2 == 0, "the tiles of a step must need the same parity of key tiles"
    per_b3 = lambda shp: pl.BlockSpec((1,) + shp, lambda b, j: (b, 0, 0))
    per_b4 = lambda shp: pl.BlockSpec((1,) + shp, lambda b, j: (b, 0, 0, 0))
    q_spec = lambda off: pl.BlockSpec((1, ATT_HEADS * HEAD_DIM, ATT_Q), lambda b, j: (b, 0, j + off))
    g_spec = lambda off: pl.BlockSpec((1, 16, ATT_Q), lambda b, j: (b, 0, j + off))
    tile_scratch = [pltpu.VMEM((LANES, wide), BF16),
                    pltpu.VMEM((ATT_K, wide), F32), pltpu.VMEM((ATT_K, wide), F32),
                    pltpu.VMEM((1, wide), F32), pltpu.VMEM((V_ROWS, wide), F32)]
    offsets = [t * stride for t in range(n)]
    out = pl.pallas_call(
        functools.partial(_att_kernel, stride=stride),
        grid=(bsz, stride),
        in_specs=[q_spec(off) for off in offsets]
                 + [per_b3((s, LANES)), per_b3((s, HEAD_DIM)),
                    per_b4((nt, V_ROWS, ATT_K)), per_b4((nt, V_ROWS, ATT_K)),
                    per_b3((n_blk, HEAD_DIM)), per_b3((HEAD_DIM, n_blk))]
                 + [g_spec(off) for off in offsets]
                 + [pl.BlockSpec((n_slc, n_blk), lambda b, j: (0, 0))],
        out_specs=pl.BlockSpec((1, n, ATT_Q, GROUP_WIDTH), lambda b, j: (b, 0, j, 0)),
        out_shape=jax.ShapeDtypeStruct((bsz, n, s // n, GROUP_WIDTH), F32),
        scratch_shapes=tile_scratch * n,
        compiler_params=_cparams("parallel", "arbitrary"),
        name="sparse_attention",
    )(*([q_t] * n), ks, kw, vs_t, vw_t, kc, vc_t, *([g_t] * n), ov_t)
    return out.reshape(bsz, s, GROUP_WIDTH)


def _mix_ffn_kernel(x_ref, yatt_ref, ca_ref, cg_ref, pin_ref, su_ref, sv_ref, cah_ref, cgh_ref, ph_ref,
                    cw_ref, cb_ref, clg_ref, clb_ref, pw_ref, psc_ref, slg_ref, slb_ref, sw_ref, sbias_ref,
                    wo_ref, gm_ref, g1_ref, w1_ref, w2_ref, g2_ref,
                    o_ref, y_ref, hp_ref, pp_ref, acc_ref, *, tiles_per_seq):
    k = pl.program_id(0)
    tm = x_ref.shape[0]

    @pl.when(k == 0)
    def _():
        y_ref[...] = jnp.zeros_like(y_ref)

    kk = jnp.minimum(k, pl.num_programs(0) - 2)
    seq_tile = kk % tiles_per_seq
    has_prev = seq_tile > 0

    def conv_piece(r0):
        y_ref[0, r0:r0 + CONV_ROWS, :] = _conv_rows(hp_ref, r0, cw_ref, cb_ref, clg_ref, clb_ref)
        return 0, r0

    def pool_piece(r0):
        y_ref[1, r0:r0 + POOL_ROWS, :] = _pool_rows(pp_ref, r0, seq_tile * tm, pw_ref, psc_ref)
        return 1, r0

    def sgu_piece(r0):
        rows = slice(r0, r0 + SGU_CHUNK)
        y_ref[2, rows, :] = _sgu_rows(su_ref[rows, :], sv_ref[rows, :], _sgu_weights(sw_ref),
                                      slg_ref, slb_ref, sbias_ref)
        return 2, r0

    def order_before_next_accumulate(n, r0):
        bits = pltpu.bitcast(y_ref[n, r0:r0 + SUBLANES, 0:LANES], jnp.uint32)
        zero = pltpu.bitcast(lax.shift_right_logical(bits, jnp.uint32(32)), F32)
        acc_ref[0:SUBLANES, 0:LANES] = acc_ref[0:SUBLANES, 0:LANES] + zero

    pieces = []
    for r0 in range(0, tm, CONV_ROWS):
        pieces += [functools.partial(conv_piece, r0), functools.partial(pool_piece, r0)]
        if r0 % SGU_CHUNK == 0:
            pieces.append(functools.partial(sgu_piece, r0))
    n_chunks = w1_ref.shape[1] // FFN_COLS
    per_chunk = -(-len(pieces) // (n_chunks - 1))

    acc = jnp.dot(yatt_ref[...].astype(BF16), wo_ref[3 * GROUP_WIDTH:4 * GROUP_WIDTH, :],
                  preferred_element_type=F32)
    for n in range(3):
        acc = acc + jnp.dot(y_ref[n].astype(BF16), wo_ref[n * GROUP_WIDTH:(n + 1) * GROUP_WIDTH, :],
                            preferred_element_type=F32)
    x1 = x_ref[...] + _rms_norm(acc, gm_ref[...])
    h = _rms_norm(x1, g1_ref[...]).astype(BF16)
    hp_ref[0:MIX_HALO, :] = jnp.where(has_prev, cah_ref[...] * _sigmoid(cgh_ref[...]), 0.0)
    hp_ref[MIX_HALO:, :] = ca_ref[...] * _sigmoid(cg_ref[...])
    pp_ref[0:MIX_HALO, :] = jnp.where(has_prev, ph_ref[...], 0.0)
    pp_ref[MIX_HALO:, :] = pin_ref[...]
    for c in range(n_chunks):
        cols = slice(c * FFN_COLS, (c + 1) * FFN_COLS)
        f = jnp.maximum(jnp.dot(h, w1_ref[:, cols], preferred_element_type=F32), 0.0)
        part = jnp.dot((f * f).astype(BF16), w2_ref[cols, :], preferred_element_type=F32)
        acc_ref[...] = part if c == 0 else acc_ref[...] + part
        for piece in pieces[c * per_chunk:(c + 1) * per_chunk]:
            order_before_next_accumulate(*piece())
    o_ref[...] = x1 + _rms_norm(acc_ref[...], g2_ref[...])


def _mix_ffn(l, x2, y_att, za2, seq_len, lw):
    t, d = x2.shape
    dff = lw["ffn_w1"].shape[2]
    tm = min(FFN_ROWS, seq_len)
    assert CONV_ROWS == POOL_ROWS and SGU_CHUNK % CONV_ROWS == 0 and tm % SGU_CHUNK == 0
    n_tiles = t // tm
    halo_per_tile = tm // MIX_HALO
    cur = lambda k: jnp.maximum(k - 1, 0)
    nxt = lambda k: jnp.minimum(k, n_tiles - 1)
    main = lambda col: pl.BlockSpec((tm, GROUP_WIDTH), lambda k: (nxt(k), col))
    halo = lambda col: pl.BlockSpec((MIX_HALO, GROUP_WIDTH),
                                    lambda k: (jnp.maximum(nxt(k) * halo_per_tile - 1, 0), col))
    vec = lambda n: _per_layer(l, 1, n)
    once = lambda *shp: _per_layer(l, *shp, pipeline_mode=pl.Buffered(1))
    return pl.pallas_call(
        functools.partial(_mix_ffn_kernel, tiles_per_seq=seq_len // tm),
        grid=(n_tiles + 1,),
        in_specs=[pl.BlockSpec((tm, d), lambda k: (cur(k), 0)),
                  pl.BlockSpec((tm, GROUP_WIDTH), lambda k: (cur(k), 0)),
                  main(COL_CVAL), main(COL_CGATE), main(COL_POOL), main(COL_SGU_U), main(COL_SGU_V),
                  halo(COL_CVAL), halo(COL_CGATE), halo(COL_POOL),
                  _per_layer(l, CONV_WIDTH, GROUP_WIDTH), vec(GROUP_WIDTH), vec(GROUP_WIDTH), vec(GROUP_WIDTH),
                  _per_layer(l, GROUP_WIDTH, GROUP_WIDTH), vec(GROUP_WIDTH),
                  vec(GROUP_WIDTH), vec(GROUP_WIDTH),
                  _per_layer(l, SGU_HEADS, SGU_CHUNK, SGU_CHUNK), _per_layer(l, SGU_CHUNK, GROUP_WIDTH),
                  once(d, d), vec(d), vec(d), once(d, dff), once(dff, d), vec(d)],
        out_specs=pl.BlockSpec((tm, d), lambda k: (cur(k), 0)),
        out_shape=jax.ShapeDtypeStruct((t, d), F32),
        scratch_shapes=[pltpu.VMEM((3, tm, GROUP_WIDTH), F32),
                        pltpu.VMEM((MIX_HALO + tm, GROUP_WIDTH), F32),
                        pltpu.VMEM((MIX_HALO + tm, GROUP_WIDTH), F32),
                        pltpu.VMEM((tm, d), F32)],
        compiler_params=_cparams("arbitrary"),
        name="mix_ffn",
    )(x2, y_att, za2, za2, za2, za2, za2, za2, za2, za2,
      lw["conv_w"], lw["conv_b"], lw["conv_lg"], lw["conv_lb"], lw["pool_bd"], lw["pool_scale"],
      lw["sgu_lg"], lw["sgu_lb"], lw["sgu_w"], lw["sgu_bias"],
      lw["w_out"], lw["post_mix"], lw["pre_ffn"], lw["ffn_w1"], lw["ffn_w2"], lw["post_ffn"])


def _rope_perm():
    idx = list(range(HEAD_DIM))
    for c in range(ROPE_HALF):
        idx[c], idx[c + ROPE_HALF] = c + ROPE_HALF, c
    return jnp.array(idx, jnp.int32)


def _split_w_in(w_in):
    gw = GROUP_WIDTH
    edges = [0, gw, 2 * gw, 3 * gw, 4 * gw, 5 * gw, 6 * gw]
    for _ in range(6):
        edges.append(edges[-1] + HEAD_DIM)
    edges.append(edges[-1] + ATT_HEADS * N_BRANCH)
    names = ("cval", "cgate", "pool", "su", "sv", "q", "kc", "vc", "ks", "vs", "kw", "vw", "g")
    return {n: w_in[..., edges[k]:edges[k + 1]] for k, n in enumerate(names)}


def _prep_weights(p):
    depth = p["w_in"].shape[0]
    perm = _rope_perm()
    w = _split_w_in(p["w_in"])
    wa = jnp.concatenate([w["cval"], w["cgate"], w["pool"], w["su"], w["sv"], w["kc"], w["vc"],
                          w["ks"], w["ks"][..., perm], w["kw"], w["kw"][..., perm]], axis=-1).astype(BF16)
    pad = jnp.zeros((depth, D_MODEL, ZB_ROWS - ROW_G - ATT_HEADS * N_BRANCH), F32)
    wb = jnp.swapaxes(jnp.concatenate([w["q"], w["vs"], w["vw"], w["g"], pad], axis=-1), 1, 2).astype(BF16)

    n_grp = len(POOL_WINDOWS)
    pool_bd = (p["pool_w"][:, :, :, None, :] * jnp.eye(n_grp, dtype=F32)[None, :, None, :, None])
    pool_bd = pool_bd.reshape(depth, GROUP_WIDTH, GROUP_WIDTH)
    sgu_bias = jnp.repeat(jnp.swapaxes(p["sgu_b"], 1, 2), GROUP_WIDTH // SGU_HEADS, axis=2)

    w1k = p["cmp_k_w1"].reshape(depth, CMP_BLOCK, HEAD_DIM, HEAD_DIM)
    w1v = p["cmp_v_w1"].reshape(depth, CMP_BLOCK, HEAD_DIM, HEAD_DIM)
    z = jnp.zeros((depth, CMP_STRIDE, HEAD_DIM, HEAD_DIM), F32)

    def kv_diag(a, b):
        return jnp.concatenate([jnp.concatenate([a, z], axis=3), jnp.concatenate([z, b], axis=3)], axis=2)

    cmp_wa = kv_diag(w1k[:, :CMP_STRIDE], w1v[:, :CMP_STRIDE]).astype(BF16)
    cmp_wb = kv_diag(w1k[:, CMP_STRIDE:], w1v[:, CMP_STRIDE:]).astype(BF16)
    ones = jnp.ones((1, 1, HEAD_DIM), F32)
    cmp_pe = jnp.concatenate([p["cmp_k_pe"].reshape(depth, -1, 1) * ones,
                              p["cmp_v_pe"].reshape(depth, -1, 1) * ones], axis=2)
    cmp_w1 = jnp.concatenate([p["cmp_k_w1"], p["cmp_v_w1"]], axis=2)
    zk = jnp.zeros((depth, HEAD_DIM, LANES), F32)
    cmp_w2k = jnp.concatenate([jnp.concatenate([p["cmp_k_w2"], p["cmp_k_w2"][..., perm]], axis=2), zk],
                              axis=1).astype(BF16)
    cmp_w2vt = jnp.concatenate([jnp.zeros((depth, HEAD_DIM, HEAD_DIM), F32),
                                jnp.swapaxes(p["cmp_v_w2"], 1, 2)], axis=2).astype(BF16)

    row = lambda v: v[:, None, :]
    return dict(
        wa=wa, wb=wb, pre_mix=row(p["pre_mix_norm"]), post_mix=row(p["post_mix_norm"]),
        pre_ffn=row(p["pre_ffn_norm"]), post_ffn=row(p["post_ffn_norm"]),
        conv_w=p["conv_w"], conv_b=row(p["conv_b"]), conv_lg=row(p["conv_ln_g"]), conv_lb=row(p["conv_ln_b"]),
        pool_bd=pool_bd.astype(BF16), pool_scale=row(p["pool_scale"]),
        sgu_lg=row(p["sgu_ln_g"]), sgu_lb=row(p["sgu_ln_b"]), sgu_w=p["sgu_w"], sgu_bias=sgu_bias,
        cmp_wa=cmp_wa, cmp_wb=cmp_wb, cmp_pe=cmp_pe, cmp_w1=cmp_w1, cmp_w2k=cmp_w2k, cmp_w2vt=cmp_w2vt,
        w_out=p["w_out"].astype(BF16), ffn_w1=p["ffn_w1"].astype(BF16), ffn_w2=p["ffn_w2"].astype(BF16),
    )


def _overlap_t(s):
    n_blk = s // CMP_STRIDE
    n_slc = s // SLC_BLOCK
    bs = jnp.arange(n_blk)[None, :] * CMP_STRIDE
    ss = jnp.arange(n_slc)[:, None] * SLC_BLOCK
    ov = jnp.clip(jnp.minimum(bs + CMP_BLOCK, ss + SLC_BLOCK) - jnp.maximum(bs, ss), 0)
    return (ov.astype(F32) / CMP_STRIDE).astype(BF16)


def _rope_tables(positions):
    bsz, s = positions.shape
    inv = (ROPE_THETA ** (-jnp.arange(ROPE_HALF, dtype=F32) * 2.0 / ROPE_DIM)).reshape(ROPE_HALF, 1)
    posf = positions.astype(F32)
    n_blk = s // CMP_STRIDE
    pos_end = posf[:, CMP_BLOCK - 1::CMP_STRIDE]
    pos_end = jnp.pad(pos_end, ((0, 0), (0, n_blk - pos_end.shape[1])))
    cos_t, sin_t = _rope_chan_table(posf[:, None, :], inv)
    cos_b, sin_b = _rope_chan_table(pos_end[:, None, :], inv)
    return cos_t, sin_t, cos_b, sin_b


def _layer(l, x, lw, tables, ov_t):
    bsz, s, d = x.shape
    za, zb = _proj(l, x, lw["pre_mix"], lw["wa"], lw["wb"])
    prep = _att_prep(l, za, zb, *tables, lw["cmp_wa"], lw["cmp_wb"],
                     lw["cmp_pe"], lw["cmp_w1"], lw["cmp_w2k"], lw["cmp_w2vt"])
    y_att = _att(*prep, ov_t)
    x2 = _mix_ffn(l, x.reshape(bsz * s, d), y_att.reshape(bsz * s, GROUP_WIDTH),
                  za.reshape(bsz * s, ZA_WIDTH), s, lw)
    return x2.reshape(bsz, s, d)


def kernel(x, positions, pre_mix_norm, post_mix_norm, pre_ffn_norm, post_ffn_norm, w_in, conv_w, conv_b, conv_ln_g, conv_ln_b, pool_w, pool_scale, sgu_ln_g, sgu_ln_b, sgu_w, sgu_b, cmp_k_pe, cmp_k_w1, cmp_k_w2, cmp_v_pe, cmp_v_w1, cmp_v_w2, w_out, ffn_w1, ffn_w2):
    params = dict(pre_mix_norm=pre_mix_norm, post_mix_norm=post_mix_norm, pre_ffn_norm=pre_ffn_norm,
                  post_ffn_norm=post_ffn_norm, w_in=w_in, conv_w=conv_w, conv_b=conv_b,
                  conv_ln_g=conv_ln_g, conv_ln_b=conv_ln_b, pool_w=pool_w, pool_scale=pool_scale,
                  sgu_ln_g=sgu_ln_g, sgu_ln_b=sgu_ln_b, sgu_w=sgu_w, sgu_b=sgu_b,
                  cmp_k_pe=cmp_k_pe, cmp_k_w1=cmp_k_w1, cmp_k_w2=cmp_k_w2, cmp_v_pe=cmp_v_pe,
                  cmp_v_w1=cmp_v_w1, cmp_v_w2=cmp_v_w2, w_out=w_out, ffn_w1=ffn_w1, ffn_w2=ffn_w2)
    bsz, s, d = x.shape
    depth = w_in.shape[0]
    tables = _rope_tables(positions)
    ov_t = _overlap_t(s)
    lw = _prep_weights(params)
    for l in range(depth):
        x = _layer(l, x, lw, tables, ov_t)
    return x
```

```python
import functools

import jax
import jax.numpy as jnp
from jax import lax
from jax.experimental import pallas as pl
from jax.experimental.pallas import tpu as pltpu

F32 = jnp.float32
BF16 = jnp.bfloat16

D_MODEL = 1024
GROUP_WIDTH = 256
CONV_WIDTH = 31
POOL_WINDOWS = (2, 4, 8, 16)
POOL_GROUP = 64
SGU_HEADS = 4
SGU_CHUNK = 128
ATT_HEADS = 4
HEAD_DIM = 64
ROPE_DIM = 16
ROPE_HALF = 8
ROPE_THETA = 500000.0
CMP_BLOCK = 32
CMP_STRIDE = 16
SLC_BLOCK = 64
N_SELECT = 8
N_LOCAL = 2
WINDOW = 512
N_BRANCH = 3
NORM_EPS = 1e-6
NEG_INF = -1e30
FORCE_SCORE = 1e9

LANES = 128
SUBLANES = 8
VMEM_LIMIT_BYTES = 56 * 1024 * 1024

PROJ_ROWS = 512
FFN_ROWS = 512
FFN_COLS = 512
CONV_ROWS = 64
POOL_ROWS = 64
ATT_Q = 256
ATT_K = 256
ATT_STREAMS = 4
LOG2_E = 1.4426950408889634
V_ROWS = 80
MASK_BIAS = -2e30

COL_CVAL, COL_CGATE, COL_POOL, COL_SGU_U, COL_SGU_V = 0, 1, 2, 3, 4
COL_KCVC, COL_KS, COL_KW = 10, 11, 12
ZA_WIDTH = 1664
ROW_Q, ROW_VS, ROW_VW, ROW_G = 0, 256, 320, 384
ZB_ROWS = 400


def _cparams(*sem):
    return pltpu.CompilerParams(dimension_semantics=sem, vmem_limit_bytes=VMEM_LIMIT_BYTES)


def _per_layer(l, *shape, pipeline_mode=None):
    zeros = (0,) * len(shape)
    extra = {} if pipeline_mode is None else {"pipeline_mode": pipeline_mode}
    return pl.BlockSpec((None,) + shape, lambda *_: (l,) + zeros, **extra)


def _gelu(x):
    return 0.5 * x * (1.0 + jnp.tanh(0.7978845608028654 * (x + 0.044715 * (x * x * x))))


def _sigmoid(x):
    return 0.5 * jnp.tanh(0.5 * x) + 0.5


def _layer_norm(x, g, b):
    mu = jnp.mean(x, axis=-1, keepdims=True)
    d = x - mu
    var = jnp.mean(d * d, axis=-1, keepdims=True)
    return d * lax.rsqrt(var + NORM_EPS) * g + b


def _rms_norm(x, g):
    return x * lax.rsqrt(jnp.mean(x * x, axis=-1, keepdims=True) + NORM_EPS) * g


def _rope_chan_kernel(pos_ref, inv_ref, cos_ref, sin_ref):
    ang = inv_ref[...] * pos_ref[0]
    cos_ref[0] = jnp.cos(ang)
    sin_ref[0] = jnp.sin(ang)


def _rope_chan_table(pos_row, inv_col):
    b, _, s = pos_row.shape
    spec = pl.BlockSpec((1, ROPE_HALF, s), lambda i: (i, 0, 0))
    return pl.pallas_call(
        _rope_chan_kernel,
        grid=(b,),
        in_specs=[pl.BlockSpec((1, 1, s), lambda i: (i, 0, 0)),
                  pl.BlockSpec((ROPE_HALF, 1), lambda i: (0, 0))],
        out_specs=[spec, spec],
        out_shape=[jax.ShapeDtypeStruct((b, ROPE_HALF, s), F32)] * 2,
        compiler_params=_cparams("parallel"),
        name="rope_chan_table",
    )(pos_row, inv_col)


def _proj_kernel(x_ref, g_ref, wa_ref, wb_ref, za_ref, zb_ref):
    h = _rms_norm(x_ref[0], g_ref[...]).astype(BF16)
    last_dims = (((1,), (1,)), ((), ()))
    za_ref[0] = lax.dot_general(h, wa_ref[...], last_dims, preferred_element_type=F32)
    zb_ref[0] = lax.dot_general(wb_ref[...], h, last_dims, preferred_element_type=F32)


def _proj(l, x, g, wa, wb):
    b, s, d = x.shape
    tm = min(PROJ_ROWS, s)
    return pl.pallas_call(
        _proj_kernel,
        grid=(b, s // tm),
        in_specs=[pl.BlockSpec((1, tm, d), lambda i, j: (i, j, 0)),
                  _per_layer(l, 1, d), _per_layer(l, ZA_WIDTH, d), _per_layer(l, ZB_ROWS, d)],
        out_specs=[pl.BlockSpec((1, tm, ZA_WIDTH), lambda i, j: (i, j, 0)),
                   pl.BlockSpec((1, ZB_ROWS, tm), lambda i, j: (i, 0, j))],
        out_shape=[jax.ShapeDtypeStruct((b, s, ZA_WIDTH), F32),
                   jax.ShapeDtypeStruct((b, ZB_ROWS, s), F32)],
        compiler_params=_cparams("parallel", "parallel"),
        name="in_proj",
    )(x, g, wa, wb)


MIX_HALO = 32


def _conv_rows(hp_ref, r0, w_ref, b_ref, lg_ref, lb_ref):
    shift = MIX_HALO - (CONV_WIDTH - 1)
    n_win = CONV_ROWS + MIX_HALO
    acc = jnp.zeros((CONV_ROWS, GROUP_WIDTH), F32) + b_ref[...]
    win = hp_ref[r0:r0 + n_win, :]
    for r in range(SUBLANES):
        taps = [k for k in range(CONV_WIDTH) if (k + shift) % SUBLANES == r]
        wr = win if r == 0 else pltpu.roll(win, n_win - r, axis=0)
        for k in taps:
            off = k + shift - r
            acc = acc + wr[off:off + CONV_ROWS, :] * w_ref[k:k + 1, :]
    y = _layer_norm(acc, lg_ref[...], lb_ref[...])
    return y * _sigmoid(y)


POOL_PAD = 16


def _pool_rows(pp_ref, r0, seq_t0, w_ref, sc_ref):
    lane = lax.broadcasted_iota(jnp.int32, (1, GROUP_WIDTH), 1)
    grp = lane // POOL_GROUP
    win = jnp.where(grp == 0, POOL_WINDOWS[0],
                    jnp.where(grp == 1, POOL_WINDOWS[1],
                              jnp.where(grp == 2, POOL_WINDOWS[2], POOL_WINDOWS[3])))
    base = r0 + MIX_HALO - POOL_PAD
    rows = pp_ref[base:base + POOL_ROWS + POOL_PAD, :]
    p = rows[POOL_PAD:POOL_PAD + POOL_ROWS, :]
    acc = rows
    sums = []
    width = 1
    for w in POOL_WINDOWS:
        while width < w:
            acc = acc + pltpu.roll(acc, width, axis=0)
            width *= 2
        sums.append(acc[POOL_PAD:POOL_PAD + POOL_ROWS, :])
    total = jnp.where(grp == 0, sums[0],
                      jnp.where(grp == 1, sums[1],
                                jnp.where(grp == 2, sums[2], sums[3])))
    t = seq_t0 + r0 + lax.broadcasted_iota(jnp.int32, (POOL_ROWS, 1), 0)
    count = jnp.minimum(t + 1, win).astype(F32)
    mixed = total / count - p
    y = jnp.dot(mixed.astype(BF16), w_ref[...], preferred_element_type=F32)
    return y * sc_ref[...]


def _sgu_weights(w_ref):
    row = lax.broadcasted_iota(jnp.int32, (SGU_CHUNK, SGU_CHUNK), 0)
    col = lax.broadcasted_iota(jnp.int32, (SGU_CHUNK, SGU_CHUNK), 1)
    return [jnp.where(row >= col, w_ref[h], 0.0).astype(BF16) for h in range(SGU_HEADS)]


def _sgu_rows(u, v, ws, lg_ref, lb_ref, bias_ref):
    head = lax.broadcasted_iota(jnp.int32, (1, GROUP_WIDTH), 1) // (GROUP_WIDTH // SGU_HEADS)
    vb = _layer_norm(_gelu(v), lg_ref[...], lb_ref[...]).astype(BF16)
    mixed = jnp.zeros((SGU_CHUNK, GROUP_WIDTH), F32)
    for h in range(SGU_HEADS):
        mixed = jnp.where(head == h, jnp.dot(ws[h], vb, preferred_element_type=F32), mixed)
    return _gelu(u) * (mixed + bias_ref[...])


def _att_prep_kernel(kcvc_ref, ksx_ref, kwx_ref, zb_ref, cos_ref, sin_ref, cosb_ref, sinb_ref,
                     wa_ref, wb_ref, pe_ref, w1_ref, w2k_ref, w2vt_ref,
                     q_ref, ks_ref, kw_ref, vs_ref, vw_ref, kc_ref, vct_ref, g_ref, sh_ref):
    s = ksx_ref.shape[1]
    n_blk = s // CMP_STRIDE

    def tok_table(cos, sin):
        n = cos.shape[1]
        rest = HEAD_DIM - ROPE_DIM
        rows = [cos, cos, jnp.ones((rest, n), F32), -sin, sin, jnp.zeros((rest, n), F32)]
        return jnp.concatenate(rows, axis=0).T

    def rope_wide(x, cs):
        r = x * cs
        return r + pltpu.roll(r, HEAD_DIM, axis=1)

    def rope_tok(x, cs):
        return rope_wide(x, cs)[:, :HEAD_DIM]

    cos = cos_ref[0]
    sin = sin_ref[0]
    cs = tok_table(cos, sin)
    lane = lax.broadcasted_iota(jnp.int32, (s, LANES), 1)
    blk = lax.broadcasted_iota(jnp.int32, (s, LANES), 0) // SLC_BLOCK
    ks_ref[0] = jnp.where(lane < HEAD_DIM, rope_wide(ksx_ref[0], cs),
                          jnp.where(lane - HEAD_DIM == blk, 1.0, 0.0)).astype(BF16)
    kw_ref[0] = rope_tok(kwx_ref[0], cs).astype(BF16)

    scale = HEAD_DIM ** -0.5 * LOG2_E
    parts = []
    for h in range(ATT_HEADS):
        r0 = ROW_Q + h * HEAD_DIM
        x1 = zb_ref[0, r0:r0 + ROPE_HALF, :]
        x2 = zb_ref[0, r0 + ROPE_HALF:r0 + ROPE_DIM, :]
        parts += [x1 * cos - x2 * sin, x2 * cos + x1 * sin, zb_ref[0, r0 + ROPE_DIM:r0 + HEAD_DIM, :]]
    q_ref[0] = (jnp.concatenate(parts, axis=0) * scale).astype(BF16)

    ones_row = jnp.where(lax.broadcasted_iota(jnp.int32, (V_ROWS - HEAD_DIM, ATT_K), 0) == 0, 1.0, 0.0)
    for c in range(s // ATT_K):
        cols = slice(c * ATT_K, (c + 1) * ATT_K)
        vs_ref[0, c] = jnp.concatenate([zb_ref[0, ROW_VS:ROW_VS + HEAD_DIM, cols], ones_row], axis=0).astype(BF16)
        vw_ref[0, c] = jnp.concatenate([zb_ref[0, ROW_VW:ROW_VW + HEAD_DIM, cols], ones_row], axis=0).astype(BF16)
    g_ref[0] = _sigmoid(zb_ref[0, ROW_G:ROW_G + 16, :])

    acc_a = jnp.zeros((n_blk, LANES), F32)
    acc_b = jnp.zeros((n_blk, LANES), F32)
    for l in range(CMP_STRIDE):
        xl = kcvc_ref[0, pl.ds(l, n_blk, stride=CMP_STRIDE), :].astype(BF16)
        acc_a = acc_a + jnp.dot(xl, wa_ref[l], preferred_element_type=F32)
        acc_b = acc_b + jnp.dot(xl, wb_ref[l], preferred_element_type=F32)
    sh_ref[0:n_blk, :] = acc_b
    sh_ref[n_blk:n_blk + SUBLANES, :] = jnp.zeros((SUBLANES, LANES), F32)
    pe_term = jnp.sum(pe_ref[...] * w1_ref[...], axis=0, keepdims=True)
    hid = _gelu(acc_a + sh_ref[1:n_blk + 1, :] + pe_term).astype(BF16)
    kk = jnp.dot(hid, w2k_ref[...], preferred_element_type=F32)
    kc_ref[0] = rope_tok(kk, tok_table(cosb_ref[0], sinb_ref[0])).astype(BF16)
    vct_ref[0] = lax.dot_general(w2vt_ref[...], hid, (((1,), (1,)), ((), ())),
                                 preferred_element_type=F32).astype(BF16)


def _att_prep(l, za, zb, cos_t, sin_t, cos_b, sin_b, wa, wb, pe, w1, w2k, w2vt):
    bsz, s, _ = za.shape
    n_blk = s // CMP_STRIDE
    nt = s // ATT_K
    tok = lambda c: pl.BlockSpec((1, s, LANES), lambda i: (i, 0, c))
    per_b3 = lambda shp: pl.BlockSpec((1,) + shp, lambda i: (i, 0, 0))
    per_b4 = lambda shp: pl.BlockSpec((1,) + shp, lambda i: (i, 0, 0, 0))
    return pl.pallas_call(
        _att_prep_kernel,
        grid=(bsz,),
        in_specs=[tok(COL_KCVC), tok(COL_KS), tok(COL_KW),
                  per_b3((ZB_ROWS, s)),
                  per_b3((ROPE_HALF, s)), per_b3((ROPE_HALF, s)),
                  per_b3((ROPE_HALF, n_blk)), per_b3((ROPE_HALF, n_blk)),
                  _per_layer(l, CMP_STRIDE, LANES, LANES), _per_layer(l, CMP_STRIDE, LANES, LANES),
                  _per_layer(l, CMP_BLOCK * HEAD_DIM, LANES), _per_layer(l, CMP_BLOCK * HEAD_DIM, LANES),
                  _per_layer(l, LANES, LANES), _per_layer(l, HEAD_DIM, LANES)],
        out_specs=[per_b3((ATT_HEADS * HEAD_DIM, s)), per_b3((s, LANES)), per_b3((s, HEAD_DIM)),
                   per_b4((nt, V_ROWS, ATT_K)), per_b4((nt, V_ROWS, ATT_K)),
                   per_b3((n_blk, HEAD_DIM)), per_b3((HEAD_DIM, n_blk)), per_b3((16, s))],
        out_shape=[jax.ShapeDtypeStruct((bsz, ATT_HEADS * HEAD_DIM, s), BF16),
                   jax.ShapeDtypeStruct((bsz, s, LANES), BF16),
                   jax.ShapeDtypeStruct((bsz, s, HEAD_DIM), BF16),
                   jax.ShapeDtypeStruct((bsz, nt, V_ROWS, ATT_K), BF16),
                   jax.ShapeDtypeStruct((bsz, nt, V_ROWS, ATT_K), BF16),
                   jax.ShapeDtypeStruct((bsz, n_blk, HEAD_DIM), BF16),
                   jax.ShapeDtypeStruct((bsz, HEAD_DIM, n_blk), BF16),
                   jax.ShapeDtypeStruct((bsz, 16, s), F32)],
        scratch_shapes=[pltpu.VMEM((n_blk + SUBLANES, LANES), F32)],
        compiler_params=_cparams("parallel"),
        name="att_prep",
    )(za, za, za, zb, cos_t, sin_t, cos_b, sin_b, wa, wb, pe, w1, w2k, w2vt)


def _att_tile(i, q_ref, g_ref, ks_ref, kw_ref, vs_ref, vw_ref, kc_ref, vct_ref, ov_ref,
              qa_ref, sa_ref, sb_ref, m_ref, acc_ref):
    n_blk = kc_ref.shape[1]
    n_slc = ov_ref.shape[0]
    tq = ATT_Q
    wide = ATT_HEADS * tq
    t0 = i * tq

    q = q_ref[0]
    qs = jnp.concatenate([q[h * HEAD_DIM:(h + 1) * HEAD_DIM] for h in range(ATT_HEADS)], axis=1)
    lane_w = lax.broadcasted_iota(jnp.int32, (1, wide), 1)
    t_w = t0 + (lane_w & (tq - 1))
    t_q = t0 + lax.broadcasted_iota(jnp.int32, (1, tq), 1)
    heads = lambda a: jnp.concatenate([a] * ATT_HEADS, axis=1)

    s_c = jnp.dot(kc_ref[0], qs, preferred_element_type=F32)
    blk_end = lax.broadcasted_iota(jnp.int32, (n_blk, 1), 0) * CMP_STRIDE + (CMP_BLOCK - 1)
    cmask = blk_end <= t_w
    s_m = jnp.where(cmask, s_c, NEG_INF)
    e = jnp.exp2(s_m - jnp.max(s_m, axis=0, keepdims=True))
    p_c = jnp.where(cmask, e * (1.0 / jnp.sum(e, axis=0, keepdims=True)), 0.0)
    o_cmp = jnp.dot(vct_ref[0], p_c.astype(BF16), preferred_element_type=F32)

    p_sum = p_c[:, 0:tq]
    for h in range(1, ATT_HEADS):
        p_sum = p_sum + p_c[:, h * tq:(h + 1) * tq]
    p_hi = p_sum.astype(BF16)
    p_lo = (p_sum - p_hi.astype(F32)).astype(BF16)
    imp = (jnp.dot(ov_ref[...], p_hi, preferred_element_type=F32)
           + jnp.dot(ov_ref[...], p_lo, preferred_element_type=F32))
    j = lax.broadcasted_iota(jnp.int32, (n_slc, 1), 0)
    back = t_q // SLC_BLOCK - j
    forced = (j == 0) | ((back >= 0) & (back < N_LOCAL))
    imp = jnp.where(forced, FORCE_SCORE, jnp.where(back < 0, -1.0, imp))
    rank = jnp.zeros((n_slc, tq), F32)
    for r in range(n_slc):
        row = imp[r:r + 1, :]
        ahead = (row > imp) | ((row == imp) & (j > r))
        rank = rank + jnp.where(ahead, 1.0, 0.0)
    sel_bias = jnp.where((rank < min(N_SELECT, n_slc)) & (back >= 0), 0.0, MASK_BIAS)
    qa_ref[0:HEAD_DIM, :] = qs
    qa_ref[HEAD_DIM:HEAD_DIM + n_slc, :] = heads(sel_bias).astype(BF16)
    qa_ref[HEAD_DIM + n_slc:, :] = jnp.zeros((LANES - HEAD_DIM - n_slc, wide), BF16)

    k_iota = lax.broadcasted_iota(jnp.int32, (ATT_K, 1), 0)

    n_win = (WINDOW + tq) // ATT_K
    first = jnp.maximum(i - WINDOW // ATT_K, 0)
    kw0 = pl.multiple_of(first * ATT_K, ATT_K)
    s_w = jnp.dot(kw_ref[0, pl.ds(kw0, n_win * ATT_K), :], qs, preferred_element_type=F32)
    diff = t_q - (kw0 + lax.broadcasted_iota(jnp.int32, (n_win * ATT_K, 1), 0))
    s_w = s_w + heads(jnp.where((diff >= 0) & (diff < WINDOW), 0.0, -jnp.inf))
    p_w = jnp.exp2(s_w - jnp.max(s_w, axis=0, keepdims=True)).astype(BF16)
    acc_w = None
    for n in range(n_win):
        part = jnp.dot(vw_ref[0, first + n], p_w[n * ATT_K:(n + 1) * ATT_K], preferred_element_type=F32)
        acc_w = part if acc_w is None else acc_w + part

    m_ref[...] = jnp.full((1, wide), NEG_INF, F32)
    acc_ref[...] = jnp.zeros((V_ROWS, wide), F32)

    def scores(kt, causal=False, live=None):
        k0 = pl.multiple_of(kt * ATT_K, ATT_K)
        s_t = jnp.dot(ks_ref[0, pl.ds(k0, ATT_K), :], qa_ref[...], preferred_element_type=F32)
        if causal:
            visible = (k0 + k_iota) <= t_q
            if live is not None:
                visible = visible & live
            s_t = jnp.where(heads(visible), s_t, -jnp.inf)
        return s_t

    def consume(s_ref, kt):
        m_old = m_ref[...]
        m_new = jnp.maximum(m_old, jnp.max(s_ref[...], axis=0, keepdims=True))
        alpha = jnp.exp2(m_old - m_new)
        p = jnp.exp2(s_ref[...] - m_new).astype(BF16)
        m_ref[...] = m_new
        acc_ref[...] = alpha * acc_ref[...] + jnp.dot(vs_ref[0, kt], p, preferred_element_type=F32)

    single = 1 - i % 2
    sa_ref[...] = scores(0, causal=True, live=single == 1)
    consume(sa_ref, 0)
    sa_ref[...] = scores(single)

    def pair(n):
        a = single + 2 * n
        sb_ref[...] = scores(a + 1, causal=True)
        consume(sa_ref, a)
        sa_ref[...] = scores(jnp.minimum(a + 2, i))
        consume(sb_ref, a + 1)

    def finish():
        acc_s = acc_ref[...]
        g = g_ref[0]
        gate = lambda br: jnp.concatenate(
            [g[h * N_BRANCH + br:h * N_BRANCH + br + 1, :] for h in range(ATT_HEADS)], axis=1)
        l_s = acc_s[HEAD_DIM:HEAD_DIM + 1, :]
        l_w = acc_w[HEAD_DIM:HEAD_DIM + 1, :]
        o = (gate(0) * o_cmp + (gate(1) * (1.0 / l_s)) * acc_s[:HEAD_DIM]
             + (gate(2) * (1.0 / l_w)) * acc_w[:HEAD_DIM])
        return jnp.concatenate([o[:, h * tq:(h + 1) * tq] for h in range(ATT_HEADS)], axis=0).T

    return pair, (i + 1 - single) // 2, finish


def _att_kernel(*refs, stride):
    n = ATT_STREAMS
    q_refs, refs = refs[:n], refs[n:]
    shared, refs = refs[:6], refs[6:]
    g_refs, refs = refs[:n], refs[n:]
    ov_ref, o_ref, scratch = refs[0], refs[1], refs[2:]
    n_scr = len(scratch) // n
    j = pl.program_id(1)
    tiles = [_att_tile(j + t * stride, q_refs[t], g_refs[t], *shared, ov_ref,
                       *scratch[t * n_scr:(t + 1) * n_scr]) for t in range(n)]
    common = tiles[0][1]

    def all_tiles(p, carry):
        for pair, _, _ in tiles:
            pair(p)
        return carry

    lax.fori_loop(0, common, all_tiles, 0)
    for extra in range((n - 1) * stride // 2):
        for t, (pair, _, _) in enumerate(tiles):
            if extra < t * stride // 2:
                pair(common + extra)
    for t, (_, _, finish) in enumerate(tiles):
        o_ref[0, t] = finish()


def _att(q_t, ks, kw, vs_t, vw_t, kc, vc_t, g_t, ov_t):
    bsz, _, s = q_t.shape
    n_blk = kc.shape[1]
    nt = s // ATT_K
    n_slc = s // SLC_BLOCK
    wide = ATT_HEADS * ATT_Q
    n = ATT_STREAMS
    stride = s // ATT_Q // n
    assert ATT_Q == ATT_K, "the selected-branch tile pairing assumes one diagonal key tile per query tile"
    assert stride % 2 == 0, "the tiles of a step must need the same parity of key tiles"
    per_b3 = lambda shp: pl.BlockSpec((1,) + shp, lambda b, j: (b, 0, 0))
    per_b4 = lambda shp: pl.BlockSpec((1,) + shp, lambda b, j: (b, 0, 0, 0))
    q_spec = lambda off: pl.BlockSpec((1, ATT_HEADS * HEAD_DIM, ATT_Q), lambda b, j: (b, 0, j + off))
    g_spec = lambda off: pl.BlockSpec((1, 16, ATT_Q), lambda b, j: (b, 0, j + off))
    tile_scratch = [pltpu.VMEM((LANES, wide), BF16),
                    pltpu.VMEM((ATT_K, wide), F32), pltpu.VMEM((ATT_K, wide), F32),
                    pltpu.VMEM((1, wide), F32), pltpu.VMEM((V_ROWS, wide), F32)]
    offsets = [t * stride for t in range(n)]
    out = pl.pallas_call(
        functools.partial(_att_kernel, stride=stride),
        grid=(bsz, stride),
        in_specs=[q_spec(off) for off in offsets]
                 + [per_b3((s, LANES)), per_b3((s, HEAD_DIM)),
                    per_b4((nt, V_ROWS, ATT_K)), per_b4((nt, V_ROWS, ATT_K)),
                    per_b3((n_blk, HEAD_DIM)), per_b3((HEAD_DIM, n_blk))]
                 + [g_spec(off) for off in offsets]
                 + [pl.BlockSpec((n_slc, n_blk), lambda b, j: (0, 0))],
        out_specs=pl.BlockSpec((1, n, ATT_Q, GROUP_WIDTH), lambda b, j: (b, 0, j, 0)),
        out_shape=jax.ShapeDtypeStruct((bsz, n, s // n, GROUP_WIDTH), F32),
        scratch_shapes=tile_scratch * n,
        compiler_params=_cparams("parallel", "arbitrary"),
        name="sparse_attention",
    )(*([q_t] * n), ks, kw, vs_t, vw_t, kc, vc_t, *([g_t] * n), ov_t)
    return out.reshape(bsz, s, GROUP_WIDTH)


def _mix_ffn_kernel(x_ref, yatt_ref, ca_ref, cg_ref, pin_ref, su_ref, sv_ref, cah_ref, cgh_ref, ph_ref,
                    cw_ref, cb_ref, clg_ref, clb_ref, pw_ref, psc_ref, slg_ref, slb_ref, sw_ref, sbias_ref,
                    wo_ref, gm_ref, g1_ref, w1_ref, w2_ref, g2_ref,
                    o_ref, y_ref, hp_ref, pp_ref, acc_ref, *, tiles_per_seq):
    k = pl.program_id(0)
    last = pl.num_programs(0) - 1
    tm = x_ref.shape[0]

    seq_tile = k % tiles_per_seq
    has_prev = seq_tile > 0

    def conv_piece(r0):
        y_ref[0, r0:r0 + CONV_ROWS, :] = _conv_rows(hp_ref, r0, cw_ref, cb_ref, clg_ref, clb_ref)
        return 0, r0

    def pool_piece(r0):
        y_ref[1, r0:r0 + POOL_ROWS, :] = _pool_rows(pp_ref, r0, seq_tile * tm, pw_ref, psc_ref)
        return 1, r0

    def sgu_piece(r0):
        rows = slice(r0, r0 + SGU_CHUNK)
        y_ref[2, rows, :] = _sgu_rows(su_ref[rows, :], sv_ref[rows, :], _sgu_weights(sw_ref),
                                      slg_ref, slb_ref, sbias_ref)
        return 2, r0

    def order_before_next_accumulate(n, r0):
        bits = pltpu.bitcast(y_ref[n, r0:r0 + SUBLANES, 0:LANES], jnp.uint32)
        zero = pltpu.bitcast(lax.shift_right_logical(bits, jnp.uint32(32)), F32)
        acc_ref[0:SUBLANES, 0:LANES] = acc_ref[0:SUBLANES, 0:LANES] + zero

    pieces = []
    for r0 in range(0, tm, CONV_ROWS):
        pieces += [functools.partial(conv_piece, r0), functools.partial(pool_piece, r0)]
        if r0 % SGU_CHUNK == 0:
            pieces.append(functools.partial(sgu_piece, r0))
    n_chunks = w1_ref.shape[1] // FFN_COLS
    per_chunk = -(-len(pieces) // (n_chunks - 1))

    def stage_mixer_inputs():
        hp_ref[0:MIX_HALO, :] = jnp.where(has_prev, cah_ref[...] * _sigmoid(cgh_ref[...]), 0.0)
        hp_ref[MIX_HALO:, :] = ca_ref[...] * _sigmoid(cg_ref[...])
        pp_ref[0:MIX_HALO, :] = jnp.where(has_prev, ph_ref[...], 0.0)
        pp_ref[MIX_HALO:, :] = pin_ref[...]

    def mlp_tile(with_mixers):
        acc = jnp.dot(yatt_ref[...].astype(BF16), wo_ref[3 * GROUP_WIDTH:4 * GROUP_WIDTH, :],
                      preferred_element_type=F32)
        for n in range(3):
            acc = acc + jnp.dot(y_ref[n].astype(BF16), wo_ref[n * GROUP_WIDTH:(n + 1) * GROUP_WIDTH, :],
                                preferred_element_type=F32)
        x1 = x_ref[...] + _rms_norm(acc, gm_ref[...])
        h = _rms_norm(x1, g1_ref[...]).astype(BF16)
        if with_mixers:
            stage_mixer_inputs()
        for c in range(n_chunks):
            cols = slice(c * FFN_COLS, (c + 1) * FFN_COLS)
            f = jnp.maximum(jnp.dot(h, w1_ref[:, cols], preferred_element_type=F32), 0.0)
            part = jnp.dot((f * f).astype(BF16), w2_ref[cols, :], preferred_element_type=F32)
            acc_ref[...] = part if c == 0 else acc_ref[...] + part
            if with_mixers:
                for piece in pieces[c * per_chunk:(c + 1) * per_chunk]:
                    order_before_next_accumulate(*piece())
        o_ref[...] = x1 + _rms_norm(acc_ref[...], g2_ref[...])

    @pl.when(k == 0)
    def _():
        stage_mixer_inputs()
        for piece in pieces:
            piece()

    @pl.when((k > 0) & (k < last))
    def _():
        mlp_tile(with_mixers=True)

    @pl.when(k == last)
    def _():
        mlp_tile(with_mixers=False)


def _mix_ffn(l, x2, y_att, za2, seq_len, lw):
    t, d = x2.shape
    dff = lw["ffn_w1"].shape[2]
    tm = min(FFN_ROWS, seq_len)
    assert CONV_ROWS == POOL_ROWS and SGU_CHUNK % CONV_ROWS == 0 and tm % SGU_CHUNK == 0
    n_tiles = t // tm
    halo_per_tile = tm // MIX_HALO
    cur = lambda k: jnp.maximum(k - 1, 0)
    nxt = lambda k: jnp.minimum(k, n_tiles - 1)
    main = lambda col: pl.BlockSpec((tm, GROUP_WIDTH), lambda k: (nxt(k), col))
    halo = lambda col: pl.BlockSpec((MIX_HALO, GROUP_WIDTH),
                                    lambda k: (jnp.maximum(nxt(k) * halo_per_tile - 1, 0), col))
    vec = lambda n: _per_layer(l, 1, n)
    once = lambda *shp: _per_layer(l, *shp, pipeline_mode=pl.Buffered(1))
    return pl.pallas_call(
        functools.partial(_mix_ffn_kernel, tiles_per_seq=seq_len // tm),
        grid=(n_tiles + 1,),
        in_specs=[pl.BlockSpec((tm, d), lambda k: (cur(k), 0)),
                  pl.BlockSpec((tm, GROUP_WIDTH), lambda k: (cur(k), 0)),
                  main(COL_CVAL), main(COL_CGATE), main(COL_POOL), main(COL_SGU_U), main(COL_SGU_V),
                  halo(COL_CVAL), halo(COL_CGATE), halo(COL_POOL),
                  _per_layer(l, CONV_WIDTH, GROUP_WIDTH), vec(GROUP_WIDTH), vec(GROUP_WIDTH), vec(GROUP_WIDTH),
                  _per_layer(l, GROUP_WIDTH, GROUP_WIDTH), vec(GROUP_WIDTH),
                  vec(GROUP_WIDTH), vec(GROUP_WIDTH),
                  _per_layer(l, SGU_HEADS, SGU_CHUNK, SGU_CHUNK), _per_layer(l, SGU_CHUNK, GROUP_WIDTH),
                  once(d, d), vec(d), vec(d), once(d, dff), once(dff, d), vec(d)],
        out_specs=pl.BlockSpec((tm, d), lambda k: (cur(k), 0)),
        out_shape=jax.ShapeDtypeStruct((t, d), F32),
        scratch_shapes=[pltpu.VMEM((3, tm, GROUP_WIDTH), F32),
                        pltpu.VMEM((MIX_HALO + tm, GROUP_WIDTH), F32),
                        pltpu.VMEM((MIX_HALO + tm, GROUP_WIDTH), F32),
                        pltpu.VMEM((tm, d), F32)],
        compiler_params=_cparams("arbitrary"),
        name="mix_ffn",
    )(x2, y_att, za2, za2, za2, za2, za2, za2, za2, za2,
      lw["conv_w"], lw["conv_b"], lw["conv_lg"], lw["conv_lb"], lw["pool_bd"], lw["pool_scale"],
      lw["sgu_lg"], lw["sgu_lb"], lw["sgu_w"], lw["sgu_bias"],
      lw["w_out"], lw["post_mix"], lw["pre_ffn"], lw["ffn_w1"], lw["ffn_w2"], lw["post_ffn"])


def _rope_perm():
    idx = list(range(HEAD_DIM))
    for c in range(ROPE_HALF):
        idx[c], idx[c + ROPE_HALF] = c + ROPE_HALF, c
    return jnp.array(idx, jnp.int32)


def _split_w_in(w_in):
    gw = GROUP_WIDTH
    edges = [0, gw, 2 * gw, 3 * gw, 4 * gw, 5 * gw, 6 * gw]
    for _ in range(6):
        edges.append(edges[-1] + HEAD_DIM)
    edges.append(edges[-1] + ATT_HEADS * N_BRANCH)
    names = ("cval", "cgate", "pool", "su", "sv", "q", "kc", "vc", "ks", "vs", "kw", "vw", "g")
    return {n: w_in[:, edges[k]:edges[k + 1], :] for k, n in enumerate(names)}


def _prep_weights(p):
    depth = p["w_in"].shape[0]
    perm = _rope_perm()
    w = _split_w_in(jnp.swapaxes(p["w_in"], 1, 2).astype(BF16))
    wa = jnp.concatenate([w["cval"], w["cgate"], w["pool"], w["su"], w["sv"], w["kc"], w["vc"],
                          w["ks"], w["ks"][:, perm], w["kw"], w["kw"][:, perm]], axis=1)
    pad = jnp.zeros((depth, ZB_ROWS - ROW_G - ATT_HEADS * N_BRANCH, D_MODEL), BF16)
    wb = jnp.concatenate([w["q"], w["vs"], w["vw"], w["g"], pad], axis=1)

    n_grp = len(POOL_WINDOWS)
    pool_bd = (p["pool_w"][:, :, :, None, :] * jnp.eye(n_grp, dtype=F32)[None, :, None, :, None])
    pool_bd = pool_bd.reshape(depth, GROUP_WIDTH, GROUP_WIDTH)
    sgu_bias = jnp.repeat(jnp.swapaxes(p["sgu_b"], 1, 2), GROUP_WIDTH // SGU_HEADS, axis=2)

    w1k = p["cmp_k_w1"].reshape(depth, CMP_BLOCK, HEAD_DIM, HEAD_DIM)
    w1v = p["cmp_v_w1"].reshape(depth, CMP_BLOCK, HEAD_DIM, HEAD_DIM)
    z = jnp.zeros((depth, CMP_STRIDE, HEAD_DIM, HEAD_DIM), F32)

    def kv_diag(a, b):
        return jnp.concatenate([jnp.concatenate([a, z], axis=3), jnp.concatenate([z, b], axis=3)], axis=2)

    cmp_wa = kv_diag(w1k[:, :CMP_STRIDE], w1v[:, :CMP_STRIDE]).astype(BF16)
    cmp_wb = kv_diag(w1k[:, CMP_STRIDE:], w1v[:, CMP_STRIDE:]).astype(BF16)
    ones = jnp.ones((1, 1, HEAD_DIM), F32)
    cmp_pe = jnp.concatenate([p["cmp_k_pe"].reshape(depth, -1, 1) * ones,
                              p["cmp_v_pe"].reshape(depth, -1, 1) * ones], axis=2)
    cmp_w1 = jnp.concatenate([p["cmp_k_w1"], p["cmp_v_w1"]], axis=2)
    zk = jnp.zeros((depth, HEAD_DIM, LANES), F32)
    cmp_w2k = jnp.concatenate([jnp.concatenate([p["cmp_k_w2"], p["cmp_k_w2"][..., perm]], axis=2), zk],
                              axis=1).astype(BF16)
    cmp_w2vt = jnp.concatenate([jnp.zeros((depth, HEAD_DIM, HEAD_DIM), F32),
                                jnp.swapaxes(p["cmp_v_w2"], 1, 2)], axis=2).astype(BF16)

    row = lambda v: v[:, None, :]
    return dict(
        wa=wa, wb=wb, pre_mix=row(p["pre_mix_norm"]), post_mix=row(p["post_mix_norm"]),
        pre_ffn=row(p["pre_ffn_norm"]), post_ffn=row(p["post_ffn_norm"]),
        conv_w=p["conv_w"], conv_b=row(p["conv_b"]), conv_lg=row(p["conv_ln_g"]), conv_lb=row(p["conv_ln_b"]),
        pool_bd=pool_bd.astype(BF16), pool_scale=row(p["pool_scale"]),
        sgu_lg=row(p["sgu_ln_g"]), sgu_lb=row(p["sgu_ln_b"]), sgu_w=p["sgu_w"], sgu_bias=sgu_bias,
        cmp_wa=cmp_wa, cmp_wb=cmp_wb, cmp_pe=cmp_pe, cmp_w1=cmp_w1, cmp_w2k=cmp_w2k, cmp_w2vt=cmp_w2vt,
        w_out=p["w_out"].astype(BF16), ffn_w1=p["ffn_w1"].astype(BF16), ffn_w2=p["ffn_w2"].astype(BF16),
    )


def _overlap_t(s):
    n_blk = s // CMP_STRIDE
    n_slc = s // SLC_BLOCK
    bs = jnp.arange(n_blk)[None, :] * CMP_STRIDE
    ss = jnp.arange(n_slc)[:, None] * SLC_BLOCK
    ov = jnp.clip(jnp.minimum(bs + CMP_BLOCK, ss + SLC_BLOCK) - jnp.maximum(bs, ss), 0)
    return (ov.astype(F32) / CMP_STRIDE).astype(BF16)


def _rope_tables(positions):
    bsz, s = positions.shape
    inv = (ROPE_THETA ** (-jnp.arange(ROPE_HALF, dtype=F32) * 2.0 / ROPE_DIM)).reshape(ROPE_HALF, 1)
    posf = positions.astype(F32)
    n_blk = s // CMP_STRIDE
    pos_end = posf[:, CMP_BLOCK - 1::CMP_STRIDE]
    pos_end = jnp.pad(pos_end, ((0, 0), (0, n_blk - pos_end.shape[1])))
    cos_t, sin_t = _rope_chan_table(posf[:, None, :], inv)
    cos_b, sin_b = _rope_chan_table(pos_end[:, None, :], inv)
    return cos_t, sin_t, cos_b, sin_b


def _layer(l, x, lw, tables, ov_t):
    bsz, s, d = x.shape
    za, zb = _proj(l, x, lw["pre_mix"], lw["wa"], lw["wb"])
    prep = _att_prep(l, za, zb, *tables, lw["cmp_wa"], lw["cmp_wb"],
                     lw["cmp_pe"], lw["cmp_w1"], lw["cmp_w2k"], lw["cmp_w2vt"])
    y_att = _att(*prep, ov_t)
    x2 = _mix_ffn(l, x.reshape(bsz * s, d), y_att.reshape(bsz * s, GROUP_WIDTH),
                  za.reshape(bsz * s, ZA_WIDTH), s, lw)
    return x2.reshape(bsz, s, d)


def kernel(x, positions, pre_mix_norm, post_mix_norm, pre_ffn_norm, post_ffn_norm, w_in, conv_w, conv_b, conv_ln_g, conv_ln_b, pool_w, pool_scale, sgu_ln_g, sgu_ln_b, sgu_w, sgu_b, cmp_k_pe, cmp_k_w1, cmp_k_w2, cmp_v_pe, cmp_v_w1, cmp_v_w2, w_out, ffn_w1, ffn_w2):
    params = dict(pre_mix_norm=pre_mix_norm, post_mix_norm=post_mix_norm, pre_ffn_norm=pre_ffn_norm,
                  post_ffn_norm=post_ffn_norm, w_in=w_in, conv_w=conv_w, conv_b=conv_b,
                  conv_ln_g=conv_ln_g, conv_ln_b=conv_ln_b, pool_w=pool_w, pool_scale=pool_scale,
                  sgu_ln_g=sgu_ln_g, sgu_ln_b=sgu_ln_b, sgu_w=sgu_w, sgu_b=sgu_b,
                  cmp_k_pe=cmp_k_pe, cmp_k_w1=cmp_k_w1, cmp_k_w2=cmp_k_w2, cmp_v_pe=cmp_v_pe,
                  cmp_v_w1=cmp_v_w1, cmp_v_w2=cmp_v_w2, w_out=w_out, ffn_w1=ffn_w1, ffn_w2=ffn_w2)
    bsz, s, d = x.shape
    depth = w_in.shape[0]
    tables = _rope_tables(positions)
    ov_t = _overlap_t(s)
    lw = _prep_weights(params)
    for l in range(depth):
        x = _layer(l, x, lw, tables, ov_t)
    return x
```

```python
import functools

import jax
import jax.numpy as jnp
from jax import lax
from jax.experimental import pallas as pl
from jax.experimental.pallas import tpu as pltpu

F32 = jnp.float32
BF16 = jnp.bfloat16

D_MODEL = 1024
GROUP_WIDTH = 256
CONV_WIDTH = 31
POOL_WINDOWS = (2, 4, 8, 16)
POOL_GROUP = 64
SGU_HEADS = 4
SGU_CHUNK = 128
ATT_HEADS = 4
HEAD_DIM = 64
ROPE_DIM = 16
ROPE_HALF = 8
ROPE_THETA = 500000.0
CMP_BLOCK = 32
CMP_STRIDE = 16
SLC_BLOCK = 64
N_SELECT = 8
N_LOCAL = 2
WINDOW = 512
N_BRANCH = 3
NORM_EPS = 1e-6
NEG_INF = -1e30
FORCE_SCORE = 1e9

LANES = 128
SUBLANES = 8
VMEM_LIMIT_BYTES = 56 * 1024 * 1024

PROJ_ROWS = 512
FFN_ROWS = 512
FFN_COLS = 512
CONV_ROWS = 64
POOL_ROWS = 64
ATT_Q = 256
ATT_K = 256
ATT_STREAMS = 4
LOG2_E = 1.4426950408889634
V_ROWS = 80
MASK_BIAS = -2e30

COL_CVAL, COL_CGATE, COL_POOL, COL_SGU_U, COL_SGU_V = 0, 1, 2, 3, 4
COL_KCVC, COL_KS, COL_KW = 10, 11, 12
ZA_WIDTH = 1664
ROW_Q, ROW_VS, ROW_VW, ROW_G = 0, 256, 320, 384
ZB_ROWS = 400


def _cparams(*sem):
    return pltpu.CompilerParams(dimension_semantics=sem, vmem_limit_bytes=VMEM_LIMIT_BYTES)


def _per_layer(l, *shape, pipeline_mode=None):
    zeros = (0,) * len(shape)
    extra = {} if pipeline_mode is None else {"pipeline_mode": pipeline_mode}
    return pl.BlockSpec((None,) + shape, lambda *_: (l,) + zeros, **extra)


def _gelu(x):
    return 0.5 * x * (1.0 + jnp.tanh(0.7978845608028654 * (x + 0.044715 * (x * x * x))))


def _sigmoid(x):
    return 0.5 * jnp.tanh(0.5 * x) + 0.5


def _layer_norm(x, g, b):
    mu = jnp.mean(x, axis=-1, keepdims=True)
    d = x - mu
    var = jnp.mean(d * d, axis=-1, keepdims=True)
    return d * lax.rsqrt(var + NORM_EPS) * g + b


def _rms_norm(x, g):
    return x * lax.rsqrt(jnp.mean(x * x, axis=-1, keepdims=True) + NORM_EPS) * g


def _rope_chan_kernel(pos_ref, inv_ref, cos_ref, sin_ref):
    ang = inv_ref[...] * pos_ref[0]
    cos_ref[0] = jnp.cos(ang)
    sin_ref[0] = jnp.sin(ang)


def _rope_chan_table(pos_row, inv_col):
    b, _, s = pos_row.shape
    spec = pl.BlockSpec((1, ROPE_HALF, s), lambda i: (i, 0, 0))
    return pl.pallas_call(
        _rope_chan_kernel,
        grid=(b,),
        in_specs=[pl.BlockSpec((1, 1, s), lambda i: (i, 0, 0)),
                  pl.BlockSpec((ROPE_HALF, 1), lambda i: (0, 0))],
        out_specs=[spec, spec],
        out_shape=[jax.ShapeDtypeStruct((b, ROPE_HALF, s), F32)] * 2,
        compiler_params=_cparams("parallel"),
        name="rope_chan_table",
    )(pos_row, inv_col)


def _proj_kernel(x_ref, g_ref, wa_ref, wb_ref, za_ref, zb_ref):
    h = _rms_norm(x_ref[0], g_ref[...]).astype(BF16)
    last_dims = (((1,), (1,)), ((), ()))
    za_ref[0] = lax.dot_general(h, wa_ref[...], last_dims, preferred_element_type=F32)
    zb_ref[0] = lax.dot_general(wb_ref[...], h, last_dims, preferred_element_type=F32)


def _proj(l, x, g, wa, wb):
    b, s, d = x.shape
    tm = min(PROJ_ROWS, s)
    return pl.pallas_call(
        _proj_kernel,
        grid=(b, s // tm),
        in_specs=[pl.BlockSpec((1, tm, d), lambda i, j: (i, j, 0)),
                  _per_layer(l, 1, d), _per_layer(l, ZA_WIDTH, d), _per_layer(l, ZB_ROWS, d)],
        out_specs=[pl.BlockSpec((1, tm, ZA_WIDTH), lambda i, j: (i, j, 0)),
                   pl.BlockSpec((1, ZB_ROWS, tm), lambda i, j: (i, 0, j))],
        out_shape=[jax.ShapeDtypeStruct((b, s, ZA_WIDTH), F32),
                   jax.ShapeDtypeStruct((b, ZB_ROWS, s), F32)],
        compiler_params=_cparams("parallel", "parallel"),
        name="in_proj",
    )(x, g, wa, wb)


MIX_HALO = 32


def _conv_rows(hp_ref, r0, w_ref, b_ref, lg_ref, lb_ref):
    shift = MIX_HALO - (CONV_WIDTH - 1)
    n_win = CONV_ROWS + MIX_HALO
    acc = jnp.zeros((CONV_ROWS, GROUP_WIDTH), F32) + b_ref[...]
    win = hp_ref[r0:r0 + n_win, :]
    for r in range(SUBLANES):
        taps = [k for k in range(CONV_WIDTH) if (k + shift) % SUBLANES == r]
        wr = win if r == 0 else pltpu.roll(win, n_win - r, axis=0)
        for k in taps:
            off = k + shift - r
            acc = acc + wr[off:off + CONV_ROWS, :] * w_ref[k:k + 1, :]
    y = _layer_norm(acc, lg_ref[...], lb_ref[...])
    return y * _sigmoid(y)


POOL_PAD = 16


def _pool_rows(pp_ref, r0, seq_t0, w_ref, sc_ref):
    lane = lax.broadcasted_iota(jnp.int32, (1, GROUP_WIDTH), 1)
    grp = lane // POOL_GROUP
    win = jnp.where(grp == 0, POOL_WINDOWS[0],
                    jnp.where(grp == 1, POOL_WINDOWS[1],
                              jnp.where(grp == 2, POOL_WINDOWS[2], POOL_WINDOWS[3])))
    base = r0 + MIX_HALO - POOL_PAD
    rows = pp_ref[base:base + POOL_ROWS + POOL_PAD, :]
    p = rows[POOL_PAD:POOL_PAD + POOL_ROWS, :]
    acc = rows
    sums = []
    width = 1
    for w in POOL_WINDOWS:
        while width < w:
            acc = acc + pltpu.roll(acc, width, axis=0)
            width *= 2
        sums.append(acc[POOL_PAD:POOL_PAD + POOL_ROWS, :])
    total = jnp.where(grp == 0, sums[0],
                      jnp.where(grp == 1, sums[1],
                                jnp.where(grp == 2, sums[2], sums[3])))
    t = seq_t0 + r0 + lax.broadcasted_iota(jnp.int32, (POOL_ROWS, 1), 0)
    count = jnp.minimum(t + 1, win).astype(F32)
    mixed = total / count - p
    y = jnp.dot(mixed.astype(BF16), w_ref[...], preferred_element_type=F32)
    return y * sc_ref[...]


def _sgu_weights(w_ref):
    row = lax.broadcasted_iota(jnp.int32, (SGU_CHUNK, SGU_CHUNK), 0)
    col = lax.broadcasted_iota(jnp.int32, (SGU_CHUNK, SGU_CHUNK), 1)
    return [jnp.where(row >= col, w_ref[h], 0.0).astype(BF16) for h in range(SGU_HEADS)]


def _sgu_rows(u, v, ws, lg_ref, lb_ref, bias_ref):
    head = lax.broadcasted_iota(jnp.int32, (1, GROUP_WIDTH), 1) // (GROUP_WIDTH // SGU_HEADS)
    vb = _layer_norm(_gelu(v), lg_ref[...], lb_ref[...]).astype(BF16)
    mixed = jnp.zeros((SGU_CHUNK, GROUP_WIDTH), F32)
    for h in range(SGU_HEADS):
        mixed = jnp.where(head == h, jnp.dot(ws[h], vb, preferred_element_type=F32), mixed)
    return _gelu(u) * (mixed + bias_ref[...])


def _att_prep_kernel(kcvc_ref, ksx_ref, kwx_ref, zb_ref, cos_ref, sin_ref, cosb_ref, sinb_ref,
                     wa_ref, wb_ref, pe_ref, w1_ref, w2k_ref, w2vt_ref,
                     q_ref, ks_ref, kw_ref, vs_ref, vw_ref, kc_ref, vct_ref, g_ref, sh_ref):
    s = ksx_ref.shape[1]
    n_blk = s // CMP_STRIDE

    def tok_table(cos, sin):
        n = cos.shape[1]
        rest = HEAD_DIM - ROPE_DIM
        rows = [cos, cos, jnp.ones((rest, n), F32), -sin, sin, jnp.zeros((rest, n), F32)]
        return jnp.concatenate(rows, axis=0).T

    def rope_wide(x, cs):
        r = x * cs
        return r + pltpu.roll(r, HEAD_DIM, axis=1)

    def rope_tok(x, cs):
        return rope_wide(x, cs)[:, :HEAD_DIM]

    cos = cos_ref[0]
    sin = sin_ref[0]
    cs = tok_table(cos, sin)
    lane = lax.broadcasted_iota(jnp.int32, (s, LANES), 1)
    blk = lax.broadcasted_iota(jnp.int32, (s, LANES), 0) // SLC_BLOCK
    ks_ref[0] = jnp.where(lane < HEAD_DIM, rope_wide(ksx_ref[0], cs),
                          jnp.where(lane - HEAD_DIM == blk, 1.0, 0.0)).astype(BF16)
    kw_ref[0] = rope_tok(kwx_ref[0], cs).astype(BF16)

    scale = HEAD_DIM ** -0.5 * LOG2_E
    parts = []
    for h in range(ATT_HEADS):
        r0 = ROW_Q + h * HEAD_DIM
        x1 = zb_ref[0, r0:r0 + ROPE_HALF, :]
        x2 = zb_ref[0, r0 + ROPE_HALF:r0 + ROPE_DIM, :]
        parts += [x1 * cos - x2 * sin, x2 * cos + x1 * sin, zb_ref[0, r0 + ROPE_DIM:r0 + HEAD_DIM, :]]
    q_ref[0] = (jnp.concatenate(parts, axis=0) * scale).astype(BF16)

    ones_row = jnp.where(lax.broadcasted_iota(jnp.int32, (V_ROWS - HEAD_DIM, ATT_K), 0) == 0, 1.0, 0.0)
    for c in range(s // ATT_K):
        cols = slice(c * ATT_K, (c + 1) * ATT_K)
        vs_ref[0, c] = jnp.concatenate([zb_ref[0, ROW_VS:ROW_VS + HEAD_DIM, cols], ones_row], axis=0).astype(BF16)
        vw_ref[0, c] = jnp.concatenate([zb_ref[0, ROW_VW:ROW_VW + HEAD_DIM, cols], ones_row], axis=0).astype(BF16)
    g_ref[0] = _sigmoid(zb_ref[0, ROW_G:ROW_G + 16, :])

    acc_a = jnp.zeros((n_blk, LANES), F32)
    acc_b = jnp.zeros((n_blk, LANES), F32)
    for l in range(CMP_STRIDE):
        xl = kcvc_ref[0, pl.ds(l, n_blk, stride=CMP_STRIDE), :].astype(BF16)
        acc_a = acc_a + jnp.dot(xl, wa_ref[l], preferred_element_type=F32)
        acc_b = acc_b + jnp.dot(xl, wb_ref[l], preferred_element_type=F32)
    sh_ref[0:n_blk, :] = acc_b
    sh_ref[n_blk:n_blk + SUBLANES, :] = jnp.zeros((SUBLANES, LANES), F32)
    pe_term = jnp.sum(pe_ref[...] * w1_ref[...], axis=0, keepdims=True)
    hid = _gelu(acc_a + sh_ref[1:n_blk + 1, :] + pe_term).astype(BF16)
    kk = jnp.dot(hid, w2k_ref[...], preferred_element_type=F32)
    kc_ref[0] = rope_tok(kk, tok_table(cosb_ref[0], sinb_ref[0])).astype(BF16)
    vct_ref[0] = lax.dot_general(w2vt_ref[...], hid, (((1,), (1,)), ((), ())),
                                 preferred_element_type=F32).astype(BF16)


def _att_prep(l, za, zb, cos_t, sin_t, cos_b, sin_b, wa, wb, pe, w1, w2k, w2vt):
    bsz, s, _ = za.shape
    n_blk = s // CMP_STRIDE
    nt = s // ATT_K
    tok = lambda c: pl.BlockSpec((1, s, LANES), lambda i: (i, 0, c))
    per_b3 = lambda shp: pl.BlockSpec((1,) + shp, lambda i: (i, 0, 0))
    per_b4 = lambda shp: pl.BlockSpec((1,) + shp, lambda i: (i, 0, 0, 0))
    return pl.pallas_call(
        _att_prep_kernel,
        grid=(bsz,),
        in_specs=[tok(COL_KCVC), tok(COL_KS), tok(COL_KW),
                  per_b3((ZB_ROWS, s)),
                  per_b3((ROPE_HALF, s)), per_b3((ROPE_HALF, s)),
                  per_b3((ROPE_HALF, n_blk)), per_b3((ROPE_HALF, n_blk)),
                  _per_layer(l, CMP_STRIDE, LANES, LANES), _per_layer(l, CMP_STRIDE, LANES, LANES),
                  _per_layer(l, CMP_BLOCK * HEAD_DIM, LANES), _per_layer(l, CMP_BLOCK * HEAD_DIM, LANES),
                  _per_layer(l, LANES, LANES), _per_layer(l, HEAD_DIM, LANES)],
        out_specs=[per_b3((ATT_HEADS * HEAD_DIM, s)), per_b3((s, LANES)), per_b3((s, HEAD_DIM)),
                   per_b4((nt, V_ROWS, ATT_K)), per_b4((nt, V_ROWS, ATT_K)),
                   per_b3((n_blk, HEAD_DIM)), per_b3((HEAD_DIM, n_blk)), per_b3((16, s))],
        out_shape=[jax.ShapeDtypeStruct((bsz, ATT_HEADS * HEAD_DIM, s), BF16),
                   jax.ShapeDtypeStruct((bsz, s, LANES), BF16),
                   jax.ShapeDtypeStruct((bsz, s, HEAD_DIM), BF16),
                   jax.ShapeDtypeStruct((bsz, nt, V_ROWS, ATT_K), BF16),
                   jax.ShapeDtypeStruct((bsz, nt, V_ROWS, ATT_K), BF16),
                   jax.ShapeDtypeStruct((bsz, n_blk, HEAD_DIM), BF16),
                   jax.ShapeDtypeStruct((bsz, HEAD_DIM, n_blk), BF16),
                   jax.ShapeDtypeStruct((bsz, 16, s), F32)],
        scratch_shapes=[pltpu.VMEM((n_blk + SUBLANES, LANES), F32)],
        compiler_params=_cparams("parallel"),
        name="att_prep",
    )(za, za, za, zb, cos_t, sin_t, cos_b, sin_b, wa, wb, pe, w1, w2k, w2vt)


def _att_tile(i, q_ref, g_ref, wmask_ref, ks_ref, kw_ref, vs_ref, vw_ref, kc_ref, vct_ref, ov_ref,
              qa_ref, sa_ref, sb_ref, m_ref, acc_ref):
    n_blk = kc_ref.shape[1]
    n_slc = ov_ref.shape[0]
    tq = ATT_Q
    wide = ATT_HEADS * tq
    t0 = i * tq

    q = q_ref[0]
    qs = jnp.concatenate([q[h * HEAD_DIM:(h + 1) * HEAD_DIM] for h in range(ATT_HEADS)], axis=1)
    lane_w = lax.broadcasted_iota(jnp.int32, (1, wide), 1)
    t_w = t0 + (lane_w & (tq - 1))
    t_q = t0 + lax.broadcasted_iota(jnp.int32, (1, tq), 1)
    heads = lambda a: jnp.concatenate([a] * ATT_HEADS, axis=1)

    s_c = jnp.dot(kc_ref[0], qs, preferred_element_type=F32)
    blk_end = lax.broadcasted_iota(jnp.int32, (n_blk, 1), 0) * CMP_STRIDE + (CMP_BLOCK - 1)
    cmask = blk_end <= t_w
    s_m = jnp.where(cmask, s_c, NEG_INF)
    e = jnp.exp2(s_m - jnp.max(s_m, axis=0, keepdims=True))
    p_c = jnp.where(cmask, e * (1.0 / jnp.sum(e, axis=0, keepdims=True)), 0.0)
    o_cmp = jnp.dot(vct_ref[0], p_c.astype(BF16), preferred_element_type=F32)

    p_sum = p_c[:, 0:tq]
    for h in range(1, ATT_HEADS):
        p_sum = p_sum + p_c[:, h * tq:(h + 1) * tq]
    p_hi = p_sum.astype(BF16)
    p_lo = (p_sum - p_hi.astype(F32)).astype(BF16)
    imp = (jnp.dot(ov_ref[...], p_hi, preferred_element_type=F32)
           + jnp.dot(ov_ref[...], p_lo, preferred_element_type=F32))
    j = lax.broadcasted_iota(jnp.int32, (n_slc, 1), 0)
    back = t_q // SLC_BLOCK - j
    forced = (j == 0) | ((back >= 0) & (back < N_LOCAL))
    imp = jnp.where(forced, FORCE_SCORE, jnp.where(back < 0, -1.0, imp))
    rank = jnp.zeros((n_slc, tq), F32)
    for r in range(n_slc):
        row = imp[r:r + 1, :]
        ahead = (row > imp) | ((row == imp) & (j > r))
        rank = rank + jnp.where(ahead, 1.0, 0.0)
    sel_bias = jnp.where((rank < min(N_SELECT, n_slc)) & (back >= 0), 0.0, MASK_BIAS)
    qa_ref[0:HEAD_DIM, :] = qs
    qa_ref[HEAD_DIM:HEAD_DIM + n_slc, :] = heads(sel_bias).astype(BF16)
    qa_ref[HEAD_DIM + n_slc:, :] = jnp.zeros((LANES - HEAD_DIM - n_slc, wide), BF16)

    k_iota = lax.broadcasted_iota(jnp.int32, (ATT_K, 1), 0)

    n_win = (WINDOW + tq) // ATT_K
    first = jnp.maximum(i - WINDOW // ATT_K, 0)
    kw0 = pl.multiple_of(first * ATT_K, ATT_K)
    s_w = jnp.dot(kw_ref[0, pl.ds(kw0, n_win * ATT_K), :], qs, preferred_element_type=F32)
    s_w = s_w + heads(wmask_ref[...])
    p_w = jnp.exp2(s_w - jnp.max(s_w, axis=0, keepdims=True)).astype(BF16)
    acc_w = None
    for n in range(n_win):
        part = jnp.dot(vw_ref[0, first + n], p_w[n * ATT_K:(n + 1) * ATT_K], preferred_element_type=F32)
        acc_w = part if acc_w is None else acc_w + part

    m_ref[...] = jnp.full((1, wide), NEG_INF, F32)
    acc_ref[...] = jnp.zeros((V_ROWS, wide), F32)

    def scores(kt, causal=False, live=None):
        k0 = pl.multiple_of(kt * ATT_K, ATT_K)
        s_t = jnp.dot(ks_ref[0, pl.ds(k0, ATT_K), :], qa_ref[...], preferred_element_type=F32)
        if causal:
            visible = (k0 + k_iota) <= t_q
            if live is not None:
                visible = visible & live
            s_t = jnp.where(heads(visible), s_t, -jnp.inf)
        return s_t

    def consume(s_ref, kt):
        m_old = m_ref[...]
        m_new = jnp.maximum(m_old, jnp.max(s_ref[...], axis=0, keepdims=True))
        alpha = jnp.exp2(m_old - m_new)
        p = jnp.exp2(s_ref[...] - m_new).astype(BF16)
        m_ref[...] = m_new
        acc_ref[...] = alpha * acc_ref[...] + jnp.dot(vs_ref[0, kt], p, preferred_element_type=F32)

    single = 1 - i % 2
    sa_ref[...] = scores(0, causal=True, live=single == 1)
    consume(sa_ref, 0)
    sa_ref[...] = scores(single)

    def pair(n):
        a = single + 2 * n
        sb_ref[...] = scores(a + 1, causal=True)
        consume(sa_ref, a)
        sa_ref[...] = scores(jnp.minimum(a + 2, i))
        consume(sb_ref, a + 1)

    def finish():
        acc_s = acc_ref[...]
        g = g_ref[0]
        gate = lambda br: jnp.concatenate(
            [g[h * N_BRANCH + br:h * N_BRANCH + br + 1, :] for h in range(ATT_HEADS)], axis=1)
        l_s = acc_s[HEAD_DIM:HEAD_DIM + 1, :]
        l_w = acc_w[HEAD_DIM:HEAD_DIM + 1, :]
        o = (gate(0) * o_cmp + (gate(1) * (1.0 / l_s)) * acc_s[:HEAD_DIM]
             + (gate(2) * (1.0 / l_w)) * acc_w[:HEAD_DIM])
        return jnp.concatenate([o[:, h * tq:(h + 1) * tq] for h in range(ATT_HEADS)], axis=0).T

    return pair, (i + 1 - single) // 2, finish


def _att_kernel(*refs, stride):
    n = ATT_STREAMS
    q_refs, refs = refs[:n], refs[n:]
    shared, refs = refs[:6], refs[6:]
    g_refs, refs = refs[:n], refs[n:]
    wmask_refs, refs = refs[:n], refs[n:]
    ov_ref, o_ref, scratch = refs[0], refs[1], refs[2:]
    n_scr = len(scratch) // n
    j = pl.program_id(1)
    tiles = [_att_tile(j + t * stride, q_refs[t], g_refs[t], wmask_refs[t], *shared, ov_ref,
                       *scratch[t * n_scr:(t + 1) * n_scr]) for t in range(n)]
    common = tiles[0][1]

    def all_tiles(p, carry):
        for pair, _, _ in tiles:
            pair(p)
        return carry

    lax.fori_loop(0, common, all_tiles, 0)
    for extra in range((n - 1) * stride // 2):
        for t, (pair, _, _) in enumerate(tiles):
            if extra < t * stride // 2:
                pair(common + extra)
    for t, (_, _, finish) in enumerate(tiles):
        o_ref[0, t] = finish()


def _att(q_t, ks, kw, vs_t, vw_t, kc, vc_t, g_t, ov_t):
    bsz, _, s = q_t.shape
    n_blk = kc.shape[1]
    nt = s // ATT_K
    n_slc = s // SLC_BLOCK
    wide = ATT_HEADS * ATT_Q
    n = ATT_STREAMS
    stride = s // ATT_Q // n
    assert ATT_Q == ATT_K, "the selected-branch tile pairing assumes one diagonal key tile per query tile"
    assert stride % 2 == 0, "the tiles of a step must need the same parity of key tiles"
    per_b3 = lambda shp: pl.BlockSpec((1,) + shp, lambda b, j: (b, 0, 0))
    per_b4 = lambda shp: pl.BlockSpec((1,) + shp, lambda b, j: (b, 0, 0, 0))
    q_spec = lambda off: pl.BlockSpec((1, ATT_HEADS * HEAD_DIM, ATT_Q), lambda b, j: (b, 0, j + off))
    g_spec = lambda off: pl.BlockSpec((1, 16, ATT_Q), lambda b, j: (b, 0, j + off))
    tile_scratch = [pltpu.VMEM((LANES, wide), BF16),
                    pltpu.VMEM((ATT_K, wide), F32), pltpu.VMEM((ATT_K, wide), F32),
                    pltpu.VMEM((1, wide), F32), pltpu.VMEM((V_ROWS, wide), F32)]
    offsets = [t * stride for t in range(n)]
    back_tiles = WINDOW // ATT_K
    span = WINDOW + ATT_Q
    tile = jnp.arange(back_tiles + 1)[:, None, None]
    key = jnp.maximum(tile - back_tiles, 0) * ATT_K + jnp.arange(span)[None, :, None]
    diff = tile * ATT_Q + jnp.arange(ATT_Q)[None, None, :] - key
    wmask = jnp.where((diff >= 0) & (diff < WINDOW), 0.0, -jnp.inf).astype(F32)
    wmask_spec = lambda off: pl.BlockSpec((None, span, ATT_Q),
                                          lambda b, j: (jnp.minimum(j + off, back_tiles), 0, 0))
    out = pl.pallas_call(
        functools.partial(_att_kernel, stride=stride),
        grid=(bsz, stride),
        in_specs=[q_spec(off) for off in offsets]
                 + [per_b3((s, LANES)), per_b3((s, HEAD_DIM)),
                    per_b4((nt, V_ROWS, ATT_K)), per_b4((nt, V_ROWS, ATT_K)),
                    per_b3((n_blk, HEAD_DIM)), per_b3((HEAD_DIM, n_blk))]
                 + [g_spec(off) for off in offsets]
                 + [wmask_spec(off) for off in offsets]
                 + [pl.BlockSpec((n_slc, n_blk), lambda b, j: (0, 0))],
        out_specs=pl.BlockSpec((1, n, ATT_Q, GROUP_WIDTH), lambda b, j: (b, 0, j, 0)),
        out_shape=jax.ShapeDtypeStruct((bsz, n, s // n, GROUP_WIDTH), F32),
        scratch_shapes=tile_scratch * n,
        compiler_params=_cparams("parallel", "arbitrary"),
        name="sparse_attention",
    )(*([q_t] * n), ks, kw, vs_t, vw_t, kc, vc_t, *([g_t] * n), *([wmask] * n), ov_t)
    return out.reshape(bsz, s, GROUP_WIDTH)


def _mix_ffn_kernel(x_ref, yatt_ref, ca_ref, cg_ref, pin_ref, su_ref, sv_ref, cah_ref, cgh_ref, ph_ref,
                    cw_ref, cb_ref, clg_ref, clb_ref, pw_ref, psc_ref, slg_ref, slb_ref, sw_ref, sbias_ref,
                    wo_ref, gm_ref, g1_ref, w1_ref, w2_ref, g2_ref,
                    o_ref, y_ref, hp_ref, pp_ref, acc_ref, *, tiles_per_seq):
    k = pl.program_id(0)
    tm = x_ref.shape[0]

    @pl.when(k == 0)
    def _():
        y_ref[...] = jnp.zeros_like(y_ref)

    kk = jnp.minimum(k, pl.num_programs(0) - 2)
    seq_tile = kk % tiles_per_seq
    has_prev = seq_tile > 0

    def conv_piece(r0):
        y_ref[0, r0:r0 + CONV_ROWS, :] = _conv_rows(hp_ref, r0, cw_ref, cb_ref, clg_ref, clb_ref)
        return 0, r0

    def pool_piece(r0):
        y_ref[1, r0:r0 + POOL_ROWS, :] = _pool_rows(pp_ref, r0, seq_tile * tm, pw_ref, psc_ref)
        return 1, r0

    def sgu_piece(r0):
        rows = slice(r0, r0 + SGU_CHUNK)
        y_ref[2, rows, :] = _sgu_rows(su_ref[rows, :], sv_ref[rows, :], _sgu_weights(sw_ref),
                                      slg_ref, slb_ref, sbias_ref)
        return 2, r0

    def order_before_next_accumulate(n, r0):
        bits = pltpu.bitcast(y_ref[n, r0:r0 + SUBLANES, 0:LANES], jnp.uint32)
        zero = pltpu.bitcast(lax.shift_right_logical(bits, jnp.uint32(32)), F32)
        acc_ref[0:SUBLANES, 0:LANES] = acc_ref[0:SUBLANES, 0:LANES] + zero

    pieces = []
    for r0 in range(0, tm, CONV_ROWS):
        pieces += [functools.partial(conv_piece, r0), functools.partial(pool_piece, r0)]
        if r0 % SGU_CHUNK == 0:
            pieces.append(functools.partial(sgu_piece, r0))
    n_chunks = w1_ref.shape[1] // FFN_COLS
    per_chunk = -(-len(pieces) // (n_chunks - 1))

    acc = jnp.dot(yatt_ref[...].astype(BF16), wo_ref[3 * GROUP_WIDTH:4 * GROUP_WIDTH, :],
                  preferred_element_type=F32)
    for n in range(3):
        acc = acc + jnp.dot(y_ref[n].astype(BF16), wo_ref[n * GROUP_WIDTH:(n + 1) * GROUP_WIDTH, :],
                            preferred_element_type=F32)
    x1 = x_ref[...] + _rms_norm(acc, gm_ref[...])
    h = _rms_norm(x1, g1_ref[...]).astype(BF16)
    hp_ref[0:MIX_HALO, :] = jnp.where(has_prev, cah_ref[...] * _sigmoid(cgh_ref[...]), 0.0)
    hp_ref[MIX_HALO:, :] = ca_ref[...] * _sigmoid(cg_ref[...])
    pp_ref[0:MIX_HALO, :] = jnp.where(has_prev, ph_ref[...], 0.0)
    pp_ref[MIX_HALO:, :] = pin_ref[...]
    for c in range(n_chunks):
        cols = slice(c * FFN_COLS, (c + 1) * FFN_COLS)
        f = jnp.maximum(jnp.dot(h, w1_ref[:, cols], preferred_element_type=F32), 0.0)
        part = jnp.dot((f * f).astype(BF16), w2_ref[cols, :], preferred_element_type=F32)
        acc_ref[...] = part if c == 0 else acc_ref[...] + part
        for piece in pieces[c * per_chunk:(c + 1) * per_chunk]:
            order_before_next_accumulate(*piece())
    o_ref[...] = x1 + _rms_norm(acc_ref[...], g2_ref[...])


def _mix_ffn(l, x2, y_att, za2, seq_len, lw):
    t, d = x2.shape
    dff = lw["ffn_w1"].shape[2]
    tm = min(FFN_ROWS, seq_len)
    assert CONV_ROWS == POOL_ROWS and SGU_CHUNK % CONV_ROWS == 0 and tm % SGU_CHUNK == 0
    n_tiles = t // tm
    halo_per_tile = tm // MIX_HALO
    cur = lambda k: jnp.maximum(k - 1, 0)
    nxt = lambda k: jnp.minimum(k, n_tiles - 1)
    main = lambda col: pl.BlockSpec((tm, GROUP_WIDTH), lambda k: (nxt(k), col))
    halo = lambda col: pl.BlockSpec((MIX_HALO, GROUP_WIDTH),
                                    lambda k: (jnp.maximum(nxt(k) * halo_per_tile - 1, 0), col))
    vec = lambda n: _per_layer(l, 1, n)
    once = lambda *shp: _per_layer(l, *shp, pipeline_mode=pl.Buffered(1))
    return pl.pallas_call(
        functools.partial(_mix_ffn_kernel, tiles_per_seq=seq_len // tm),
        grid=(n_tiles + 1,),
        in_specs=[pl.BlockSpec((tm, d), lambda k: (cur(k), 0)),
                  pl.BlockSpec((tm, GROUP_WIDTH), lambda k: (cur(k), 0)),
                  main(COL_CVAL), main(COL_CGATE), main(COL_POOL), main(COL_SGU_U), main(COL_SGU_V),
                  halo(COL_CVAL), halo(COL_CGATE), halo(COL_POOL),
                  _per_layer(l, CONV_WIDTH, GROUP_WIDTH), vec(GROUP_WIDTH), vec(GROUP_WIDTH), vec(GROUP_WIDTH),
                  _per_layer(l, GROUP_WIDTH, GROUP_WIDTH), vec(GROUP_WIDTH),
                  vec(GROUP_WIDTH), vec(GROUP_WIDTH),
                  _per_layer(l, SGU_HEADS, SGU_CHUNK, SGU_CHUNK), _per_layer(l, SGU_CHUNK, GROUP_WIDTH),
                  once(d, d), vec(d), vec(d), once(d, dff), once(dff, d), vec(d)],
        out_specs=pl.BlockSpec((tm, d), lambda k: (cur(k), 0)),
        out_shape=jax.ShapeDtypeStruct((t, d), F32),
        scratch_shapes=[pltpu.VMEM((3, tm, GROUP_WIDTH), F32),
                        pltpu.VMEM((MIX_HALO + tm, GROUP_WIDTH), F32),
                        pltpu.VMEM((MIX_HALO + tm, GROUP_WIDTH), F32),
                        pltpu.VMEM((tm, d), F32)],
        compiler_params=_cparams("arbitrary"),
        name="mix_ffn",
    )(x2, y_att, za2, za2, za2, za2, za2, za2, za2, za2,
      lw["conv_w"], lw["conv_b"], lw["conv_lg"], lw["conv_lb"], lw["pool_bd"], lw["pool_scale"],
      lw["sgu_lg"], lw["sgu_lb"], lw["sgu_w"], lw["sgu_bias"],
      lw["w_out"], lw["post_mix"], lw["pre_ffn"], lw["ffn_w1"], lw["ffn_w2"], lw["post_ffn"])


def _rope_perm():
    idx = list(range(HEAD_DIM))
    for c in range(ROPE_HALF):
        idx[c], idx[c + ROPE_HALF] = c + ROPE_HALF, c
    return jnp.array(idx, jnp.int32)


def _split_w_in(w_in):
    gw = GROUP_WIDTH
    edges = [0, gw, 2 * gw, 3 * gw, 4 * gw, 5 * gw, 6 * gw]
    for _ in range(6):
        edges.append(edges[-1] + HEAD_DIM)
    edges.append(edges[-1] + ATT_HEADS * N_BRANCH)
    names = ("cval", "cgate", "pool", "su", "sv", "q", "kc", "vc", "ks", "vs", "kw", "vw", "g")
    return {n: w_in[:, edges[k]:edges[k + 1], :] for k, n in enumerate(names)}


def _prep_weights(p):
    depth = p["w_in"].shape[0]
    perm = _rope_perm()
    w = _split_w_in(jnp.swapaxes(p["w_in"], 1, 2).astype(BF16))
    wa = jnp.concatenate([w["cval"], w["cgate"], w["pool"], w["su"], w["sv"], w["kc"], w["vc"],
                          w["ks"], w["ks"][:, perm], w["kw"], w["kw"][:, perm]], axis=1)
    pad = jnp.zeros((depth, ZB_ROWS - ROW_G - ATT_HEADS * N_BRANCH, D_MODEL), BF16)
    wb = jnp.concatenate([w["q"], w["vs"], w["vw"], w["g"], pad], axis=1)

    n_grp = len(POOL_WINDOWS)
    pool_bd = (p["pool_w"][:, :, :, None, :] * jnp.eye(n_grp, dtype=F32)[None, :, None, :, None])
    pool_bd = pool_bd.reshape(depth, GROUP_WIDTH, GROUP_WIDTH)
    sgu_bias = jnp.repeat(jnp.swapaxes(p["sgu_b"], 1, 2), GROUP_WIDTH // SGU_HEADS, axis=2)

    w1k = p["cmp_k_w1"].reshape(depth, CMP_BLOCK, HEAD_DIM, HEAD_DIM)
    w1v = p["cmp_v_w1"].reshape(depth, CMP_BLOCK, HEAD_DIM, HEAD_DIM)
    z = jnp.zeros((depth, CMP_STRIDE, HEAD_DIM, HEAD_DIM), F32)

    def kv_diag(a, b):
        return jnp.concatenate([jnp.concatenate([a, z], axis=3), jnp.concatenate([z, b], axis=3)], axis=2)

    cmp_wa = kv_diag(w1k[:, :CMP_STRIDE], w1v[:, :CMP_STRIDE]).astype(BF16)
    cmp_wb = kv_diag(w1k[:, CMP_STRIDE:], w1v[:, CMP_STRIDE:]).astype(BF16)
    ones = jnp.ones((1, 1, HEAD_DIM), F32)
    cmp_pe = jnp.concatenate([p["cmp_k_pe"].reshape(depth, -1, 1) * ones,
                              p["cmp_v_pe"].reshape(depth, -1, 1) * ones], axis=2)
    cmp_w1 = jnp.concatenate([p["cmp_k_w1"], p["cmp_v_w1"]], axis=2)
    zk = jnp.zeros((depth, HEAD_DIM, LANES), F32)
    cmp_w2k = jnp.concatenate([jnp.concatenate([p["cmp_k_w2"], p["cmp_k_w2"][..., perm]], axis=2), zk],
                              axis=1).astype(BF16)
    cmp_w2vt = jnp.concatenate([jnp.zeros((depth, HEAD_DIM, HEAD_DIM), F32),
                                jnp.swapaxes(p["cmp_v_w2"], 1, 2)], axis=2).astype(BF16)

    row = lambda v: v[:, None, :]
    return dict(
        wa=wa, wb=wb, pre_mix=row(p["pre_mix_norm"]), post_mix=row(p["post_mix_norm"]),
        pre_ffn=row(p["pre_ffn_norm"]), post_ffn=row(p["post_ffn_norm"]),
        conv_w=p["conv_w"], conv_b=row(p["conv_b"]), conv_lg=row(p["conv_ln_g"]), conv_lb=row(p["conv_ln_b"]),
        pool_bd=pool_bd.astype(BF16), pool_scale=row(p["pool_scale"]),
        sgu_lg=row(p["sgu_ln_g"]), sgu_lb=row(p["sgu_ln_b"]), sgu_w=p["sgu_w"], sgu_bias=sgu_bias,
        cmp_wa=cmp_wa, cmp_wb=cmp_wb, cmp_pe=cmp_pe, cmp_w1=cmp_w1, cmp_w2k=cmp_w2k, cmp_w2vt=cmp_w2vt,
        w_out=p["w_out"].astype(BF16), ffn_w1=p["ffn_w1"].astype(BF16), ffn_w2=p["ffn_w2"].astype(BF16),
    )


def _overlap_t(s):
    n_blk = s // CMP_STRIDE
    n_slc = s // SLC_BLOCK
    bs = jnp.arange(n_blk)[None, :] * CMP_STRIDE
    ss = jnp.arange(n_slc)[:, None] * SLC_BLOCK
    ov = jnp.clip(jnp.minimum(bs + CMP_BLOCK, ss + SLC_BLOCK) - jnp.maximum(bs, ss), 0)
    return (ov.astype(F32) / CMP_STRIDE).astype(BF16)


def _rope_tables(positions):
    bsz, s = positions.shape
    inv = (ROPE_THETA ** (-jnp.arange(ROPE_HALF, dtype=F32) * 2.0 / ROPE_DIM)).reshape(ROPE_HALF, 1)
    posf = positions.astype(F32)
    n_blk = s // CMP_STRIDE
    pos_end = posf[:, CMP_BLOCK - 1::CMP_STRIDE]
    pos_end = jnp.pad(pos_end, ((0, 0), (0, n_blk - pos_end.shape[1])))
    cos_t, sin_t = _rope_chan_table(posf[:, None, :], inv)
    cos_b, sin_b = _rope_chan_table(pos_end[:, None, :], inv)
    return cos_t, sin_t, cos_b, sin_b


def _layer(l, x, lw, tables, ov_t):
    bsz, s, d = x.shape
    za, zb = _proj(l, x, lw["pre_mix"], lw["wa"], lw["wb"])
    prep = _att_prep(l, za, zb, *tables, lw["cmp_wa"], lw["cmp_wb"],
                     lw["cmp_pe"], lw["cmp_w1"], lw["cmp_w2k"], lw["cmp_w2vt"])
    y_att = _att(*prep, ov_t)
    x2 = _mix_ffn(l, x.reshape(bsz * s, d), y_att.reshape(bsz * s, GROUP_WIDTH),
                  za.reshape(bsz * s, ZA_WIDTH), s, lw)
    return x2.reshape(bsz, s, d)


def kernel(x, positions, pre_mix_norm, post_mix_norm, pre_ffn_norm, post_ffn_norm, w_in, conv_w, conv_b, conv_ln_g, conv_ln_b, pool_w, pool_scale, sgu_ln_g, sgu_ln_b, sgu_w, sgu_b, cmp_k_pe, cmp_k_w1, cmp_k_w2, cmp_v_pe, cmp_v_w1, cmp_v_w2, w_out, ffn_w1, ffn_w2):
    params = dict(pre_mix_norm=pre_mix_norm, post_mix_norm=post_mix_norm, pre_ffn_norm=pre_ffn_norm,
                  post_ffn_norm=post_ffn_norm, w_in=w_in, conv_w=conv_w, conv_b=conv_b,
                  conv_ln_g=conv_ln_g, conv_ln_b=conv_ln_b, pool_w=pool_w, pool_scale=pool_scale,
                  sgu_ln_g=sgu_ln_g, sgu_ln_b=sgu_ln_b, sgu_w=sgu_w, sgu_b=sgu_b,
                  cmp_k_pe=cmp_k_pe, cmp_k_w1=cmp_k_w1, cmp_k_w2=cmp_k_w2, cmp_v_pe=cmp_v_pe,
                  cmp_v_w1=cmp_v_w1, cmp_v_w2=cmp_v_w2, w_out=w_out, ffn_w1=ffn_w1, ffn_w2=ffn_w2)
    bsz, s, d = x.shape
    depth = w_in.shape[0]
    tables = _rope_tables(positions)
    ov_t = _overlap_t(s)
    lw = _prep_weights(params)
    for l in range(depth):
        x = _layer(l, x, lw, tables, ov_t)
    return x
```

```python
import functools

import jax
import jax.numpy as jnp
from jax import lax
from jax.experimental import pallas as pl
from jax.experimental.pallas import tpu as pltpu

F32 = jnp.float32
BF16 = jnp.bfloat16

D_MODEL = 1024
GROUP_WIDTH = 256
CONV_WIDTH = 31
POOL_WINDOWS = (2, 4, 8, 16)
POOL_GROUP = 64
SGU_HEADS = 4
SGU_CHUNK = 128
ATT_HEADS = 4
HEAD_DIM = 64
ROPE_DIM = 16
ROPE_HALF = 8
ROPE_THETA = 500000.0
CMP_BLOCK = 32
CMP_STRIDE = 16
SLC_BLOCK = 64
N_SELECT = 8
N_LOCAL = 2
WINDOW = 512
N_BRANCH = 3
NORM_EPS = 1e-6
NEG_INF = -1e30
FORCE_SCORE = 1e9

LANES = 128
SUBLANES = 8
VMEM_LIMIT_BYTES = 56 * 1024 * 1024

PROJ_ROWS = 1024
FFN_ROWS = 512
FFN_COLS = 512
CONV_ROWS = 64
POOL_ROWS = 64
ATT_Q = 256
ATT_K = 256
ATT_STREAMS = 4
LOG2_E = 1.4426950408889634
V_ROWS = 80
MASK_BIAS = -2e30

COL_CVAL, COL_CGATE, COL_POOL, COL_SGU_U, COL_SGU_V = 0, 1, 2, 3, 4
COL_KCVC, COL_KS, COL_KW = 10, 11, 12
ZA_WIDTH = 1664
ROW_Q, ROW_VS, ROW_VW, ROW_G = 0, 256, 320, 384
ZB_ROWS = 400


def _cparams(*sem):
    return pltpu.CompilerParams(dimension_semantics=sem, vmem_limit_bytes=VMEM_LIMIT_BYTES)


def _per_layer(l, *shape, pipeline_mode=None):
    zeros = (0,) * len(shape)
    extra = {} if pipeline_mode is None else {"pipeline_mode": pipeline_mode}
    return pl.BlockSpec((None,) + shape, lambda *_: (l,) + zeros, **extra)


def _gelu(x):
    return 0.5 * x * (1.0 + jnp.tanh(0.7978845608028654 * (x + 0.044715 * (x * x * x))))


def _sigmoid(x):
    return 0.5 * jnp.tanh(0.5 * x) + 0.5


def _layer_norm(x, g, b):
    mu = jnp.mean(x, axis=-1, keepdims=True)
    d = x - mu
    var = jnp.mean(d * d, axis=-1, keepdims=True)
    return d * lax.rsqrt(var + NORM_EPS) * g + b


def _rms_norm(x, g):
    return x * lax.rsqrt(jnp.mean(x * x, axis=-1, keepdims=True) + NORM_EPS) * g


def _rope_chan_kernel(pos_ref, inv_ref, cos_ref, sin_ref):
    ang = inv_ref[...] * pos_ref[0]
    cos_ref[0] = jnp.cos(ang)
    sin_ref[0] = jnp.sin(ang)


def _rope_chan_table(pos_row, inv_col):
    b, _, s = pos_row.shape
    spec = pl.BlockSpec((1, ROPE_HALF, s), lambda i: (i, 0, 0))
    return pl.pallas_call(
        _rope_chan_kernel,
        grid=(b,),
        in_specs=[pl.BlockSpec((1, 1, s), lambda i: (i, 0, 0)),
                  pl.BlockSpec((ROPE_HALF, 1), lambda i: (0, 0))],
        out_specs=[spec, spec],
        out_shape=[jax.ShapeDtypeStruct((b, ROPE_HALF, s), F32)] * 2,
        compiler_params=_cparams("parallel"),
        name="rope_chan_table",
    )(pos_row, inv_col)


def _proj_kernel(x_ref, g_ref, wa_ref, wb_ref, za_ref, zb_ref):
    h = _rms_norm(x_ref[0], g_ref[...]).astype(BF16)
    last_dims = (((1,), (1,)), ((), ()))
    za_ref[0] = lax.dot_general(h, wa_ref[...], last_dims, preferred_element_type=F32)
    zb_ref[0] = lax.dot_general(wb_ref[...], h, last_dims, preferred_element_type=F32)


def _proj(l, x, g, wa, wb):
    b, s, d = x.shape
    tm = min(PROJ_ROWS, s)
    return pl.pallas_call(
        _proj_kernel,
        grid=(b, s // tm),
        in_specs=[pl.BlockSpec((1, tm, d), lambda i, j: (i, j, 0)),
                  _per_layer(l, 1, d), _per_layer(l, ZA_WIDTH, d), _per_layer(l, ZB_ROWS, d)],
        out_specs=[pl.BlockSpec((1, tm, ZA_WIDTH), lambda i, j: (i, j, 0)),
                   pl.BlockSpec((1, ZB_ROWS, tm), lambda i, j: (i, 0, j))],
        out_shape=[jax.ShapeDtypeStruct((b, s, ZA_WIDTH), F32),
                   jax.ShapeDtypeStruct((b, ZB_ROWS, s), F32)],
        compiler_params=_cparams("parallel", "parallel"),
        name="in_proj",
    )(x, g, wa, wb)


MIX_HALO = 32


def _conv_rows(hp_ref, r0, w_ref, b_ref, lg_ref, lb_ref):
    shift = MIX_HALO - (CONV_WIDTH - 1)
    n_win = CONV_ROWS + MIX_HALO
    acc = jnp.zeros((CONV_ROWS, GROUP_WIDTH), F32) + b_ref[...]
    win = hp_ref[r0:r0 + n_win, :]
    for r in range(SUBLANES):
        taps = [k for k in range(CONV_WIDTH) if (k + shift) % SUBLANES == r]
        wr = win if r == 0 else pltpu.roll(win, n_win - r, axis=0)
        for k in taps:
            off = k + shift - r
            acc = acc + wr[off:off + CONV_ROWS, :] * w_ref[k:k + 1, :]
    y = _layer_norm(acc, lg_ref[...], lb_ref[...])
    return y * _sigmoid(y)


POOL_PAD = 16


def _pool_rows(pp_ref, r0, seq_t0, w_ref, sc_ref):
    lane = lax.broadcasted_iota(jnp.int32, (1, GROUP_WIDTH), 1)
    grp = lane // POOL_GROUP
    win = jnp.where(grp == 0, POOL_WINDOWS[0],
                    jnp.where(grp == 1, POOL_WINDOWS[1],
                              jnp.where(grp == 2, POOL_WINDOWS[2], POOL_WINDOWS[3])))
    base = r0 + MIX_HALO - POOL_PAD
    rows = pp_ref[base:base + POOL_ROWS + POOL_PAD, :]
    p = rows[POOL_PAD:POOL_PAD + POOL_ROWS, :]
    acc = rows
    sums = []
    width = 1
    for w in POOL_WINDOWS:
        while width < w:
            acc = acc + pltpu.roll(acc, width, axis=0)
            width *= 2
        sums.append(acc[POOL_PAD:POOL_PAD + POOL_ROWS, :])
    total = jnp.where(grp == 0, sums[0],
                      jnp.where(grp == 1, sums[1],
                                jnp.where(grp == 2, sums[2], sums[3])))
    t = seq_t0 + r0 + lax.broadcasted_iota(jnp.int32, (POOL_ROWS, 1), 0)
    count = jnp.minimum(t + 1, win).astype(F32)
    mixed = total / count - p
    y = jnp.dot(mixed.astype(BF16), w_ref[...], preferred_element_type=F32)
    return y * sc_ref[...]


def _sgu_weights(w_ref):
    row = lax.broadcasted_iota(jnp.int32, (SGU_CHUNK, SGU_CHUNK), 0)
    col = lax.broadcasted_iota(jnp.int32, (SGU_CHUNK, SGU_CHUNK), 1)
    return [jnp.where(row >= col, w_ref[h], 0.0).astype(BF16) for h in range(SGU_HEADS)]


def _sgu_rows(u, v, ws, lg_ref, lb_ref, bias_ref):
    head = lax.broadcasted_iota(jnp.int32, (1, GROUP_WIDTH), 1) // (GROUP_WIDTH // SGU_HEADS)
    vb = _layer_norm(_gelu(v), lg_ref[...], lb_ref[...]).astype(BF16)
    mixed = jnp.zeros((SGU_CHUNK, GROUP_WIDTH), F32)
    for h in range(SGU_HEADS):
        mixed = jnp.where(head == h, jnp.dot(ws[h], vb, preferred_element_type=F32), mixed)
    return _gelu(u) * (mixed + bias_ref[...])


def _att_prep_kernel(kcvc_ref, ksx_ref, kwx_ref, zb_ref, cos_ref, sin_ref, cosb_ref, sinb_ref,
                     wa_ref, wb_ref, pe_ref, w1_ref, w2k_ref, w2vt_ref,
                     q_ref, ks_ref, kw_ref, vs_ref, vw_ref, kc_ref, vct_ref, g_ref, sh_ref):
    s = ksx_ref.shape[1]
    n_blk = s // CMP_STRIDE

    def tok_table(cos, sin):
        n = cos.shape[1]
        rest = HEAD_DIM - ROPE_DIM
        rows = [cos, cos, jnp.ones((rest, n), F32), -sin, sin, jnp.zeros((rest, n), F32)]
        return jnp.concatenate(rows, axis=0).T

    def rope_wide(x, cs):
        r = x * cs
        return r + pltpu.roll(r, HEAD_DIM, axis=1)

    def rope_tok(x, cs):
        return rope_wide(x, cs)[:, :HEAD_DIM]

    cos = cos_ref[0]
    sin = sin_ref[0]
    cs = tok_table(cos, sin)
    lane = lax.broadcasted_iota(jnp.int32, (s, LANES), 1)
    blk = lax.broadcasted_iota(jnp.int32, (s, LANES), 0) // SLC_BLOCK
    ks_ref[0] = jnp.where(lane < HEAD_DIM, rope_wide(ksx_ref[0], cs),
                          jnp.where(lane - HEAD_DIM == blk, 1.0, 0.0)).astype(BF16)
    kw_ref[0] = rope_tok(kwx_ref[0], cs).astype(BF16)

    scale = HEAD_DIM ** -0.5 * LOG2_E
    parts = []
    for h in range(ATT_HEADS):
        r0 = ROW_Q + h * HEAD_DIM
        x1 = zb_ref[0, r0:r0 + ROPE_HALF, :]
        x2 = zb_ref[0, r0 + ROPE_HALF:r0 + ROPE_DIM, :]
        parts += [x1 * cos - x2 * sin, x2 * cos + x1 * sin, zb_ref[0, r0 + ROPE_DIM:r0 + HEAD_DIM, :]]
    q_ref[0] = (jnp.concatenate(parts, axis=0) * scale).astype(BF16)

    ones_row = jnp.where(lax.broadcasted_iota(jnp.int32, (V_ROWS - HEAD_DIM, ATT_K), 0) == 0, 1.0, 0.0)
    for c in range(s // ATT_K):
        cols = slice(c * ATT_K, (c + 1) * ATT_K)
        vs_ref[0, c] = jnp.concatenate([zb_ref[0, ROW_VS:ROW_VS + HEAD_DIM, cols], ones_row], axis=0).astype(BF16)
        vw_ref[0, c] = jnp.concatenate([zb_ref[0, ROW_VW:ROW_VW + HEAD_DIM, cols], ones_row], axis=0).astype(BF16)
    g_ref[0] = _sigmoid(zb_ref[0, ROW_G:ROW_G + 16, :])

    acc_a = jnp.zeros((n_blk, LANES), F32)
    acc_b = jnp.zeros((n_blk, LANES), F32)
    for l in range(CMP_STRIDE):
        xl = kcvc_ref[0, pl.ds(l, n_blk, stride=CMP_STRIDE), :].astype(BF16)
        acc_a = acc_a + jnp.dot(xl, wa_ref[l], preferred_element_type=F32)
        acc_b = acc_b + jnp.dot(xl, wb_ref[l], preferred_element_type=F32)
    sh_ref[0:n_blk, :] = acc_b
    sh_ref[n_blk:n_blk + SUBLANES, :] = jnp.zeros((SUBLANES, LANES), F32)
    pe_term = jnp.sum(pe_ref[...] * w1_ref[...], axis=0, keepdims=True)
    hid = _gelu(acc_a + sh_ref[1:n_blk + 1, :] + pe_term).astype(BF16)
    kk = jnp.dot(hid, w2k_ref[...], preferred_element_type=F32)
    kc_ref[0] = rope_tok(kk, tok_table(cosb_ref[0], sinb_ref[0])).astype(BF16)
    vct_ref[0] = lax.dot_general(w2vt_ref[...], hid, (((1,), (1,)), ((), ())),
                                 preferred_element_type=F32).astype(BF16)


def _att_prep(l, za, zb, cos_t, sin_t, cos_b, sin_b, wa, wb, pe, w1, w2k, w2vt):
    bsz, s, _ = za.shape
    n_blk = s // CMP_STRIDE
    nt = s // ATT_K
    tok = lambda c: pl.BlockSpec((1, s, LANES), lambda i: (i, 0, c))
    per_b3 = lambda shp: pl.BlockSpec((1,) + shp, lambda i: (i, 0, 0))
    per_b4 = lambda shp: pl.BlockSpec((1,) + shp, lambda i: (i, 0, 0, 0))
    return pl.pallas_call(
        _att_prep_kernel,
        grid=(bsz,),
        in_specs=[tok(COL_KCVC), tok(COL_KS), tok(COL_KW),
                  per_b3((ZB_ROWS, s)),
                  per_b3((ROPE_HALF, s)), per_b3((ROPE_HALF, s)),
                  per_b3((ROPE_HALF, n_blk)), per_b3((ROPE_HALF, n_blk)),
                  _per_layer(l, CMP_STRIDE, LANES, LANES), _per_layer(l, CMP_STRIDE, LANES, LANES),
                  _per_layer(l, CMP_BLOCK * HEAD_DIM, LANES), _per_layer(l, CMP_BLOCK * HEAD_DIM, LANES),
                  _per_layer(l, LANES, LANES), _per_layer(l, HEAD_DIM, LANES)],
        out_specs=[per_b3((ATT_HEADS * HEAD_DIM, s)), per_b3((s, LANES)), per_b3((s, HEAD_DIM)),
                   per_b4((nt, V_ROWS, ATT_K)), per_b4((nt, V_ROWS, ATT_K)),
                   per_b3((n_blk, HEAD_DIM)), per_b3((HEAD_DIM, n_blk)), per_b3((16, s))],
        out_shape=[jax.ShapeDtypeStruct((bsz, ATT_HEADS * HEAD_DIM, s), BF16),
                   jax.ShapeDtypeStruct((bsz, s, LANES), BF16),
                   jax.ShapeDtypeStruct((bsz, s, HEAD_DIM), BF16),
                   jax.ShapeDtypeStruct((bsz, nt, V_ROWS, ATT_K), BF16),
                   jax.ShapeDtypeStruct((bsz, nt, V_ROWS, ATT_K), BF16),
                   jax.ShapeDtypeStruct((bsz, n_blk, HEAD_DIM), BF16),
                   jax.ShapeDtypeStruct((bsz, HEAD_DIM, n_blk), BF16),
                   jax.ShapeDtypeStruct((bsz, 16, s), F32)],
        scratch_shapes=[pltpu.VMEM((n_blk + SUBLANES, LANES), F32)],
        compiler_params=_cparams("parallel"),
        name="att_prep",
    )(za, za, za, zb, cos_t, sin_t, cos_b, sin_b, wa, wb, pe, w1, w2k, w2vt)


def _att_tile(i, q_ref, g_ref, wmask_ref, ks_ref, kw_ref, vs_ref, vw_ref, kc_ref, vct_ref, ov_ref,
              qa_ref, sa_ref, sb_ref, m_ref, acc_ref):
    n_blk = kc_ref.shape[1]
    n_slc = ov_ref.shape[0]
    tq = ATT_Q
    wide = ATT_HEADS * tq
    t0 = i * tq

    q = q_ref[0]
    qs = jnp.concatenate([q[h * HEAD_DIM:(h + 1) * HEAD_DIM] for h in range(ATT_HEADS)], axis=1)
    lane_w = lax.broadcasted_iota(jnp.int32, (1, wide), 1)
    t_w = t0 + (lane_w & (tq - 1))
    t_q = t0 + lax.broadcasted_iota(jnp.int32, (1, tq), 1)
    heads = lambda a: jnp.concatenate([a] * ATT_HEADS, axis=1)

    s_c = jnp.dot(kc_ref[0], qs, preferred_element_type=F32)
    blk_end = lax.broadcasted_iota(jnp.int32, (n_blk, 1), 0) * CMP_STRIDE + (CMP_BLOCK - 1)
    cmask = blk_end <= t_w
    s_m = jnp.where(cmask, s_c, NEG_INF)
    e = jnp.exp2(s_m - jnp.max(s_m, axis=0, keepdims=True))
    p_c = jnp.where(cmask, e * (1.0 / jnp.sum(e, axis=0, keepdims=True)), 0.0)
    o_cmp = jnp.dot(vct_ref[0], p_c.astype(BF16), preferred_element_type=F32)

    p_sum = p_c[:, 0:tq]
    for h in range(1, ATT_HEADS):
        p_sum = p_sum + p_c[:, h * tq:(h + 1) * tq]
    p_hi = p_sum.astype(BF16)
    p_lo = (p_sum - p_hi.astype(F32)).astype(BF16)
    imp = (jnp.dot(ov_ref[...], p_hi, preferred_element_type=F32)
           + jnp.dot(ov_ref[...], p_lo, preferred_element_type=F32))
    j = lax.broadcasted_iota(jnp.int32, (n_slc, 1), 0)
    back = t_q // SLC_BLOCK - j
    forced = (j == 0) | ((back >= 0) & (back < N_LOCAL))
    imp = jnp.where(forced, FORCE_SCORE, jnp.where(back < 0, -1.0, imp))
    rank = jnp.zeros((n_slc, tq), F32)
    for r in range(n_slc):
        row = imp[r:r + 1, :]
        ahead = (row > imp) | ((row == imp) & (j > r))
        rank = rank + jnp.where(ahead, 1.0, 0.0)
    sel_bias = jnp.where((rank < min(N_SELECT, n_slc)) & (back >= 0), 0.0, MASK_BIAS)
    qa_ref[0:HEAD_DIM, :] = qs
    qa_ref[HEAD_DIM:HEAD_DIM + n_slc, :] = heads(sel_bias).astype(BF16)
    qa_ref[HEAD_DIM + n_slc:, :] = jnp.zeros((LANES - HEAD_DIM - n_slc, wide), BF16)

    k_iota = lax.broadcasted_iota(jnp.int32, (ATT_K, 1), 0)

    n_win = (WINDOW + tq) // ATT_K
    first = jnp.maximum(i - WINDOW // ATT_K, 0)
    kw0 = pl.multiple_of(first * ATT_K, ATT_K)
    s_w = jnp.dot(kw_ref[0, pl.ds(kw0, n_win * ATT_K), :], qs, preferred_element_type=F32)
    s_w = s_w + heads(wmask_ref[...])
    p_w = jnp.exp2(s_w - jnp.max(s_w, axis=0, keepdims=True)).astype(BF16)
    acc_w = None
    for n in range(n_win):
        part = jnp.dot(vw_ref[0, first + n], p_w[n * ATT_K:(n + 1) * ATT_K], preferred_element_type=F32)
        acc_w = part if acc_w is None else acc_w + part

    m_ref[...] = jnp.full((1, wide), NEG_INF, F32)
    acc_ref[...] = jnp.zeros((V_ROWS, wide), F32)

    def scores(kt, causal=False, live=None):
        k0 = pl.multiple_of(kt * ATT_K, ATT_K)
        s_t = jnp.dot(ks_ref[0, pl.ds(k0, ATT_K), :], qa_ref[...], preferred_element_type=F32)
        if causal:
            visible = (k0 + k_iota) <= t_q
            if live is not None:
                visible = visible & live
            s_t = jnp.where(heads(visible), s_t, -jnp.inf)
        return s_t

    def consume(s_ref, kt):
        m_old = m_ref[...]
        m_new = jnp.maximum(m_old, jnp.max(s_ref[...], axis=0, keepdims=True))
        alpha = jnp.exp2(m_old - m_new)
        p = jnp.exp2(s_ref[...] - m_new).astype(BF16)
        m_ref[...] = m_new
        acc_ref[...] = alpha * acc_ref[...] + jnp.dot(vs_ref[0, kt], p, preferred_element_type=F32)

    single = 1 - i % 2
    sa_ref[...] = scores(0, causal=True, live=single == 1)
    consume(sa_ref, 0)
    sa_ref[...] = scores(single)

    def pair(n):
        a = single + 2 * n
        sb_ref[...] = scores(a + 1, causal=True)
        consume(sa_ref, a)
        sa_ref[...] = scores(jnp.minimum(a + 2, i))
        consume(sb_ref, a + 1)

    def finish():
        acc_s = acc_ref[...]
        g = g_ref[0]
        gate = lambda br: jnp.concatenate(
            [g[h * N_BRANCH + br:h * N_BRANCH + br + 1, :] for h in range(ATT_HEADS)], axis=1)
        l_s = acc_s[HEAD_DIM:HEAD_DIM + 1, :]
        l_w = acc_w[HEAD_DIM:HEAD_DIM + 1, :]
        o = (gate(0) * o_cmp + (gate(1) * (1.0 / l_s)) * acc_s[:HEAD_DIM]
             + (gate(2) * (1.0 / l_w)) * acc_w[:HEAD_DIM])
        return jnp.concatenate([o[:, h * tq:(h + 1) * tq] for h in range(ATT_HEADS)], axis=0).T

    return pair, (i + 1 - single) // 2, finish


def _att_kernel(*refs, stride):
    n = ATT_STREAMS
    q_refs, refs = refs[:n], refs[n:]
    shared, refs = refs[:6], refs[6:]
    g_refs, refs = refs[:n], refs[n:]
    wmask_refs, refs = refs[:n], refs[n:]
    ov_ref, o_ref, scratch = refs[0], refs[1], refs[2:]
    n_scr = len(scratch) // n
    j = pl.program_id(1)
    tiles = [_att_tile(j + t * stride, q_refs[t], g_refs[t], wmask_refs[t], *shared, ov_ref,
                       *scratch[t * n_scr:(t + 1) * n_scr]) for t in range(n)]
    common = tiles[0][1]

    def all_tiles(p, carry):
        for pair, _, _ in tiles:
            pair(p)
        return carry

    lax.fori_loop(0, common, all_tiles, 0)
    for extra in range((n - 1) * stride // 2):
        for t, (pair, _, _) in enumerate(tiles):
            if extra < t * stride // 2:
                pair(common + extra)
    for t, (_, _, finish) in enumerate(tiles):
        o_ref[0, t] = finish()


def _att(q_t, ks, kw, vs_t, vw_t, kc, vc_t, g_t, ov_t):
    bsz, _, s = q_t.shape
    n_blk = kc.shape[1]
    nt = s // ATT_K
    n_slc = s // SLC_BLOCK
    wide = ATT_HEADS * ATT_Q
    n = ATT_STREAMS
    stride = s // ATT_Q // n
    assert ATT_Q == ATT_K, "the selected-branch tile pairing assumes one diagonal key tile per query tile"
    assert stride % 2 == 0, "the tiles of a step must need the same parity of key tiles"
    per_b3 = lambda shp: pl.BlockSpec((1,) + shp, lambda b, j: (b, 0, 0))
    per_b4 = lambda shp: pl.BlockSpec((1,) + shp, lambda b, j: (b, 0, 0, 0))
    q_spec = lambda off: pl.BlockSpec((1, ATT_HEADS * HEAD_DIM, ATT_Q), lambda b, j: (b, 0, j + off))
    g_spec = lambda off: pl.BlockSpec((1, 16, ATT_Q), lambda b, j: (b, 0, j + off))
    tile_scratch = [pltpu.VMEM((LANES, wide), BF16),
                    pltpu.VMEM((ATT_K, wide), F32), pltpu.VMEM((ATT_K, wide), F32),
                    pltpu.VMEM((1, wide), F32), pltpu.VMEM((V_ROWS, wide), F32)]
    offsets = [t * stride for t in range(n)]
    back_tiles = WINDOW // ATT_K
    span = WINDOW + ATT_Q
    tile = jnp.arange(back_tiles + 1)[:, None, None]
    key = jnp.maximum(tile - back_tiles, 0) * ATT_K + jnp.arange(span)[None, :, None]
    diff = tile * ATT_Q + jnp.arange(ATT_Q)[None, None, :] - key
    wmask = jnp.where((diff >= 0) & (diff < WINDOW), 0.0, -jnp.inf).astype(F32)
    wmask_spec = lambda off: pl.BlockSpec((None, span, ATT_Q),
                                          lambda b, j: (jnp.minimum(j + off, back_tiles), 0, 0))
    out = pl.pallas_call(
        functools.partial(_att_kernel, stride=stride),
        grid=(bsz, stride),
        in_specs=[q_spec(off) for off in offsets]
                 + [per_b3((s, LANES)), per_b3((s, HEAD_DIM)),
                    per_b4((nt, V_ROWS, ATT_K)), per_b4((nt, V_ROWS, ATT_K)),
                    per_b3((n_blk, HEAD_DIM)), per_b3((HEAD_DIM, n_blk))]
                 + [g_spec(off) for off in offsets]
                 + [wmask_spec(off) for off in offsets]
                 + [pl.BlockSpec((n_slc, n_blk), lambda b, j: (0, 0))],
        out_specs=pl.BlockSpec((1, n, ATT_Q, GROUP_WIDTH), lambda b, j: (b, 0, j, 0)),
        out_shape=jax.ShapeDtypeStruct((bsz, n, s // n, GROUP_WIDTH), F32),
        scratch_shapes=tile_scratch * n,
        compiler_params=_cparams("parallel", "arbitrary"),
        name="sparse_attention",
    )(*([q_t] * n), ks, kw, vs_t, vw_t, kc, vc_t, *([g_t] * n), *([wmask] * n), ov_t)
    return out.reshape(bsz, s, GROUP_WIDTH)


def _mix_ffn_kernel(x_ref, yatt_ref, ca_ref, cg_ref, pin_ref, su_ref, sv_ref, cah_ref, cgh_ref, ph_ref,
                    vec_ref, cw_ref, pw_ref, sw_ref, sbias_ref, wo_ref, w1_ref, w2_ref,
                    o_ref, y_ref, hp_ref, pp_ref, acc_ref, *, tiles_per_seq):
    k = pl.program_id(0)
    tm, d = x_ref.shape
    widths = [GROUP_WIDTH] * 6 + [d] * 3
    starts = [sum(widths[:n]) for n in range(len(widths))]
    (cb_ref, clg_ref, clb_ref, psc_ref, slg_ref, slb_ref, gm_ref, g1_ref, g2_ref) = [
        vec_ref.at[:, a:a + w] for a, w in zip(starts, widths)]

    @pl.when(k == 0)
    def _():
        y_ref[...] = jnp.zeros_like(y_ref)

    kk = jnp.minimum(k, pl.num_programs(0) - 2)
    seq_tile = kk % tiles_per_seq
    has_prev = seq_tile > 0

    def conv_piece(r0):
        y_ref[0, r0:r0 + CONV_ROWS, :] = _conv_rows(hp_ref, r0, cw_ref, cb_ref, clg_ref, clb_ref)
        return 0, r0

    def pool_piece(r0):
        y_ref[1, r0:r0 + POOL_ROWS, :] = _pool_rows(pp_ref, r0, seq_tile * tm, pw_ref, psc_ref)
        return 1, r0

    def sgu_piece(r0):
        rows = slice(r0, r0 + SGU_CHUNK)
        y_ref[2, rows, :] = _sgu_rows(su_ref[rows, :], sv_ref[rows, :], _sgu_weights(sw_ref),
                                      slg_ref, slb_ref, sbias_ref)
        return 2, r0

    def order_before_next_accumulate(n, r0):
        bits = pltpu.bitcast(y_ref[n, r0:r0 + SUBLANES, 0:LANES], jnp.uint32)
        zero = pltpu.bitcast(lax.shift_right_logical(bits, jnp.uint32(32)), F32)
        acc_ref[0:SUBLANES, 0:LANES] = acc_ref[0:SUBLANES, 0:LANES] + zero

    pieces = []
    for r0 in range(0, tm, CONV_ROWS):
        pieces += [functools.partial(conv_piece, r0), functools.partial(pool_piece, r0)]
        if r0 % SGU_CHUNK == 0:
            pieces.append(functools.partial(sgu_piece, r0))
    n_chunks = w1_ref.shape[1] // FFN_COLS
    per_chunk = -(-len(pieces) // (n_chunks - 1))

    acc = jnp.dot(yatt_ref[...].astype(BF16), wo_ref[3 * GROUP_WIDTH:4 * GROUP_WIDTH, :],
                  preferred_element_type=F32)
    for n in range(3):
        acc = acc + jnp.dot(y_ref[n].astype(BF16), wo_ref[n * GROUP_WIDTH:(n + 1) * GROUP_WIDTH, :],
                            preferred_element_type=F32)
    x1 = x_ref[...] + _rms_norm(acc, gm_ref[...])
    h = _rms_norm(x1, g1_ref[...]).astype(BF16)
    hp_ref[0:MIX_HALO, :] = jnp.where(has_prev, cah_ref[...] * _sigmoid(cgh_ref[...]), 0.0)
    hp_ref[MIX_HALO:, :] = ca_ref[...] * _sigmoid(cg_ref[...])
    pp_ref[0:MIX_HALO, :] = jnp.where(has_prev, ph_ref[...], 0.0)
    pp_ref[MIX_HALO:, :] = pin_ref[...]
    for c in range(n_chunks):
        cols = slice(c * FFN_COLS, (c + 1) * FFN_COLS)
        f = jnp.maximum(jnp.dot(h, w1_ref[:, cols], preferred_element_type=F32), 0.0)
        part = jnp.dot((f * f).astype(BF16), w2_ref[cols, :], preferred_element_type=F32)
        acc_ref[...] = part if c == 0 else acc_ref[...] + part
        for piece in pieces[c * per_chunk:(c + 1) * per_chunk]:
            order_before_next_accumulate(*piece())
    o_ref[...] = x1 + _rms_norm(acc_ref[...], g2_ref[...])


def _mix_ffn(l, x2, y_att, za2, seq_len, lw):
    t, d = x2.shape
    dff = lw["ffn_w1"].shape[2]
    tm = min(FFN_ROWS, seq_len)
    assert CONV_ROWS == POOL_ROWS and SGU_CHUNK % CONV_ROWS == 0 and tm % SGU_CHUNK == 0
    n_tiles = t // tm
    halo_per_tile = tm // MIX_HALO
    cur = lambda k: jnp.maximum(k - 1, 0)
    nxt = lambda k: jnp.minimum(k, n_tiles - 1)
    main = lambda col: pl.BlockSpec((tm, GROUP_WIDTH), lambda k: (nxt(k), col))
    halo = lambda col: pl.BlockSpec((MIX_HALO, GROUP_WIDTH),
                                    lambda k: (jnp.maximum(nxt(k) * halo_per_tile - 1, 0), col))
    once = lambda *shp: _per_layer(l, *shp, pipeline_mode=pl.Buffered(1))
    return pl.pallas_call(
        functools.partial(_mix_ffn_kernel, tiles_per_seq=seq_len // tm),
        grid=(n_tiles + 1,),
        in_specs=[pl.BlockSpec((tm, d), lambda k: (cur(k), 0)),
                  pl.BlockSpec((tm, GROUP_WIDTH), lambda k: (cur(k), 0)),
                  main(COL_CVAL), main(COL_CGATE), main(COL_POOL), main(COL_SGU_U), main(COL_SGU_V),
                  halo(COL_CVAL), halo(COL_CGATE), halo(COL_POOL),
                  _per_layer(l, 1, lw["mix_vec"].shape[2]),
                  _per_layer(l, CONV_WIDTH, GROUP_WIDTH), _per_layer(l, GROUP_WIDTH, GROUP_WIDTH),
                  _per_layer(l, SGU_HEADS, SGU_CHUNK, SGU_CHUNK), _per_layer(l, SGU_CHUNK, GROUP_WIDTH),
                  once(d, d), once(d, dff), once(dff, d)],
        out_specs=pl.BlockSpec((tm, d), lambda k: (cur(k), 0)),
        out_shape=jax.ShapeDtypeStruct((t, d), F32),
        scratch_shapes=[pltpu.VMEM((3, tm, GROUP_WIDTH), F32),
                        pltpu.VMEM((MIX_HALO + tm, GROUP_WIDTH), F32),
                        pltpu.VMEM((MIX_HALO + tm, GROUP_WIDTH), F32),
                        pltpu.VMEM((tm, d), F32)],
        compiler_params=_cparams("arbitrary"),
        name="mix_ffn",
    )(x2, y_att, za2, za2, za2, za2, za2, za2, za2, za2,
      lw["mix_vec"], lw["conv_w"], lw["pool_bd"], lw["sgu_w"], lw["sgu_bias"],
      lw["w_out"], lw["ffn_w1"], lw["ffn_w2"])


def _rope_perm():
    idx = list(range(HEAD_DIM))
    for c in range(ROPE_HALF):
        idx[c], idx[c + ROPE_HALF] = c + ROPE_HALF, c
    return jnp.array(idx, jnp.int32)


def _split_w_in(w_in):
    gw = GROUP_WIDTH
    edges = [0, gw, 2 * gw, 3 * gw, 4 * gw, 5 * gw, 6 * gw]
    for _ in range(6):
        edges.append(edges[-1] + HEAD_DIM)
    edges.append(edges[-1] + ATT_HEADS * N_BRANCH)
    names = ("cval", "cgate", "pool", "su", "sv", "q", "kc", "vc", "ks", "vs", "kw", "vw", "g")
    return {n: w_in[:, edges[k]:edges[k + 1], :] for k, n in enumerate(names)}


def _prep_weights(p):
    depth = p["w_in"].shape[0]
    perm = _rope_perm()
    w = _split_w_in(jnp.swapaxes(p["w_in"], 1, 2).astype(BF16))
    wa = jnp.concatenate([w["cval"], w["cgate"], w["pool"], w["su"], w["sv"], w["kc"], w["vc"],
                          w["ks"], w["ks"][:, perm], w["kw"], w["kw"][:, perm]], axis=1)
    pad = jnp.zeros((depth, ZB_ROWS - ROW_G - ATT_HEADS * N_BRANCH, D_MODEL), BF16)
    wb = jnp.concatenate([w["q"], w["vs"], w["vw"], w["g"], pad], axis=1)

    n_grp = len(POOL_WINDOWS)
    pool_bd = (p["pool_w"][:, :, :, None, :] * jnp.eye(n_grp, dtype=F32)[None, :, None, :, None])
    pool_bd = pool_bd.reshape(depth, GROUP_WIDTH, GROUP_WIDTH)
    sgu_bias = jnp.repeat(jnp.swapaxes(p["sgu_b"], 1, 2), GROUP_WIDTH // SGU_HEADS, axis=2)

    w1k = p["cmp_k_w1"].reshape(depth, CMP_BLOCK, HEAD_DIM, HEAD_DIM)
    w1v = p["cmp_v_w1"].reshape(depth, CMP_BLOCK, HEAD_DIM, HEAD_DIM)
    z = jnp.zeros((depth, CMP_STRIDE, HEAD_DIM, HEAD_DIM), F32)

    def kv_diag(a, b):
        return jnp.concatenate([jnp.concatenate([a, z], axis=3), jnp.concatenate([z, b], axis=3)], axis=2)

    cmp_wa = kv_diag(w1k[:, :CMP_STRIDE], w1v[:, :CMP_STRIDE]).astype(BF16)
    cmp_wb = kv_diag(w1k[:, CMP_STRIDE:], w1v[:, CMP_STRIDE:]).astype(BF16)
    ones = jnp.ones((1, 1, HEAD_DIM), F32)
    cmp_pe = jnp.concatenate([p["cmp_k_pe"].reshape(depth, -1, 1) * ones,
                              p["cmp_v_pe"].reshape(depth, -1, 1) * ones], axis=2)
    cmp_w1 = jnp.concatenate([p["cmp_k_w1"], p["cmp_v_w1"]], axis=2)
    zk = jnp.zeros((depth, HEAD_DIM, LANES), F32)
    cmp_w2k = jnp.concatenate([jnp.concatenate([p["cmp_k_w2"], p["cmp_k_w2"][..., perm]], axis=2), zk],
                              axis=1).astype(BF16)
    cmp_w2vt = jnp.concatenate([jnp.zeros((depth, HEAD_DIM, HEAD_DIM), F32),
                                jnp.swapaxes(p["cmp_v_w2"], 1, 2)], axis=2).astype(BF16)

    row = lambda v: v[:, None, :]
    mix_vec = jnp.concatenate([p["conv_b"], p["conv_ln_g"], p["conv_ln_b"], p["pool_scale"],
                               p["sgu_ln_g"], p["sgu_ln_b"], p["post_mix_norm"], p["pre_ffn_norm"],
                               p["post_ffn_norm"]], axis=1)
    return dict(
        wa=wa, wb=wb, pre_mix=row(p["pre_mix_norm"]), mix_vec=row(mix_vec),
        conv_w=p["conv_w"], pool_bd=pool_bd.astype(BF16), sgu_w=p["sgu_w"], sgu_bias=sgu_bias,
        cmp_wa=cmp_wa, cmp_wb=cmp_wb, cmp_pe=cmp_pe, cmp_w1=cmp_w1, cmp_w2k=cmp_w2k, cmp_w2vt=cmp_w2vt,
        w_out=p["w_out"].astype(BF16), ffn_w1=p["ffn_w1"].astype(BF16), ffn_w2=p["ffn_w2"].astype(BF16),
    )


def _overlap_t(s):
    n_blk = s // CMP_STRIDE
    n_slc = s // SLC_BLOCK
    bs = jnp.arange(n_blk)[None, :] * CMP_STRIDE
    ss = jnp.arange(n_slc)[:, None] * SLC_BLOCK
    ov = jnp.clip(jnp.minimum(bs + CMP_BLOCK, ss + SLC_BLOCK) - jnp.maximum(bs, ss), 0)
    return (ov.astype(F32) / CMP_STRIDE).astype(BF16)


def _rope_tables(positions):
    bsz, s = positions.shape
    inv = (ROPE_THETA ** (-jnp.arange(ROPE_HALF, dtype=F32) * 2.0 / ROPE_DIM)).reshape(ROPE_HALF, 1)
    posf = positions.astype(F32)
    n_blk = s // CMP_STRIDE
    pos_end = posf[:, CMP_BLOCK - 1::CMP_STRIDE]
    pos_end = jnp.pad(pos_end, ((0, 0), (0, n_blk - pos_end.shape[1])))
    cos_t, sin_t = _rope_chan_table(posf[:, None, :], inv)
    cos_b, sin_b = _rope_chan_table(pos_end[:, None, :], inv)
    return cos_t, sin_t, cos_b, sin_b


def _layer(l, x, lw, tables, ov_t):
    bsz, s, d = x.shape
    za, zb = _proj(l, x, lw["pre_mix"], lw["wa"], lw["wb"])
    prep = _att_prep(l, za, zb, *tables, lw["cmp_wa"], lw["cmp_wb"],
                     lw["cmp_pe"], lw["cmp_w1"], lw["cmp_w2k"], lw["cmp_w2vt"])
    y_att = _att(*prep, ov_t)
    x2 = _mix_ffn(l, x.reshape(bsz * s, d), y_att.reshape(bsz * s, GROUP_WIDTH),
                  za.reshape(bsz * s, ZA_WIDTH), s, lw)
    return x2.reshape(bsz, s, d)


def kernel(x, positions, pre_mix_norm, post_mix_norm, pre_ffn_norm, post_ffn_norm, w_in, conv_w, conv_b, conv_ln_g, conv_ln_b, pool_w, pool_scale, sgu_ln_g, sgu_ln_b, sgu_w, sgu_b, cmp_k_pe, cmp_k_w1, cmp_k_w2, cmp_v_pe, cmp_v_w1, cmp_v_w2, w_out, ffn_w1, ffn_w2):
    params = dict(pre_mix_norm=pre_mix_norm, post_mix_norm=post_mix_norm, pre_ffn_norm=pre_ffn_norm,
                  post_ffn_norm=post_ffn_norm, w_in=w_in, conv_w=conv_w, conv_b=conv_b,
                  conv_ln_g=conv_ln_g, conv_ln_b=conv_ln_b, pool_w=pool_w, pool_scale=pool_scale,
                  sgu_ln_g=sgu_ln_g, sgu_ln_b=sgu_ln_b, sgu_w=sgu_w, sgu_b=sgu_b,
                  cmp_k_pe=cmp_k_pe, cmp_k_w1=cmp_k_w1, cmp_k_w2=cmp_k_w2, cmp_v_pe=cmp_v_pe,
                  cmp_v_w1=cmp_v_w1, cmp_v_w2=cmp_v_w2, w_out=w_out, ffn_w1=ffn_w1, ffn_w2=ffn_w2)
    bsz, s, d = x.shape
    depth = w_in.shape[0]
    tables = _rope_tables(positions)
    ov_t = _overlap_t(s)
    lw = _prep_weights(params)
    for l in range(depth):
        x = _layer(l, x, lw, tables, ov_t)
    return x
```

```python
import functools

import jax
import jax.numpy as jnp
from jax import lax
from jax.experimental import pallas as pl
from jax.experimental.pallas import tpu as pltpu

F32 = jnp.float32
BF16 = jnp.bfloat16

D_MODEL = 1024
GROUP_WIDTH = 256
CONV_WIDTH = 31
POOL_WINDOWS = (2, 4, 8, 16)
POOL_GROUP = 64
SGU_HEADS = 4
SGU_CHUNK = 128
ATT_HEADS = 4
HEAD_DIM = 64
ROPE_DIM = 16
ROPE_HALF = 8
ROPE_THETA = 500000.0
CMP_BLOCK = 32
CMP_STRIDE = 16
SLC_BLOCK = 64
N_SELECT = 8
N_LOCAL = 2
WINDOW = 512
N_BRANCH = 3
NORM_EPS = 1e-6
NEG_INF = -1e30
FORCE_SCORE = 1e9

LANES = 128
SUBLANES = 8
VMEM_LIMIT_BYTES = 56 * 1024 * 1024

PROJ_ROWS = 1024
FFN_ROWS = 512
FFN_COLS = 512
CONV_ROWS = 64
POOL_ROWS = 64
ATT_Q = 256
ATT_K = 256
ATT_STREAMS = 4
LOG2_E = 1.4426950408889634
V_ROWS = 80
MASK_BIAS = -2e30

COL_CVAL, COL_CGATE, COL_POOL, COL_SGU_U, COL_SGU_V = 0, 1, 2, 3, 4
COL_KCVC, COL_KS, COL_KW = 10, 11, 12
ZA_WIDTH = 1664
ROW_Q, ROW_VS, ROW_VW, ROW_G = 0, 256, 320, 384
ZB_ROWS = 400


def _cparams(*sem):
    return pltpu.CompilerParams(dimension_semantics=sem, vmem_limit_bytes=VMEM_LIMIT_BYTES)


def _per_layer(l, *shape, pipeline_mode=None):
    zeros = (0,) * len(shape)
    extra = {} if pipeline_mode is None else {"pipeline_mode": pipeline_mode}
    return pl.BlockSpec((None,) + shape, lambda *_: (l,) + zeros, **extra)


def _gelu(x):
    return 0.5 * x * (1.0 + jnp.tanh(0.7978845608028654 * (x + 0.044715 * (x * x * x))))


def _sigmoid(x):
    return 0.5 * jnp.tanh(0.5 * x) + 0.5


def _layer_norm(x, g, b):
    mu = jnp.mean(x, axis=-1, keepdims=True)
    d = x - mu
    var = jnp.mean(d * d, axis=-1, keepdims=True)
    return d * lax.rsqrt(var + NORM_EPS) * g + b


def _rms_norm(x, g):
    return x * lax.rsqrt(jnp.mean(x * x, axis=-1, keepdims=True) + NORM_EPS) * g


def _rope_chan_kernel(pos_ref, inv_ref, cos_ref, sin_ref):
    ang = inv_ref[...] * pos_ref[0]
    cos_ref[0] = jnp.cos(ang)
    sin_ref[0] = jnp.sin(ang)


def _rope_chan_table(pos_row, inv_col):
    b, _, s = pos_row.shape
    spec = pl.BlockSpec((1, ROPE_HALF, s), lambda i: (i, 0, 0))
    return pl.pallas_call(
        _rope_chan_kernel,
        grid=(b,),
        in_specs=[pl.BlockSpec((1, 1, s), lambda i: (i, 0, 0)),
                  pl.BlockSpec((ROPE_HALF, 1), lambda i: (0, 0))],
        out_specs=[spec, spec],
        out_shape=[jax.ShapeDtypeStruct((b, ROPE_HALF, s), F32)] * 2,
        compiler_params=_cparams("parallel"),
        name="rope_chan_table",
    )(pos_row, inv_col)


def _proj_kernel(x_ref, g_ref, wa_ref, wb_ref, za_ref, zb_ref):
    h = _rms_norm(x_ref[0], g_ref[...]).astype(BF16)
    last_dims = (((1,), (1,)), ((), ()))
    za_ref[0] = lax.dot_general(h, wa_ref[...], last_dims, preferred_element_type=F32)
    zb_ref[0] = lax.dot_general(wb_ref[...], h, last_dims, preferred_element_type=F32)


def _proj(l, x, g, wa, wb):
    b, s, d = x.shape
    tm = min(PROJ_ROWS, s)
    return pl.pallas_call(
        _proj_kernel,
        grid=(b, s // tm),
        in_specs=[pl.BlockSpec((1, tm, d), lambda i, j: (i, j, 0)),
                  _per_layer(l, 1, d), _per_layer(l, ZA_WIDTH, d), _per_layer(l, ZB_ROWS, d)],
        out_specs=[pl.BlockSpec((1, tm, ZA_WIDTH), lambda i, j: (i, j, 0)),
                   pl.BlockSpec((1, ZB_ROWS, tm), lambda i, j: (i, 0, j))],
        out_shape=[jax.ShapeDtypeStruct((b, s, ZA_WIDTH), F32),
                   jax.ShapeDtypeStruct((b, ZB_ROWS, s), F32)],
        compiler_params=_cparams("parallel", "parallel"),
        name="in_proj",
    )(x, g, wa, wb)


MIX_HALO = 32


def _conv_rows(hp_ref, r0, w_ref, b_ref, lg_ref, lb_ref):
    shift = MIX_HALO - (CONV_WIDTH - 1)
    n_win = CONV_ROWS + MIX_HALO
    acc = jnp.zeros((CONV_ROWS, GROUP_WIDTH), F32) + b_ref[...]
    win = hp_ref[r0:r0 + n_win, :]
    for r in range(SUBLANES):
        taps = [k for k in range(CONV_WIDTH) if (k + shift) % SUBLANES == r]
        wr = win if r == 0 else pltpu.roll(win, n_win - r, axis=0)
        for k in taps:
            off = k + shift - r
            acc = acc + wr[off:off + CONV_ROWS, :] * w_ref[k:k + 1, :]
    y = _layer_norm(acc, lg_ref[...], lb_ref[...])
    return y * _sigmoid(y)


POOL_PAD = 16


def _pool_rows(pp_ref, r0, seq_t0, w_ref, sc_ref):
    lane = lax.broadcasted_iota(jnp.int32, (1, GROUP_WIDTH), 1)
    grp = lane // POOL_GROUP
    win = jnp.where(grp == 0, POOL_WINDOWS[0],
                    jnp.where(grp == 1, POOL_WINDOWS[1],
                              jnp.where(grp == 2, POOL_WINDOWS[2], POOL_WINDOWS[3])))
    base = r0 + MIX_HALO - POOL_PAD
    rows = pp_ref[base:base + POOL_ROWS + POOL_PAD, :]
    p = rows[POOL_PAD:POOL_PAD + POOL_ROWS, :]
    acc = rows
    sums = []
    width = 1
    for w in POOL_WINDOWS:
        while width < w:
            acc = acc + pltpu.roll(acc, width, axis=0)
            width *= 2
        sums.append(acc[POOL_PAD:POOL_PAD + POOL_ROWS, :])
    total = jnp.where(grp == 0, sums[0],
                      jnp.where(grp == 1, sums[1],
                                jnp.where(grp == 2, sums[2], sums[3])))
    t = seq_t0 + r0 + lax.broadcasted_iota(jnp.int32, (POOL_ROWS, 1), 0)
    count = jnp.minimum(t + 1, win).astype(F32)
    mixed = total / count - p
    y = jnp.dot(mixed.astype(BF16), w_ref[...], preferred_element_type=F32)
    return y * sc_ref[...]


def _sgu_weights(w_ref):
    row = lax.broadcasted_iota(jnp.int32, (SGU_CHUNK, SGU_CHUNK), 0)
    col = lax.broadcasted_iota(jnp.int32, (SGU_CHUNK, SGU_CHUNK), 1)
    return [jnp.where(row >= col, w_ref[h], 0.0).astype(BF16) for h in range(SGU_HEADS)]


def _sgu_rows(u, v, ws, lg_ref, lb_ref, bias_ref):
    head = lax.broadcasted_iota(jnp.int32, (1, GROUP_WIDTH), 1) // (GROUP_WIDTH // SGU_HEADS)
    vb = _layer_norm(_gelu(v), lg_ref[...], lb_ref[...]).astype(BF16)
    mixed = jnp.zeros((SGU_CHUNK, GROUP_WIDTH), F32)
    for h in range(SGU_HEADS):
        mixed = jnp.where(head == h, jnp.dot(ws[h], vb, preferred_element_type=F32), mixed)
    return _gelu(u) * (mixed + bias_ref[...])


def _att_prep_kernel(kcvc_ref, ksx_ref, kwx_ref, zb_ref, cos_ref, sin_ref, cosb_ref, sinb_ref,
                     wa_ref, wb_ref, pe_ref, w1_ref, w2k_ref, w2vt_ref,
                     q_ref, ks_ref, kw_ref, vs_ref, vw_ref, kc_ref, vct_ref, g_ref, sh_ref):
    s = ksx_ref.shape[1]
    n_blk = s // CMP_STRIDE

    def tok_table(cos, sin):
        n = cos.shape[1]
        rest = HEAD_DIM - ROPE_DIM
        rows = [cos, cos, jnp.ones((rest, n), F32), -sin, sin, jnp.zeros((rest, n), F32)]
        return jnp.concatenate(rows, axis=0).T

    def rope_wide(x, cs):
        r = x * cs
        return r + pltpu.roll(r, HEAD_DIM, axis=1)

    def rope_tok(x, cs):
        return rope_wide(x, cs)[:, :HEAD_DIM]

    cos = cos_ref[0]
    sin = sin_ref[0]
    cs = tok_table(cos, sin)
    lane = lax.broadcasted_iota(jnp.int32, (s, LANES), 1)
    blk = lax.broadcasted_iota(jnp.int32, (s, LANES), 0) // SLC_BLOCK
    ks_ref[0] = jnp.where(lane < HEAD_DIM, rope_wide(ksx_ref[0], cs),
                          jnp.where(lane - HEAD_DIM == blk, 1.0, 0.0)).astype(BF16)
    kw_ref[0] = rope_tok(kwx_ref[0], cs).astype(BF16)

    scale = HEAD_DIM ** -0.5 * LOG2_E
    parts = []
    for h in range(ATT_HEADS):
        r0 = ROW_Q + h * HEAD_DIM
        x1 = zb_ref[0, r0:r0 + ROPE_HALF, :]
        x2 = zb_ref[0, r0 + ROPE_HALF:r0 + ROPE_DIM, :]
        parts += [x1 * cos - x2 * sin, x2 * cos + x1 * sin, zb_ref[0, r0 + ROPE_DIM:r0 + HEAD_DIM, :]]
    q_ref[0] = (jnp.concatenate(parts, axis=0) * scale).astype(BF16)

    ones_row = jnp.where(lax.broadcasted_iota(jnp.int32, (V_ROWS - HEAD_DIM, ATT_K), 0) == 0, 1.0, 0.0)
    for c in range(s // ATT_K):
        cols = slice(c * ATT_K, (c + 1) * ATT_K)
        vs_ref[0, c] = jnp.concatenate([zb_ref[0, ROW_VS:ROW_VS + HEAD_DIM, cols], ones_row], axis=0).astype(BF16)
        vw_ref[0, c] = jnp.concatenate([zb_ref[0, ROW_VW:ROW_VW + HEAD_DIM, cols], ones_row], axis=0).astype(BF16)
    g_ref[0] = _sigmoid(zb_ref[0, ROW_G:ROW_G + 16, :])

    acc_a = jnp.zeros((n_blk, LANES), F32)
    acc_b = jnp.zeros((n_blk, LANES), F32)
    for l in range(CMP_STRIDE):
        xl = kcvc_ref[0, pl.ds(l, n_blk, stride=CMP_STRIDE), :].astype(BF16)
        acc_a = acc_a + jnp.dot(xl, wa_ref[l], preferred_element_type=F32)
        acc_b = acc_b + jnp.dot(xl, wb_ref[l], preferred_element_type=F32)
    sh_ref[0:n_blk, :] = acc_b
    sh_ref[n_blk:n_blk + SUBLANES, :] = jnp.zeros((SUBLANES, LANES), F32)
    pe_term = jnp.sum(pe_ref[...] * w1_ref[...], axis=0, keepdims=True)
    hid = _gelu(acc_a + sh_ref[1:n_blk + 1, :] + pe_term).astype(BF16)
    kk = jnp.dot(hid, w2k_ref[...], preferred_element_type=F32)
    kc_ref[0] = rope_tok(kk, tok_table(cosb_ref[0], sinb_ref[0])).astype(BF16)
    vct_ref[0] = lax.dot_general(w2vt_ref[...], hid, (((1,), (1,)), ((), ())),
                                 preferred_element_type=F32).astype(BF16)


def _att_prep(l, za, zb, cos_t, sin_t, cos_b, sin_b, wa, wb, pe, w1, w2k, w2vt):
    bsz, s, _ = za.shape
    n_blk = s // CMP_STRIDE
    nt = s // ATT_K
    tok = lambda c: pl.BlockSpec((1, s, LANES), lambda i: (i, 0, c))
    per_b3 = lambda shp: pl.BlockSpec((1,) + shp, lambda i: (i, 0, 0))
    per_b4 = lambda shp: pl.BlockSpec((1,) + shp, lambda i: (i, 0, 0, 0))
    return pl.pallas_call(
        _att_prep_kernel,
        grid=(bsz,),
        in_specs=[tok(COL_KCVC), tok(COL_KS), tok(COL_KW),
                  per_b3((ZB_ROWS, s)),
                  per_b3((ROPE_HALF, s)), per_b3((ROPE_HALF, s)),
                  per_b3((ROPE_HALF, n_blk)), per_b3((ROPE_HALF, n_blk)),
                  _per_layer(l, CMP_STRIDE, LANES, LANES), _per_layer(l, CMP_STRIDE, LANES, LANES),
                  _per_layer(l, CMP_BLOCK * HEAD_DIM, LANES), _per_layer(l, CMP_BLOCK * HEAD_DIM, LANES),
                  _per_layer(l, LANES, LANES), _per_layer(l, HEAD_DIM, LANES)],
        out_specs=[per_b3((ATT_HEADS * HEAD_DIM, s)), per_b3((s, LANES)), per_b3((s, HEAD_DIM)),
                   per_b4((nt, V_ROWS, ATT_K)), per_b4((nt, V_ROWS, ATT_K)),
                   per_b3((n_blk, HEAD_DIM)), per_b3((HEAD_DIM, n_blk)), per_b3((16, s))],
        out_shape=[jax.ShapeDtypeStruct((bsz, ATT_HEADS * HEAD_DIM, s), BF16),
                   jax.ShapeDtypeStruct((bsz, s, LANES), BF16),
                   jax.ShapeDtypeStruct((bsz, s, HEAD_DIM), BF16),
                   jax.ShapeDtypeStruct((bsz, nt, V_ROWS, ATT_K), BF16),
                   jax.ShapeDtypeStruct((bsz, nt, V_ROWS, ATT_K), BF16),
                   jax.ShapeDtypeStruct((bsz, n_blk, HEAD_DIM), BF16),
                   jax.ShapeDtypeStruct((bsz, HEAD_DIM, n_blk), BF16),
                   jax.ShapeDtypeStruct((bsz, 16, s), F32)],
        scratch_shapes=[pltpu.VMEM((n_blk + SUBLANES, LANES), F32)],
        compiler_params=_cparams("parallel"),
        name="att_prep",
    )(za, za, za, zb, cos_t, sin_t, cos_b, sin_b, wa, wb, pe, w1, w2k, w2vt)


def _att_tile(i, q, g, wmask_ref, ks_ref, kw_ref, vs_ref, vw_ref, kc_ref, vct_ref, ov_ref,
              qa_ref, sa_ref, sb_ref, m_ref, acc_ref):
    n_blk = kc_ref.shape[1]
    n_slc = ov_ref.shape[0]
    tq = ATT_Q
    wide = ATT_HEADS * tq
    t0 = i * tq

    qs = jnp.concatenate([q[h * HEAD_DIM:(h + 1) * HEAD_DIM] for h in range(ATT_HEADS)], axis=1)
    lane_w = lax.broadcasted_iota(jnp.int32, (1, wide), 1)
    t_w = t0 + (lane_w & (tq - 1))
    t_q = t0 + lax.broadcasted_iota(jnp.int32, (1, tq), 1)
    heads = lambda a: jnp.concatenate([a] * ATT_HEADS, axis=1)

    s_c = jnp.dot(kc_ref[0], qs, preferred_element_type=F32)
    blk_end = lax.broadcasted_iota(jnp.int32, (n_blk, 1), 0) * CMP_STRIDE + (CMP_BLOCK - 1)
    cmask = blk_end <= t_w
    s_m = jnp.where(cmask, s_c, NEG_INF)
    e = jnp.exp2(s_m - jnp.max(s_m, axis=0, keepdims=True))
    p_c = jnp.where(cmask, e * (1.0 / jnp.sum(e, axis=0, keepdims=True)), 0.0)
    o_cmp = jnp.dot(vct_ref[0], p_c.astype(BF16), preferred_element_type=F32)

    p_sum = p_c[:, 0:tq]
    for h in range(1, ATT_HEADS):
        p_sum = p_sum + p_c[:, h * tq:(h + 1) * tq]
    p_hi = p_sum.astype(BF16)
    p_lo = (p_sum - p_hi.astype(F32)).astype(BF16)
    imp = (jnp.dot(ov_ref[...], p_hi, preferred_element_type=F32)
           + jnp.dot(ov_ref[...], p_lo, preferred_element_type=F32))
    j = lax.broadcasted_iota(jnp.int32, (n_slc, 1), 0)
    back = t_q // SLC_BLOCK - j
    forced = (j == 0) | ((back >= 0) & (back < N_LOCAL))
    imp = jnp.where(forced, FORCE_SCORE, jnp.where(back < 0, -1.0, imp))
    rank = jnp.zeros((n_slc, tq), F32)
    for r in range(n_slc):
        row = imp[r:r + 1, :]
        ahead = (row > imp) | ((row == imp) & (j > r))
        rank = rank + jnp.where(ahead, 1.0, 0.0)
    sel_bias = jnp.where((rank < min(N_SELECT, n_slc)) & (back >= 0), 0.0, MASK_BIAS)
    qa_ref[0:HEAD_DIM, :] = qs
    qa_ref[HEAD_DIM:HEAD_DIM + n_slc, :] = heads(sel_bias).astype(BF16)
    qa_ref[HEAD_DIM + n_slc:, :] = jnp.zeros((LANES - HEAD_DIM - n_slc, wide), BF16)

    k_iota = lax.broadcasted_iota(jnp.int32, (ATT_K, 1), 0)

    back_tiles = WINDOW // ATT_K
    first = max(i - back_tiles, 0)
    n_win = i - first + 1
    s_w = jnp.dot(kw_ref[0, first * ATT_K:(i + 1) * ATT_K, :], qs, preferred_element_type=F32)
    s_w = s_w + heads(wmask_ref[min(i, back_tiles), 0:n_win * ATT_K, :])
    p_w = jnp.exp2(s_w - jnp.max(s_w, axis=0, keepdims=True)).astype(BF16)
    acc_w = None
    for n in range(n_win):
        part = jnp.dot(vw_ref[0, first + n], p_w[n * ATT_K:(n + 1) * ATT_K], preferred_element_type=F32)
        acc_w = part if acc_w is None else acc_w + part

    m_ref[...] = jnp.full((1, wide), NEG_INF, F32)
    acc_ref[...] = jnp.zeros((V_ROWS, wide), F32)

    def scores(kt):
        s_t = jnp.dot(ks_ref[0, kt * ATT_K:(kt + 1) * ATT_K, :], qa_ref[...], preferred_element_type=F32)
        if kt == i:
            s_t = jnp.where(heads((kt * ATT_K + k_iota) <= t_q), s_t, -jnp.inf)
        return s_t

    def consume(s_ref, kt):
        m_old = m_ref[...]
        m_new = jnp.maximum(m_old, jnp.max(s_ref[...], axis=0, keepdims=True))
        alpha = jnp.exp2(m_old - m_new)
        p = jnp.exp2(s_ref[...] - m_new).astype(BF16)
        m_ref[...] = m_new
        acc_ref[...] = alpha * acc_ref[...] + jnp.dot(vs_ref[0, kt], p, preferred_element_type=F32)

    bufs = (sa_ref, sb_ref)
    bufs[0][...] = scores(0)
    for kt in range(i + 1):
        if kt < i:
            bufs[(kt + 1) % 2][...] = scores(kt + 1)
        consume(bufs[kt % 2], kt)

    acc_s = acc_ref[...]
    gate = lambda br: jnp.concatenate(
        [g[h * N_BRANCH + br:h * N_BRANCH + br + 1, :] for h in range(ATT_HEADS)], axis=1)
    l_s = acc_s[HEAD_DIM:HEAD_DIM + 1, :]
    l_w = acc_w[HEAD_DIM:HEAD_DIM + 1, :]
    o = (gate(0) * o_cmp + (gate(1) * (1.0 / l_s)) * acc_s[:HEAD_DIM]
         + (gate(2) * (1.0 / l_w)) * acc_w[:HEAD_DIM])
    return jnp.concatenate([o[:, h * tq:(h + 1) * tq] for h in range(ATT_HEADS)], axis=0).T


def _att_kernel(q_ref, g_ref, ks_ref, kw_ref, vs_ref, vw_ref, kc_ref, vct_ref, wmask_ref, ov_ref,
                o_ref, *scratch):
    n = ATT_STREAMS
    n_scr = len(scratch) // n
    n_tiles = q_ref.shape[2] // ATT_Q
    for i in range(n_tiles):
        lanes = slice(i * ATT_Q, (i + 1) * ATT_Q)
        slot = i % n
        o_ref[0, lanes, :] = _att_tile(
            i, q_ref[0, :, lanes], g_ref[0, :, lanes], wmask_ref, ks_ref, kw_ref, vs_ref, vw_ref,
            kc_ref, vct_ref, ov_ref, *scratch[slot * n_scr:(slot + 1) * n_scr])


def _att(q_t, ks, kw, vs_t, vw_t, kc, vc_t, g_t, ov_t):
    bsz, _, s = q_t.shape
    n_blk = kc.shape[1]
    nt = s // ATT_K
    n_slc = s // SLC_BLOCK
    wide = ATT_HEADS * ATT_Q
    assert ATT_Q == ATT_K, "one diagonal key tile per query tile"
    per_b3 = lambda shp: pl.BlockSpec((1,) + shp, lambda b: (b, 0, 0))
    per_b4 = lambda shp: pl.BlockSpec((1,) + shp, lambda b: (b, 0, 0, 0))
    tile_scratch = [pltpu.VMEM((LANES, wide), BF16),
                    pltpu.VMEM((ATT_K, wide), F32), pltpu.VMEM((ATT_K, wide), F32),
                    pltpu.VMEM((1, wide), F32), pltpu.VMEM((V_ROWS, wide), F32)]
    back_tiles = WINDOW // ATT_K
    span = WINDOW + ATT_Q
    tile = jnp.arange(back_tiles + 1)[:, None, None]
    key = jnp.maximum(tile - back_tiles, 0) * ATT_K + jnp.arange(span)[None, :, None]
    diff = tile * ATT_Q + jnp.arange(ATT_Q)[None, None, :] - key
    wmask = jnp.where((diff >= 0) & (diff < WINDOW), 0.0, -jnp.inf).astype(F32)
    return pl.pallas_call(
        _att_kernel,
        grid=(bsz,),
        in_specs=[per_b3((ATT_HEADS * HEAD_DIM, s)), per_b3((16, s)),
                  per_b3((s, LANES)), per_b3((s, HEAD_DIM)),
                  per_b4((nt, V_ROWS, ATT_K)), per_b4((nt, V_ROWS, ATT_K)),
                  per_b3((n_blk, HEAD_DIM)), per_b3((HEAD_DIM, n_blk)),
                  pl.BlockSpec((back_tiles + 1, span, ATT_Q), lambda b: (0, 0, 0)),
                  pl.BlockSpec((n_slc, n_blk), lambda b: (0, 0))],
        out_specs=per_b3((s, GROUP_WIDTH)),
        out_shape=jax.ShapeDtypeStruct((bsz, s, GROUP_WIDTH), F32),
        scratch_shapes=tile_scratch * ATT_STREAMS,
        compiler_params=_cparams("parallel"),
        name="sparse_attention",
    )(q_t, g_t, ks, kw, vs_t, vw_t, kc, vc_t, wmask, ov_t)


def _mix_ffn_kernel(x_ref, yatt_ref, ca_ref, cg_ref, pin_ref, su_ref, sv_ref, cah_ref, cgh_ref, ph_ref,
                    vec_ref, cw_ref, pw_ref, sw_ref, sbias_ref, wo_ref, w1_ref, w2_ref,
                    o_ref, y_ref, hp_ref, pp_ref, acc_ref, *, tiles_per_seq):
    k = pl.program_id(0)
    tm, d = x_ref.shape
    widths = [GROUP_WIDTH] * 6 + [d] * 3
    starts = [sum(widths[:n]) for n in range(len(widths))]
    (cb_ref, clg_ref, clb_ref, psc_ref, slg_ref, slb_ref, gm_ref, g1_ref, g2_ref) = [
        vec_ref.at[:, a:a + w] for a, w in zip(starts, widths)]

    @pl.when(k == 0)
    def _():
        y_ref[...] = jnp.zeros_like(y_ref)

    kk = jnp.minimum(k, pl.num_programs(0) - 2)
    seq_tile = kk % tiles_per_seq
    has_prev = seq_tile > 0

    def conv_piece(r0):
        y_ref[0, r0:r0 + CONV_ROWS, :] = _conv_rows(hp_ref, r0, cw_ref, cb_ref, clg_ref, clb_ref)
        return 0, r0

    def pool_piece(r0):
        y_ref[1, r0:r0 + POOL_ROWS, :] = _pool_rows(pp_ref, r0, seq_tile * tm, pw_ref, psc_ref)
        return 1, r0

    def sgu_piece(r0):
        rows = slice(r0, r0 + SGU_CHUNK)
        y_ref[2, rows, :] = _sgu_rows(su_ref[rows, :], sv_ref[rows, :], _sgu_weights(sw_ref),
                                      slg_ref, slb_ref, sbias_ref)
        return 2, r0

    def order_before_next_accumulate(n, r0):
        bits = pltpu.bitcast(y_ref[n, r0:r0 + SUBLANES, 0:LANES], jnp.uint32)
        zero = pltpu.bitcast(lax.shift_right_logical(bits, jnp.uint32(32)), F32)
        acc_ref[0:SUBLANES, 0:LANES] = acc_ref[0:SUBLANES, 0:LANES] + zero

    pieces = []
    for r0 in range(0, tm, CONV_ROWS):
        pieces += [functools.partial(conv_piece, r0), functools.partial(pool_piece, r0)]
        if r0 % SGU_CHUNK == 0:
            pieces.append(functools.partial(sgu_piece, r0))
    n_chunks = w1_ref.shape[1] // FFN_COLS
    per_chunk = -(-len(pieces) // (n_chunks - 1))

    acc = jnp.dot(yatt_ref[...].astype(BF16), wo_ref[3 * GROUP_WIDTH:4 * GROUP_WIDTH, :],
                  preferred_element_type=F32)
    for n in range(3):
        acc = acc + jnp.dot(y_ref[n].astype(BF16), wo_ref[n * GROUP_WIDTH:(n + 1) * GROUP_WIDTH, :],
                            preferred_element_type=F32)
    x1 = x_ref[...] + _rms_norm(acc, gm_ref[...])
    h = _rms_norm(x1, g1_ref[...]).astype(BF16)
    hp_ref[0:MIX_HALO, :] = jnp.where(has_prev, cah_ref[...] * _sigmoid(cgh_ref[...]), 0.0)
    hp_ref[MIX_HALO:, :] = ca_ref[...] * _sigmoid(cg_ref[...])
    pp_ref[0:MIX_HALO, :] = jnp.where(has_prev, ph_ref[...], 0.0)
    pp_ref[MIX_HALO:, :] = pin_ref[...]
    for c in range(n_chunks):
        cols = slice(c * FFN_COLS, (c + 1) * FFN_COLS)
        f = jnp.maximum(jnp.dot(h, w1_ref[:, cols], preferred_element_type=F32), 0.0)
        part = jnp.dot((f * f).astype(BF16), w2_ref[cols, :], preferred_element_type=F32)
        acc_ref[...] = part if c == 0 else acc_ref[...] + part
        for piece in pieces[c * per_chunk:(c + 1) * per_chunk]:
            order_before_next_accumulate(*piece())
    o_ref[...] = x1 + _rms_norm(acc_ref[...], g2_ref[...])


def _mix_ffn(l, x2, y_att, za2, seq_len, lw):
    t, d = x2.shape
    dff = lw["ffn_w1"].shape[2]
    tm = min(FFN_ROWS, seq_len)
    assert CONV_ROWS == POOL_ROWS and SGU_CHUNK % CONV_ROWS == 0 and tm % SGU_CHUNK == 0
    n_tiles = t // tm
    halo_per_tile = tm // MIX_HALO
    cur = lambda k: jnp.maximum(k - 1, 0)
    nxt = lambda k: jnp.minimum(k, n_tiles - 1)
    main = lambda col: pl.BlockSpec((tm, GROUP_WIDTH), lambda k: (nxt(k), col))
    halo = lambda col: pl.BlockSpec((MIX_HALO, GROUP_WIDTH),
                                    lambda k: (jnp.maximum(nxt(k) * halo_per_tile - 1, 0), col))
    once = lambda *shp: _per_layer(l, *shp, pipeline_mode=pl.Buffered(1))
    return pl.pallas_call(
        functools.partial(_mix_ffn_kernel, tiles_per_seq=seq_len // tm),
        grid=(n_tiles + 1,),
        in_specs=[pl.BlockSpec((tm, d), lambda k: (cur(k), 0)),
                  pl.BlockSpec((tm, GROUP_WIDTH), lambda k: (cur(k), 0)),
                  main(COL_CVAL), main(COL_CGATE), main(COL_POOL), main(COL_SGU_U), main(COL_SGU_V),
                  halo(COL_CVAL), halo(COL_CGATE), halo(COL_POOL),
                  _per_layer(l, 1, lw["mix_vec"].shape[2]),
                  _per_layer(l, CONV_WIDTH, GROUP_WIDTH), _per_layer(l, GROUP_WIDTH, GROUP_WIDTH),
                  _per_layer(l, SGU_HEADS, SGU_CHUNK, SGU_CHUNK), _per_layer(l, SGU_CHUNK, GROUP_WIDTH),
                  once(d, d), once(d, dff), once(dff, d)],
        out_specs=pl.BlockSpec((tm, d), lambda k: (cur(k), 0)),
        out_shape=jax.ShapeDtypeStruct((t, d), F32),
        scratch_shapes=[pltpu.VMEM((3, tm, GROUP_WIDTH), F32),
                        pltpu.VMEM((MIX_HALO + tm, GROUP_WIDTH), F32),
                        pltpu.VMEM((MIX_HALO + tm, GROUP_WIDTH), F32),
                        pltpu.VMEM((tm, d), F32)],
        compiler_params=_cparams("arbitrary"),
        name="mix_ffn",
    )(x2, y_att, za2, za2, za2, za2, za2, za2, za2, za2,
      lw["mix_vec"], lw["conv_w"], lw["pool_bd"], lw["sgu_w"], lw["sgu_bias"],
      lw["w_out"], lw["ffn_w1"], lw["ffn_w2"])


def _rope_perm():
    idx = list(range(HEAD_DIM))
    for c in range(ROPE_HALF):
        idx[c], idx[c + ROPE_HALF] = c + ROPE_HALF, c
    return jnp.array(idx, jnp.int32)


def _split_w_in(w_in):
    gw = GROUP_WIDTH
    edges = [0, gw, 2 * gw, 3 * gw, 4 * gw, 5 * gw, 6 * gw]
    for _ in range(6):
        edges.append(edges[-1] + HEAD_DIM)
    edges.append(edges[-1] + ATT_HEADS * N_BRANCH)
    names = ("cval", "cgate", "pool", "su", "sv", "q", "kc", "vc", "ks", "vs", "kw", "vw", "g")
    return {n: w_in[:, edges[k]:edges[k + 1], :] for k, n in enumerate(names)}


def _prep_weights(p):
    depth = p["w_in"].shape[0]
    perm = _rope_perm()
    w = _split_w_in(jnp.swapaxes(p["w_in"], 1, 2).astype(BF16))
    wa = jnp.concatenate([w["cval"], w["cgate"], w["pool"], w["su"], w["sv"], w["kc"], w["vc"],
                          w["ks"], w["ks"][:, perm], w["kw"], w["kw"][:, perm]], axis=1)
    pad = jnp.zeros((depth, ZB_ROWS - ROW_G - ATT_HEADS * N_BRANCH, D_MODEL), BF16)
    wb = jnp.concatenate([w["q"], w["vs"], w["vw"], w["g"], pad], axis=1)

    n_grp = len(POOL_WINDOWS)
    pool_bd = (p["pool_w"][:, :, :, None, :] * jnp.eye(n_grp, dtype=F32)[None, :, None, :, None])
    pool_bd = pool_bd.reshape(depth, GROUP_WIDTH, GROUP_WIDTH)
    sgu_bias = jnp.repeat(jnp.swapaxes(p["sgu_b"], 1, 2), GROUP_WIDTH // SGU_HEADS, axis=2)

    w1k = p["cmp_k_w1"].reshape(depth, CMP_BLOCK, HEAD_DIM, HEAD_DIM)
    w1v = p["cmp_v_w1"].reshape(depth, CMP_BLOCK, HEAD_DIM, HEAD_DIM)
    z = jnp.zeros((depth, CMP_STRIDE, HEAD_DIM, HEAD_DIM), F32)

    def kv_diag(a, b):
        return jnp.concatenate([jnp.concatenate([a, z], axis=3), jnp.concatenate([z, b], axis=3)], axis=2)

    cmp_wa = kv_diag(w1k[:, :CMP_STRIDE], w1v[:, :CMP_STRIDE]).astype(BF16)
    cmp_wb = kv_diag(w1k[:, CMP_STRIDE:], w1v[:, CMP_STRIDE:]).astype(BF16)
    ones = jnp.ones((1, 1, HEAD_DIM), F32)
    cmp_pe = jnp.concatenate([p["cmp_k_pe"].reshape(depth, -1, 1) * ones,
                              p["cmp_v_pe"].reshape(depth, -1, 1) * ones], axis=2)
    cmp_w1 = jnp.concatenate([p["cmp_k_w1"], p["cmp_v_w1"]], axis=2)
    zk = jnp.zeros((depth, HEAD_DIM, LANES), F32)
    cmp_w2k = jnp.concatenate([jnp.concatenate([p["cmp_k_w2"], p["cmp_k_w2"][..., perm]], axis=2), zk],
                              axis=1).astype(BF16)
    cmp_w2vt = jnp.concatenate([jnp.zeros((depth, HEAD_DIM, HEAD_DIM), F32),
                                jnp.swapaxes(p["cmp_v_w2"], 1, 2)], axis=2).astype(BF16)

    row = lambda v: v[:, None, :]
    mix_vec = jnp.concatenate([p["conv_b"], p["conv_ln_g"], p["conv_ln_b"], p["pool_scale"],
                               p["sgu_ln_g"], p["sgu_ln_b"], p["post_mix_norm"], p["pre_ffn_norm"],
                               p["post_ffn_norm"]], axis=1)
    return dict(
        wa=wa, wb=wb, pre_mix=row(p["pre_mix_norm"]), mix_vec=row(mix_vec),
        conv_w=p["conv_w"], pool_bd=pool_bd.astype(BF16), sgu_w=p["sgu_w"], sgu_bias=sgu_bias,
        cmp_wa=cmp_wa, cmp_wb=cmp_wb, cmp_pe=cmp_pe, cmp_w1=cmp_w1, cmp_w2k=cmp_w2k, cmp_w2vt=cmp_w2vt,
        w_out=p["w_out"].astype(BF16), ffn_w1=p["ffn_w1"].astype(BF16), ffn_w2=p["ffn_w2"].astype(BF16),
    )


def _overlap_t(s):
    n_blk = s // CMP_STRIDE
    n_slc = s // SLC_BLOCK
    bs = jnp.arange(n_blk)[None, :] * CMP_STRIDE
    ss = jnp.arange(n_slc)[:, None] * SLC_BLOCK
    ov = jnp.clip(jnp.minimum(bs + CMP_BLOCK, ss + SLC_BLOCK) - jnp.maximum(bs, ss), 0)
    return (ov.astype(F32) / CMP_STRIDE).astype(BF16)


def _rope_tables(positions):
    bsz, s = positions.shape
    inv = (ROPE_THETA ** (-jnp.arange(ROPE_HALF, dtype=F32) * 2.0 / ROPE_DIM)).reshape(ROPE_HALF, 1)
    posf = positions.astype(F32)
    n_blk = s // CMP_STRIDE
    pos_end = posf[:, CMP_BLOCK - 1::CMP_STRIDE]
    pos_end = jnp.pad(pos_end, ((0, 0), (0, n_blk - pos_end.shape[1])))
    cos_t, sin_t = _rope_chan_table(posf[:, None, :], inv)
    cos_b, sin_b = _rope_chan_table(pos_end[:, None, :], inv)
    return cos_t, sin_t, cos_b, sin_b


def _layer(l, x, lw, tables, ov_t):
    bsz, s, d = x.shape
    za, zb = _proj(l, x, lw["pre_mix"], lw["wa"], lw["wb"])
    prep = _att_prep(l, za, zb, *tables, lw["cmp_wa"], lw["cmp_wb"],
                     lw["cmp_pe"], lw["cmp_w1"], lw["cmp_w2k"], lw["cmp_w2vt"])
    y_att = _att(*prep, ov_t)
    x2 = _mix_ffn(l, x.reshape(bsz * s, d), y_att.reshape(bsz * s, GROUP_WIDTH),
                  za.reshape(bsz * s, ZA_WIDTH), s, lw)
    return x2.reshape(bsz, s, d)


def kernel(x, positions, pre_mix_norm, post_mix_norm, pre_ffn_norm, post_ffn_norm, w_in, conv_w, conv_b, conv_ln_g, conv_ln_b, pool_w, pool_scale, sgu_ln_g, sgu_ln_b, sgu_w, sgu_b, cmp_k_pe, cmp_k_w1, cmp_k_w2, cmp_v_pe, cmp_v_w1, cmp_v_w2, w_out, ffn_w1, ffn_w2):
    params = dict(pre_mix_norm=pre_mix_norm, post_mix_norm=post_mix_norm, pre_ffn_norm=pre_ffn_norm,
                  post_ffn_norm=post_ffn_norm, w_in=w_in, conv_w=conv_w, conv_b=conv_b,
                  conv_ln_g=conv_ln_g, conv_ln_b=conv_ln_b, pool_w=pool_w, pool_scale=pool_scale,
                  sgu_ln_g=sgu_ln_g, sgu_ln_b=sgu_ln_b, sgu_w=sgu_w, sgu_b=sgu_b,
                  cmp_k_pe=cmp_k_pe, cmp_k_w1=cmp_k_w1, cmp_k_w2=cmp_k_w2, cmp_v_pe=cmp_v_pe,
                  cmp_v_w1=cmp_v_w1, cmp_v_w2=cmp_v_w2, w_out=w_out, ffn_w1=ffn_w1, ffn_w2=ffn_w2)
    bsz, s, d = x.shape
    depth = w_in.shape[0]
    tables = _rope_tables(positions)
    ov_t = _overlap_t(s)
    lw = _prep_weights(params)
    for l in range(depth):
        x = _layer(l, x, lw, tables, ov_t)
    return x
```

```python
import functools

import jax
import jax.numpy as jnp
from jax import lax
from jax.experimental import pallas as pl
from jax.experimental.pallas import tpu as pltpu

F32 = jnp.float32
BF16 = jnp.bfloat16

D_MODEL = 1024
GROUP_WIDTH = 256
CONV_WIDTH = 31
POOL_WINDOWS = (2, 4, 8, 16)
POOL_GROUP = 64
SGU_HEADS = 4
SGU_CHUNK = 128
ATT_HEADS = 4
HEAD_DIM = 64
ROPE_DIM = 16
ROPE_HALF = 8
ROPE_THETA = 500000.0
CMP_BLOCK = 32
CMP_STRIDE = 16
SLC_BLOCK = 64
N_SELECT = 8
N_LOCAL = 2
WINDOW = 512
N_BRANCH = 3
NORM_EPS = 1e-6
NEG_INF = -1e30
FORCE_SCORE = 1e9

LANES = 128
SUBLANES = 8
VMEM_LIMIT_BYTES = 56 * 1024 * 1024

PROJ_ROWS = 1024
FFN_ROWS = 512
FFN_COLS = 512
CONV_ROWS = 64
POOL_ROWS = 64
ATT_Q = 256
ATT_K = 256
ATT_STREAMS = 4
LOG2_E = 1.4426950408889634
V_ROWS = 80
MASK_BIAS = -2e30

COL_CVAL, COL_CGATE, COL_POOL, COL_SGU_U, COL_SGU_V = 0, 1, 2, 3, 4
COL_KCVC, COL_KS, COL_KW = 10, 11, 12
ZA_WIDTH = 1664
ROW_Q, ROW_VS, ROW_VW, ROW_G = 0, 256, 320, 384
ZB_ROWS = 400


def _cparams(*sem):
    return pltpu.CompilerParams(dimension_semantics=sem, vmem_limit_bytes=VMEM_LIMIT_BYTES)


def _per_layer(l, *shape, pipeline_mode=None):
    zeros = (0,) * len(shape)
    extra = {} if pipeline_mode is None else {"pipeline_mode": pipeline_mode}
    return pl.BlockSpec((None,) + shape, lambda *_: (l,) + zeros, **extra)


def _gelu(x):
    return 0.5 * x * (1.0 + jnp.tanh(0.7978845608028654 * (x + 0.044715 * (x * x * x))))


def _sigmoid(x):
    return 0.5 * jnp.tanh(0.5 * x) + 0.5


def _layer_norm(x, g, b):
    mu = jnp.mean(x, axis=-1, keepdims=True)
    d = x - mu
    var = jnp.mean(d * d, axis=-1, keepdims=True)
    return d * lax.rsqrt(var + NORM_EPS) * g + b


def _rms_norm(x, g):
    return x * lax.rsqrt(jnp.mean(x * x, axis=-1, keepdims=True) + NORM_EPS) * g


def _rope_chan_kernel(pos_ref, inv_ref, cos_ref, sin_ref):
    ang = inv_ref[...] * pos_ref[0]
    cos_ref[0] = jnp.cos(ang)
    sin_ref[0] = jnp.sin(ang)


def _rope_chan_table(pos_row, inv_col):
    b, _, s = pos_row.shape
    spec = pl.BlockSpec((1, ROPE_HALF, s), lambda i: (i, 0, 0))
    return pl.pallas_call(
        _rope_chan_kernel,
        grid=(b,),
        in_specs=[pl.BlockSpec((1, 1, s), lambda i: (i, 0, 0)),
                  pl.BlockSpec((ROPE_HALF, 1), lambda i: (0, 0))],
        out_specs=[spec, spec],
        out_shape=[jax.ShapeDtypeStruct((b, ROPE_HALF, s), F32)] * 2,
        compiler_params=_cparams("parallel"),
        name="rope_chan_table",
    )(pos_row, inv_col)


MIX_HALO = 32


def _conv_rows(hp_ref, r0, w_ref, b_ref, lg_ref, lb_ref):
    shift = MIX_HALO - (CONV_WIDTH - 1)
    n_win = CONV_ROWS + MIX_HALO
    acc = jnp.zeros((CONV_ROWS, GROUP_WIDTH), F32) + b_ref[...]
    win = hp_ref[r0:r0 + n_win, :]
    for r in range(SUBLANES):
        taps = [k for k in range(CONV_WIDTH) if (k + shift) % SUBLANES == r]
        wr = win if r == 0 else pltpu.roll(win, n_win - r, axis=0)
        for k in taps:
            off = k + shift - r
            acc = acc + wr[off:off + CONV_ROWS, :] * w_ref[k:k + 1, :]
    y = _layer_norm(acc, lg_ref[...], lb_ref[...])
    return y * _sigmoid(y)


POOL_PAD = 16


def _pool_rows(pp_ref, r0, seq_t0, w_ref, sc_ref):
    lane = lax.broadcasted_iota(jnp.int32, (1, GROUP_WIDTH), 1)
    grp = lane // POOL_GROUP
    win = jnp.where(grp == 0, POOL_WINDOWS[0],
                    jnp.where(grp == 1, POOL_WINDOWS[1],
                              jnp.where(grp == 2, POOL_WINDOWS[2], POOL_WINDOWS[3])))
    base = r0 + MIX_HALO - POOL_PAD
    rows = pp_ref[base:base + POOL_ROWS + POOL_PAD, :]
    p = rows[POOL_PAD:POOL_PAD + POOL_ROWS, :]
    acc = rows
    sums = []
    width = 1
    for w in POOL_WINDOWS:
        while width < w:
            acc = acc + pltpu.roll(acc, width, axis=0)
            width *= 2
        sums.append(acc[POOL_PAD:POOL_PAD + POOL_ROWS, :])
    total = jnp.where(grp == 0, sums[0],
                      jnp.where(grp == 1, sums[1],
                                jnp.where(grp == 2, sums[2], sums[3])))
    t = seq_t0 + r0 + lax.broadcasted_iota(jnp.int32, (POOL_ROWS, 1), 0)
    count = jnp.minimum(t + 1, win).astype(F32)
    mixed = total / count - p
    y = jnp.dot(mixed.astype(BF16), w_ref[...], preferred_element_type=F32)
    return y * sc_ref[...]


def _sgu_weights(w_ref):
    row = lax.broadcasted_iota(jnp.int32, (SGU_CHUNK, SGU_CHUNK), 0)
    col = lax.broadcasted_iota(jnp.int32, (SGU_CHUNK, SGU_CHUNK), 1)
    return [jnp.where(row >= col, w_ref[h], 0.0).astype(BF16) for h in range(SGU_HEADS)]


def _sgu_rows(u, v, ws, lg_ref, lb_ref, bias_ref):
    head = lax.broadcasted_iota(jnp.int32, (1, GROUP_WIDTH), 1) // (GROUP_WIDTH // SGU_HEADS)
    vb = _layer_norm(_gelu(v), lg_ref[...], lb_ref[...]).astype(BF16)
    mixed = jnp.zeros((SGU_CHUNK, GROUP_WIDTH), F32)
    for h in range(SGU_HEADS):
        mixed = jnp.where(head == h, jnp.dot(ws[h], vb, preferred_element_type=F32), mixed)
    return _gelu(u) * (mixed + bias_ref[...])


def _att_prep_kernel(kcvc_ref, ksx_ref, kwx_ref, zb_ref, cos_ref, sin_ref, cosb_ref, sinb_ref,
                     wa_ref, wb_ref, pe_ref, w1_ref, w2k_ref, w2vt_ref,
                     q_ref, ks_ref, kw_ref, vs_ref, vw_ref, kc_ref, vct_ref, g_ref, sh_ref):
    s = ksx_ref.shape[1]
    n_blk = s // CMP_STRIDE

    def tok_table(cos, sin):
        n = cos.shape[1]
        rest = HEAD_DIM - ROPE_DIM
        rows = [cos, cos, jnp.ones((rest, n), F32), -sin, sin, jnp.zeros((rest, n), F32)]
        return jnp.concatenate(rows, axis=0).T

    def rope_wide(x, cs):
        r = x * cs
        return r + pltpu.roll(r, HEAD_DIM, axis=1)

    def rope_tok(x, cs):
        return rope_wide(x, cs)[:, :HEAD_DIM]

    cos = cos_ref[0]
    sin = sin_ref[0]
    cs = tok_table(cos, sin)
    lane = lax.broadcasted_iota(jnp.int32, (s, LANES), 1)
    blk = lax.broadcasted_iota(jnp.int32, (s, LANES), 0) // SLC_BLOCK
    ks_ref[0] = jnp.where(lane < HEAD_DIM, rope_wide(ksx_ref[0], cs),
                          jnp.where(lane - HEAD_DIM == blk, 1.0, 0.0)).astype(BF16)
    kw_ref[0] = rope_tok(kwx_ref[0], cs).astype(BF16)

    scale = HEAD_DIM ** -0.5 * LOG2_E
    parts = []
    for h in range(ATT_HEADS):
        r0 = ROW_Q + h * HEAD_DIM
        x1 = zb_ref[0, r0:r0 + ROPE_HALF, :]
        x2 = zb_ref[0, r0 + ROPE_HALF:r0 + ROPE_DIM, :]
        parts += [x1 * cos - x2 * sin, x2 * cos + x1 * sin, zb_ref[0, r0 + ROPE_DIM:r0 + HEAD_DIM, :]]
    q_ref[0] = (jnp.concatenate(parts, axis=0) * scale).astype(BF16)

    ones_row = jnp.where(lax.broadcasted_iota(jnp.int32, (V_ROWS - HEAD_DIM, ATT_K), 0) == 0, 1.0, 0.0)
    for c in range(s // ATT_K):
        cols = slice(c * ATT_K, (c + 1) * ATT_K)
        vs_ref[0, c] = jnp.concatenate([zb_ref[0, ROW_VS:ROW_VS + HEAD_DIM, cols], ones_row], axis=0).astype(BF16)
        vw_ref[0, c] = jnp.concatenate([zb_ref[0, ROW_VW:ROW_VW + HEAD_DIM, cols], ones_row], axis=0).astype(BF16)
    g_ref[0] = _sigmoid(zb_ref[0, ROW_G:ROW_G + 16, :])

    acc_a = jnp.zeros((n_blk, LANES), F32)
    acc_b = jnp.zeros((n_blk, LANES), F32)
    for l in range(CMP_STRIDE):
        xl = kcvc_ref[0, pl.ds(l, n_blk, stride=CMP_STRIDE), :].astype(BF16)
        acc_a = acc_a + jnp.dot(xl, wa_ref[l], preferred_element_type=F32)
        acc_b = acc_b + jnp.dot(xl, wb_ref[l], preferred_element_type=F32)
    sh_ref[0:n_blk, :] = acc_b
    sh_ref[n_blk:n_blk + SUBLANES, :] = jnp.zeros((SUBLANES, LANES), F32)
    pe_term = jnp.sum(pe_ref[...] * w1_ref[...], axis=0, keepdims=True)
    hid = _gelu(acc_a + sh_ref[1:n_blk + 1, :] + pe_term).astype(BF16)
    kk = jnp.dot(hid, w2k_ref[...], preferred_element_type=F32)
    kc_ref[0] = rope_tok(kk, tok_table(cosb_ref[0], sinb_ref[0])).astype(BF16)
    vct_ref[0] = lax.dot_general(w2vt_ref[...], hid, (((1,), (1,)), ((), ())),
                                 preferred_element_type=F32).astype(BF16)


def _proj_prep_kernel(x_ref, g_ref, wa_ref, wb_ref, cos_ref, sin_ref, cosb_ref, sinb_ref,
                      cwa_ref, cwb_ref, pe_ref, w1_ref, w2k_ref, w2vt_ref,
                      za_ref, q_ref, ks_ref, kw_ref, vs_ref, vw_ref, kc_ref, vct_ref, gate_ref,
                      kcvc_s, ksx_s, kwx_s, zb_s, sh_ref, *, row_tiles):
    j = pl.program_id(1)
    tm = x_ref.shape[1]
    h = _rms_norm(x_ref[0], g_ref[...]).astype(BF16)
    last_dims = (((1,), (1,)), ((), ()))
    z = lax.dot_general(h, wa_ref[...], last_dims, preferred_element_type=F32)
    local = COL_KCVC * LANES
    za_ref[0] = z[:, :local]
    rows = pl.ds(pl.multiple_of(j * tm, tm), tm)
    kcvc_s[0, rows, :] = z[:, local:local + LANES]
    ksx_s[0, rows, :] = z[:, local + LANES:local + 2 * LANES]
    kwx_s[0, rows, :] = z[:, local + 2 * LANES:local + 3 * LANES]
    zb = lax.dot_general(wb_ref[...], h, last_dims, preferred_element_type=F32)
    for jj in range(row_tiles):
        @pl.when(j == jj)
        def _():
            zb_s[0, :, jj * tm:(jj + 1) * tm] = zb

    @pl.when(j == row_tiles - 1)
    def _():
        _att_prep_kernel(kcvc_s, ksx_s, kwx_s, zb_s, cos_ref, sin_ref, cosb_ref, sinb_ref,
                         cwa_ref, cwb_ref, pe_ref, w1_ref, w2k_ref, w2vt_ref,
                         q_ref, ks_ref, kw_ref, vs_ref, vw_ref, kc_ref, vct_ref, gate_ref, sh_ref)


def _proj_prep(l, x, g, wa, wb, cos_t, sin_t, cos_b, sin_b, cwa, cwb, pe, w1, w2k, w2vt):
    bsz, s, d = x.shape
    tm = min(PROJ_ROWS, s)
    n_blk = s // CMP_STRIDE
    nt = s // ATT_K
    local = COL_KCVC * LANES
    per_b3 = lambda shp: pl.BlockSpec((1,) + shp, lambda i, j: (i, 0, 0))
    per_b4 = lambda shp: pl.BlockSpec((1,) + shp, lambda i, j: (i, 0, 0, 0))
    return pl.pallas_call(
        functools.partial(_proj_prep_kernel, row_tiles=s // tm),
        grid=(bsz, s // tm),
        in_specs=[pl.BlockSpec((1, tm, d), lambda i, j: (i, j, 0)),
                  _per_layer(l, 1, d), _per_layer(l, ZA_WIDTH, d), _per_layer(l, ZB_ROWS, d),
                  per_b3((ROPE_HALF, s)), per_b3((ROPE_HALF, s)),
                  per_b3((ROPE_HALF, n_blk)), per_b3((ROPE_HALF, n_blk)),
                  _per_layer(l, CMP_STRIDE, LANES, LANES), _per_layer(l, CMP_STRIDE, LANES, LANES),
                  _per_layer(l, CMP_BLOCK * HEAD_DIM, LANES), _per_layer(l, CMP_BLOCK * HEAD_DIM, LANES),
                  _per_layer(l, LANES, LANES), _per_layer(l, HEAD_DIM, LANES)],
        out_specs=[pl.BlockSpec((1, tm, local), lambda i, j: (i, j, 0)),
                   per_b3((ATT_HEADS * HEAD_DIM, s)), per_b3((s, LANES)), per_b3((s, HEAD_DIM)),
                   per_b4((nt, V_ROWS, ATT_K)), per_b4((nt, V_ROWS, ATT_K)),
                   per_b3((n_blk, HEAD_DIM)), per_b3((HEAD_DIM, n_blk)), per_b3((16, s))],
        out_shape=[jax.ShapeDtypeStruct((bsz, s, local), F32),
                   jax.ShapeDtypeStruct((bsz, ATT_HEADS * HEAD_DIM, s), BF16),
                   jax.ShapeDtypeStruct((bsz, s, LANES), BF16),
                   jax.ShapeDtypeStruct((bsz, s, HEAD_DIM), BF16),
                   jax.ShapeDtypeStruct((bsz, nt, V_ROWS, ATT_K), BF16),
                   jax.ShapeDtypeStruct((bsz, nt, V_ROWS, ATT_K), BF16),
                   jax.ShapeDtypeStruct((bsz, n_blk, HEAD_DIM), BF16),
                   jax.ShapeDtypeStruct((bsz, HEAD_DIM, n_blk), BF16),
                   jax.ShapeDtypeStruct((bsz, 16, s), F32)],
        scratch_shapes=[pltpu.VMEM((1, s, LANES), F32), pltpu.VMEM((1, s, LANES), F32),
                        pltpu.VMEM((1, s, LANES), F32), pltpu.VMEM((1, ZB_ROWS, s), F32),
                        pltpu.VMEM((n_blk + SUBLANES, LANES), F32)],
        compiler_params=_cparams("parallel", "arbitrary"),
        name="proj_prep",
    )(x, g, wa, wb, cos_t, sin_t, cos_b, sin_b, cwa, cwb, pe, w1, w2k, w2vt)


def _att_tile(i, q, g, wmask_ref, ks_ref, kw_ref, vs_ref, vw_ref, kc_ref, vct_ref, ov_ref,
              qa_ref, sa_ref, sb_ref, m_ref, acc_ref):
    n_blk = kc_ref.shape[1]
    n_slc = ov_ref.shape[0]
    tq = ATT_Q
    wide = ATT_HEADS * tq
    t0 = i * tq

    qs = jnp.concatenate([q[h * HEAD_DIM:(h + 1) * HEAD_DIM] for h in range(ATT_HEADS)], axis=1)
    lane_w = lax.broadcasted_iota(jnp.int32, (1, wide), 1)
    t_w = t0 + (lane_w & (tq - 1))
    t_q = t0 + lax.broadcasted_iota(jnp.int32, (1, tq), 1)
    heads = lambda a: jnp.concatenate([a] * ATT_HEADS, axis=1)

    s_c = jnp.dot(kc_ref[0], qs, preferred_element_type=F32)
    blk_end = lax.broadcasted_iota(jnp.int32, (n_blk, 1), 0) * CMP_STRIDE + (CMP_BLOCK - 1)
    cmask = blk_end <= t_w
    s_m = jnp.where(cmask, s_c, NEG_INF)
    e = jnp.exp2(s_m - jnp.max(s_m, axis=0, keepdims=True))
    p_c = jnp.where(cmask, e * (1.0 / jnp.sum(e, axis=0, keepdims=True)), 0.0)
    o_cmp = jnp.dot(vct_ref[0], p_c.astype(BF16), preferred_element_type=F32)

    p_sum = p_c[:, 0:tq]
    for h in range(1, ATT_HEADS):
        p_sum = p_sum + p_c[:, h * tq:(h + 1) * tq]
    p_hi = p_sum.astype(BF16)
    p_lo = (p_sum - p_hi.astype(F32)).astype(BF16)
    imp = (jnp.dot(ov_ref[...], p_hi, preferred_element_type=F32)
           + jnp.dot(ov_ref[...], p_lo, preferred_element_type=F32))
    j = lax.broadcasted_iota(jnp.int32, (n_slc, 1), 0)
    back = t_q // SLC_BLOCK - j
    forced = (j == 0) | ((back >= 0) & (back < N_LOCAL))
    imp = jnp.where(forced, FORCE_SCORE, jnp.where(back < 0, -1.0, imp))
    rank = jnp.zeros((n_slc, tq), F32)
    for r in range(n_slc):
        row = imp[r:r + 1, :]
        ahead = (row > imp) | ((row == imp) & (j > r))
        rank = rank + jnp.where(ahead, 1.0, 0.0)
    sel_bias = jnp.where((rank < min(N_SELECT, n_slc)) & (back >= 0), 0.0, MASK_BIAS)
    qa_ref[0:HEAD_DIM, :] = qs
    qa_ref[HEAD_DIM:HEAD_DIM + n_slc, :] = heads(sel_bias).astype(BF16)
    qa_ref[HEAD_DIM + n_slc:, :] = jnp.zeros((LANES - HEAD_DIM - n_slc, wide), BF16)

    k_iota = lax.broadcasted_iota(jnp.int32, (ATT_K, 1), 0)

    back_tiles = WINDOW // ATT_K
    first = max(i - back_tiles, 0)
    n_win = i - first + 1
    s_w = jnp.dot(kw_ref[0, first * ATT_K:(i + 1) * ATT_K, :], qs, preferred_element_type=F32)
    s_w = s_w + heads(wmask_ref[min(i, back_tiles), 0:n_win * ATT_K, :])
    p_w = jnp.exp2(s_w - jnp.max(s_w, axis=0, keepdims=True)).astype(BF16)
    acc_w = None
    for n in range(n_win):
        part = jnp.dot(vw_ref[0, first + n], p_w[n * ATT_K:(n + 1) * ATT_K], preferred_element_type=F32)
        acc_w = part if acc_w is None else acc_w + part

    m_ref[...] = jnp.full((1, wide), NEG_INF, F32)
    acc_ref[...] = jnp.zeros((V_ROWS, wide), F32)

    def scores(kt):
        s_t = jnp.dot(ks_ref[0, kt * ATT_K:(kt + 1) * ATT_K, :], qa_ref[...], preferred_element_type=F32)
        if kt == i:
            s_t = jnp.where(heads((kt * ATT_K + k_iota) <= t_q), s_t, -jnp.inf)
        return s_t

    def consume(s_ref, kt):
        m_old = m_ref[...]
        m_new = jnp.maximum(m_old, jnp.max(s_ref[...], axis=0, keepdims=True))
        alpha = jnp.exp2(m_old - m_new)
        p = jnp.exp2(s_ref[...] - m_new).astype(BF16)
        m_ref[...] = m_new
        acc_ref[...] = alpha * acc_ref[...] + jnp.dot(vs_ref[0, kt], p, preferred_element_type=F32)

    bufs = (sa_ref, sb_ref)
    bufs[0][...] = scores(0)
    for kt in range(i + 1):
        if kt < i:
            bufs[(kt + 1) % 2][...] = scores(kt + 1)
        consume(bufs[kt % 2], kt)

    acc_s = acc_ref[...]
    gate = lambda br: jnp.concatenate(
        [g[h * N_BRANCH + br:h * N_BRANCH + br + 1, :] for h in range(ATT_HEADS)], axis=1)
    l_s = acc_s[HEAD_DIM:HEAD_DIM + 1, :]
    l_w = acc_w[HEAD_DIM:HEAD_DIM + 1, :]
    o = (gate(0) * o_cmp + (gate(1) * (1.0 / l_s)) * acc_s[:HEAD_DIM]
         + (gate(2) * (1.0 / l_w)) * acc_w[:HEAD_DIM])
    return jnp.concatenate([o[:, h * tq:(h + 1) * tq] for h in range(ATT_HEADS)], axis=0).T


def _att_kernel(q_ref, g_ref, ks_ref, kw_ref, vs_ref, vw_ref, kc_ref, vct_ref, wmask_ref, ov_ref,
                o_ref, *scratch):
    n = ATT_STREAMS
    n_scr = len(scratch) // n
    n_tiles = q_ref.shape[2] // ATT_Q
    for i in range(n_tiles):
        lanes = slice(i * ATT_Q, (i + 1) * ATT_Q)
        slot = i % n
        o_ref[0, lanes, :] = _att_tile(
            i, q_ref[0, :, lanes], g_ref[0, :, lanes], wmask_ref, ks_ref, kw_ref, vs_ref, vw_ref,
            kc_ref, vct_ref, ov_ref, *scratch[slot * n_scr:(slot + 1) * n_scr])


def _att(q_t, ks, kw, vs_t, vw_t, kc, vc_t, g_t, ov_t):
    bsz, _, s = q_t.shape
    n_blk = kc.shape[1]
    nt = s // ATT_K
    n_slc = s // SLC_BLOCK
    wide = ATT_HEADS * ATT_Q
    assert ATT_Q == ATT_K, "one diagonal key tile per query tile"
    per_b3 = lambda shp: pl.BlockSpec((1,) + shp, lambda b: (b, 0, 0))
    per_b4 = lambda shp: pl.BlockSpec((1,) + shp, lambda b: (b, 0, 0, 0))
    tile_scratch = [pltpu.VMEM((LANES, wide), BF16),
                    pltpu.VMEM((ATT_K, wide), F32), pltpu.VMEM((ATT_K, wide), F32),
                    pltpu.VMEM((1, wide), F32), pltpu.VMEM((V_ROWS, wide), F32)]
    back_tiles = WINDOW // ATT_K
    span = WINDOW + ATT_Q
    tile = jnp.arange(back_tiles + 1)[:, None, None]
    key = jnp.maximum(tile - back_tiles, 0) * ATT_K + jnp.arange(span)[None, :, None]
    diff = tile * ATT_Q + jnp.arange(ATT_Q)[None, None, :] - key
    wmask = jnp.where((diff >= 0) & (diff < WINDOW), 0.0, -jnp.inf).astype(F32)
    return pl.pallas_call(
        _att_kernel,
        grid=(bsz,),
        in_specs=[per_b3((ATT_HEADS * HEAD_DIM, s)), per_b3((16, s)),
                  per_b3((s, LANES)), per_b3((s, HEAD_DIM)),
                  per_b4((nt, V_ROWS, ATT_K)), per_b4((nt, V_ROWS, ATT_K)),
                  per_b3((n_blk, HEAD_DIM)), per_b3((HEAD_DIM, n_blk)),
                  pl.BlockSpec((back_tiles + 1, span, ATT_Q), lambda b: (0, 0, 0)),
                  pl.BlockSpec((n_slc, n_blk), lambda b: (0, 0))],
        out_specs=per_b3((s, GROUP_WIDTH)),
        out_shape=jax.ShapeDtypeStruct((bsz, s, GROUP_WIDTH), F32),
        scratch_shapes=tile_scratch * ATT_STREAMS,
        compiler_params=_cparams("parallel"),
        name="sparse_attention",
    )(q_t, g_t, ks, kw, vs_t, vw_t, kc, vc_t, wmask, ov_t)


def _mix_ffn_kernel(x_ref, yatt_ref, ca_ref, cg_ref, pin_ref, su_ref, sv_ref, cah_ref, cgh_ref, ph_ref,
                    vec_ref, cw_ref, pw_ref, sw_ref, sbias_ref, wo_ref, w1_ref, w2_ref,
                    o_ref, y_ref, hp_ref, pp_ref, acc_ref, *, tiles_per_seq):
    k = pl.program_id(0)
    tm, d = x_ref.shape
    widths = [GROUP_WIDTH] * 6 + [d] * 3
    starts = [sum(widths[:n]) for n in range(len(widths))]
    (cb_ref, clg_ref, clb_ref, psc_ref, slg_ref, slb_ref, gm_ref, g1_ref, g2_ref) = [
        vec_ref.at[:, a:a + w] for a, w in zip(starts, widths)]

    @pl.when(k == 0)
    def _():
        y_ref[...] = jnp.zeros_like(y_ref)

    kk = jnp.minimum(k, pl.num_programs(0) - 2)
    seq_tile = kk % tiles_per_seq
    has_prev = seq_tile > 0

    def conv_piece(r0):
        y_ref[0, r0:r0 + CONV_ROWS, :] = _conv_rows(hp_ref, r0, cw_ref, cb_ref, clg_ref, clb_ref)
        return 0, r0

    def pool_piece(r0):
        y_ref[1, r0:r0 + POOL_ROWS, :] = _pool_rows(pp_ref, r0, seq_tile * tm, pw_ref, psc_ref)
        return 1, r0

    def sgu_piece(r0):
        rows = slice(r0, r0 + SGU_CHUNK)
        y_ref[2, rows, :] = _sgu_rows(su_ref[rows, :], sv_ref[rows, :], _sgu_weights(sw_ref),
                                      slg_ref, slb_ref, sbias_ref)
        return 2, r0

    def order_before_next_accumulate(n, r0):
        bits = pltpu.bitcast(y_ref[n, r0:r0 + SUBLANES, 0:LANES], jnp.uint32)
        zero = pltpu.bitcast(lax.shift_right_logical(bits, jnp.uint32(32)), F32)
        acc_ref[0:SUBLANES, 0:LANES] = acc_ref[0:SUBLANES, 0:LANES] + zero

    pieces = []
    for r0 in range(0, tm, CONV_ROWS):
        pieces += [functools.partial(conv_piece, r0), functools.partial(pool_piece, r0)]
        if r0 % SGU_CHUNK == 0:
            pieces.append(functools.partial(sgu_piece, r0))
    n_chunks = w1_ref.shape[1] // FFN_COLS
    per_chunk = -(-len(pieces) // (n_chunks - 1))

    acc = jnp.dot(yatt_ref[...].astype(BF16), wo_ref[3 * GROUP_WIDTH:4 * GROUP_WIDTH, :],
                  preferred_element_type=F32)
    for n in range(3):
        acc = acc + jnp.dot(y_ref[n].astype(BF16), wo_ref[n * GROUP_WIDTH:(n + 1) * GROUP_WIDTH, :],
                            preferred_element_type=F32)
    x1 = x_ref[...] + _rms_norm(acc, gm_ref[...])
    h = _rms_norm(x1, g1_ref[...]).astype(BF16)
    hp_ref[0:MIX_HALO, :] = jnp.where(has_prev, cah_ref[...] * _sigmoid(cgh_ref[...]), 0.0)
    hp_ref[MIX_HALO:, :] = ca_ref[...] * _sigmoid(cg_ref[...])
    pp_ref[0:MIX_HALO, :] = jnp.where(has_prev, ph_ref[...], 0.0)
    pp_ref[MIX_HALO:, :] = pin_ref[...]
    for c in range(n_chunks):
        cols = slice(c * FFN_COLS, (c + 1) * FFN_COLS)
        f = jnp.maximum(jnp.dot(h, w1_ref[:, cols], preferred_element_type=F32), 0.0)
        part = jnp.dot((f * f).astype(BF16), w2_ref[cols, :], preferred_element_type=F32)
        acc_ref[...] = part if c == 0 else acc_ref[...] + part
        for piece in pieces[c * per_chunk:(c + 1) * per_chunk]:
            order_before_next_accumulate(*piece())
    o_ref[...] = x1 + _rms_norm(acc_ref[...], g2_ref[...])


def _mix_ffn(l, x2, y_att, za2, seq_len, lw):
    t, d = x2.shape
    dff = lw["ffn_w1"].shape[2]
    tm = min(FFN_ROWS, seq_len)
    assert CONV_ROWS == POOL_ROWS and SGU_CHUNK % CONV_ROWS == 0 and tm % SGU_CHUNK == 0
    n_tiles = t // tm
    halo_per_tile = tm // MIX_HALO
    cur = lambda k: jnp.maximum(k - 1, 0)
    nxt = lambda k: jnp.minimum(k, n_tiles - 1)
    main = lambda col: pl.BlockSpec((tm, GROUP_WIDTH), lambda k: (nxt(k), col))
    halo = lambda col: pl.BlockSpec((MIX_HALO, GROUP_WIDTH),
                                    lambda k: (jnp.maximum(nxt(k) * halo_per_tile - 1, 0), col))
    once = lambda *shp: _per_layer(l, *shp, pipeline_mode=pl.Buffered(1))
    return pl.pallas_call(
        functools.partial(_mix_ffn_kernel, tiles_per_seq=seq_len // tm),
        grid=(n_tiles + 1,),
        in_specs=[pl.BlockSpec((tm, d), lambda k: (cur(k), 0)),
                  pl.BlockSpec((tm, GROUP_WIDTH), lambda k: (cur(k), 0)),
                  main(COL_CVAL), main(COL_CGATE), main(COL_POOL), main(COL_SGU_U), main(COL_SGU_V),
                  halo(COL_CVAL), halo(COL_CGATE), halo(COL_POOL),
                  _per_layer(l, 1, lw["mix_vec"].shape[2]),
                  _per_layer(l, CONV_WIDTH, GROUP_WIDTH), _per_layer(l, GROUP_WIDTH, GROUP_WIDTH),
                  _per_layer(l, SGU_HEADS, SGU_CHUNK, SGU_CHUNK), _per_layer(l, SGU_CHUNK, GROUP_WIDTH),
                  once(d, d), once(d, dff), once(dff, d)],
        out_specs=pl.BlockSpec((tm, d), lambda k: (cur(k), 0)),
        out_shape=jax.ShapeDtypeStruct((t, d), F32),
        scratch_shapes=[pltpu.VMEM((3, tm, GROUP_WIDTH), F32),
                        pltpu.VMEM((MIX_HALO + tm, GROUP_WIDTH), F32),
                        pltpu.VMEM((MIX_HALO + tm, GROUP_WIDTH), F32),
                        pltpu.VMEM((tm, d), F32)],
        compiler_params=_cparams("arbitrary"),
        name="mix_ffn",
    )(x2, y_att, za2, za2, za2, za2, za2, za2, za2, za2,
      lw["mix_vec"], lw["conv_w"], lw["pool_bd"], lw["sgu_w"], lw["sgu_bias"],
      lw["w_out"], lw["ffn_w1"], lw["ffn_w2"])


def _rope_perm():
    idx = list(range(HEAD_DIM))
    for c in range(ROPE_HALF):
        idx[c], idx[c + ROPE_HALF] = c + ROPE_HALF, c
    return jnp.array(idx, jnp.int32)


def _split_w_in(w_in):
    gw = GROUP_WIDTH
    edges = [0, gw, 2 * gw, 3 * gw, 4 * gw, 5 * gw, 6 * gw]
    for _ in range(6):
        edges.append(edges[-1] + HEAD_DIM)
    edges.append(edges[-1] + ATT_HEADS * N_BRANCH)
    names = ("cval", "cgate", "pool", "su", "sv", "q", "kc", "vc", "ks", "vs", "kw", "vw", "g")
    return {n: w_in[:, edges[k]:edges[k + 1], :] for k, n in enumerate(names)}


def _prep_weights(p):
    depth = p["w_in"].shape[0]
    perm = _rope_perm()
    w = _split_w_in(jnp.swapaxes(p["w_in"], 1, 2).astype(BF16))
    wa = jnp.concatenate([w["cval"], w["cgate"], w["pool"], w["su"], w["sv"], w["kc"], w["vc"],
                          w["ks"], w["ks"][:, perm], w["kw"], w["kw"][:, perm]], axis=1)
    pad = jnp.zeros((depth, ZB_ROWS - ROW_G - ATT_HEADS * N_BRANCH, D_MODEL), BF16)
    wb = jnp.concatenate([w["q"], w["vs"], w["vw"], w["g"], pad], axis=1)

    n_grp = len(POOL_WINDOWS)
    pool_bd = (p["pool_w"][:, :, :, None, :] * jnp.eye(n_grp, dtype=F32)[None, :, None, :, None])
    pool_bd = pool_bd.reshape(depth, GROUP_WIDTH, GROUP_WIDTH)
    sgu_bias = jnp.repeat(jnp.swapaxes(p["sgu_b"], 1, 2), GROUP_WIDTH // SGU_HEADS, axis=2)

    w1k = p["cmp_k_w1"].reshape(depth, CMP_BLOCK, HEAD_DIM, HEAD_DIM)
    w1v = p["cmp_v_w1"].reshape(depth, CMP_BLOCK, HEAD_DIM, HEAD_DIM)
    z = jnp.zeros((depth, CMP_STRIDE, HEAD_DIM, HEAD_DIM), F32)

    def kv_diag(a, b):
        return jnp.concatenate([jnp.concatenate([a, z], axis=3), jnp.concatenate([z, b], axis=3)], axis=2)

    cmp_wa = kv_diag(w1k[:, :CMP_STRIDE], w1v[:, :CMP_STRIDE]).astype(BF16)
    cmp_wb = kv_diag(w1k[:, CMP_STRIDE:], w1v[:, CMP_STRIDE:]).astype(BF16)
    ones = jnp.ones((1, 1, HEAD_DIM), F32)
    cmp_pe = jnp.concatenate([p["cmp_k_pe"].reshape(depth, -1, 1) * ones,
                              p["cmp_v_pe"].reshape(depth, -1, 1) * ones], axis=2)
    cmp_w1 = jnp.concatenate([p["cmp_k_w1"], p["cmp_v_w1"]], axis=2)
    zk = jnp.zeros((depth, HEAD_DIM, LANES), F32)
    cmp_w2k = jnp.concatenate([jnp.concatenate([p["cmp_k_w2"], p["cmp_k_w2"][..., perm]], axis=2), zk],
                              axis=1).astype(BF16)
    cmp_w2vt = jnp.concatenate([jnp.zeros((depth, HEAD_DIM, HEAD_DIM), F32),
                                jnp.swapaxes(p["cmp_v_w2"], 1, 2)], axis=2).astype(BF16)

    row = lambda v: v[:, None, :]
    mix_vec = jnp.concatenate([p["conv_b"], p["conv_ln_g"], p["conv_ln_b"], p["pool_scale"],
                               p["sgu_ln_g"], p["sgu_ln_b"], p["post_mix_norm"], p["pre_ffn_norm"],
                               p["post_ffn_norm"]], axis=1)
    return dict(
        wa=wa, wb=wb, pre_mix=row(p["pre_mix_norm"]), mix_vec=row(mix_vec),
        conv_w=p["conv_w"], pool_bd=pool_bd.astype(BF16), sgu_w=p["sgu_w"], sgu_bias=sgu_bias,
        cmp_wa=cmp_wa, cmp_wb=cmp_wb, cmp_pe=cmp_pe, cmp_w1=cmp_w1, cmp_w2k=cmp_w2k, cmp_w2vt=cmp_w2vt,
        w_out=p["w_out"].astype(BF16), ffn_w1=p["ffn_w1"].astype(BF16), ffn_w2=p["ffn_w2"].astype(BF16),
    )


def _overlap_t(s):
    n_blk = s // CMP_STRIDE
    n_slc = s // SLC_BLOCK
    bs = jnp.arange(n_blk)[None, :] * CMP_STRIDE
    ss = jnp.arange(n_slc)[:, None] * SLC_BLOCK
    ov = jnp.clip(jnp.minimum(bs + CMP_BLOCK, ss + SLC_BLOCK) - jnp.maximum(bs, ss), 0)
    return (ov.astype(F32) / CMP_STRIDE).astype(BF16)


def _rope_tables(positions):
    bsz, s = positions.shape
    inv = (ROPE_THETA ** (-jnp.arange(ROPE_HALF, dtype=F32) * 2.0 / ROPE_DIM)).reshape(ROPE_HALF, 1)
    posf = positions.astype(F32)
    n_blk = s // CMP_STRIDE
    pos_end = posf[:, CMP_BLOCK - 1::CMP_STRIDE]
    pos_end = jnp.pad(pos_end, ((0, 0), (0, n_blk - pos_end.shape[1])))
    cos_t, sin_t = _rope_chan_table(posf[:, None, :], inv)
    cos_b, sin_b = _rope_chan_table(pos_end[:, None, :], inv)
    return cos_t, sin_t, cos_b, sin_b


def _layer(l, x, lw, tables, ov_t):
    bsz, s, d = x.shape
    za, *prep = _proj_prep(l, x, lw["pre_mix"], lw["wa"], lw["wb"], *tables, lw["cmp_wa"], lw["cmp_wb"],
                           lw["cmp_pe"], lw["cmp_w1"], lw["cmp_w2k"], lw["cmp_w2vt"])
    y_att = _att(*prep, ov_t)
    x2 = _mix_ffn(l, x.reshape(bsz * s, d), y_att.reshape(bsz * s, GROUP_WIDTH),
                  za.reshape(bsz * s, za.shape[-1]), s, lw)
    return x2.reshape(bsz, s, d)


def kernel(x, positions, pre_mix_norm, post_mix_norm, pre_ffn_norm, post_ffn_norm, w_in, conv_w, conv_b, conv_ln_g, conv_ln_b, pool_w, pool_scale, sgu_ln_g, sgu_ln_b, sgu_w, sgu_b, cmp_k_pe, cmp_k_w1, cmp_k_w2, cmp_v_pe, cmp_v_w1, cmp_v_w2, w_out, ffn_w1, ffn_w2):
    params = dict(pre_mix_norm=pre_mix_norm, post_mix_norm=post_mix_norm, pre_ffn_norm=pre_ffn_norm,
                  post_ffn_norm=post_ffn_norm, w_in=w_in, conv_w=conv_w, conv_b=conv_b,
                  conv_ln_g=conv_ln_g, conv_ln_b=conv_ln_b, pool_w=pool_w, pool_scale=pool_scale,
                  sgu_ln_g=sgu_ln_g, sgu_ln_b=sgu_ln_b, sgu_w=sgu_w, sgu_b=sgu_b,
                  cmp_k_pe=cmp_k_pe, cmp_k_w1=cmp_k_w1, cmp_k_w2=cmp_k_w2, cmp_v_pe=cmp_v_pe,
                  cmp_v_w1=cmp_v_w1, cmp_v_w2=cmp_v_w2, w_out=w_out, ffn_w1=ffn_w1, ffn_w2=ffn_w2)
    bsz, s, d = x.shape
    depth = w_in.shape[0]
    tables = _rope_tables(positions)
    ov_t = _overlap_t(s)
    lw = _prep_weights(params)
    for l in range(depth):
        x = _layer(l, x, lw, tables, ov_t)
    return x
```

```python
import functools

import jax
import jax.numpy as jnp
from jax import lax
from jax.experimental import pallas as pl
from jax.experimental.pallas import tpu as pltpu

F32 = jnp.float32
BF16 = jnp.bfloat16

D_MODEL = 1024
GROUP_WIDTH = 256
CONV_WIDTH = 31
POOL_WINDOWS = (2, 4, 8, 16)
POOL_GROUP = 64
SGU_HEADS = 4
SGU_CHUNK = 128
ATT_HEADS = 4
HEAD_DIM = 64
ROPE_DIM = 16
ROPE_HALF = 8
ROPE_THETA = 500000.0
CMP_BLOCK = 32
CMP_STRIDE = 16
SLC_BLOCK = 64
N_SELECT = 8
N_LOCAL = 2
WINDOW = 512
N_BRANCH = 3
NORM_EPS = 1e-6
NEG_INF = -1e30
FORCE_SCORE = 1e9

LANES = 128
SUBLANES = 8
VMEM_LIMIT_BYTES = 56 * 1024 * 1024

PROJ_ROWS = 1024
FFN_ROWS = 512
FFN_COLS = 512
CONV_ROWS = 64
POOL_ROWS = 64
ATT_Q = 256
ATT_K = 256
ATT_STREAMS = 4
LOG2_E = 1.4426950408889634
V_ROWS = 80
MASK_BIAS = -2e30

COL_CVAL, COL_CGATE, COL_POOL, COL_SGU_U, COL_SGU_V = 0, 1, 2, 3, 4
COL_KCVC, COL_KS, COL_KW = 10, 11, 12
ZA_WIDTH = 1664
ROW_Q, ROW_VS, ROW_VW, ROW_G = 0, 256, 320, 384
ZB_ROWS = 400


def _cparams(*sem):
    return pltpu.CompilerParams(dimension_semantics=sem, vmem_limit_bytes=VMEM_LIMIT_BYTES)


def _per_layer(l, *shape, pipeline_mode=None):
    zeros = (0,) * len(shape)
    extra = {} if pipeline_mode is None else {"pipeline_mode": pipeline_mode}
    return pl.BlockSpec((None,) + shape, lambda *_: (l,) + zeros, **extra)


def _gelu(x):
    return 0.5 * x * (1.0 + jnp.tanh(0.7978845608028654 * (x + 0.044715 * (x * x * x))))


def _sigmoid(x):
    return 0.5 * jnp.tanh(0.5 * x) + 0.5


def _layer_norm(x, g, b):
    mu = jnp.mean(x, axis=-1, keepdims=True)
    d = x - mu
    var = jnp.mean(d * d, axis=-1, keepdims=True)
    return d * lax.rsqrt(var + NORM_EPS) * g + b


def _rms_norm(x, g):
    return x * lax.rsqrt(jnp.mean(x * x, axis=-1, keepdims=True) + NORM_EPS) * g


def _rope_chan_kernel(pos_ref, inv_ref, cos_ref, sin_ref):
    ang = inv_ref[...] * pos_ref[0]
    cos_ref[0] = jnp.cos(ang)
    sin_ref[0] = jnp.sin(ang)


def _rope_chan_table(pos_row, inv_col):
    b, _, s = pos_row.shape
    spec = pl.BlockSpec((1, ROPE_HALF, s), lambda i: (i, 0, 0))
    return pl.pallas_call(
        _rope_chan_kernel,
        grid=(b,),
        in_specs=[pl.BlockSpec((1, 1, s), lambda i: (i, 0, 0)),
                  pl.BlockSpec((ROPE_HALF, 1), lambda i: (0, 0))],
        out_specs=[spec, spec],
        out_shape=[jax.ShapeDtypeStruct((b, ROPE_HALF, s), F32)] * 2,
        compiler_params=_cparams("parallel"),
        name="rope_chan_table",
    )(pos_row, inv_col)


MIX_HALO = 32


def _conv_rows(hp_ref, r0, w_ref, b_ref, lg_ref, lb_ref):
    shift = MIX_HALO - (CONV_WIDTH - 1)
    n_win = CONV_ROWS + MIX_HALO
    acc = jnp.zeros((CONV_ROWS, GROUP_WIDTH), F32) + b_ref[...]
    win = hp_ref[r0:r0 + n_win, :]
    for r in range(SUBLANES):
        taps = [k for k in range(CONV_WIDTH) if (k + shift) % SUBLANES == r]
        wr = win if r == 0 else pltpu.roll(win, n_win - r, axis=0)
        for k in taps:
            off = k + shift - r
            acc = acc + wr[off:off + CONV_ROWS, :] * w_ref[k:k + 1, :]
    y = _layer_norm(acc, lg_ref[...], lb_ref[...])
    return y * _sigmoid(y)


POOL_PAD = 16


def _pool_rows(pp_ref, r0, seq_t0, w_ref, sc_ref):
    lane = lax.broadcasted_iota(jnp.int32, (1, GROUP_WIDTH), 1)
    grp = lane // POOL_GROUP
    win = jnp.where(grp == 0, POOL_WINDOWS[0],
                    jnp.where(grp == 1, POOL_WINDOWS[1],
                              jnp.where(grp == 2, POOL_WINDOWS[2], POOL_WINDOWS[3])))
    base = r0 + MIX_HALO - POOL_PAD
    rows = pp_ref[base:base + POOL_ROWS + POOL_PAD, :]
    p = rows[POOL_PAD:POOL_PAD + POOL_ROWS, :]
    acc = rows
    sums = []
    width = 1
    for w in POOL_WINDOWS:
        while width < w:
            acc = acc + pltpu.roll(acc, width, axis=0)
            width *= 2
        sums.append(acc[POOL_PAD:POOL_PAD + POOL_ROWS, :])
    total = jnp.where(grp == 0, sums[0],
                      jnp.where(grp == 1, sums[1],
                                jnp.where(grp == 2, sums[2], sums[3])))
    t = seq_t0 + r0 + lax.broadcasted_iota(jnp.int32, (POOL_ROWS, 1), 0)
    count = jnp.minimum(t + 1, win).astype(F32)
    mixed = total / count - p
    y = jnp.dot(mixed.astype(BF16), w_ref[...], preferred_element_type=F32)
    return y * sc_ref[...]


def _sgu_weights(w_ref):
    row = lax.broadcasted_iota(jnp.int32, (SGU_CHUNK, SGU_CHUNK), 0)
    col = lax.broadcasted_iota(jnp.int32, (SGU_CHUNK, SGU_CHUNK), 1)
    return [jnp.where(row >= col, w_ref[h], 0.0).astype(BF16) for h in range(SGU_HEADS)]


def _sgu_rows(u, v, ws, lg_ref, lb_ref, bias_ref):
    head = lax.broadcasted_iota(jnp.int32, (1, GROUP_WIDTH), 1) // (GROUP_WIDTH // SGU_HEADS)
    vb = _layer_norm(_gelu(v), lg_ref[...], lb_ref[...]).astype(BF16)
    mixed = jnp.zeros((SGU_CHUNK, GROUP_WIDTH), F32)
    for h in range(SGU_HEADS):
        mixed = jnp.where(head == h, jnp.dot(ws[h], vb, preferred_element_type=F32), mixed)
    return _gelu(u) * (mixed + bias_ref[...])


def _att_prep_kernel(kcvc_ref, ksx_ref, kwx_ref, zb_ref, cos_ref, sin_ref, cosb_ref, sinb_ref,
                     wa_ref, wb_ref, pe_ref, w1_ref, w2k_ref, w2vt_ref,
                     q_ref, ks_ref, kw_ref, vs_ref, vw_ref, kc_ref, vct_ref, g_ref, sh_ref):
    s = ksx_ref.shape[1]
    n_blk = s // CMP_STRIDE

    def tok_table(cos, sin):
        n = cos.shape[1]
        rest = HEAD_DIM - ROPE_DIM
        rows = [cos, cos, jnp.ones((rest, n), F32), -sin, sin, jnp.zeros((rest, n), F32)]
        return jnp.concatenate(rows, axis=0).T

    def rope_wide(x, cs):
        r = x * cs
        return r + pltpu.roll(r, HEAD_DIM, axis=1)

    def rope_tok(x, cs):
        return rope_wide(x, cs)[:, :HEAD_DIM]

    cos = cos_ref[0]
    sin = sin_ref[0]
    cs = tok_table(cos, sin)
    lane = lax.broadcasted_iota(jnp.int32, (s, LANES), 1)
    blk = lax.broadcasted_iota(jnp.int32, (s, LANES), 0) // SLC_BLOCK
    ks_ref[0] = jnp.where(lane < HEAD_DIM, rope_wide(ksx_ref[0], cs),
                          jnp.where(lane - HEAD_DIM == blk, 1.0, 0.0)).astype(BF16)
    kw_ref[0] = rope_tok(kwx_ref[0], cs).astype(BF16)

    scale = HEAD_DIM ** -0.5 * LOG2_E
    parts = []
    for h in range(ATT_HEADS):
        r0 = ROW_Q + h * HEAD_DIM
        x1 = zb_ref[0, r0:r0 + ROPE_HALF, :]
        x2 = zb_ref[0, r0 + ROPE_HALF:r0 + ROPE_DIM, :]
        parts += [x1 * cos - x2 * sin, x2 * cos + x1 * sin, zb_ref[0, r0 + ROPE_DIM:r0 + HEAD_DIM, :]]
    q_ref[0] = (jnp.concatenate(parts, axis=0) * scale).astype(BF16)

    ones_row = jnp.where(lax.broadcasted_iota(jnp.int32, (V_ROWS - HEAD_DIM, ATT_K), 0) == 0, 1.0, 0.0)
    for c in range(s // ATT_K):
        cols = slice(c * ATT_K, (c + 1) * ATT_K)
        vs_ref[0, c] = jnp.concatenate([zb_ref[0, ROW_VS:ROW_VS + HEAD_DIM, cols], ones_row], axis=0).astype(BF16)
        vw_ref[0, c] = jnp.concatenate([zb_ref[0, ROW_VW:ROW_VW + HEAD_DIM, cols], ones_row], axis=0).astype(BF16)
    g_ref[0] = _sigmoid(zb_ref[0, ROW_G:ROW_G + 16, :])

    acc_a = jnp.zeros((n_blk, LANES), F32)
    acc_b = jnp.zeros((n_blk, LANES), F32)
    for l in range(CMP_STRIDE):
        xl = kcvc_ref[0, pl.ds(l, n_blk, stride=CMP_STRIDE), :].astype(BF16)
        acc_a = acc_a + jnp.dot(xl, wa_ref[l], preferred_element_type=F32)
        acc_b = acc_b + jnp.dot(xl, wb_ref[l], preferred_element_type=F32)
    sh_ref[0:n_blk, :] = acc_b
    sh_ref[n_blk:n_blk + SUBLANES, :] = jnp.zeros((SUBLANES, LANES), F32)
    pe_term = jnp.sum(pe_ref[...] * w1_ref[...], axis=0, keepdims=True)
    hid = _gelu(acc_a + sh_ref[1:n_blk + 1, :] + pe_term).astype(BF16)
    kk = jnp.dot(hid, w2k_ref[...], preferred_element_type=F32)
    kc_ref[0] = rope_tok(kk, tok_table(cosb_ref[0], sinb_ref[0])).astype(BF16)
    vct_ref[0] = lax.dot_general(w2vt_ref[...], hid, (((1,), (1,)), ((), ())),
                                 preferred_element_type=F32).astype(BF16)


def _proj_prep_kernel(x_ref, g_ref, wa_ref, wb_ref, cos_ref, sin_ref, cosb_ref, sinb_ref,
                      cwa_ref, cwb_ref, pe_ref, w1_ref, w2k_ref, w2vt_ref, wo_f_ref, f1_f_ref, f2_f_ref,
                      za_ref, q_ref, ks_ref, kw_ref, vs_ref, vw_ref, kc_ref, vct_ref, gate_ref,
                      wo_b_ref, f1_b_ref, f2_b_ref,
                      kcvc_s, ksx_s, kwx_s, zb_s, sh_ref, *, row_tiles):
    wo_b_ref[0] = wo_f_ref[...].astype(BF16)
    f1_b_ref[0] = f1_f_ref[...].astype(BF16)
    f2_b_ref[0] = f2_f_ref[...].astype(BF16)
    j = pl.program_id(1)
    tm = x_ref.shape[1]
    h = _rms_norm(x_ref[0], g_ref[...]).astype(BF16)
    last_dims = (((1,), (1,)), ((), ()))
    z = lax.dot_general(h, wa_ref[...], last_dims, preferred_element_type=F32)
    local = COL_KCVC * LANES
    za_ref[0] = z[:, :local]
    rows = pl.ds(pl.multiple_of(j * tm, tm), tm)
    kcvc_s[0, rows, :] = z[:, local:local + LANES]
    ksx_s[0, rows, :] = z[:, local + LANES:local + 2 * LANES]
    kwx_s[0, rows, :] = z[:, local + 2 * LANES:local + 3 * LANES]
    zb = lax.dot_general(wb_ref[...], h, last_dims, preferred_element_type=F32)
    for jj in range(row_tiles):
        @pl.when(j == jj)
        def _():
            zb_s[0, :, jj * tm:(jj + 1) * tm] = zb

    @pl.when(j == row_tiles - 1)
    def _():
        _att_prep_kernel(kcvc_s, ksx_s, kwx_s, zb_s, cos_ref, sin_ref, cosb_ref, sinb_ref,
                         cwa_ref, cwb_ref, pe_ref, w1_ref, w2k_ref, w2vt_ref,
                         q_ref, ks_ref, kw_ref, vs_ref, vw_ref, kc_ref, vct_ref, gate_ref, sh_ref)


def _proj_prep(l, x, g, wa, wb, cos_t, sin_t, cos_b, sin_b, cwa, cwb, pe, w1, w2k, w2vt, w_out, ffn_w1, ffn_w2):
    bsz, s, d = x.shape
    tm = min(PROJ_ROWS, s)
    n_blk = s // CMP_STRIDE
    nt = s // ATT_K
    local = COL_KCVC * LANES
    row_tiles = s // tm
    steps = bsz * row_tiles
    dff = ffn_w1.shape[2]
    assert d % (steps * 16) == 0 and dff % (steps * LANES) == 0
    step = lambda i, j: i * row_tiles + j
    f32_slice = lambda shp, idx: pl.BlockSpec((None,) + shp, lambda i, j: (l,) + idx(step(i, j)))
    bf16_slice = lambda shp, idx: pl.BlockSpec((1,) + shp, lambda i, j: (0,) + idx(step(i, j)))
    by_rows = lambda k: (k, 0)
    by_cols = lambda k: (0, k)
    per_b3 = lambda shp: pl.BlockSpec((1,) + shp, lambda i, j: (i, 0, 0))
    per_b4 = lambda shp: pl.BlockSpec((1,) + shp, lambda i, j: (i, 0, 0, 0))
    return pl.pallas_call(
        functools.partial(_proj_prep_kernel, row_tiles=row_tiles),
        grid=(bsz, row_tiles),
        in_specs=[pl.BlockSpec((1, tm, d), lambda i, j: (i, j, 0)),
                  _per_layer(l, 1, d), _per_layer(l, ZA_WIDTH, d), _per_layer(l, ZB_ROWS, d),
                  per_b3((ROPE_HALF, s)), per_b3((ROPE_HALF, s)),
                  per_b3((ROPE_HALF, n_blk)), per_b3((ROPE_HALF, n_blk)),
                  _per_layer(l, CMP_STRIDE, LANES, LANES), _per_layer(l, CMP_STRIDE, LANES, LANES),
                  _per_layer(l, CMP_BLOCK * HEAD_DIM, LANES), _per_layer(l, CMP_BLOCK * HEAD_DIM, LANES),
                  _per_layer(l, LANES, LANES), _per_layer(l, HEAD_DIM, LANES),
                  f32_slice((d // steps, d), by_rows), f32_slice((d, dff // steps), by_cols),
                  f32_slice((dff // steps, d), by_rows)],
        out_specs=[pl.BlockSpec((1, tm, local), lambda i, j: (i, j, 0)),
                   per_b3((ATT_HEADS * HEAD_DIM, s)), per_b3((s, LANES)), per_b3((s, HEAD_DIM)),
                   per_b4((nt, V_ROWS, ATT_K)), per_b4((nt, V_ROWS, ATT_K)),
                   per_b3((n_blk, HEAD_DIM)), per_b3((HEAD_DIM, n_blk)), per_b3((16, s)),
                   bf16_slice((d // steps, d), by_rows), bf16_slice((d, dff // steps), by_cols),
                   bf16_slice((dff // steps, d), by_rows)],
        out_shape=[jax.ShapeDtypeStruct((bsz, s, local), F32),
                   jax.ShapeDtypeStruct((bsz, ATT_HEADS * HEAD_DIM, s), BF16),
                   jax.ShapeDtypeStruct((bsz, s, LANES), BF16),
                   jax.ShapeDtypeStruct((bsz, s, HEAD_DIM), BF16),
                   jax.ShapeDtypeStruct((bsz, nt, V_ROWS, ATT_K), BF16),
                   jax.ShapeDtypeStruct((bsz, nt, V_ROWS, ATT_K), BF16),
                   jax.ShapeDtypeStruct((bsz, n_blk, HEAD_DIM), BF16),
                   jax.ShapeDtypeStruct((bsz, HEAD_DIM, n_blk), BF16),
                   jax.ShapeDtypeStruct((bsz, 16, s), F32),
                   jax.ShapeDtypeStruct((1, d, d), BF16),
                   jax.ShapeDtypeStruct((1, d, dff), BF16),
                   jax.ShapeDtypeStruct((1, dff, d), BF16)],
        scratch_shapes=[pltpu.VMEM((1, s, LANES), F32), pltpu.VMEM((1, s, LANES), F32),
                        pltpu.VMEM((1, s, LANES), F32), pltpu.VMEM((1, ZB_ROWS, s), F32),
                        pltpu.VMEM((n_blk + SUBLANES, LANES), F32)],
        compiler_params=_cparams("parallel", "arbitrary"),
        name="proj_prep",
    )(x, g, wa, wb, cos_t, sin_t, cos_b, sin_b, cwa, cwb, pe, w1, w2k, w2vt, w_out, ffn_w1, ffn_w2)


def _att_tile(i, q, g, wmask_ref, ks_ref, kw_ref, vs_ref, vw_ref, kc_ref, vct_ref, ov_ref,
              qa_ref, sa_ref, sb_ref, m_ref, acc_ref):
    n_blk = kc_ref.shape[1]
    n_slc = ov_ref.shape[0]
    tq = ATT_Q
    wide = ATT_HEADS * tq
    t0 = i * tq

    qs = jnp.concatenate([q[h * HEAD_DIM:(h + 1) * HEAD_DIM] for h in range(ATT_HEADS)], axis=1)
    lane_w = lax.broadcasted_iota(jnp.int32, (1, wide), 1)
    t_w = t0 + (lane_w & (tq - 1))
    t_q = t0 + lax.broadcasted_iota(jnp.int32, (1, tq), 1)
    heads = lambda a: jnp.concatenate([a] * ATT_HEADS, axis=1)

    s_c = jnp.dot(kc_ref[0], qs, preferred_element_type=F32)
    blk_end = lax.broadcasted_iota(jnp.int32, (n_blk, 1), 0) * CMP_STRIDE + (CMP_BLOCK - 1)
    cmask = blk_end <= t_w
    s_m = jnp.where(cmask, s_c, NEG_INF)
    e = jnp.exp2(s_m - jnp.max(s_m, axis=0, keepdims=True))
    p_c = jnp.where(cmask, e * (1.0 / jnp.sum(e, axis=0, keepdims=True)), 0.0)
    o_cmp = jnp.dot(vct_ref[0], p_c.astype(BF16), preferred_element_type=F32)

    p_sum = p_c[:, 0:tq]
    for h in range(1, ATT_HEADS):
        p_sum = p_sum + p_c[:, h * tq:(h + 1) * tq]
    p_hi = p_sum.astype(BF16)
    p_lo = (p_sum - p_hi.astype(F32)).astype(BF16)
    imp = (jnp.dot(ov_ref[...], p_hi, preferred_element_type=F32)
           + jnp.dot(ov_ref[...], p_lo, preferred_element_type=F32))
    j = lax.broadcasted_iota(jnp.int32, (n_slc, 1), 0)
    back = t_q // SLC_BLOCK - j
    forced = (j == 0) | ((back >= 0) & (back < N_LOCAL))
    imp = jnp.where(forced, FORCE_SCORE, jnp.where(back < 0, -1.0, imp))
    rank = jnp.zeros((n_slc, tq), F32)
    for r in range(n_slc):
        row = imp[r:r + 1, :]
        ahead = (row > imp) | ((row == imp) & (j > r))
        rank = rank + jnp.where(ahead, 1.0, 0.0)
    sel_bias = jnp.where((rank < min(N_SELECT, n_slc)) & (back >= 0), 0.0, MASK_BIAS)
    qa_ref[0:HEAD_DIM, :] = qs
    qa_ref[HEAD_DIM:HEAD_DIM + n_slc, :] = heads(sel_bias).astype(BF16)
    qa_ref[HEAD_DIM + n_slc:, :] = jnp.zeros((LANES - HEAD_DIM - n_slc, wide), BF16)

    k_iota = lax.broadcasted_iota(jnp.int32, (ATT_K, 1), 0)

    back_tiles = WINDOW // ATT_K
    first = max(i - back_tiles, 0)
    n_win = i - first + 1
    s_w = jnp.dot(kw_ref[0, first * ATT_K:(i + 1) * ATT_K, :], qs, preferred_element_type=F32)
    s_w = s_w + heads(wmask_ref[min(i, back_tiles), 0:n_win * ATT_K, :])
    p_w = jnp.exp2(s_w - jnp.max(s_w, axis=0, keepdims=True)).astype(BF16)
    acc_w = None
    for n in range(n_win):
        part = jnp.dot(vw_ref[0, first + n], p_w[n * ATT_K:(n + 1) * ATT_K], preferred_element_type=F32)
        acc_w = part if acc_w is None else acc_w + part

    m_ref[...] = jnp.full((1, wide), NEG_INF, F32)
    acc_ref[...] = jnp.zeros((V_ROWS, wide), F32)

    def scores(kt):
        s_t = jnp.dot(ks_ref[0, kt * ATT_K:(kt + 1) * ATT_K, :], qa_ref[...], preferred_element_type=F32)
        if kt == i:
            s_t = jnp.where(heads((kt * ATT_K + k_iota) <= t_q), s_t, -jnp.inf)
        return s_t

    def consume(s_ref, kt):
        m_old = m_ref[...]
        m_new = jnp.maximum(m_old, jnp.max(s_ref[...], axis=0, keepdims=True))
        alpha = jnp.exp2(m_old - m_new)
        p = jnp.exp2(s_ref[...] - m_new).astype(BF16)
        m_ref[...] = m_new
        acc_ref[...] = alpha * acc_ref[...] + jnp.dot(vs_ref[0, kt], p, preferred_element_type=F32)

    bufs = (sa_ref, sb_ref)
    bufs[0][...] = scores(0)
    for kt in range(i + 1):
        if kt < i:
            bufs[(kt + 1) % 2][...] = scores(kt + 1)
        consume(bufs[kt % 2], kt)

    acc_s = acc_ref[...]
    gate = lambda br: jnp.concatenate(
        [g[h * N_BRANCH + br:h * N_BRANCH + br + 1, :] for h in range(ATT_HEADS)], axis=1)
    l_s = acc_s[HEAD_DIM:HEAD_DIM + 1, :]
    l_w = acc_w[HEAD_DIM:HEAD_DIM + 1, :]
    o = (gate(0) * o_cmp + (gate(1) * (1.0 / l_s)) * acc_s[:HEAD_DIM]
         + (gate(2) * (1.0 / l_w)) * acc_w[:HEAD_DIM])
    return jnp.concatenate([o[:, h * tq:(h + 1) * tq] for h in range(ATT_HEADS)], axis=0).T


def _att_kernel(q_ref, g_ref, ks_ref, kw_ref, vs_ref, vw_ref, kc_ref, vct_ref, wmask_ref, ov_ref,
                o_ref, *scratch):
    n = ATT_STREAMS
    n_scr = len(scratch) // n
    n_tiles = q_ref.shape[2] // ATT_Q
    for i in range(n_tiles):
        lanes = slice(i * ATT_Q, (i + 1) * ATT_Q)
        slot = i % n
        o_ref[0, lanes, :] = _att_tile(
            i, q_ref[0, :, lanes], g_ref[0, :, lanes], wmask_ref, ks_ref, kw_ref, vs_ref, vw_ref,
            kc_ref, vct_ref, ov_ref, *scratch[slot * n_scr:(slot + 1) * n_scr])


def _att(q_t, ks, kw, vs_t, vw_t, kc, vc_t, g_t, ov_t):
    bsz, _, s = q_t.shape
    n_blk = kc.shape[1]
    nt = s // ATT_K
    n_slc = s // SLC_BLOCK
    wide = ATT_HEADS * ATT_Q
    assert ATT_Q == ATT_K, "one diagonal key tile per query tile"
    per_b3 = lambda shp: pl.BlockSpec((1,) + shp, lambda b: (b, 0, 0))
    per_b4 = lambda shp: pl.BlockSpec((1,) + shp, lambda b: (b, 0, 0, 0))
    tile_scratch = [pltpu.VMEM((LANES, wide), BF16),
                    pltpu.VMEM((ATT_K, wide), F32), pltpu.VMEM((ATT_K, wide), F32),
                    pltpu.VMEM((1, wide), F32), pltpu.VMEM((V_ROWS, wide), F32)]
    back_tiles = WINDOW // ATT_K
    span = WINDOW + ATT_Q
    tile = jnp.arange(back_tiles + 1)[:, None, None]
    key = jnp.maximum(tile - back_tiles, 0) * ATT_K + jnp.arange(span)[None, :, None]
    diff = tile * ATT_Q + jnp.arange(ATT_Q)[None, None, :] - key
    wmask = jnp.where((diff >= 0) & (diff < WINDOW), 0.0, -jnp.inf).astype(F32)
    return pl.pallas_call(
        _att_kernel,
        grid=(bsz,),
        in_specs=[per_b3((ATT_HEADS * HEAD_DIM, s)), per_b3((16, s)),
                  per_b3((s, LANES)), per_b3((s, HEAD_DIM)),
                  per_b4((nt, V_ROWS, ATT_K)), per_b4((nt, V_ROWS, ATT_K)),
                  per_b3((n_blk, HEAD_DIM)), per_b3((HEAD_DIM, n_blk)),
                  pl.BlockSpec((back_tiles + 1, span, ATT_Q), lambda b: (0, 0, 0)),
                  pl.BlockSpec((n_slc, n_blk), lambda b: (0, 0))],
        out_specs=per_b3((s, GROUP_WIDTH)),
        out_shape=jax.ShapeDtypeStruct((bsz, s, GROUP_WIDTH), F32),
        scratch_shapes=tile_scratch * ATT_STREAMS,
        compiler_params=_cparams("parallel"),
        name="sparse_attention",
    )(q_t, g_t, ks, kw, vs_t, vw_t, kc, vc_t, wmask, ov_t)


def _mix_ffn_kernel(x_ref, yatt_ref, ca_ref, cg_ref, pin_ref, su_ref, sv_ref, cah_ref, cgh_ref, ph_ref,
                    vec_ref, cw_ref, pw_ref, sw_ref, sbias_ref, wo_ref, w1_ref, w2_ref,
                    o_ref, y_ref, hp_ref, pp_ref, acc_ref, *, tiles_per_seq):
    k = pl.program_id(0)
    tm, d = x_ref.shape
    widths = [GROUP_WIDTH] * 6 + [d] * 3
    starts = [sum(widths[:n]) for n in range(len(widths))]
    (cb_ref, clg_ref, clb_ref, psc_ref, slg_ref, slb_ref, gm_ref, g1_ref, g2_ref) = [
        vec_ref.at[:, a:a + w] for a, w in zip(starts, widths)]

    @pl.when(k == 0)
    def _():
        y_ref[...] = jnp.zeros_like(y_ref)

    kk = jnp.minimum(k, pl.num_programs(0) - 2)
    seq_tile = kk % tiles_per_seq
    has_prev = seq_tile > 0

    def conv_piece(r0):
        y_ref[0, r0:r0 + CONV_ROWS, :] = _conv_rows(hp_ref, r0, cw_ref, cb_ref, clg_ref, clb_ref)
        return 0, r0

    def pool_piece(r0):
        y_ref[1, r0:r0 + POOL_ROWS, :] = _pool_rows(pp_ref, r0, seq_tile * tm, pw_ref, psc_ref)
        return 1, r0

    def sgu_piece(r0):
        rows = slice(r0, r0 + SGU_CHUNK)
        y_ref[2, rows, :] = _sgu_rows(su_ref[rows, :], sv_ref[rows, :], _sgu_weights(sw_ref),
                                      slg_ref, slb_ref, sbias_ref)
        return 2, r0

    def order_before_next_accumulate(n, r0):
        bits = pltpu.bitcast(y_ref[n, r0:r0 + SUBLANES, 0:LANES], jnp.uint32)
        zero = pltpu.bitcast(lax.shift_right_logical(bits, jnp.uint32(32)), F32)
        acc_ref[0:SUBLANES, 0:LANES] = acc_ref[0:SUBLANES, 0:LANES] + zero

    pieces = []
    for r0 in range(0, tm, CONV_ROWS):
        pieces += [functools.partial(conv_piece, r0), functools.partial(pool_piece, r0)]
        if r0 % SGU_CHUNK == 0:
            pieces.append(functools.partial(sgu_piece, r0))
    n_chunks = w1_ref.shape[1] // FFN_COLS
    per_chunk = -(-len(pieces) // (n_chunks - 1))

    acc = jnp.dot(yatt_ref[...].astype(BF16), wo_ref[3 * GROUP_WIDTH:4 * GROUP_WIDTH, :],
                  preferred_element_type=F32)
    for n in range(3):
        acc = acc + jnp.dot(y_ref[n].astype(BF16), wo_ref[n * GROUP_WIDTH:(n + 1) * GROUP_WIDTH, :],
                            preferred_element_type=F32)
    x1 = x_ref[...] + _rms_norm(acc, gm_ref[...])
    h = _rms_norm(x1, g1_ref[...]).astype(BF16)
    hp_ref[0:MIX_HALO, :] = jnp.where(has_prev, cah_ref[...] * _sigmoid(cgh_ref[...]), 0.0)
    hp_ref[MIX_HALO:, :] = ca_ref[...] * _sigmoid(cg_ref[...])
    pp_ref[0:MIX_HALO, :] = jnp.where(has_prev, ph_ref[...], 0.0)
    pp_ref[MIX_HALO:, :] = pin_ref[...]
    for c in range(n_chunks):
        cols = slice(c * FFN_COLS, (c + 1) * FFN_COLS)
        f = jnp.maximum(jnp.dot(h, w1_ref[:, cols], preferred_element_type=F32), 0.0)
        part = jnp.dot((f * f).astype(BF16), w2_ref[cols, :], preferred_element_type=F32)
        acc_ref[...] = part if c == 0 else acc_ref[...] + part
        for piece in pieces[c * per_chunk:(c + 1) * per_chunk]:
            order_before_next_accumulate(*piece())
    o_ref[...] = x1 + _rms_norm(acc_ref[...], g2_ref[...])


def _mix_ffn(l, x2, y_att, za2, seq_len, lw, w_out, ffn_w1, ffn_w2):
    t, d = x2.shape
    dff = ffn_w1.shape[2]
    tm = min(FFN_ROWS, seq_len)
    assert CONV_ROWS == POOL_ROWS and SGU_CHUNK % CONV_ROWS == 0 and tm % SGU_CHUNK == 0
    n_tiles = t // tm
    halo_per_tile = tm // MIX_HALO
    cur = lambda k: jnp.maximum(k - 1, 0)
    nxt = lambda k: jnp.minimum(k, n_tiles - 1)
    main = lambda col: pl.BlockSpec((tm, GROUP_WIDTH), lambda k: (nxt(k), col))
    halo = lambda col: pl.BlockSpec((MIX_HALO, GROUP_WIDTH),
                                    lambda k: (jnp.maximum(nxt(k) * halo_per_tile - 1, 0), col))
    once = lambda *shp: _per_layer(0, *shp, pipeline_mode=pl.Buffered(1))
    return pl.pallas_call(
        functools.partial(_mix_ffn_kernel, tiles_per_seq=seq_len // tm),
        grid=(n_tiles + 1,),
        in_specs=[pl.BlockSpec((tm, d), lambda k: (cur(k), 0)),
                  pl.BlockSpec((tm, GROUP_WIDTH), lambda k: (cur(k), 0)),
                  main(COL_CVAL), main(COL_CGATE), main(COL_POOL), main(COL_SGU_U), main(COL_SGU_V),
                  halo(COL_CVAL), halo(COL_CGATE), halo(COL_POOL),
                  _per_layer(l, 1, lw["mix_vec"].shape[2]),
                  _per_layer(l, CONV_WIDTH, GROUP_WIDTH), _per_layer(l, GROUP_WIDTH, GROUP_WIDTH),
                  _per_layer(l, SGU_HEADS, SGU_CHUNK, SGU_CHUNK), _per_layer(l, SGU_CHUNK, GROUP_WIDTH),
                  once(d, d), once(d, dff), once(dff, d)],
        out_specs=pl.BlockSpec((tm, d), lambda k: (cur(k), 0)),
        out_shape=jax.ShapeDtypeStruct((t, d), F32),
        scratch_shapes=[pltpu.VMEM((3, tm, GROUP_WIDTH), F32),
                        pltpu.VMEM((MIX_HALO + tm, GROUP_WIDTH), F32),
                        pltpu.VMEM((MIX_HALO + tm, GROUP_WIDTH), F32),
                        pltpu.VMEM((tm, d), F32)],
        compiler_params=_cparams("arbitrary"),
        name="mix_ffn",
    )(x2, y_att, za2, za2, za2, za2, za2, za2, za2, za2,
      lw["mix_vec"], lw["conv_w"], lw["pool_bd"], lw["sgu_w"], lw["sgu_bias"],
      w_out, ffn_w1, ffn_w2)


def _rope_perm():
    idx = list(range(HEAD_DIM))
    for c in range(ROPE_HALF):
        idx[c], idx[c + ROPE_HALF] = c + ROPE_HALF, c
    return jnp.array(idx, jnp.int32)


def _split_w_in(w_in):
    gw = GROUP_WIDTH
    edges = [0, gw, 2 * gw, 3 * gw, 4 * gw, 5 * gw, 6 * gw]
    for _ in range(6):
        edges.append(edges[-1] + HEAD_DIM)
    edges.append(edges[-1] + ATT_HEADS * N_BRANCH)
    names = ("cval", "cgate", "pool", "su", "sv", "q", "kc", "vc", "ks", "vs", "kw", "vw", "g")
    return {n: w_in[:, edges[k]:edges[k + 1], :] for k, n in enumerate(names)}


def _prep_weights(p):
    depth = p["w_in"].shape[0]
    perm = _rope_perm()
    w = _split_w_in(jnp.swapaxes(p["w_in"], 1, 2).astype(BF16))
    wa = jnp.concatenate([w["cval"], w["cgate"], w["pool"], w["su"], w["sv"], w["kc"], w["vc"],
                          w["ks"], w["ks"][:, perm], w["kw"], w["kw"][:, perm]], axis=1)
    pad = jnp.zeros((depth, ZB_ROWS - ROW_G - ATT_HEADS * N_BRANCH, D_MODEL), BF16)
    wb = jnp.concatenate([w["q"], w["vs"], w["vw"], w["g"], pad], axis=1)

    n_grp = len(POOL_WINDOWS)
    pool_bd = (p["pool_w"][:, :, :, None, :] * jnp.eye(n_grp, dtype=F32)[None, :, None, :, None])
    pool_bd = pool_bd.reshape(depth, GROUP_WIDTH, GROUP_WIDTH)
    sgu_bias = jnp.repeat(jnp.swapaxes(p["sgu_b"], 1, 2), GROUP_WIDTH // SGU_HEADS, axis=2)

    w1k = p["cmp_k_w1"].reshape(depth, CMP_BLOCK, HEAD_DIM, HEAD_DIM)
    w1v = p["cmp_v_w1"].reshape(depth, CMP_BLOCK, HEAD_DIM, HEAD_DIM)
    z = jnp.zeros((depth, CMP_STRIDE, HEAD_DIM, HEAD_DIM), F32)

    def kv_diag(a, b):
        return jnp.concatenate([jnp.concatenate([a, z], axis=3), jnp.concatenate([z, b], axis=3)], axis=2)

    cmp_wa = kv_diag(w1k[:, :CMP_STRIDE], w1v[:, :CMP_STRIDE]).astype(BF16)
    cmp_wb = kv_diag(w1k[:, CMP_STRIDE:], w1v[:, CMP_STRIDE:]).astype(BF16)
    ones = jnp.ones((1, 1, HEAD_DIM), F32)
    cmp_pe = jnp.concatenate([p["cmp_k_pe"].reshape(depth, -1, 1) * ones,
                              p["cmp_v_pe"].reshape(depth, -1, 1) * ones], axis=2)
    cmp_w1 = jnp.concatenate([p["cmp_k_w1"], p["cmp_v_w1"]], axis=2)
    zk = jnp.zeros((depth, HEAD_DIM, LANES), F32)
    cmp_w2k = jnp.concatenate([jnp.concatenate([p["cmp_k_w2"], p["cmp_k_w2"][..., perm]], axis=2), zk],
                              axis=1).astype(BF16)
    cmp_w2vt = jnp.concatenate([jnp.zeros((depth, HEAD_DIM, HEAD_DIM), F32),
                                jnp.swapaxes(p["cmp_v_w2"], 1, 2)], axis=2).astype(BF16)

    row = lambda v: v[:, None, :]
    mix_vec = jnp.concatenate([p["conv_b"], p["conv_ln_g"], p["conv_ln_b"], p["pool_scale"],
                               p["sgu_ln_g"], p["sgu_ln_b"], p["post_mix_norm"], p["pre_ffn_norm"],
                               p["post_ffn_norm"]], axis=1)
    return dict(
        wa=wa, wb=wb, pre_mix=row(p["pre_mix_norm"]), mix_vec=row(mix_vec),
        conv_w=p["conv_w"], pool_bd=pool_bd.astype(BF16), sgu_w=p["sgu_w"], sgu_bias=sgu_bias,
        cmp_wa=cmp_wa, cmp_wb=cmp_wb, cmp_pe=cmp_pe, cmp_w1=cmp_w1, cmp_w2k=cmp_w2k, cmp_w2vt=cmp_w2vt,
        w_out=p["w_out"], ffn_w1=p["ffn_w1"], ffn_w2=p["ffn_w2"],
    )


def _overlap_t(s):
    n_blk = s // CMP_STRIDE
    n_slc = s // SLC_BLOCK
    bs = jnp.arange(n_blk)[None, :] * CMP_STRIDE
    ss = jnp.arange(n_slc)[:, None] * SLC_BLOCK
    ov = jnp.clip(jnp.minimum(bs + CMP_BLOCK, ss + SLC_BLOCK) - jnp.maximum(bs, ss), 0)
    return (ov.astype(F32) / CMP_STRIDE).astype(BF16)


def _rope_tables(positions):
    bsz, s = positions.shape
    inv = (ROPE_THETA ** (-jnp.arange(ROPE_HALF, dtype=F32) * 2.0 / ROPE_DIM)).reshape(ROPE_HALF, 1)
    posf = positions.astype(F32)
    n_blk = s // CMP_STRIDE
    pos_end = posf[:, CMP_BLOCK - 1::CMP_STRIDE]
    pos_end = jnp.pad(pos_end, ((0, 0), (0, n_blk - pos_end.shape[1])))
    cos_t, sin_t = _rope_chan_table(posf[:, None, :], inv)
    cos_b, sin_b = _rope_chan_table(pos_end[:, None, :], inv)
    return cos_t, sin_t, cos_b, sin_b


def _layer(l, x, lw, tables, ov_t):
    bsz, s, d = x.shape
    za, *prep = _proj_prep(l, x, lw["pre_mix"], lw["wa"], lw["wb"], *tables, lw["cmp_wa"], lw["cmp_wb"],
                           lw["cmp_pe"], lw["cmp_w1"], lw["cmp_w2k"], lw["cmp_w2vt"],
                           lw["w_out"], lw["ffn_w1"], lw["ffn_w2"])
    *prep, w_out, ffn_w1, ffn_w2 = prep
    y_att = _att(*prep, ov_t)
    x2 = _mix_ffn(l, x.reshape(bsz * s, d), y_att.reshape(bsz * s, GROUP_WIDTH),
                  za.reshape(bsz * s, za.shape[-1]), s, lw, w_out, ffn_w1, ffn_w2)
    return x2.reshape(bsz, s, d)


def kernel(x, positions, pre_mix_norm, post_mix_norm, pre_ffn_norm, post_ffn_norm, w_in, conv_w, conv_b, conv_ln_g, conv_ln_b, pool_w, pool_scale, sgu_ln_g, sgu_ln_b, sgu_w, sgu_b, cmp_k_pe, cmp_k_w1, cmp_k_w2, cmp_v_pe, cmp_v_w1, cmp_v_w2, w_out, ffn_w1, ffn_w2):
    params = dict(pre_mix_norm=pre_mix_norm, post_mix_norm=post_mix_norm, pre_ffn_norm=pre_ffn_norm,
                  post_ffn_norm=post_ffn_norm, w_in=w_in, conv_w=conv_w, conv_b=conv_b,
                  conv_ln_g=conv_ln_g, conv_ln_b=conv_ln_b, pool_w=pool_w, pool_scale=pool_scale,
                  sgu_ln_g=sgu_ln_g, sgu_ln_b=sgu_ln_b, sgu_w=sgu_w, sgu_b=sgu_b,
                  cmp_k_pe=cmp_k_pe, cmp_k_w1=cmp_k_w1, cmp_k_w2=cmp_k_w2, cmp_v_pe=cmp_v_pe,
                  cmp_v_w1=cmp_v_w1, cmp_v_w2=cmp_v_w2, w_out=w_out, ffn_w1=ffn_w1, ffn_w2=ffn_w2)
    bsz, s, d = x.shape
    depth = w_in.shape[0]
    tables = _rope_tables(positions)
    ov_t = _overlap_t(s)
    lw = _prep_weights(params)
    for l in range(depth):
        x = _layer(l, x, lw, tables, ov_t)
    return x
```

```python
import functools

import jax
import jax.numpy as jnp
from jax import lax
from jax.experimental import pallas as pl
from jax.experimental.pallas import tpu as pltpu

F32 = jnp.float32
BF16 = jnp.bfloat16

D_MODEL = 1024
GROUP_WIDTH = 256
CONV_WIDTH = 31
POOL_WINDOWS = (2, 4, 8, 16)
POOL_GROUP = 64
SGU_HEADS = 4
SGU_CHUNK = 128
ATT_HEADS = 4
HEAD_DIM = 64
ROPE_DIM = 16
ROPE_HALF = 8
ROPE_THETA = 500000.0
CMP_BLOCK = 32
CMP_STRIDE = 16
SLC_BLOCK = 64
N_SELECT = 8
N_LOCAL = 2
WINDOW = 512
N_BRANCH = 3
NORM_EPS = 1e-6
NEG_INF = -1e30
FORCE_SCORE = 1e9

LANES = 128
SUBLANES = 8
VMEM_LIMIT_BYTES = 56 * 1024 * 1024

PROJ_ROWS = 1024
FFN_ROWS = 512
FFN_COLS = 512
CONV_ROWS = 64
POOL_ROWS = 64
ATT_Q = 256
ATT_K = 256
ATT_STREAMS = 4
LOG2_E = 1.4426950408889634
V_ROWS = 80
MASK_BIAS = -2e30

COL_CVAL, COL_CGATE, COL_POOL, COL_SGU_U, COL_SGU_V = 0, 1, 2, 3, 4
COL_KCVC = 10
ZA_WIDTH = 1664
ROW_Q, ROW_VS, ROW_VW, ROW_G = 0, 256, 320, 384
ZB_ROWS = 400


def _cparams(*sem):
    return pltpu.CompilerParams(dimension_semantics=sem, vmem_limit_bytes=VMEM_LIMIT_BYTES)


def _per_layer(l, *shape, pipeline_mode=None):
    zeros = (0,) * len(shape)
    extra = {} if pipeline_mode is None else {"pipeline_mode": pipeline_mode}
    return pl.BlockSpec((None,) + shape, lambda *_: (l,) + zeros, **extra)


def _gelu(x):
    return 0.5 * x * (1.0 + jnp.tanh(0.7978845608028654 * (x + 0.044715 * (x * x * x))))


def _sigmoid(x):
    return 0.5 * jnp.tanh(0.5 * x) + 0.5


def _layer_norm(x, g, b):
    mu = jnp.mean(x, axis=-1, keepdims=True)
    d = x - mu
    var = jnp.mean(d * d, axis=-1, keepdims=True)
    return d * lax.rsqrt(var + NORM_EPS) * g + b


def _rms_norm(x, g):
    return x * lax.rsqrt(jnp.mean(x * x, axis=-1, keepdims=True) + NORM_EPS) * g


def _rope_chan_kernel(pos_ref, inv_ref, cos_ref, sin_ref):
    ang = inv_ref[...] * pos_ref[0]
    cos_ref[0] = jnp.cos(ang)
    sin_ref[0] = jnp.sin(ang)


def _rope_chan_table(pos_row, inv_col):
    b, _, s = pos_row.shape
    spec = pl.BlockSpec((1, ROPE_HALF, s), lambda i: (i, 0, 0))
    return pl.pallas_call(
        _rope_chan_kernel,
        grid=(b,),
        in_specs=[pl.BlockSpec((1, 1, s), lambda i: (i, 0, 0)),
                  pl.BlockSpec((ROPE_HALF, 1), lambda i: (0, 0))],
        out_specs=[spec, spec],
        out_shape=[jax.ShapeDtypeStruct((b, ROPE_HALF, s), F32)] * 2,
        compiler_params=_cparams("parallel"),
        name="rope_chan_table",
    )(pos_row, inv_col)


MIX_HALO = 32


def _conv_rows(hp_ref, r0, w_ref, b_ref, lg_ref, lb_ref):
    shift = MIX_HALO - (CONV_WIDTH - 1)
    n_win = CONV_ROWS + MIX_HALO
    acc = jnp.zeros((CONV_ROWS, GROUP_WIDTH), F32) + b_ref[...]
    win = hp_ref[r0:r0 + n_win, :]
    for r in range(SUBLANES):
        taps = [k for k in range(CONV_WIDTH) if (k + shift) % SUBLANES == r]
        wr = win if r == 0 else pltpu.roll(win, n_win - r, axis=0)
        for k in taps:
            off = k + shift - r
            acc = acc + wr[off:off + CONV_ROWS, :] * w_ref[k:k + 1, :]
    y = _layer_norm(acc, lg_ref[...], lb_ref[...])
    return y * _sigmoid(y)


POOL_PAD = 16


def _pool_rows(pp_ref, r0, seq_t0, w_ref, sc_ref):
    lane = lax.broadcasted_iota(jnp.int32, (1, GROUP_WIDTH), 1)
    grp = lane // POOL_GROUP
    win = jnp.where(grp == 0, POOL_WINDOWS[0],
                    jnp.where(grp == 1, POOL_WINDOWS[1],
                              jnp.where(grp == 2, POOL_WINDOWS[2], POOL_WINDOWS[3])))
    base = r0 + MIX_HALO - POOL_PAD
    rows = pp_ref[base:base + POOL_ROWS + POOL_PAD, :]
    p = rows[POOL_PAD:POOL_PAD + POOL_ROWS, :]
    acc = rows
    sums = []
    width = 1
    for w in POOL_WINDOWS:
        while width < w:
            acc = acc + pltpu.roll(acc, width, axis=0)
            width *= 2
        sums.append(acc[POOL_PAD:POOL_PAD + POOL_ROWS, :])
    total = jnp.where(grp == 0, sums[0],
                      jnp.where(grp == 1, sums[1],
                                jnp.where(grp == 2, sums[2], sums[3])))
    t = seq_t0 + r0 + lax.broadcasted_iota(jnp.int32, (POOL_ROWS, 1), 0)
    count = jnp.minimum(t + 1, win).astype(F32)
    mixed = total / count - p
    y = jnp.dot(mixed.astype(BF16), w_ref[...], preferred_element_type=F32)
    return y * sc_ref[...]


def _sgu_weights(w_ref):
    row = lax.broadcasted_iota(jnp.int32, (SGU_CHUNK, SGU_CHUNK), 0)
    col = lax.broadcasted_iota(jnp.int32, (SGU_CHUNK, SGU_CHUNK), 1)
    return [jnp.where(row >= col, w_ref[h], 0.0).astype(BF16) for h in range(SGU_HEADS)]


def _sgu_rows(u, v, ws, lg_ref, lb_ref, bias_ref):
    head = lax.broadcasted_iota(jnp.int32, (1, GROUP_WIDTH), 1) // (GROUP_WIDTH // SGU_HEADS)
    vb = _layer_norm(_gelu(v), lg_ref[...], lb_ref[...]).astype(BF16)
    mixed = jnp.zeros((SGU_CHUNK, GROUP_WIDTH), F32)
    for h in range(SGU_HEADS):
        mixed = jnp.where(head == h, jnp.dot(ws[h], vb, preferred_element_type=F32), mixed)
    return _gelu(u) * (mixed + bias_ref[...])


def _att_prep_kernel(kcvc_ref, ksx_ref, kwx_ref, zb_ref, cos_ref, sin_ref, cosb_ref, sinb_ref,
                     wa_ref, wb_ref, pe_ref, w1_ref, w2k_ref, w2vt_ref,
                     q_ref, ks_ref, kw_ref, vs_ref, vw_ref, kc_ref, vct_ref, g_ref, sh_ref):
    s = ksx_ref.shape[1]
    n_blk = s // CMP_STRIDE

    def tok_table(cos, sin):
        n = cos.shape[1]
        rest = HEAD_DIM - ROPE_DIM
        rows = [cos, cos, jnp.ones((rest, n), F32), -sin, sin, jnp.zeros((rest, n), F32)]
        return jnp.concatenate(rows, axis=0).T

    def rope_wide(x, cs):
        r = x * cs
        return r + pltpu.roll(r, HEAD_DIM, axis=1)

    def rope_tok(x, cs):
        return rope_wide(x, cs)[:, :HEAD_DIM]

    cos = cos_ref[0]
    sin = sin_ref[0]
    cs = tok_table(cos, sin)
    lane = lax.broadcasted_iota(jnp.int32, (s, LANES), 1)
    blk = lax.broadcasted_iota(jnp.int32, (s, LANES), 0) // SLC_BLOCK
    ks_ref[0] = jnp.where(lane < HEAD_DIM, rope_wide(ksx_ref[0], cs),
                          jnp.where(lane - HEAD_DIM == blk, 1.0, 0.0)).astype(BF16)
    kw_ref[0] = rope_tok(kwx_ref[0], cs).astype(BF16)

    scale = HEAD_DIM ** -0.5 * LOG2_E
    parts = []
    for h in range(ATT_HEADS):
        r0 = ROW_Q + h * HEAD_DIM
        x1 = zb_ref[0, r0:r0 + ROPE_HALF, :]
        x2 = zb_ref[0, r0 + ROPE_HALF:r0 + ROPE_DIM, :]
        parts += [x1 * cos - x2 * sin, x2 * cos + x1 * sin, zb_ref[0, r0 + ROPE_DIM:r0 + HEAD_DIM, :]]
    q_ref[0] = (jnp.concatenate(parts, axis=0) * scale).astype(BF16)

    ones_row = jnp.where(lax.broadcasted_iota(jnp.int32, (V_ROWS - HEAD_DIM, ATT_K), 0) == 0, 1.0, 0.0)
    for c in range(s // ATT_K):
        cols = slice(c * ATT_K, (c + 1) * ATT_K)
        vs_ref[0, c] = jnp.concatenate([zb_ref[0, ROW_VS:ROW_VS + HEAD_DIM, cols], ones_row], axis=0).astype(BF16)
        vw_ref[0, c] = jnp.concatenate([zb_ref[0, ROW_VW:ROW_VW + HEAD_DIM, cols], ones_row], axis=0).astype(BF16)
    g_ref[0] = _sigmoid(zb_ref[0, ROW_G:ROW_G + 16, :])

    acc_a = jnp.zeros((n_blk, LANES), F32)
    acc_b = jnp.zeros((n_blk, LANES), F32)
    for l in range(CMP_STRIDE):
        xl = kcvc_ref[0, pl.ds(l, n_blk, stride=CMP_STRIDE), :].astype(BF16)
        acc_a = acc_a + jnp.dot(xl, wa_ref[l], preferred_element_type=F32)
        acc_b = acc_b + jnp.dot(xl, wb_ref[l], preferred_element_type=F32)
    sh_ref[0:n_blk, :] = acc_b
    sh_ref[n_blk:n_blk + SUBLANES, :] = jnp.zeros((SUBLANES, LANES), F32)
    pe_term = jnp.sum(pe_ref[...] * w1_ref[...], axis=0, keepdims=True)
    hid = _gelu(acc_a + sh_ref[1:n_blk + 1, :] + pe_term).astype(BF16)
    kk = jnp.dot(hid, w2k_ref[...], preferred_element_type=F32)
    kc_ref[0] = rope_tok(kk, tok_table(cosb_ref[0], sinb_ref[0])).astype(BF16)
    vct_ref[0] = lax.dot_general(w2vt_ref[...], hid, (((1,), (1,)), ((), ())),
                                 preferred_element_type=F32).astype(BF16)


def _proj_prep_kernel(x_ref, g_ref, wa_ref, wb_ref, cos_ref, sin_ref, cosb_ref, sinb_ref,
                      cwa_ref, cwb_ref, pe_ref, w1_ref, w2k_ref, w2vt_ref, wo_f_ref, f1_f_ref, f2_f_ref,
                      za_ref, q_ref, ks_ref, kw_ref, vs_ref, vw_ref, kc_ref, vct_ref, gate_ref,
                      wo_b_ref, f1_b_ref, f2_b_ref,
                      kcvc_s, ksx_s, kwx_s, zb_s, sh_ref, *, row_tiles):
    wo_b_ref[0] = wo_f_ref[...].astype(BF16)
    f1_b_ref[0] = f1_f_ref[...].astype(BF16)
    f2_b_ref[0] = f2_f_ref[...].astype(BF16)
    j = pl.program_id(1)
    tm = x_ref.shape[1]
    h = _rms_norm(x_ref[0], g_ref[...]).astype(BF16)
    last_dims = (((1,), (1,)), ((), ()))
    z = lax.dot_general(h, wa_ref[...], last_dims, preferred_element_type=F32)
    local = COL_KCVC * LANES
    za_ref[0] = z[:, :local]
    rows = pl.ds(pl.multiple_of(j * tm, tm), tm)
    kcvc_s[0, rows, :] = z[:, local:local + LANES]
    ksx_s[0, rows, :] = z[:, local + LANES:local + 2 * LANES]
    kwx_s[0, rows, :] = z[:, local + 2 * LANES:local + 3 * LANES]
    zb = lax.dot_general(wb_ref[...], h, last_dims, preferred_element_type=F32)
    for jj in range(row_tiles):
        @pl.when(j == jj)
        def _():
            zb_s[0, :, jj * tm:(jj + 1) * tm] = zb

    @pl.when(j == row_tiles - 1)
    def _():
        _att_prep_kernel(kcvc_s, ksx_s, kwx_s, zb_s, cos_ref, sin_ref, cosb_ref, sinb_ref,
                         cwa_ref, cwb_ref, pe_ref, w1_ref, w2k_ref, w2vt_ref,
                         q_ref, ks_ref, kw_ref, vs_ref, vw_ref, kc_ref, vct_ref, gate_ref, sh_ref)


def _proj_prep(l, x, g, wa, wb, cos_t, sin_t, cos_b, sin_b, cwa, cwb, pe, w1, w2k, w2vt, w_out, ffn_w1, ffn_w2):
    bsz, s, d = x.shape
    tm = min(PROJ_ROWS, s)
    n_blk = s // CMP_STRIDE
    nt = s // ATT_K
    local = COL_KCVC * LANES
    row_tiles = s // tm
    steps = bsz * row_tiles
    dff = ffn_w1.shape[2]
    assert d % (steps * 16) == 0 and dff % (steps * LANES) == 0
    step = lambda i, j: i * row_tiles + j
    f32_slice = lambda shp, idx: pl.BlockSpec((None,) + shp, lambda i, j: (l,) + idx(step(i, j)))
    bf16_slice = lambda shp, idx: pl.BlockSpec((1,) + shp, lambda i, j: (0,) + idx(step(i, j)))
    by_rows = lambda k: (k, 0)
    by_cols = lambda k: (0, k)
    per_b3 = lambda shp: pl.BlockSpec((1,) + shp, lambda i, j: (i, 0, 0))
    per_b4 = lambda shp: pl.BlockSpec((1,) + shp, lambda i, j: (i, 0, 0, 0))
    return pl.pallas_call(
        functools.partial(_proj_prep_kernel, row_tiles=row_tiles),
        grid=(bsz, row_tiles),
        in_specs=[pl.BlockSpec((1, tm, d), lambda i, j: (i, j, 0)),
                  _per_layer(l, 1, d), _per_layer(l, ZA_WIDTH, d), _per_layer(l, ZB_ROWS, d),
                  per_b3((ROPE_HALF, s)), per_b3((ROPE_HALF, s)),
                  per_b3((ROPE_HALF, n_blk)), per_b3((ROPE_HALF, n_blk)),
                  _per_layer(l, CMP_STRIDE, LANES, LANES), _per_layer(l, CMP_STRIDE, LANES, LANES),
                  _per_layer(l, CMP_BLOCK * HEAD_DIM, LANES), _per_layer(l, CMP_BLOCK * HEAD_DIM, LANES),
                  _per_layer(l, LANES, LANES), _per_layer(l, HEAD_DIM, LANES),
                  f32_slice((d // steps, d), by_rows), f32_slice((d, dff // steps), by_cols),
                  f32_slice((dff // steps, d), by_rows)],
        out_specs=[pl.BlockSpec((1, tm, local), lambda i, j: (i, j, 0)),
                   per_b3((ATT_HEADS * HEAD_DIM, s)), per_b3((s, LANES)), per_b3((s, HEAD_DIM)),
                   per_b4((nt, V_ROWS, ATT_K)), per_b4((nt, V_ROWS, ATT_K)),
                   per_b3((n_blk, HEAD_DIM)), per_b3((HEAD_DIM, n_blk)), per_b3((16, s)),
                   bf16_slice((d // steps, d), by_rows), bf16_slice((d, dff // steps), by_cols),
                   bf16_slice((dff // steps, d), by_rows)],
        out_shape=[jax.ShapeDtypeStruct((bsz, s, local), F32),
                   jax.ShapeDtypeStruct((bsz, ATT_HEADS * HEAD_DIM, s), BF16),
                   jax.ShapeDtypeStruct((bsz, s, LANES), BF16),
                   jax.ShapeDtypeStruct((bsz, s, HEAD_DIM), BF16),
                   jax.ShapeDtypeStruct((bsz, nt, V_ROWS, ATT_K), BF16),
                   jax.ShapeDtypeStruct((bsz, nt, V_ROWS, ATT_K), BF16),
                   jax.ShapeDtypeStruct((bsz, n_blk, HEAD_DIM), BF16),
                   jax.ShapeDtypeStruct((bsz, HEAD_DIM, n_blk), BF16),
                   jax.ShapeDtypeStruct((bsz, 16, s), F32),
                   jax.ShapeDtypeStruct((1, d, d), BF16),
                   jax.ShapeDtypeStruct((1, d, dff), BF16),
                   jax.ShapeDtypeStruct((1, dff, d), BF16)],
        scratch_shapes=[pltpu.VMEM((1, s, LANES), F32), pltpu.VMEM((1, s, LANES), F32),
                        pltpu.VMEM((1, s, LANES), F32), pltpu.VMEM((1, ZB_ROWS, s), F32),
                        pltpu.VMEM((n_blk + SUBLANES, LANES), F32)],
        compiler_params=_cparams("parallel", "arbitrary"),
        name="proj_prep",
    )(x, g, wa, wb, cos_t, sin_t, cos_b, sin_b, cwa, cwb, pe, w1, w2k, w2vt, w_out, ffn_w1, ffn_w2)


def _att_tile(i, q, g, wmask_ref, ks_ref, kw_ref, vs_ref, vw_ref, kc_ref, vct_ref, ov_ref,
              qa_ref, sa_ref, sb_ref, m_ref, acc_ref):
    n_blk = kc_ref.shape[1]
    n_slc = ov_ref.shape[0]
    tq = ATT_Q
    wide = ATT_HEADS * tq
    t0 = i * tq

    qs = jnp.concatenate([q[h * HEAD_DIM:(h + 1) * HEAD_DIM] for h in range(ATT_HEADS)], axis=1)
    lane_w = lax.broadcasted_iota(jnp.int32, (1, wide), 1)
    t_w = t0 + (lane_w & (tq - 1))
    t_q = t0 + lax.broadcasted_iota(jnp.int32, (1, tq), 1)
    heads = lambda a: jnp.concatenate([a] * ATT_HEADS, axis=1)

    s_c = jnp.dot(kc_ref[0], qs, preferred_element_type=F32)
    blk_end = lax.broadcasted_iota(jnp.int32, (n_blk, 1), 0) * CMP_STRIDE + (CMP_BLOCK - 1)
    cmask = blk_end <= t_w
    s_m = jnp.where(cmask, s_c, NEG_INF)
    e = jnp.exp2(s_m - jnp.max(s_m, axis=0, keepdims=True))
    p_c = jnp.where(cmask, e * (1.0 / jnp.sum(e, axis=0, keepdims=True)), 0.0)
    o_cmp = jnp.dot(vct_ref[0], p_c.astype(BF16), preferred_element_type=F32)

    p_sum = p_c[:, 0:tq]
    for h in range(1, ATT_HEADS):
        p_sum = p_sum + p_c[:, h * tq:(h + 1) * tq]
    p_hi = p_sum.astype(BF16)
    p_lo = (p_sum - p_hi.astype(F32)).astype(BF16)
    imp = (jnp.dot(ov_ref[...], p_hi, preferred_element_type=F32)
           + jnp.dot(ov_ref[...], p_lo, preferred_element_type=F32))
    j = lax.broadcasted_iota(jnp.int32, (n_slc, 1), 0)
    back = t_q // SLC_BLOCK - j
    forced = (j == 0) | ((back >= 0) & (back < N_LOCAL))
    imp = jnp.where(forced, FORCE_SCORE, jnp.where(back < 0, -1.0, imp))
    rank = jnp.zeros((n_slc, tq), F32)
    for r in range(n_slc):
        row = imp[r:r + 1, :]
        ahead = (row > imp) | ((row == imp) & (j > r))
        rank = rank + jnp.where(ahead, 1.0, 0.0)
    sel_bias = jnp.where((rank < min(N_SELECT, n_slc)) & (back >= 0), 0.0, MASK_BIAS)
    qa_ref[0:HEAD_DIM, :] = qs
    qa_ref[HEAD_DIM:HEAD_DIM + n_slc, :] = heads(sel_bias).astype(BF16)
    qa_ref[HEAD_DIM + n_slc:, :] = jnp.zeros((LANES - HEAD_DIM - n_slc, wide), BF16)

    k_iota = lax.broadcasted_iota(jnp.int32, (ATT_K, 1), 0)

    back_tiles = WINDOW // ATT_K
    first = max(i - back_tiles, 0)
    n_win = i - first + 1
    s_w = jnp.dot(kw_ref[0, first * ATT_K:(i + 1) * ATT_K, :], qs, preferred_element_type=F32)
    s_w = s_w + heads(wmask_ref[min(i, back_tiles), 0:n_win * ATT_K, :])
    p_w = jnp.exp2(s_w - jnp.max(s_w, axis=0, keepdims=True)).astype(BF16)
    acc_w = None
    for n in range(n_win):
        part = jnp.dot(vw_ref[0, first + n], p_w[n * ATT_K:(n + 1) * ATT_K], preferred_element_type=F32)
        acc_w = part if acc_w is None else acc_w + part

    m_ref[...] = jnp.full((1, wide), NEG_INF, F32)
    acc_ref[...] = jnp.zeros((V_ROWS, wide), F32)

    def scores(kt):
        s_t = jnp.dot(ks_ref[0, kt * ATT_K:(kt + 1) * ATT_K, :], qa_ref[...], preferred_element_type=F32)
        if kt == i:
            s_t = jnp.where(heads((kt * ATT_K + k_iota) <= t_q), s_t, -jnp.inf)
        return s_t

    def consume(s_ref, kt):
        m_old = m_ref[...]
        m_new = jnp.maximum(m_old, jnp.max(s_ref[...], axis=0, keepdims=True))
        alpha = jnp.exp2(m_old - m_new)
        p = jnp.exp2(s_ref[...] - m_new).astype(BF16)
        m_ref[...] = m_new
        acc_ref[...] = alpha * acc_ref[...] + jnp.dot(vs_ref[0, kt], p, preferred_element_type=F32)

    bufs = (sa_ref, sb_ref)
    bufs[0][...] = scores(0)
    for kt in range(i + 1):
        if kt < i:
            bufs[(kt + 1) % 2][...] = scores(kt + 1)
        consume(bufs[kt % 2], kt)

    acc_s = acc_ref[...]
    gate = lambda br: jnp.concatenate(
        [g[h * N_BRANCH + br:h * N_BRANCH + br + 1, :] for h in range(ATT_HEADS)], axis=1)
    l_s = acc_s[HEAD_DIM:HEAD_DIM + 1, :]
    l_w = acc_w[HEAD_DIM:HEAD_DIM + 1, :]
    o = (gate(0) * o_cmp + (gate(1) * (1.0 / l_s)) * acc_s[:HEAD_DIM]
         + (gate(2) * (1.0 / l_w)) * acc_w[:HEAD_DIM])
    return jnp.concatenate([o[:, h * tq:(h + 1) * tq] for h in range(ATT_HEADS)], axis=0).T


def _att_kernel(q_ref, g_ref, ks_ref, kw_ref, vs_ref, vw_ref, kc_ref, vct_ref, wmask_ref, ov_ref,
                o_ref, *scratch):
    n = ATT_STREAMS
    n_scr = len(scratch) // n
    n_tiles = q_ref.shape[2] // ATT_Q
    for i in range(n_tiles):
        lanes = slice(i * ATT_Q, (i + 1) * ATT_Q)
        slot = i % n
        o_ref[0, lanes, :] = _att_tile(
            i, q_ref[0, :, lanes], g_ref[0, :, lanes], wmask_ref, ks_ref, kw_ref, vs_ref, vw_ref,
            kc_ref, vct_ref, ov_ref, *scratch[slot * n_scr:(slot + 1) * n_scr])


def _att(q_t, ks, kw, vs_t, vw_t, kc, vc_t, g_t, ov_t):
    bsz, _, s = q_t.shape
    n_blk = kc.shape[1]
    nt = s // ATT_K
    n_slc = s // SLC_BLOCK
    wide = ATT_HEADS * ATT_Q
    assert ATT_Q == ATT_K, "one diagonal key tile per query tile"
    per_b3 = lambda shp: pl.BlockSpec((1,) + shp, lambda b: (b, 0, 0))
    per_b4 = lambda shp: pl.BlockSpec((1,) + shp, lambda b: (b, 0, 0, 0))
    tile_scratch = [pltpu.VMEM((LANES, wide), BF16),
                    pltpu.VMEM((ATT_K, wide), F32), pltpu.VMEM((ATT_K, wide), F32),
                    pltpu.VMEM((1, wide), F32), pltpu.VMEM((V_ROWS, wide), F32)]
    back_tiles = WINDOW // ATT_K
    span = WINDOW + ATT_Q
    tile = jnp.arange(back_tiles + 1)[:, None, None]
    key = jnp.maximum(tile - back_tiles, 0) * ATT_K + jnp.arange(span)[None, :, None]
    diff = tile * ATT_Q + jnp.arange(ATT_Q)[None, None, :] - key
    wmask = jnp.where((diff >= 0) & (diff < WINDOW), 0.0, -jnp.inf).astype(F32)
    return pl.pallas_call(
        _att_kernel,
        grid=(bsz,),
        in_specs=[per_b3((ATT_HEADS * HEAD_DIM, s)), per_b3((16, s)),
                  per_b3((s, LANES)), per_b3((s, HEAD_DIM)),
                  per_b4((nt, V_ROWS, ATT_K)), per_b4((nt, V_ROWS, ATT_K)),
                  per_b3((n_blk, HEAD_DIM)), per_b3((HEAD_DIM, n_blk)),
                  pl.BlockSpec((back_tiles + 1, span, ATT_Q), lambda b: (0, 0, 0)),
                  pl.BlockSpec((n_slc, n_blk), lambda b: (0, 0))],
        out_specs=per_b3((s, GROUP_WIDTH)),
        out_shape=jax.ShapeDtypeStruct((bsz, s, GROUP_WIDTH), F32),
        scratch_shapes=tile_scratch * ATT_STREAMS,
        compiler_params=_cparams("parallel"),
        name="sparse_attention",
    )(q_t, g_t, ks, kw, vs_t, vw_t, kc, vc_t, wmask, ov_t)


def _mix_ffn_kernel(x_ref, yatt_ref, ca_ref, cg_ref, pin_ref, su_ref, sv_ref, cah_ref, cgh_ref, ph_ref,
                    vec_ref, cw_ref, pw_ref, sw_ref, sbias_ref, wo_ref, w1_ref, w2_ref,
                    o_ref, y_ref, hp_ref, pp_ref, acc_ref, *, tiles_per_seq):
    k = pl.program_id(0)
    tm, d = x_ref.shape
    widths = [GROUP_WIDTH] * 6 + [d] * 3
    starts = [sum(widths[:n]) for n in range(len(widths))]
    (cb_ref, clg_ref, clb_ref, psc_ref, slg_ref, slb_ref, gm_ref, g1_ref, g2_ref) = [
        vec_ref.at[:, a:a + w] for a, w in zip(starts, widths)]

    @pl.when(k == 0)
    def _():
        y_ref[...] = jnp.zeros_like(y_ref)

    kk = jnp.minimum(k, pl.num_programs(0) - 2)
    seq_tile = kk % tiles_per_seq
    has_prev = seq_tile > 0

    def conv_piece(r0):
        y_ref[0, r0:r0 + CONV_ROWS, :] = _conv_rows(hp_ref, r0, cw_ref, cb_ref, clg_ref, clb_ref)
        return 0, r0

    def pool_piece(r0):
        y_ref[1, r0:r0 + POOL_ROWS, :] = _pool_rows(pp_ref, r0, seq_tile * tm, pw_ref, psc_ref)
        return 1, r0

    def sgu_piece(r0):
        rows = slice(r0, r0 + SGU_CHUNK)
        y_ref[2, rows, :] = _sgu_rows(su_ref[rows, :], sv_ref[rows, :], _sgu_weights(sw_ref),
                                      slg_ref, slb_ref, sbias_ref)
        return 2, r0

    def order_before_next_accumulate(n, r0):
        bits = pltpu.bitcast(y_ref[n, r0:r0 + SUBLANES, 0:LANES], jnp.uint32)
        zero = pltpu.bitcast(lax.shift_right_logical(bits, jnp.uint32(32)), F32)
        acc_ref[0:SUBLANES, 0:LANES] = acc_ref[0:SUBLANES, 0:LANES] + zero

    pieces = []
    for r0 in range(0, tm, CONV_ROWS):
        pieces += [functools.partial(conv_piece, r0), functools.partial(pool_piece, r0)]
        if r0 % SGU_CHUNK == 0:
            pieces.append(functools.partial(sgu_piece, r0))
    n_chunks = w1_ref.shape[1] // FFN_COLS
    per_chunk = -(-len(pieces) // (n_chunks - 1))

    acc = jnp.dot(yatt_ref[...].astype(BF16), wo_ref[3 * GROUP_WIDTH:4 * GROUP_WIDTH, :],
                  preferred_element_type=F32)
    for n in range(3):
        acc = acc + jnp.dot(y_ref[n].astype(BF16), wo_ref[n * GROUP_WIDTH:(n + 1) * GROUP_WIDTH, :],
                            preferred_element_type=F32)
    x1 = x_ref[...] + _rms_norm(acc, gm_ref[...])
    h = _rms_norm(x1, g1_ref[...]).astype(BF16)
    hp_ref[0:MIX_HALO, :] = jnp.where(has_prev, cah_ref[...] * _sigmoid(cgh_ref[...]), 0.0)
    hp_ref[MIX_HALO:, :] = ca_ref[...] * _sigmoid(cg_ref[...])
    pp_ref[0:MIX_HALO, :] = jnp.where(has_prev, ph_ref[...], 0.0)
    pp_ref[MIX_HALO:, :] = pin_ref[...]
    for c in range(n_chunks):
        cols = slice(c * FFN_COLS, (c + 1) * FFN_COLS)
        f = jnp.maximum(jnp.dot(h, w1_ref[:, cols], preferred_element_type=F32), 0.0)
        part = jnp.dot((f * f).astype(BF16), w2_ref[cols, :], preferred_element_type=F32)
        acc_ref[...] = part if c == 0 else acc_ref[...] + part
        for piece in pieces[c * per_chunk:(c + 1) * per_chunk]:
            order_before_next_accumulate(*piece())
    o_ref[...] = x1 + _rms_norm(acc_ref[...], g2_ref[...])


def _mix_ffn(l, x2, y_att, za2, seq_len, lw, w_out, ffn_w1, ffn_w2):
    t, d = x2.shape
    dff = ffn_w1.shape[2]
    tm = min(FFN_ROWS, seq_len)
    assert CONV_ROWS == POOL_ROWS and SGU_CHUNK % CONV_ROWS == 0 and tm % SGU_CHUNK == 0
    n_tiles = t // tm
    halo_per_tile = tm // MIX_HALO
    cur = lambda k: jnp.maximum(k - 1, 0)
    nxt = lambda k: jnp.minimum(k, n_tiles - 1)
    main = lambda col: pl.BlockSpec((tm, GROUP_WIDTH), lambda k: (nxt(k), col))
    halo = lambda col: pl.BlockSpec((MIX_HALO, GROUP_WIDTH),
                                    lambda k: (jnp.maximum(nxt(k) * halo_per_tile - 1, 0), col))
    once = lambda *shp: _per_layer(0, *shp, pipeline_mode=pl.Buffered(1))
    return pl.pallas_call(
        functools.partial(_mix_ffn_kernel, tiles_per_seq=seq_len // tm),
        grid=(n_tiles + 1,),
        in_specs=[pl.BlockSpec((tm, d), lambda k: (cur(k), 0)),
                  pl.BlockSpec((tm, GROUP_WIDTH), lambda k: (cur(k), 0)),
                  main(COL_CVAL), main(COL_CGATE), main(COL_POOL), main(COL_SGU_U), main(COL_SGU_V),
                  halo(COL_CVAL), halo(COL_CGATE), halo(COL_POOL),
                  _per_layer(l, 1, lw["mix_vec"].shape[2]),
                  _per_layer(l, CONV_WIDTH, GROUP_WIDTH), _per_layer(l, GROUP_WIDTH, GROUP_WIDTH),
                  _per_layer(l, SGU_HEADS, SGU_CHUNK, SGU_CHUNK), _per_layer(l, SGU_CHUNK, GROUP_WIDTH),
                  once(d, d), once(d, dff), once(dff, d)],
        out_specs=pl.BlockSpec((tm, d), lambda k: (cur(k), 0)),
        out_shape=jax.ShapeDtypeStruct((t, d), F32),
        scratch_shapes=[pltpu.VMEM((3, tm, GROUP_WIDTH), F32),
                        pltpu.VMEM((MIX_HALO + tm, GROUP_WIDTH), F32),
                        pltpu.VMEM((MIX_HALO + tm, GROUP_WIDTH), F32),
                        pltpu.VMEM((tm, d), F32)],
        compiler_params=_cparams("arbitrary"),
        name="mix_ffn",
    )(x2, y_att, za2, za2, za2, za2, za2, za2, za2, za2,
      lw["mix_vec"], lw["conv_w"], lw["pool_bd"], lw["sgu_w"], lw["sgu_bias"],
      w_out, ffn_w1, ffn_w2)


def _rope_perm():
    idx = list(range(HEAD_DIM))
    for c in range(ROPE_HALF):
        idx[c], idx[c + ROPE_HALF] = c + ROPE_HALF, c
    return jnp.array(idx, jnp.int32)


def _split_w_in(w_in):
    gw = GROUP_WIDTH
    edges = [0, gw, 2 * gw, 3 * gw, 4 * gw, 5 * gw, 6 * gw]
    for _ in range(6):
        edges.append(edges[-1] + HEAD_DIM)
    edges.append(edges[-1] + ATT_HEADS * N_BRANCH)
    names = ("cval", "cgate", "pool", "su", "sv", "q", "kc", "vc", "ks", "vs", "kw", "vw", "g")
    return {n: w_in[:, edges[k]:edges[k + 1], :] for k, n in enumerate(names)}


def _prep_weights(p):
    depth = p["w_in"].shape[0]
    perm = _rope_perm()
    w = _split_w_in(jnp.swapaxes(p["w_in"], 1, 2).astype(BF16))
    wa = jnp.concatenate([w["cval"], w["cgate"], w["pool"], w["su"], w["sv"], w["kc"], w["vc"],
                          w["ks"], w["ks"][:, perm], w["kw"], w["kw"][:, perm]], axis=1)
    pad = jnp.zeros((depth, ZB_ROWS - ROW_G - ATT_HEADS * N_BRANCH, D_MODEL), BF16)
    wb = jnp.concatenate([w["q"], w["vs"], w["vw"], w["g"], pad], axis=1)

    n_grp = len(POOL_WINDOWS)
    pool_bd = (p["pool_w"][:, :, :, None, :] * jnp.eye(n_grp, dtype=F32)[None, :, None, :, None])
    pool_bd = pool_bd.reshape(depth, GROUP_WIDTH, GROUP_WIDTH)
    sgu_bias = jnp.repeat(jnp.swapaxes(p["sgu_b"], 1, 2), GROUP_WIDTH // SGU_HEADS, axis=2)

    w1k = p["cmp_k_w1"].reshape(depth, CMP_BLOCK, HEAD_DIM, HEAD_DIM)
    w1v = p["cmp_v_w1"].reshape(depth, CMP_BLOCK, HEAD_DIM, HEAD_DIM)
    z = jnp.zeros((depth, CMP_STRIDE, HEAD_DIM, HEAD_DIM), F32)

    def kv_diag(a, b):
        return jnp.concatenate([jnp.concatenate([a, z], axis=3), jnp.concatenate([z, b], axis=3)], axis=2)

    cmp_wa = kv_diag(w1k[:, :CMP_STRIDE], w1v[:, :CMP_STRIDE]).astype(BF16)
    cmp_wb = kv_diag(w1k[:, CMP_STRIDE:], w1v[:, CMP_STRIDE:]).astype(BF16)
    ones = jnp.ones((1, 1, HEAD_DIM), F32)
    cmp_pe = jnp.concatenate([p["cmp_k_pe"].reshape(depth, -1, 1) * ones,
                              p["cmp_v_pe"].reshape(depth, -1, 1) * ones], axis=2)
    cmp_w1 = jnp.concatenate([p["cmp_k_w1"], p["cmp_v_w1"]], axis=2)
    zk = jnp.zeros((depth, HEAD_DIM, LANES), F32)
    cmp_w2k = jnp.concatenate([jnp.concatenate([p["cmp_k_w2"], p["cmp_k_w2"][..., perm]], axis=2), zk],
                              axis=1).astype(BF16)
    cmp_w2vt = jnp.concatenate([jnp.zeros((depth, HEAD_DIM, HEAD_DIM), F32),
                                jnp.swapaxes(p["cmp_v_w2"], 1, 2)], axis=2).astype(BF16)

    row = lambda v: v[:, None, :]
    mix_vec = jnp.concatenate([p["conv_b"], p["conv_ln_g"], p["conv_ln_b"], p["pool_scale"],
                               p["sgu_ln_g"], p["sgu_ln_b"], p["post_mix_norm"], p["pre_ffn_norm"],
                               p["post_ffn_norm"]], axis=1)
    return dict(
        wa=wa, wb=wb, pre_mix=row(p["pre_mix_norm"]), mix_vec=row(mix_vec),
        conv_w=p["conv_w"], pool_bd=pool_bd.astype(BF16), sgu_w=p["sgu_w"], sgu_bias=sgu_bias,
        cmp_wa=cmp_wa, cmp_wb=cmp_wb, cmp_pe=cmp_pe, cmp_w1=cmp_w1, cmp_w2k=cmp_w2k, cmp_w2vt=cmp_w2vt,
        w_out=p["w_out"], ffn_w1=p["ffn_w1"], ffn_w2=p["ffn_w2"],
    )


def _overlap_t(s):
    n_blk = s // CMP_STRIDE
    n_slc = s // SLC_BLOCK
    bs = jnp.arange(n_blk)[None, :] * CMP_STRIDE
    ss = jnp.arange(n_slc)[:, None] * SLC_BLOCK
    ov = jnp.clip(jnp.minimum(bs + CMP_BLOCK, ss + SLC_BLOCK) - jnp.maximum(bs, ss), 0)
    return (ov.astype(F32) / CMP_STRIDE).astype(BF16)


def _rope_tables(positions):
    bsz, s = positions.shape
    inv = (ROPE_THETA ** (-jnp.arange(ROPE_HALF, dtype=F32) * 2.0 / ROPE_DIM)).reshape(ROPE_HALF, 1)
    posf = positions.astype(F32)
    n_blk = s // CMP_STRIDE
    pos_end = posf[:, CMP_BLOCK - 1::CMP_STRIDE]
    pos_end = jnp.pad(pos_end, ((0, 0), (0, n_blk - pos_end.shape[1])))
    cos_t, sin_t = _rope_chan_table(posf[:, None, :], inv)
    cos_b, sin_b = _rope_chan_table(pos_end[:, None, :], inv)
    return cos_t, sin_t, cos_b, sin_b


def _layer(l, x, lw, tables, ov_t):
    bsz, s, d = x.shape
    za, *prep = _proj_prep(l, x, lw["pre_mix"], lw["wa"], lw["wb"], *tables, lw["cmp_wa"], lw["cmp_wb"],
                           lw["cmp_pe"], lw["cmp_w1"], lw["cmp_w2k"], lw["cmp_w2vt"],
                           lw["w_out"], lw["ffn_w1"], lw["ffn_w2"])
    *prep, w_out, ffn_w1, ffn_w2 = prep
    y_att = _att(*prep, ov_t)
    x2 = _mix_ffn(l, x.reshape(bsz * s, d), y_att.reshape(bsz * s, GROUP_WIDTH),
                  za.reshape(bsz * s, za.shape[-1]), s, lw, w_out, ffn_w1, ffn_w2)
    return x2.reshape(bsz, s, d)


def kernel(x, positions, pre_mix_norm, post_mix_norm, pre_ffn_norm, post_ffn_norm, w_in, conv_w, conv_b, conv_ln_g, conv_ln_b, pool_w, pool_scale, sgu_ln_g, sgu_ln_b, sgu_w, sgu_b, cmp_k_pe, cmp_k_w1, cmp_k_w2, cmp_v_pe, cmp_v_w1, cmp_v_w2, w_out, ffn_w1, ffn_w2):
    params = dict(pre_mix_norm=pre_mix_norm, post_mix_norm=post_mix_norm, pre_ffn_norm=pre_ffn_norm,
                  post_ffn_norm=post_ffn_norm, w_in=w_in, conv_w=conv_w, conv_b=conv_b,
                  conv_ln_g=conv_ln_g, conv_ln_b=conv_ln_b, pool_w=pool_w, pool_scale=pool_scale,
                  sgu_ln_g=sgu_ln_g, sgu_ln_b=sgu_ln_b, sgu_w=sgu_w, sgu_b=sgu_b,
                  cmp_k_pe=cmp_k_pe, cmp_k_w1=cmp_k_w1, cmp_k_w2=cmp_k_w2, cmp_v_pe=cmp_v_pe,
                  cmp_v_w1=cmp_v_w1, cmp_v_w2=cmp_v_w2, w_out=w_out, ffn_w1=ffn_w1, ffn_w2=ffn_w2)
    bsz, s, d = x.shape
    depth = w_in.shape[0]
    tables = _rope_tables(positions)
    ov_t = _overlap_t(s)
    lw = _prep_weights(params)
    for l in range(depth):
        x = _layer(l, x, lw, tables, ov_t)
    return x
```
